```python
import math
import jax, jax.numpy as jnp
from jax import lax
import numpy as np

D_MODEL = 1024
BATCH = 8
SEQ = 4096
DEPTH = 4

CTX_LEN = 256
GRID_W = 64
HEAD_DIM = 64
HY_WIDTH = D_MODEL // 4
GQA_HEADS = (D_MODEL // 2) // HEAD_DIM
GQA_KV_HEADS = 2
NA_HEADS = (D_MODEL // 4) // HEAD_DIM
NA_KH = 8
NA_KW = 16
Q_BLOCK = 128
ROPE_THETA = 10000.0
HY_EMB_DIM = 33
HY_FILTER_WIDTH = 64
HY_SHORT = 3
HY_DECAY_TARGET = 1e-2
HY_FAST_DECAY = 0.3
HY_SLOW_DECAY = 1.5
FFN_HIDDEN = int(math.ceil(8 * D_MODEL / 3 / 256)) * 256
NORM_EPS = 1e-6
ATTN_SCALE = HEAD_DIM ** -0.5

HY_COLS = 3 * HY_WIDTH
GQA_Q_COLS = GQA_HEADS * HEAD_DIM
GQA_KV_COLS = GQA_KV_HEADS * HEAD_DIM
NA_COLS = NA_HEADS * HEAD_DIM
IN_COLS = HY_COLS + GQA_Q_COLS + 2 * GQA_KV_COLS + 3 * NA_COLS
MIX_WIDTH = HY_WIDTH + GQA_Q_COLS + NA_COLS
SPLIT_POINTS = [int(v) for v in np.cumsum([HY_COLS, GQA_Q_COLS, GQA_KV_COLS, GQA_KV_COLS, NA_COLS, NA_COLS])]

kernel_name = 'hybrid_hyena_gqa_natten_dit'


def rms_norm(x, g):
    xf = x.astype(jnp.float32)
    y = xf * lax.rsqrt(jnp.mean(xf * xf, axis=-1, keepdims=True) + NORM_EPS)
    return (y * g.astype(jnp.float32)).astype(x.dtype)


def modulate(h, shift, scale):
    return h * (1 + scale) + shift


def axial_rope_tables(L):
    pos = jnp.arange(L)
    row = (pos // GRID_W).astype(jnp.float32)
    col = (pos % GRID_W).astype(jnp.float32)
    n_f = HEAD_DIM // 4
    inv = ROPE_THETA ** (-jnp.arange(n_f, dtype=jnp.float32) / n_f)
    ang = jnp.concatenate([row[:, None] * inv, col[:, None] * inv], axis=-1)
    return jnp.cos(ang), jnp.sin(ang)


def apply_rope(x, cos, sin):
    x1, x2 = jnp.split(x, 2, axis=-1)
    c = cos[None, :, None, :]
    s = sin[None, :, None, :]
    return jnp.concatenate([x1 * c - x2 * s, x1 * s + x2 * c], axis=-1).astype(x.dtype)


def split_proj(p):
    B, L, _ = p.shape
    hy, q, k, v, nq, nk, nv = jnp.split(p, SPLIT_POINTS, axis=-1)
    return (hy,
            q.reshape(B, L, GQA_HEADS, HEAD_DIM),
            k.reshape(B, L, GQA_KV_HEADS, HEAD_DIM),
            v.reshape(B, L, GQA_KV_HEADS, HEAD_DIM),
            nq.reshape(B, L, NA_HEADS, HEAD_DIM),
            nk.reshape(B, L, NA_HEADS, HEAD_DIM),
            nv.reshape(B, L, NA_HEADS, HEAD_DIM))


def attend(q, k, v):
    B, Lq, H, hd = q.shape
    kvh = k.shape[2]
    qg = q.reshape(B, Lq, kvh, H // kvh, hd)
    s = jnp.einsum('bqkgd,bskd->bkgqs', qg, k).astype(jnp.float32) * ATTN_SCALE
    p = jax.nn.softmax(s, axis=-1).astype(v.dtype)
    o = jnp.einsum('bkgqs,bskd->bqkgd', p, v)
    return o.reshape(B, Lq, H * hd)


def gqa_latent(q, k, v):
    B, S, H, hd = q.shape
    nb = S // Q_BLOCK
    qb = jnp.moveaxis(q.reshape(B, nb, Q_BLOCK, H, hd), 1, 0)
    o = lax.map(lambda qi: attend(qi, k, v), qb)
    return jnp.moveaxis(o, 0, 1).reshape(B, S, H * hd)


def natten_latent(q, k, v, kc, vc, rpb):
    B, S, H, hd = q.shape
    rows = S // GRID_W
    kh = min(NA_KH, rows)
    kw = NA_KW
    qg = q.reshape(B, rows, GRID_W, H, hd)
    kg = k.reshape(B, rows, GRID_W, H, hd)
    vg = v.reshape(B, rows, GRID_W, H, hd)
    cols = jnp.arange(GRID_W)
    cs = jnp.clip(cols - kw // 2, 0, GRID_W - kw)
    key_cols = cs[:, None] + jnp.arange(kw)[None, :]
    dcol = key_cols - cols[:, None] + (NA_KW - 1)

    def row_block(r):
        rs = jnp.clip(r - kh // 2, 0, rows - kh)
        kr = lax.dynamic_slice_in_dim(kg, rs, kh, axis=1)
        vr = lax.dynamic_slice_in_dim(vg, rs, kh, axis=1)
        kwin = kr[:, :, key_cols]
        vwin = vr[:, :, key_cols]
        qr = lax.dynamic_index_in_dim(qg, r, axis=1, keepdims=False)
        drow = rs + jnp.arange(kh) - r + (NA_KH - 1)
        bias = rpb[:, drow[None, :, None], dcol[:, None, :]]
        s_loc = jnp.einsum('bwhd,biwjhd->bhwij', qr, kwin).astype(jnp.float32) * ATTN_SCALE
        s_loc = (s_loc + bias.astype(jnp.float32)).reshape(B, H, GRID_W, kh * kw)
        s_ctx = jnp.einsum('bwhd,bchd->bhwc', qr, kc).astype(jnp.float32) * ATTN_SCALE
        p = jax.nn.softmax(jnp.concatenate([s_loc, s_ctx], axis=-1), axis=-1).astype(v.dtype)
        p_loc = p[..., :kh * kw].reshape(B, H, GRID_W, kh, kw)
        p_ctx = p[..., kh * kw:]
        return (jnp.einsum('bhwij,biwjhd->bwhd', p_loc, vwin)
                + jnp.einsum('bhwc,bchd->bwhd', p_ctx, vc))

    o = lax.map(row_block, jnp.arange(rows))
    return jnp.moveaxis(o, 0, 1).reshape(B, S, H * hd)


def hyena_filter(L, w1, b1, w2, b2, w3, b3, w_out, freq):
    f32 = jnp.float32
    t = jnp.linspace(0.0, 1.0, L, dtype=f32)[:, None]
    bands = (HY_EMB_DIM - 1) // 2
    w = 2.0 * math.pi * jnp.arange(L, dtype=f32)[:, None] / L
    fr = jnp.linspace(1e-4, bands - 1, bands, dtype=f32)[None, :]
    z = jnp.concatenate([t, jnp.cos(fr * w), -jnp.sin(fr * w)], axis=-1)
    om = freq.astype(f32)
    h = jnp.sin(om * (z @ w1.astype(f32) + b1.astype(f32)))
    h = jnp.sin(om * (h @ w2.astype(f32) + b2.astype(f32)))
    h = jnp.sin(om * (h @ w3.astype(f32) + b3.astype(f32)))
    h = h @ w_out.astype(f32)
    deltas = jnp.linspace(math.log(HY_DECAY_TARGET) / HY_SLOW_DECAY,
                          math.log(HY_DECAY_TARGET) / HY_FAST_DECAY, HY_WIDTH, dtype=f32)
    decay = jnp.exp(-t * jnp.abs(deltas))
    h_fwd = h[:, :HY_WIDTH] * decay
    h_bwd = h[:, HY_WIDTH:] * decay
    return jnp.concatenate([h_fwd, jnp.zeros((1, HY_WIDTH), f32), h_bwd[:0:-1]], axis=0)


def fft_long_conv(u, filt2):
    L = u.shape[1]
    uf = jnp.fft.rfft(u.astype(jnp.float32), n=2 * L, axis=1)
    ff = jnp.fft.rfft(filt2, n=2 * L, axis=0)
    y = jnp.fft.irfft(uf * ff[None], n=2 * L, axis=1)[:, :L]
    return y.astype(u.dtype)


def short_conv(u, w, b):
    L = u.shape[1]
    pad = HY_SHORT // 2
    up = jnp.pad(u, ((0, 0), (pad, HY_SHORT - 1 - pad), (0, 0)))
    return sum(up[:, i:i + L] * w[i] for i in range(HY_SHORT)) + b


def hyena_mixer(u, filt2, conv_w, conv_b, skip):
    u = short_conv(u, conv_w, conv_b)
    v, x1, x2 = jnp.split(u, 3, axis=-1)
    v = v * x1
    return x2 * (fft_long_conv(v, filt2) + v * skip)


def swiglu(h, w_in_ffn, w_out_ffn):
    g, u = jnp.split(h @ w_in_ffn, 2, axis=-1)
    return (jax.nn.silu(g) * u) @ w_out_ffn


def setup_inputs(seed: int = 0) -> dict:
    key = jax.random.key(seed)
    ks = iter(jax.random.split(key, 32))

    def nrm(shape, s):
        return jax.random.normal(next(ks), shape, jnp.float32) * s

    D = D_MODEL
    L = DEPTH
    F = HY_FILTER_WIDTH
    return {
        'x': nrm((BATCH, SEQ, D), 1.0),
        'c': nrm((BATCH, D), 1.0),
        'ctx': nrm((BATCH, CTX_LEN, D), 1.0),
        'c_ctx': nrm((D,), 1.0),
        'w_mod': nrm((L, D, 6 * D), 0.5 * D ** -0.5),
        'b_mod': nrm((L, 6 * D), 0.02),
        'g_mix': 1.0 + nrm((L, D), 0.05),
        'g_ffn': 1.0 + nrm((L, D), 0.05),
        'w_in': nrm((L, D, IN_COLS), D ** -0.5),
        'w_out': nrm((L, MIX_WIDTH, D), MIX_WIDTH ** -0.5),
        'hy_conv_w': nrm((L, HY_SHORT, HY_COLS), 0.5),
        'hy_conv_b': nrm((L, HY_COLS), 0.02),
        'hy_f_w1': nrm((L, HY_EMB_DIM, F), HY_EMB_DIM ** -0.5),
        'hy_f_b1': nrm((L, F), 0.1),
        'hy_f_w2': nrm((L, F, F), F ** -0.5),
        'hy_f_b2': nrm((L, F), 0.1),
        'hy_f_w3': nrm((L, F, F), F ** -0.5),
        'hy_f_b3': nrm((L, F), 0.1),
        'hy_f_wout': nrm((L, F, 2 * HY_WIDTH), 0.05 * F ** -0.5),
        'hy_f_freq': 1.0 + nrm((L, F), 0.05),
        'hy_skip': nrm((L, HY_WIDTH), 0.5),
        'qk_g_q': 1.0 + nrm((L, HEAD_DIM), 0.05),
        'qk_g_k': 1.0 + nrm((L, HEAD_DIM), 0.05),
        'na_rpb': nrm((L, NA_HEADS, 2 * NA_KH - 1, 2 * NA_KW - 1), 0.1),
        'w_ffn_in': nrm((L, D, 2 * FFN_HIDDEN), D ** -0.5),
        'w_ffn_out': nrm((L, FFN_HIDDEN, D), FFN_HIDDEN ** -0.5),
        'g_final': 1.0 + nrm((D,), 0.05),
    }


def reference(x, c, ctx, c_ctx, w_mod, b_mod, g_mix, g_ffn, w_in, w_out, hy_conv_w, hy_conv_b,
              hy_f_w1, hy_f_b1, hy_f_w2, hy_f_b2, hy_f_w3, hy_f_b3, hy_f_wout, hy_f_freq, hy_skip,
              qk_g_q, qk_g_k, na_rpb, w_ffn_in, w_ffn_out, g_final):
    S = x.shape[1]
    Lc = ctx.shape[1]
    cos, sin = axial_rope_tables(S)
    silu_c = jax.nn.silu(c)
    silu_cc = jax.nn.silu(c_ctx)
    xl, xc = x, ctx
    for l in range(DEPTH):
        last = l == DEPTH - 1
        mod_l = (silu_c @ w_mod[l] + b_mod[l])[:, None, :]
        mod_c = silu_cc @ w_mod[l] + b_mod[l]
        sh1, sc1, gt1, sh2, sc2, gt2 = jnp.split(mod_l, 6, axis=-1)
        csh1, csc1, cgt1, csh2, csc2, cgt2 = jnp.split(mod_c, 6, axis=-1)

        pl = modulate(rms_norm(xl, g_mix[l]), sh1, sc1) @ w_in[l]
        pc = modulate(rms_norm(xc, g_mix[l]), csh1, csc1) @ w_in[l]
        hy_l, q_l, k_l, v_l, nq_l, nk_l, nv_l = split_proj(pl)
        hy_c, q_c, k_c, v_c, nq_c, nk_c, nv_c = split_proj(pc)

        q_l = apply_rope(rms_norm(q_l, qk_g_q[l]), cos, sin)
        k_l = apply_rope(rms_norm(k_l, qk_g_k[l]), cos, sin)
        k_c = rms_norm(k_c, qk_g_k[l])
        gqa_l = gqa_latent(q_l, jnp.concatenate([k_c, k_l], axis=1),
                           jnp.concatenate([v_c, v_l], axis=1))
        na_l = natten_latent(nq_l, nk_l, nv_l, nk_c, nv_c, na_rpb[l])
        filt_l = hyena_filter(S, hy_f_w1[l], hy_f_b1[l], hy_f_w2[l], hy_f_b2[l], hy_f_w3[l],
                              hy_f_b3[l], hy_f_wout[l], hy_f_freq[l])
        hyo_l = hyena_mixer(hy_l, filt_l, hy_conv_w[l], hy_conv_b[l], hy_skip[l])

        mix_l = jnp.concatenate([hyo_l, gqa_l, na_l], axis=-1) @ w_out[l]
        xl = xl + gt1 * mix_l
        xl = xl + gt2 * swiglu(modulate(rms_norm(xl, g_ffn[l]), sh2, sc2), w_ffn_in[l], w_ffn_out[l])

        if not last:
            q_c = rms_norm(q_c, qk_g_q[l])
            gqa_c = attend(q_c, k_c, v_c)
            na_c = attend(nq_c, nk_c, nv_c)
            filt_c = hyena_filter(Lc, hy_f_w1[l], hy_f_b1[l], hy_f_w2[l], hy_f_b2[l], hy_f_w3[l],
                                  hy_f_b3[l], hy_f_wout[l], hy_f_freq[l])
            hyo_c = hyena_mixer(hy_c, filt_c, hy_conv_w[l], hy_conv_b[l], hy_skip[l])
            mix_c = jnp.concatenate([hyo_c, gqa_c, na_c], axis=-1) @ w_out[l]
            xc = xc + cgt1 * mix_c
            xc = xc + cgt2 * swiglu(modulate(rms_norm(xc, g_ffn[l]), csh2, csc2),
                                    w_ffn_in[l], w_ffn_out[l])
    return rms_norm(xl, g_final)
```

```python
import functools
import math

import numpy as np
import jax
import jax.numpy as jnp
from jax import lax
from jax.experimental import pallas as pl
from jax.experimental.pallas import tpu as pltpu

F32 = jnp.float32
BF16 = jnp.bfloat16
HIGHEST = lax.Precision.HIGHEST

GRID_W = 64
HEAD_DIM = 64
GQA_HEADS = 8
GQA_KV_HEADS = 2
GQA_GROUP = GQA_HEADS // GQA_KV_HEADS
NA_HEADS = 4
NA_KH = 8
NA_KW = 16
ROPE_THETA = 10000.0
HY_EMB_DIM = 33
HY_DECAY_TARGET = 1e-2
HY_FAST_DECAY = 0.3
HY_SLOW_DECAY = 1.5
NORM_EPS = 1e-6
ATTN_SCALE = HEAD_DIM ** -0.5
NEG = -1e30

LANES = 128
TOK_TILE = 256
FFT_CHUNK = LANES
HY_CT = 64
NA_ROWS = TOK_TILE // GRID_W
NA_WIN = NA_ROWS + NA_KH
VMEM_LIMIT = 56 * 1024 * 1024


def _cparams(sem):
    return pltpu.CompilerParams(dimension_semantics=sem, vmem_limit_bytes=VMEM_LIMIT)


def _dot(a, b, **kw):
    return jnp.dot(a, b, preferred_element_type=F32, **kw)


def _dot_nt(a, b, **kw):
    return lax.dot_general(a, b, (((1,), (1,)), ((), ())), preferred_element_type=F32, **kw)


def _mod_kernel(cs_ref, w_ref, b_ref, o_ref):
    cs = cs_ref[...]
    s = cs * jax.nn.sigmoid(cs)
    o_ref[...] = _dot(s, w_ref[...], precision=HIGHEST) + b_ref[...]


def _mod_call(cs, w_mod, b_mod):
    depth, d, n = w_mod.shape
    nb = cs.shape[0]
    tn = 1024
    return pl.pallas_call(
        _mod_kernel,
        grid=(depth, n // tn),
        in_specs=[pl.BlockSpec((nb, d), lambda l, j: (0, 0)),
                  pl.BlockSpec((None, d, tn), lambda l, j: (l, 0, j)),
                  pl.BlockSpec((None, 1, tn), lambda l, j: (l, 0, j))],
        out_specs=pl.BlockSpec((None, nb, tn), lambda l, j: (l, 0, j)),
        out_shape=jax.ShapeDtypeStruct((depth, nb, n), F32),
        compiler_params=_cparams(("parallel", "parallel")),
    )(cs, w_mod, b_mod.reshape(depth, 1, n))


def _mod_row(mod_ref, row, k, d):
    return mod_ref[pl.ds(row, 1), k * d:(k + 1) * d]


def _norm_mod(x, g, shift, scale):
    ms = jnp.mean(x * x, axis=-1, keepdims=True)
    y = x * lax.rsqrt(ms + NORM_EPS) * g
    return y * (1.0 + scale) + shift


def _inproj_kernel(n_lat_tiles, ctx_row, x_ref, mod_ref, g_ref, why_ref, wr_ref, gq_ref, gk_ref,
                   cos_ref, sin_ref, hy_ref, q_ref, k_ref, v_ref, nq_ref, nk_ref, nv_ref):
    b = pl.program_id(0)
    j = pl.program_id(1)
    d = x_ref.shape[-1]
    row = jnp.where(j < n_lat_tiles, b, ctx_row)
    h = _norm_mod(x_ref[...], g_ref[...], _mod_row(mod_ref, row, 0, d), _mod_row(mod_ref, row, 1, d))
    h = h.astype(BF16)
    hy_ref[...] = _dot_nt(why_ref[...], h)
    p = _dot(h, wr_ref[...])

    ri = lax.broadcasted_iota(jnp.int32, (LANES, LANES), 0) // HEAD_DIM
    ci = lax.broadcasted_iota(jnp.int32, (LANES, LANES), 1) // HEAD_DIM
    avg = jnp.where(ri == ci, 1.0 / HEAD_DIM, 0.0).astype(BF16)
    lane = lax.broadcasted_iota(jnp.int32, (1, LANES), 1)
    first_half = (lane % HEAD_DIM) < (HEAD_DIM // 2)
    cos = cos_ref[...]
    sin = sin_ref[...]

    def norm_rope(xc, g, scale):
        ms = _dot((xc * xc).astype(BF16), avg)
        xn = xc * lax.rsqrt(ms + NORM_EPS) * g
        partner = jnp.where(first_half, pltpu.roll(xn, LANES - HEAD_DIM // 2, 1),
                            pltpu.roll(xn, HEAD_DIM // 2, 1))
        return (xn * cos + partner * sin) * scale

    per = LANES // HEAD_DIM
    nq_chunks = GQA_HEADS // per
    for c in range(nq_chunks):
        y = norm_rope(p[:, c * LANES:(c + 1) * LANES], gq_ref[...], ATTN_SCALE).astype(BF16)
        for e in range(per):
            q_ref[c * per + e] = y[:, e * HEAD_DIM:(e + 1) * HEAD_DIM]
    off = nq_chunks * LANES
    y = norm_rope(p[:, off:off + LANES], gk_ref[...], 1.0).astype(BF16)
    for e in range(GQA_KV_HEADS):
        k_ref[e] = y[:, e * HEAD_DIM:(e + 1) * HEAD_DIM]
    off += LANES
    for e in range(GQA_KV_HEADS):
        v_ref[e] = p[:, off + e * HEAD_DIM:off + (e + 1) * HEAD_DIM].astype(BF16)
    off += LANES
    for ref, scale in ((nq_ref, ATTN_SCALE), (nk_ref, 1.0), (nv_ref, 1.0)):
        for e in range(NA_HEADS):
            ref[e] = (p[:, off + e * HEAD_DIM:off + (e + 1) * HEAD_DIM] * scale).astype(BF16)
        off += NA_HEADS * HEAD_DIM


def _inproj_call(xa, mod_l, g, w_hy_t, w_rest, gq2, gk2, cos_t, sin_t, s_len):
    bsz, t_all, d = xa.shape
    tm = TOK_TILE
    nb = mod_l.shape[0]
    hyc = w_hy_t.shape[0]
    nr = w_rest.shape[1]
    grid = (bsz, t_all // tm)
    const = lambda b, j: (0, 0)
    heads = lambda n: pl.BlockSpec((None, n, tm, HEAD_DIM), lambda b, j: (b, 0, j, 0))
    hshape = lambda n: jax.ShapeDtypeStruct((bsz, n, t_all, HEAD_DIM), BF16)
    return pl.pallas_call(
        functools.partial(_inproj_kernel, s_len // tm, bsz),
        grid=grid,
        in_specs=[pl.BlockSpec((None, tm, d), lambda b, j: (b, j, 0)),
                  pl.BlockSpec((nb, mod_l.shape[1]), const),
                  pl.BlockSpec((1, d), const),
                  pl.BlockSpec((hyc, d), const),
                  pl.BlockSpec((d, nr), const),
                  pl.BlockSpec((1, LANES), const),
                  pl.BlockSpec((1, LANES), const),
                  pl.BlockSpec((tm, LANES), lambda b, j: (j, 0)),
                  pl.BlockSpec((tm, LANES), lambda b, j: (j, 0))],
        out_specs=[pl.BlockSpec((None, hyc, tm), lambda b, j: (b, 0, j)),
                   heads(GQA_HEADS), heads(GQA_KV_HEADS), heads(GQA_KV_HEADS),
                   heads(NA_HEADS), heads(NA_HEADS), heads(NA_HEADS)],
        out_shape=[jax.ShapeDtypeStruct((bsz, hyc, t_all), F32),
                   hshape(GQA_HEADS), hshape(GQA_KV_HEADS), hshape(GQA_KV_HEADS),
                   hshape(NA_HEADS), hshape(NA_HEADS), hshape(NA_HEADS)],
        compiler_params=_cparams(("parallel", "parallel")),
    )(xa, mod_l, g, w_hy_t, w_rest, gq2, gk2, cos_t, sin_t)


def _gqa_kernel(s_len, t_all, tk, q_ref, k_ref, v_ref, o_ref, m_scr, l_scr, acc_scr):
    i = pl.program_id(2)
    tm = q_ref.shape[1]
    q = q_ref[...].reshape(GQA_GROUP * tm, HEAD_DIM)
    start = jnp.where(i * tm >= s_len, s_len // tk, 0)
    m_scr[...] = jnp.full(m_scr.shape, NEG, F32)
    l_scr[...] = jnp.zeros(l_scr.shape, F32)
    acc_scr[...] = jnp.zeros(acc_scr.shape, F32)

    def body(c, carry):
        ks = pl.multiple_of(c * tk, tk)
        kc = k_ref[pl.ds(ks, tk), :]
        vc = v_ref[pl.ds(ks, tk), :]
        s = _dot_nt(q, kc)
        m_prev = m_scr[...]
        m_new = jnp.maximum(m_prev, jnp.max(s, axis=1, keepdims=True))
        alpha = jnp.exp(m_prev - m_new)
        p = jnp.exp(s - m_new)
        l_scr[...] = alpha * l_scr[...] + jnp.sum(p, axis=1, keepdims=True)
        acc_scr[...] = alpha * acc_scr[...] + _dot(p.astype(BF16), vc)
        m_scr[...] = m_new
        return carry

    lax.fori_loop(start, t_all // tk, body, 0)
    o = acc_scr[...] / l_scr[...]
    for h in range(GQA_GROUP):
        o_ref[:, h * HEAD_DIM:(h + 1) * HEAD_DIM] = o[h * tm:(h + 1) * tm].astype(o_ref.dtype)


def _gqa_call(q, k, v, s_len):
    bsz, _, t_all, hd = q.shape
    tm = TOK_TILE
    tk = TOK_TILE
    q5 = q.reshape(bsz, GQA_KV_HEADS, GQA_GROUP, t_all, hd)
    return pl.pallas_call(
        functools.partial(_gqa_kernel, s_len, t_all, tk),
        grid=(bsz, GQA_KV_HEADS, t_all // tm),
        in_specs=[pl.BlockSpec((None, None, GQA_GROUP, tm, hd), lambda b, g, i: (b, g, 0, i, 0)),
                  pl.BlockSpec((None, None, t_all, hd), lambda b, g, i: (b, g, 0, 0)),
                  pl.BlockSpec((None, None, t_all, hd), lambda b, g, i: (b, g, 0, 0))],
        out_specs=pl.BlockSpec((None, tm, GQA_GROUP * hd), lambda b, g, i: (b, i, g)),
        out_shape=jax.ShapeDtypeStruct((bsz, t_all, GQA_HEADS * hd), BF16),
        scratch_shapes=[pltpu.VMEM((GQA_GROUP * tm, 1), F32),
                        pltpu.VMEM((GQA_GROUP * tm, 1), F32),
                        pltpu.VMEM((GQA_GROUP * tm, hd), F32)],
        compiler_params=_cparams(("parallel", "parallel", "parallel")),
    )(q5, k, v)


def _na_bias_index(rows):
    assert rows >= NA_WIN + NA_ROWS
    idx = np.zeros((3, TOK_TILE, NA_WIN * GRID_W), np.int32)
    mask = np.zeros((3, TOK_TILE, NA_WIN * GRID_W), bool)
    for kind, r0 in enumerate((0, NA_ROWS, rows - NA_ROWS)):
        kr0 = int(np.clip(r0 - NA_KH // 2, 0, rows - NA_WIN))
        r = r0 + np.arange(TOK_TILE) // GRID_W
        w = np.arange(TOK_TILE) % GRID_W
        kr = kr0 + np.arange(NA_WIN * GRID_W) // GRID_W
        kc = np.arange(NA_WIN * GRID_W) % GRID_W
        rs = np.clip(r - NA_KH // 2, 0, rows - NA_KH)
        cs = np.clip(w - NA_KW // 2, 0, GRID_W - NA_KW)
        drow = kr[None, :] - r[:, None] + NA_KH - 1
        dcol = kc[None, :] - w[:, None] + NA_KW - 1
        ok = ((kr[None, :] >= rs[:, None]) & (kr[None, :] < rs[:, None] + NA_KH)
              & (kc[None, :] >= cs[:, None]) & (kc[None, :] < cs[:, None] + NA_KW))
        idx[kind] = np.where(ok, drow * (2 * NA_KW - 1) + dcol, 0)
        mask[kind] = ok
    return idx, mask


def _na_bias_tables(na_rpb, rows):
    idx, mask = _na_bias_index(rows)
    depth, heads = na_rpb.shape[:2]
    flat = na_rpb.reshape(depth, heads, -1).astype(F32)
    g = flat[:, :, idx]
    g = jnp.where(mask[None, None], g, NEG)
    return jnp.transpose(g, (0, 2, 1, 3, 4))


def _softmax_pv(parts):
    m = None
    for s, _ in parts:
        mm = jnp.max(s, axis=1, keepdims=True)
        m = mm if m is None else jnp.maximum(m, mm)
    l = 0.0
    o = 0.0
    for s, v in parts:
        p = jnp.exp(s - m)
        l = l + jnp.sum(p, axis=1, keepdims=True)
        o = o + _dot(p.astype(BF16), v)
    return o / l


def _na_kernel(s_len, lc, rows, q_ref, k_ref, v_ref, tab_ref, o_ref):
    i = pl.program_id(1)
    nblk = s_len // TOK_TILE
    nloc = NA_WIN * GRID_W

    @pl.when(i < nblk)
    def _():
        kind = jnp.where(i == 0, 0, jnp.where(i == nblk - 1, 2, 1))
        kr0 = jnp.clip(i * NA_ROWS - NA_KH // 2, 0, rows - NA_WIN)
        ks = pl.multiple_of(kr0 * GRID_W, GRID_W)
        for h in range(NA_HEADS):
            q = q_ref[h]
            s_loc = _dot_nt(q, k_ref[h, pl.ds(ks, nloc), :]) + tab_ref[kind, h]
            s_ctx = _dot_nt(q, k_ref[h, pl.ds(s_len, lc), :])
            o = _softmax_pv([(s_loc, v_ref[h, pl.ds(ks, nloc), :]),
                             (s_ctx, v_ref[h, pl.ds(s_len, lc), :])])
            o_ref[:, h * HEAD_DIM:(h + 1) * HEAD_DIM] = o.astype(o_ref.dtype)

    @pl.when(i >= nblk)
    def _():
        for h in range(NA_HEADS):
            q = q_ref[h]
            s_ctx = _dot_nt(q, k_ref[h, pl.ds(s_len, lc), :])
            o = _softmax_pv([(s_ctx, v_ref[h, pl.ds(s_len, lc), :])])
            o_ref[:, h * HEAD_DIM:(h + 1) * HEAD_DIM] = o.astype(o_ref.dtype)


def _na_call(nq, nk, nv, table, s_len):
    bsz, heads, t_all, hd = nq.shape
    tm = TOK_TILE
    rows = s_len // GRID_W
    whole = pl.BlockSpec((None, heads, t_all, hd), lambda b, i: (b, 0, 0, 0))
    return pl.pallas_call(
        functools.partial(_na_kernel, s_len, t_all - s_len, rows),
        grid=(bsz, t_all // tm),
        in_specs=[pl.BlockSpec((None, heads, tm, hd), lambda b, i: (b, 0, i, 0)),
                  whole, whole,
                  pl.BlockSpec(table.shape, lambda b, i: (0, 0, 0, 0))],
        out_specs=pl.BlockSpec((None, tm, heads * hd), lambda b, i: (b, i, 0)),
        out_shape=jax.ShapeDtypeStruct((bsz, t_all, heads * hd), BF16),
        compiler_params=_cparams(("parallel", "parallel")),
    )(nq, nk, nv, table)


def _hy_filter_kernel(z_ref, t_ref, side_ref, w1_ref, b1_ref, w2_ref, b2_ref, w3_ref, b3_ref,
                      wo_ref, fr_ref, dl_ref, o_ref):
    om = fr_ref[...]
    h = jnp.sin(om * (_dot(z_ref[...], w1_ref[...], precision=HIGHEST) + b1_ref[...]))
    h = jnp.sin(om * (_dot(h, w2_ref[...], precision=HIGHEST) + b2_ref[...]))
    h = jnp.sin(om * (_dot(h, w3_ref[...], precision=HIGHEST) + b3_ref[...]))
    ht = _dot_nt(wo_ref[...], h, precision=HIGHEST)
    c = o_ref.shape[0]
    decay = jnp.exp(-t_ref[...] * jnp.abs(dl_ref[...]))
    side = side_ref[...]
    o_ref[...] = decay * (jnp.where(side > 0.5, ht[:c], 0.0) + jnp.where(side < -0.5, ht[c:], 0.0))


def _hy_filter_consts(length):
    n = np.arange(2 * length)
    pos = np.where(n < length, n, 2 * length - n)
    pos = np.where(n == length, 0, pos)
    side = np.where(n < length, 1.0, -1.0)
    side = np.where(n == length, 0.0, side)
    t = np.linspace(0.0, 1.0, length)[pos]
    bands = (HY_EMB_DIM - 1) // 2
    w = 2.0 * math.pi * pos / length
    fr = np.linspace(1e-4, bands - 1, bands)
    z = np.concatenate([t[:, None], np.cos(fr[None, :] * w[:, None]), -np.sin(fr[None, :] * w[:, None])], axis=-1)
    zp = np.zeros((2 * length, 64), np.float32)
    zp[:, :HY_EMB_DIM] = z
    return zp, t.astype(np.float32)[None, :], side.astype(np.float32)[None, :]


def _hy_filter_call(length, w1, b1, w2, b2, w3, b3, wout, freq):
    f = w2.shape[0]
    c = wout.shape[1] // 2
    zp, t, side = _hy_filter_consts(length)
    w1p = jnp.zeros((64, f), F32).at[:HY_EMB_DIM].set(w1)
    deltas = np.linspace(math.log(HY_DECAY_TARGET) / HY_SLOW_DECAY,
                         math.log(HY_DECAY_TARGET) / HY_FAST_DECAY, c).astype(np.float32)[:, None]
    args = (jnp.asarray(zp), jnp.asarray(t), jnp.asarray(side), w1p, b1[None], w2, b2[None], w3, b3[None],
            wout.T, freq[None], jnp.asarray(deltas))
    return pl.pallas_call(
        _hy_filter_kernel,
        in_specs=[pl.BlockSpec(a.shape, lambda: (0,) * a.ndim) for a in args],
        out_specs=pl.BlockSpec((c, 2 * length), lambda: (0, 0)),
        out_shape=jax.ShapeDtypeStruct((c, 2 * length), F32),
        grid=(),
        compiler_params=pltpu.CompilerParams(vmem_limit_bytes=VMEM_LIMIT),
    )(*args)


def _bitrev(n):
    bits = int(math.log2(n))
    return np.array([int(format(i, "0%db" % bits)[::-1], 2) if bits else 0 for i in range(n)])


def _fft_consts(n1, ct):
    n = n1 * FFT_CHUNK
    stages = max(int(math.log2(n1)), 1)
    tw = np.zeros((2, stages, max(n1 // 2, 1)), np.float32)
    for s in range(int(math.log2(n1))):
        half = n1 >> (s + 1)
        ang = -2.0 * math.pi * np.arange(half) / (2 * half)
        tw[0, s, :half] = np.cos(ang)
        tw[1, s, :half] = np.sin(ang)
    k1 = _bitrev(n1)
    ang = -2.0 * math.pi * (np.arange(FFT_CHUNK)[None, :] * k1[:, None]) / n
    twr = np.repeat(np.cos(ang), ct, axis=0).astype(np.float32)
    twi = np.repeat(np.sin(ang), ct, axis=0).astype(np.float32)
    a = -2.0 * math.pi * np.outer(np.arange(FFT_CHUNK), np.arange(FFT_CHUNK)) / FFT_CHUNK
    wr, wi = np.cos(a), np.sin(a)
    wf = np.block([[wr, wi], [-wi, wr]]).astype(np.float32)
    wb = np.block([[wr, -wi], [wi, wr]]).astype(np.float32)
    return tw, twr, twi, wf, wb


def _slab_fft(zr, zi, tw_ref, n1, ct, inverse, half_input=False, half_output=False):
    nst = int(math.log2(n1))
    order = range(nst - 1, -1, -1) if inverse else range(nst)
    for s in order:
        half = n1 >> (s + 1)
        lh = int(math.log2(half))
        first = s == 0

        def body(idx, carry, s=s, half=half, lh=lh, first=first):
            blk = idx >> lh
            j = idx & (half - 1)
            r0 = pl.multiple_of(((blk << (lh + 1)) + j) * ct, ct)
            r1 = pl.multiple_of(r0 + half * ct, ct)
            wr = tw_ref[0, s, j]
            wi = tw_ref[1, s, j]
            ar = zr[pl.ds(r0, ct), :]
            ai = zi[pl.ds(r0, ct), :]
            if not inverse:
                if first and half_input:
                    zr[pl.ds(r1, ct), :] = ar * wr - ai * wi
                    zi[pl.ds(r1, ct), :] = ar * wi + ai * wr
                else:
                    br = zr[pl.ds(r1, ct), :]
                    bi = zi[pl.ds(r1, ct), :]
                    dr = ar - br
                    di = ai - bi
                    zr[pl.ds(r0, ct), :] = ar + br
                    zi[pl.ds(r0, ct), :] = ai + bi
                    zr[pl.ds(r1, ct), :] = dr * wr - di * wi
                    zi[pl.ds(r1, ct), :] = dr * wi + di * wr
            else:
                br = zr[pl.ds(r1, ct), :]
                bi = zi[pl.ds(r1, ct), :]
                tr = br * wr + bi * wi
                ti = bi * wr - br * wi
                zr[pl.ds(r0, ct), :] = ar + tr
                zi[pl.ds(r0, ct), :] = ai + ti
                if not (first and half_output):
                    zr[pl.ds(r1, ct), :] = ar - tr
                    zi[pl.ds(r1, ct), :] = ai - ti
            return carry

        lax.fori_loop(0, n1 // 2, body, 0)


def _chunk_dft(xr, xi, w_ref):
    y = _dot(jnp.concatenate([xr, xi], axis=1), w_ref[...], precision=HIGHEST)
    return y[:, :FFT_CHUNK], y[:, FFT_CHUNK:]


def _fft_groups(n1, ct):
    g = min(8, n1)
    return n1 // g, g * ct


def _hy_spec_kernel(n1, ct, f_ref, tw_ref, twr_ref, twi_ref, wf_ref, fr_ref, fi_ref, zr, zi):
    for s in range(n1):
        zr[s * ct:(s + 1) * ct, :] = f_ref[:, s * FFT_CHUNK:(s + 1) * FFT_CHUNK]
    zi[...] = jnp.zeros(zi.shape, F32)
    _slab_fft(zr, zi, tw_ref, n1, ct, inverse=False)
    ngroups, gr = _fft_groups(n1, ct)

    def body(g, carry):
        r0 = pl.multiple_of(g * gr, gr)
        a = zr[pl.ds(r0, gr), :]
        b = zi[pl.ds(r0, gr), :]
        tr = twr_ref[pl.ds(r0, gr), :]
        ti = twi_ref[pl.ds(r0, gr), :]
        yr, yi = _chunk_dft(a * tr - b * ti, a * ti + b * tr, wf_ref)
        fr_ref[pl.ds(r0, gr), :] = yr
        fi_ref[pl.ds(r0, gr), :] = yi
        return carry

    lax.fori_loop(0, ngroups, body, 0)


def _hy_spec_call(filt_t):
    c, n = filt_t.shape
    n1 = n // FFT_CHUNK
    ct = HY_CT
    tw, twr, twi, wf, _ = _fft_consts(n1, ct)
    const2 = lambda i: (0, 0)
    out = jax.ShapeDtypeStruct((c // ct, n1 * ct, FFT_CHUNK), F32)
    ospec = pl.BlockSpec((None, n1 * ct, FFT_CHUNK), lambda i: (i, 0, 0))
    return pl.pallas_call(
        functools.partial(_hy_spec_kernel, n1, ct),
        grid=(c // ct,),
        in_specs=[pl.BlockSpec((ct, n), lambda i: (i, 0)),
                  pl.BlockSpec(memory_space=pltpu.SMEM),
                  pl.BlockSpec(twr.shape, const2), pl.BlockSpec(twi.shape, const2),
                  pl.BlockSpec(wf.shape, const2)],
        out_specs=[ospec, ospec],
        out_shape=[out, out],
        scratch_shapes=[pltpu.VMEM((n1 * ct, FFT_CHUNK), F32), pltpu.VMEM((n1 * ct, FFT_CHUNK), F32)],
        compiler_params=_cparams(("parallel",)),
    )(filt_t, jnp.asarray(tw), jnp.asarray(twr), jnp.asarray(twi), jnp.asarray(wf))


def _hy_conv_kernel(n1, ct, length, v_ref, x1_ref, x2_ref, cwv_ref, cw1_ref, cw2_ref, skip_ref,
                    fr_ref, fi_ref, tw_ref, twr_ref, twi_ref, wf_ref, wb_ref, o_ref,
                    zr, zi, u_scr, g_scr):
    lane = lax.broadcasted_iota(jnp.int32, (1, length), 1)

    def short_conv(x, cw_ref):
        xm = jnp.where(lane >= 1, pltpu.roll(x, 1, 1), 0.0)
        xp = jnp.where(lane <= length - 2, pltpu.roll(x, length - 1, 1), 0.0)
        return xm * cw_ref[:, 0:1] + x * cw_ref[:, 1:2] + xp * cw_ref[:, 2:3] + cw_ref[:, 3:4]

    for a in range(2):
        u_scr[a] = short_conv(v_ref[a], cwv_ref) * short_conv(x1_ref[a], cw1_ref)
        g_scr[a] = short_conv(x2_ref[a], cw2_ref)
    nz = length // FFT_CHUNK
    for s in range(nz):
        zr[s * ct:(s + 1) * ct, :] = u_scr[0, :, s * FFT_CHUNK:(s + 1) * FFT_CHUNK]
        zi[s * ct:(s + 1) * ct, :] = u_scr[1, :, s * FFT_CHUNK:(s + 1) * FFT_CHUNK]
    _slab_fft(zr, zi, tw_ref, n1, ct, inverse=False, half_input=True)
    ngroups, gr = _fft_groups(n1, ct)

    def body(g, carry):
        r0 = pl.multiple_of(g * gr, gr)
        a = zr[pl.ds(r0, gr), :]
        b = zi[pl.ds(r0, gr), :]
        tr = twr_ref[pl.ds(r0, gr), :]
        ti = twi_ref[pl.ds(r0, gr), :]
        yr, yi = _chunk_dft(a * tr - b * ti, a * ti + b * tr, wf_ref)
        fr = fr_ref[pl.ds(r0, gr), :]
        fi = fi_ref[pl.ds(r0, gr), :]
        qr, qi = _chunk_dft(yr * fr - yi * fi, yr * fi + yi * fr, wb_ref)
        zr[pl.ds(r0, gr), :] = qr * tr + qi * ti
        zi[pl.ds(r0, gr), :] = qi * tr - qr * ti
        return carry

    lax.fori_loop(0, ngroups, body, 0)
    _slab_fft(zr, zi, tw_ref, n1, ct, inverse=True, half_output=True)
    inv_n = 1.0 / (n1 * FFT_CHUNK)
    skip = skip_ref[...]
    for s in range(nz):
        sl = slice(s * FFT_CHUNK, (s + 1) * FFT_CHUNK)
        for a, z in ((0, zr), (1, zi)):
            y = z[s * ct:(s + 1) * ct, :] * inv_n
            u = u_scr[a, :, sl]
            o_ref[a, :, sl] = (g_scr[a, :, sl] * (y + u * skip)).astype(o_ref.dtype)


def _hy_conv_call(hy_t, spec_r, spec_i, cw, skip, length, lane_block, prev=None):
    _, bh, c3, t_all = hy_t.shape
    c = c3 // 3
    ct = HY_CT
    nct = c // ct
    n1 = 2 * length // FFT_CHUNK
    tw, twr, twi, wf, wb = _fft_consts(n1, ct)
    const2 = lambda p, i: (0, 0)
    xin = lambda k: pl.BlockSpec((2, None, ct, length), lambda p, i, k=k: (0, p, k * nct + i, lane_block))
    cwin = lambda k: pl.BlockSpec((ct, 4), lambda p, i, k=k: (k * nct + i, 0))
    fin = pl.BlockSpec((None, n1 * ct, FFT_CHUNK), lambda p, i: (i, 0, 0))
    in_specs = [xin(0), xin(1), xin(2), cwin(0), cwin(1), cwin(2),
                pl.BlockSpec((ct, 1), lambda p, i: (i, 0)),
                fin, fin,
                pl.BlockSpec(memory_space=pltpu.SMEM),
                pl.BlockSpec(twr.shape, const2), pl.BlockSpec(twi.shape, const2),
                pl.BlockSpec(wf.shape, const2), pl.BlockSpec(wb.shape, const2)]
    args = [hy_t, hy_t, hy_t, cw, cw, cw, skip, spec_r, spec_i, jnp.asarray(tw), jnp.asarray(twr),
            jnp.asarray(twi), jnp.asarray(wf), jnp.asarray(wb)]
    kern = functools.partial(_hy_conv_kernel, n1, ct, length)
    aliases = {}
    if prev is not None:
        in_specs.append(pl.BlockSpec(memory_space=pl.ANY))
        args.append(prev)
        aliases = {len(args) - 1: 0}
        kern = lambda *refs, kern=kern: kern(*refs[:14], *refs[15:])
    return pl.pallas_call(
        kern,
        grid=(bh, nct),
        in_specs=in_specs,
        out_specs=pl.BlockSpec((2, None, ct, length), lambda p, i: (0, p, i, lane_block)),
        out_shape=jax.ShapeDtypeStruct((2, bh, c, t_all), BF16),
        scratch_shapes=[pltpu.VMEM((n1 * ct, FFT_CHUNK), F32), pltpu.VMEM((n1 * ct, FFT_CHUNK), F32),
                        pltpu.VMEM((2, ct, length), F32), pltpu.VMEM((2, ct, length), F32)],
        input_output_aliases=aliases,
        compiler_params=_cparams(("parallel", "parallel")),
    )(*args)


def _outproj_kernel(n_lat_tiles, ctx_row, x_ref, hy_ref, ga_ref, na_ref, w_ref, mod_ref, o_ref):
    b = pl.program_id(0)
    j = pl.program_id(1)
    d = x_ref.shape[-1]
    row = jnp.where(j < n_lat_tiles, b, ctx_row)
    hy = hy_ref[...].astype(F32).T.astype(BF16)
    mix = jnp.concatenate([hy, ga_ref[...], na_ref[...]], axis=1)
    o_ref[...] = x_ref[...] + _mod_row(mod_ref, row, 2, d) * _dot(mix, w_ref[...])


def _outproj_call(xa, hyo_t, gqa, na, w_out, mod_l, s_len):
    bsz, t_all, d = xa.shape
    tm = TOK_TILE
    c = hyo_t.shape[1]
    const = lambda b, j: (0, 0)
    tok = lambda n: pl.BlockSpec((None, tm, n), lambda b, j: (b, j, 0))
    return pl.pallas_call(
        functools.partial(_outproj_kernel, s_len // tm, bsz),
        grid=(bsz, t_all // tm),
        in_specs=[tok(d), pl.BlockSpec((None, c, tm), lambda b, j: (b, 0, j)),
                  tok(gqa.shape[-1]), tok(na.shape[-1]),
                  pl.BlockSpec(w_out.shape, const), pl.BlockSpec(mod_l.shape, const)],
        out_specs=tok(d),
        out_shape=jax.ShapeDtypeStruct(xa.shape, F32),
        compiler_params=_cparams(("parallel", "parallel")),
    )(xa, hyo_t, gqa, na, w_out, mod_l)


def _ffn_kernel(tiles_per_batch, n_lat_tiles, ctx_row, x_ref, mod_ref, g_ref, wg_ref, wu_ref, wo_ref,
                o_ref, h_scr, acc_scr):
    i = pl.program_id(0)
    j = pl.program_id(1)
    d = x_ref.shape[-1]
    groups = x_ref.shape[0] // TOK_TILE

    def mod_row(s):
        gi = i * groups + s
        jj = gi % tiles_per_batch
        return jnp.where(jj < n_lat_tiles, gi // tiles_per_batch, ctx_row)

    @pl.when(j == 0)
    def _():
        for s in range(groups):
            row = mod_row(s)
            sl = slice(s * TOK_TILE, (s + 1) * TOK_TILE)
            h = _norm_mod(x_ref[sl, :], g_ref[...], _mod_row(mod_ref, row, 3, d), _mod_row(mod_ref, row, 4, d))
            h_scr[sl, :] = h.astype(BF16)
        acc_scr[...] = jnp.zeros(acc_scr.shape, F32)

    h = h_scr[...]
    gate = _dot(h, wg_ref[...])
    up = _dot(h, wu_ref[...])
    act = (gate * jax.nn.sigmoid(gate) * up).astype(BF16)
    acc_scr[...] += _dot(act, wo_ref[...])

    @pl.when(j == pl.num_programs(1) - 1)
    def _():
        for s in range(groups):
            row = mod_row(s)
            sl = slice(s * TOK_TILE, (s + 1) * TOK_TILE)
            o_ref[sl, :] = x_ref[sl, :] + _mod_row(mod_ref, row, 5, d) * acc_scr[sl, :]


def _ffn_call(xa, mod_l, g, w_in, w_out, s_len):
    bsz, t_all, d = xa.shape
    f = w_out.shape[0]
    r = bsz * t_all
    tm = 1024 if r % 1024 == 0 else TOK_TILE
    th = 256
    nh = f // th
    const = lambda i, j: (0, 0)
    out = pl.pallas_call(
        functools.partial(_ffn_kernel, t_all // TOK_TILE, s_len // TOK_TILE, bsz),
        grid=(r // tm, nh),
        in_specs=[pl.BlockSpec((tm, d), lambda i, j: (i, 0)),
                  pl.BlockSpec(mod_l.shape, const),
                  pl.BlockSpec((1, d), const),
                  pl.BlockSpec((d, th), lambda i, j: (0, j)),
                  pl.BlockSpec((d, th), lambda i, j: (0, nh + j)),
                  pl.BlockSpec((th, d), lambda i, j: (j, 0))],
        out_specs=pl.BlockSpec((tm, d), lambda i, j: (i, 0)),
        out_shape=jax.ShapeDtypeStruct((r, d), F32),
        scratch_shapes=[pltpu.VMEM((tm, d), BF16), pltpu.VMEM((tm, d), F32)],
        compiler_params=_cparams(("parallel", "arbitrary")),
    )(xa.reshape(r, d), mod_l, g, w_in, w_in, w_out)
    return out.reshape(bsz, t_all, d)


def _final_kernel(x_ref, g_ref, o_ref):
    x = x_ref[...]
    ms = jnp.mean(x * x, axis=-1, keepdims=True)
    o_ref[...] = x * lax.rsqrt(ms + NORM_EPS) * g_ref[...]


def _final_call(xa, g, s_len):
    bsz, _, d = xa.shape
    tm = TOK_TILE
    return pl.pallas_call(
        _final_kernel,
        grid=(bsz, s_len // tm),
        in_specs=[pl.BlockSpec((None, tm, d), lambda b, j: (b, j, 0)),
                  pl.BlockSpec((1, d), lambda b, j: (0, 0))],
        out_specs=pl.BlockSpec((None, tm, d), lambda b, j: (b, j, 0)),
        out_shape=jax.ShapeDtypeStruct((bsz, s_len, d), F32),
        compiler_params=_cparams(("parallel", "parallel")),
    )(xa, g)


def _rope_tables(s_len, lc):
    pos = np.arange(s_len)
    row = (pos // GRID_W).astype(np.float64)
    col = (pos % GRID_W).astype(np.float64)
    n_f = HEAD_DIM // 4
    inv = (ROPE_THETA ** (-np.arange(n_f, dtype=np.float32) / n_f)).astype(np.float64)
    ang = np.concatenate([row[:, None] * inv, col[:, None] * inv], axis=-1).astype(np.float32).astype(np.float64)
    lane = np.arange(LANES)
    sign = np.where((lane % HEAD_DIM) < HEAD_DIM // 2, -1.0, 1.0)
    cos = np.cos(ang)[:, lane % (HEAD_DIM // 2)]
    sin = np.sin(ang)[:, lane % (HEAD_DIM // 2)] * sign[None, :]
    cos = np.concatenate([cos, np.ones((lc, LANES))], axis=0).astype(np.float32)
    sin = np.concatenate([sin, np.zeros((lc, LANES))], axis=0).astype(np.float32)
    return jnp.asarray(cos), jnp.asarray(sin)


def kernel(x, c, ctx, c_ctx, w_mod, b_mod, g_mix, g_ffn, w_in, w_out, hy_conv_w, hy_conv_b,
           hy_f_w1, hy_f_b1, hy_f_w2, hy_f_b2, hy_f_w3, hy_f_b3, hy_f_wout, hy_f_freq, hy_skip,
           qk_g_q, qk_g_k, na_rpb, w_ffn_in, w_ffn_out, g_final):
    bsz, s_len, d = x.shape
    lc = ctx.shape[1]
    depth = w_mod.shape[0]
    t_all = s_len + lc
    hyw = hy_skip.shape[1]
    hy_cols = 3 * hyw
    assert bsz % 2 == 0 and s_len % TOK_TILE == 0 and lc == TOK_TILE and s_len % lc == 0

    nb = -(-(bsz + 1) // 8) * 8
    cs = jnp.zeros((nb, d), F32).at[:bsz].set(c).at[bsz].set(c_ctx)
    mod = _mod_call(cs, w_mod, b_mod)

    cos_t, sin_t = _rope_tables(s_len, lc)
    tables = _na_bias_tables(na_rpb, s_len // GRID_W)
    w_hy_t = jnp.swapaxes(w_in[:, :, :hy_cols], 1, 2).astype(BF16)
    w_rest = w_in[:, :, hy_cols:].astype(BF16)
    w_out_b = w_out.astype(BF16)
    w_fi = w_ffn_in.astype(BF16)
    w_fo = w_ffn_out.astype(BF16)
    cw = jnp.concatenate([jnp.swapaxes(hy_conv_w, 1, 2), hy_conv_b[:, :, None]], axis=2)
    per = LANES // HEAD_DIM

    xa = jnp.concatenate([x, ctx], axis=1)
    for l in range(depth):
        last = l == depth - 1
        gq2 = jnp.tile(qk_g_q[l], per)[None]
        gk2 = jnp.tile(qk_g_k[l], per)[None]
        hy_t, q, k, v, nq, nk, nv = _inproj_call(xa, mod[l], g_mix[l][None], w_hy_t[l], w_rest[l],
                                                  gq2, gk2, cos_t, sin_t, s_len)
        gqa = _gqa_call(q, k, v, s_len)
        na = _na_call(nq, nk, nv, tables[l], s_len)

        fargs = (hy_f_w1[l], hy_f_b1[l], hy_f_w2[l], hy_f_b2[l], hy_f_w3[l], hy_f_b3[l],
                 hy_f_wout[l], hy_f_freq[l])
        hy_p = hy_t.reshape(2, bsz // 2, hy_cols, t_all)
        skip = hy_skip[l][:, None]
        fr, fi = _hy_spec_call(_hy_filter_call(s_len, *fargs))
        hyo = _hy_conv_call(hy_p, fr, fi, cw[l], skip, s_len, 0)
        if not last:
            frc, fic = _hy_spec_call(_hy_filter_call(lc, *fargs))
            hyo = _hy_conv_call(hy_p, frc, fic, cw[l], skip, lc, s_len // lc, prev=hyo)
        hyo = hyo.reshape(bsz, hyw, t_all)

        xa = _outproj_call(xa, hyo, gqa, na, w_out_b[l], mod[l], s_len)
        xa = _ffn_call(xa, mod[l], g_ffn[l][None], w_fi[l], w_fo[l], s_len)
    return _final_call(xa, g_final[None], s_len)
```

```python
import functools
import math

import numpy as np
import jax
import jax.numpy as jnp
from jax import lax
from jax.experimental import pallas as pl
from jax.experimental.pallas import tpu as pltpu

F32 = jnp.float32
BF16 = jnp.bfloat16
HIGHEST = lax.Precision.HIGHEST

GRID_W = 64
HEAD_DIM = 64
GQA_HEADS = 8
GQA_KV_HEADS = 2
GQA_GROUP = GQA_HEADS // GQA_KV_HEADS
NA_HEADS = 4
NA_KH = 8
NA_KW = 16
ROPE_THETA = 10000.0
HY_EMB_DIM = 33
HY_DECAY_TARGET = 1e-2
HY_FAST_DECAY = 0.3
HY_SLOW_DECAY = 1.5
NORM_EPS = 1e-6
ATTN_SCALE = HEAD_DIM ** -0.5
NEG = -1e30
LOG2E = math.log2(math.e)

LANES = 128
TOK_TILE = 256
FFT_CHUNK = LANES
HY_CT = 64
NA_ROWS = TOK_TILE // GRID_W
NA_WIN = NA_ROWS + NA_KH
VMEM_LIMIT = 56 * 1024 * 1024


def _cparams(sem):
    return pltpu.CompilerParams(dimension_semantics=sem, vmem_limit_bytes=VMEM_LIMIT)


def _dot(a, b, **kw):
    return jnp.dot(a, b, preferred_element_type=F32, **kw)


def _dot_nt(a, b, **kw):
    return lax.dot_general(a, b, (((1,), (1,)), ((), ())), preferred_element_type=F32, **kw)


def _mod_kernel(cs_ref, w_ref, b_ref, o_ref):
    cs = cs_ref[...]
    s = cs * jax.nn.sigmoid(cs)
    o_ref[...] = _dot(s, w_ref[...], precision=HIGHEST) + b_ref[...]


def _mod_call(cs, w_mod, b_mod):
    depth, d, n = w_mod.shape
    nb = cs.shape[0]
    tn = 1024
    return pl.pallas_call(
        _mod_kernel,
        grid=(depth, n // tn),
        in_specs=[pl.BlockSpec((nb, d), lambda l, j: (0, 0)),
                  pl.BlockSpec((None, d, tn), lambda l, j: (l, 0, j)),
                  pl.BlockSpec((None, 1, tn), lambda l, j: (l, 0, j))],
        out_specs=pl.BlockSpec((None, nb, tn), lambda l, j: (l, 0, j)),
        out_shape=jax.ShapeDtypeStruct((depth, nb, n), F32),
        name="adaln_mod",
        compiler_params=_cparams(("parallel", "parallel")),
    )(cs, w_mod, b_mod.reshape(depth, 1, n))


def _mod_row(mod_ref, row, k, d):
    return mod_ref[pl.ds(row, 1), k * d:(k + 1) * d]


def _norm_mod(x, g, shift, scale):
    ms = jnp.mean(x * x, axis=-1, keepdims=True)
    y = x * lax.rsqrt(ms + NORM_EPS) * g
    return y * (1.0 + scale) + shift


def _inproj_kernel(n_lat_tiles, ctx_row, x_ref, mod_ref, g_ref, why_ref, wr_ref, gq_ref, gk_ref,
                   cos_ref, sin_ref, hy_ref, q_ref, k_ref, v_ref, nq_ref, nk_ref, nv_ref):
    b = pl.program_id(0)
    j = pl.program_id(1)
    d = x_ref.shape[-1]
    row = jnp.where(j < n_lat_tiles, b, ctx_row)
    h = _norm_mod(x_ref[...], g_ref[...], _mod_row(mod_ref, row, 0, d), _mod_row(mod_ref, row, 1, d))
    h = h.astype(BF16)
    hy_ref[...] = _dot_nt(why_ref[...], h)
    p = _dot(h, wr_ref[...])

    ri = lax.broadcasted_iota(jnp.int32, (LANES, LANES), 0) // HEAD_DIM
    ci = lax.broadcasted_iota(jnp.int32, (LANES, LANES), 1) // HEAD_DIM
    avg = jnp.where(ri == ci, 1.0 / HEAD_DIM, 0.0).astype(BF16)
    lane = lax.broadcasted_iota(jnp.int32, (1, LANES), 1)
    first_half = (lane % HEAD_DIM) < (HEAD_DIM // 2)
    cos = cos_ref[...]
    sin = sin_ref[...]

    def norm_rope(xc, g, scale):
        ms = _dot((xc * xc).astype(BF16), avg)
        xn = xc * lax.rsqrt(ms + NORM_EPS) * g
        partner = jnp.where(first_half, pltpu.roll(xn, LANES - HEAD_DIM // 2, 1),
                            pltpu.roll(xn, HEAD_DIM // 2, 1))
        return (xn * cos + partner * sin) * scale

    per = LANES // HEAD_DIM
    nq_chunks = GQA_HEADS // per
    for c in range(nq_chunks):
        y = norm_rope(p[:, c * LANES:(c + 1) * LANES], gq_ref[...], ATTN_SCALE * LOG2E).astype(BF16)
        for e in range(per):
            q_ref[c * per + e] = y[:, e * HEAD_DIM:(e + 1) * HEAD_DIM]
    off = nq_chunks * LANES
    y = norm_rope(p[:, off:off + LANES], gk_ref[...], 1.0).astype(BF16)
    for e in range(GQA_KV_HEADS):
        k_ref[e] = y[:, e * HEAD_DIM:(e + 1) * HEAD_DIM]
    off += LANES
    for e in range(GQA_KV_HEADS):
        v_ref[e] = p[:, off + e * HEAD_DIM:off + (e + 1) * HEAD_DIM].astype(BF16)
    off += LANES
    for ref, scale in ((nq_ref, ATTN_SCALE), (nk_ref, 1.0), (nv_ref, 1.0)):
        for e in range(NA_HEADS):
            ref[e] = (p[:, off + e * HEAD_DIM:off + (e + 1) * HEAD_DIM] * scale).astype(BF16)
        off += NA_HEADS * HEAD_DIM


def _inproj_call(xa, mod_l, g, w_hy_t, w_rest, gq2, gk2, cos_t, sin_t, s_len):
    bsz, t_all, d = xa.shape
    tm = TOK_TILE
    nb = mod_l.shape[0]
    hyc = w_hy_t.shape[0]
    nr = w_rest.shape[1]
    grid = (bsz, t_all // tm)
    const = lambda b, j: (0, 0)
    heads = lambda n: pl.BlockSpec((None, n, tm, HEAD_DIM), lambda b, j: (b, 0, j, 0))
    hshape = lambda n: jax.ShapeDtypeStruct((bsz, n, t_all, HEAD_DIM), BF16)
    return pl.pallas_call(
        functools.partial(_inproj_kernel, s_len // tm, bsz),
        grid=grid,
        in_specs=[pl.BlockSpec((None, tm, d), lambda b, j: (b, j, 0)),
                  pl.BlockSpec((nb, mod_l.shape[1]), const),
                  pl.BlockSpec((1, d), const),
                  pl.BlockSpec((hyc, d), const),
                  pl.BlockSpec((d, nr), const),
                  pl.BlockSpec((1, LANES), const),
                  pl.BlockSpec((1, LANES), const),
                  pl.BlockSpec((tm, LANES), lambda b, j: (j, 0)),
                  pl.BlockSpec((tm, LANES), lambda b, j: (j, 0))],
        out_specs=[pl.BlockSpec((None, hyc, tm), lambda b, j: (b, 0, j)),
                   heads(GQA_HEADS), heads(GQA_KV_HEADS), heads(GQA_KV_HEADS),
                   heads(NA_HEADS), heads(NA_HEADS), heads(NA_HEADS)],
        out_shape=[jax.ShapeDtypeStruct((bsz, hyc, t_all), F32),
                   hshape(GQA_HEADS), hshape(GQA_KV_HEADS), hshape(GQA_KV_HEADS),
                   hshape(NA_HEADS), hshape(NA_HEADS), hshape(NA_HEADS)],
        name="in_proj",
        compiler_params=_cparams(("parallel", "parallel")),
    )(xa, mod_l, g, w_hy_t, w_rest, gq2, gk2, cos_t, sin_t)


def _gqa_kernel(s_len, lc, tk, q_ref, k_ref, v_ref, o_ref, s_scr, sc_scr, m_scr, l_scr, acc_scr):
    i = pl.program_id(2)
    tm = q_ref.shape[1]
    rows = GQA_GROUP * tm
    q = q_ref[...].reshape(rows, HEAD_DIM)
    n_lat = jnp.where(i * tm < s_len, s_len // tk, 0)
    m_scr[...] = jnp.full(m_scr.shape, NEG, F32)

    def scores(keys, dst):
        s = _dot_nt(q, keys)
        dst[...] = s
        m = m_scr[...]
        for g in range(s.shape[1] // LANES):
            m = jnp.maximum(m, s[:, g * LANES:(g + 1) * LANES])
        m_scr[...] = m

    def qk(c, carry):
        scores(k_ref[pl.ds(pl.multiple_of(c * tk, tk), tk), :], s_scr.at[c])
        return carry

    lax.fori_loop(0, n_lat, qk, 0)
    scores(k_ref[s_len:s_len + lc, :], sc_scr)
    m_scr[...] = jnp.broadcast_to(jnp.max(m_scr[...], axis=1, keepdims=True), m_scr.shape)
    l_scr[...] = jnp.zeros(l_scr.shape, F32)
    acc_scr[...] = jnp.zeros(acc_scr.shape, F32)

    def apply(src, values):
        s = src[...]
        m = m_scr[...]
        l = l_scr[...]
        ps = []
        for g in range(s.shape[1] // LANES):
            p = jnp.exp2(s[:, g * LANES:(g + 1) * LANES] - m)
            l = l + p
            ps.append(p.astype(BF16))
        l_scr[...] = l
        acc_scr[...] += _dot(jnp.concatenate(ps, axis=1), values)

    def pv(c, carry):
        apply(s_scr.at[c], v_ref[pl.ds(pl.multiple_of(c * tk, tk), tk), :])
        return carry

    lax.fori_loop(0, n_lat, pv, 0)
    apply(sc_scr, v_ref[s_len:s_len + lc, :])
    o = acc_scr[...] / jnp.sum(l_scr[...], axis=1, keepdims=True)
    for h in range(GQA_GROUP):
        o_ref[:, h * HEAD_DIM:(h + 1) * HEAD_DIM] = o[h * tm:(h + 1) * tm].astype(o_ref.dtype)


def _gqa_call(q, k, v, s_len, n_query):
    bsz, _, t_all, hd = q.shape
    tm = TOK_TILE
    tk = 2 * TOK_TILE
    lc = t_all - s_len
    q5 = q.reshape(bsz, GQA_KV_HEADS, GQA_GROUP, t_all, hd)
    return pl.pallas_call(
        functools.partial(_gqa_kernel, s_len, lc, tk),
        grid=(bsz, GQA_KV_HEADS, n_query // tm),
        in_specs=[pl.BlockSpec((None, None, GQA_GROUP, tm, hd), lambda b, g, i: (b, g, 0, i, 0)),
                  pl.BlockSpec((None, None, t_all, hd), lambda b, g, i: (b, g, 0, 0)),
                  pl.BlockSpec((None, None, t_all, hd), lambda b, g, i: (b, g, 0, 0))],
        out_specs=pl.BlockSpec((None, tm, GQA_GROUP * hd), lambda b, g, i: (b, i, g)),
        out_shape=jax.ShapeDtypeStruct((bsz, t_all, GQA_HEADS * hd), BF16),
        scratch_shapes=[pltpu.VMEM((s_len // tk, GQA_GROUP * tm, tk), F32),
                        pltpu.VMEM((GQA_GROUP * tm, lc), F32),
                        pltpu.VMEM((GQA_GROUP * tm, LANES), F32),
                        pltpu.VMEM((GQA_GROUP * tm, LANES), F32),
                        pltpu.VMEM((GQA_GROUP * tm, hd), F32)],
        name="gqa_attention",
        compiler_params=_cparams(("parallel", "parallel", "parallel")),
    )(q5, k, v)


def _na_bias_consts(rows):
    assert rows >= NA_WIN + NA_ROWS
    nrow, ncol = 2 * NA_KH - 1, 2 * NA_KW - 1
    rsel = np.zeros((3, NA_ROWS, NA_WIN, nrow), np.float32)
    mrow = np.zeros((3, NA_ROWS, NA_WIN), bool)
    for kind, r0 in enumerate((0, NA_ROWS, rows - NA_ROWS)):
        kr0 = int(np.clip(r0 - NA_KH // 2, 0, rows - NA_WIN))
        for qr in range(NA_ROWS):
            r = r0 + qr
            rs = int(np.clip(r - NA_KH // 2, 0, rows - NA_KH))
            for kr in range(NA_WIN):
                ka = kr0 + kr
                if rs <= ka < rs + NA_KH:
                    mrow[kind, qr, kr] = True
                    rsel[kind, qr, kr, ka - r + NA_KH - 1] = 1.0
    csel = np.zeros((GRID_W, GRID_W, ncol), np.float32)
    mcol = np.zeros((GRID_W, GRID_W), bool)
    for w in range(GRID_W):
        cs = int(np.clip(w - NA_KW // 2, 0, GRID_W - NA_KW))
        for kc in range(cs, cs + NA_KW):
            mcol[w, kc] = True
            csel[w, kc, kc - w + NA_KW - 1] = 1.0
    mask = mrow[:, :, None, :, None] & mcol[None, None, :, None, :]
    return rsel, csel, mask.reshape(3, TOK_TILE, NA_WIN * GRID_W)


def _na_bias_tables(na_rpb, rows):
    rsel, csel, mask = _na_bias_consts(rows)
    depth, heads = na_rpb.shape[:2]
    a = jnp.einsum("kqra,lhab->lkhqrb", jnp.asarray(rsel), na_rpb.astype(F32), precision=HIGHEST)
    t = jnp.einsum("lkhqrb,wcb->lkhqwrc", a, jnp.asarray(csel), precision=HIGHEST)
    t = t.reshape(depth, 3, heads, TOK_TILE, NA_WIN * GRID_W)
    return jnp.where(jnp.asarray(mask)[None, :, None], t, NEG)


def _softmax_pv(parts):
    m = None
    for s, _ in parts:
        mm = jnp.max(s, axis=1, keepdims=True)
        m = mm if m is None else jnp.maximum(m, mm)
    l = 0.0
    o = 0.0
    for s, v in parts:
        p = jnp.exp(s - m)
        l = l + jnp.sum(p, axis=1, keepdims=True)
        o = o + _dot(p.astype(BF16), v)
    return o / l


def _na_kernel(s_len, lc, rows, q_ref, k_ref, v_ref, tab_ref, o_ref):
    i = pl.program_id(1)
    nblk = s_len // TOK_TILE
    nloc = NA_WIN * GRID_W

    @pl.when(i < nblk)
    def _():
        kind = jnp.where(i == 0, 0, jnp.where(i == nblk - 1, 2, 1))
        kr0 = jnp.clip(i * NA_ROWS - NA_KH // 2, 0, rows - NA_WIN)
        ks = pl.multiple_of(kr0 * GRID_W, GRID_W)
        for h in range(NA_HEADS):
            q = q_ref[h]
            s_loc = _dot_nt(q, k_ref[h, pl.ds(ks, nloc), :]) + tab_ref[kind, h]
            s_ctx = _dot_nt(q, k_ref[h, pl.ds(s_len, lc), :])
            o = _softmax_pv([(s_loc, v_ref[h, pl.ds(ks, nloc), :]),
                             (s_ctx, v_ref[h, pl.ds(s_len, lc), :])])
            o_ref[:, h * HEAD_DIM:(h + 1) * HEAD_DIM] = o.astype(o_ref.dtype)

    @pl.when(i >= nblk)
    def _():
        for h in range(NA_HEADS):
            q = q_ref[h]
            s_ctx = _dot_nt(q, k_ref[h, pl.ds(s_len, lc), :])
            o = _softmax_pv([(s_ctx, v_ref[h, pl.ds(s_len, lc), :])])
            o_ref[:, h * HEAD_DIM:(h + 1) * HEAD_DIM] = o.astype(o_ref.dtype)


def _na_call(nq, nk, nv, table, s_len, n_query):
    bsz, heads, t_all, hd = nq.shape
    tm = TOK_TILE
    rows = s_len // GRID_W
    whole = pl.BlockSpec((None, heads, t_all, hd), lambda b, i: (b, 0, 0, 0))
    return pl.pallas_call(
        functools.partial(_na_kernel, s_len, t_all - s_len, rows),
        grid=(bsz, n_query // tm),
        in_specs=[pl.BlockSpec((None, heads, tm, hd), lambda b, i: (b, 0, i, 0)),
                  whole, whole,
                  pl.BlockSpec(table.shape, lambda b, i: (0, 0, 0, 0))],
        out_specs=pl.BlockSpec((None, tm, heads * hd), lambda b, i: (b, i, 0)),
        out_shape=jax.ShapeDtypeStruct((bsz, t_all, heads * hd), BF16),
        name="neighbourhood_attention",
        compiler_params=_cparams(("parallel", "parallel")),
    )(nq, nk, nv, table)


def _hy_filter_kernel(z_ref, t_ref, side_ref, w1_ref, b1_ref, w2_ref, b2_ref, w3_ref, b3_ref,
                      wo_ref, fr_ref, dl_ref, o_ref):
    om = fr_ref[...]
    h = jnp.sin(om * (_dot(z_ref[...], w1_ref[...], precision=HIGHEST) + b1_ref[...]))
    h = jnp.sin(om * (_dot(h, w2_ref[...], precision=HIGHEST) + b2_ref[...]))
    h = jnp.sin(om * (_dot(h, w3_ref[...], precision=HIGHEST) + b3_ref[...]))
    ht = _dot_nt(wo_ref[...], h, precision=HIGHEST)
    c = o_ref.shape[0]
    decay = jnp.exp(-t_ref[...] * jnp.abs(dl_ref[...]))
    side = side_ref[...]
    o_ref[...] = decay * (jnp.where(side > 0.5, ht[:c], 0.0) + jnp.where(side < -0.5, ht[c:], 0.0))


def _hy_filter_consts(length):
    n = np.arange(2 * length)
    pos = np.where(n < length, n, 2 * length - n)
    pos = np.where(n == length, 0, pos)
    side = np.where(n < length, 1.0, -1.0)
    side = np.where(n == length, 0.0, side)
    t = np.linspace(0.0, 1.0, length)[pos]
    bands = (HY_EMB_DIM - 1) // 2
    w = 2.0 * math.pi * pos / length
    fr = np.linspace(1e-4, bands - 1, bands)
    z = np.concatenate([t[:, None], np.cos(fr[None, :] * w[:, None]), -np.sin(fr[None, :] * w[:, None])], axis=-1)
    zp = np.zeros((2 * length, 64), np.float32)
    zp[:, :HY_EMB_DIM] = z
    return zp, t.astype(np.float32)[None, :], side.astype(np.float32)[None, :]


def _hy_filter_call(length, w1, b1, w2, b2, w3, b3, wout, freq):
    f = w2.shape[0]
    c = wout.shape[1] // 2
    zp, t, side = _hy_filter_consts(length)
    w1p = jnp.zeros((64, f), F32).at[:HY_EMB_DIM].set(w1)
    deltas = np.linspace(math.log(HY_DECAY_TARGET) / HY_SLOW_DECAY,
                         math.log(HY_DECAY_TARGET) / HY_FAST_DECAY, c).astype(np.float32)[:, None]
    args = (jnp.asarray(zp), jnp.asarray(t), jnp.asarray(side), w1p, b1[None], w2, b2[None], w3, b3[None],
            wout.T, freq[None], jnp.asarray(deltas))
    return pl.pallas_call(
        _hy_filter_kernel,
        in_specs=[pl.BlockSpec(a.shape, lambda: (0,) * a.ndim) for a in args],
        out_specs=pl.BlockSpec((c, 2 * length), lambda: (0, 0)),
        out_shape=jax.ShapeDtypeStruct((c, 2 * length), F32),
        grid=(),
        name="hyena_filter",
        compiler_params=pltpu.CompilerParams(vmem_limit_bytes=VMEM_LIMIT),
    )(*args)


def _bitrev(n):
    bits = int(math.log2(n))
    return np.array([int(format(i, "0%db" % bits)[::-1], 2) if bits else 0 for i in range(n)])


def _fft_consts(n1, ct):
    n = n1 * FFT_CHUNK
    stages = max(int(math.log2(n1)), 1)
    tw = np.zeros((2, stages, max(n1 // 2, 1)), np.float32)
    for s in range(int(math.log2(n1))):
        half = n1 >> (s + 1)
        ang = -2.0 * math.pi * np.arange(half) / (2 * half)
        tw[0, s, :half] = np.cos(ang)
        tw[1, s, :half] = np.sin(ang)
    k1 = _bitrev(n1)
    ang = -2.0 * math.pi * (np.arange(FFT_CHUNK)[None, :] * k1[:, None]) / n
    twr = np.repeat(np.cos(ang), ct, axis=0).astype(np.float32)
    twi = np.repeat(np.sin(ang), ct, axis=0).astype(np.float32)
    a = -2.0 * math.pi * np.outer(np.arange(FFT_CHUNK), np.arange(FFT_CHUNK)) / FFT_CHUNK
    wr, wi = np.cos(a), np.sin(a)
    wf = np.block([[wr, wi], [-wi, wr]]).astype(np.float32)
    wb = np.block([[wr, -wi], [wi, wr]]).astype(np.float32)
    return tw, twr, twi, wf, wb


def _slab_fft(zr, zi, tw_ref, n1, ct, inverse, half_input=False, half_output=False):
    nst = int(math.log2(n1))
    order = range(nst - 1, -1, -1) if inverse else range(nst)
    for s in order:
        half = n1 >> (s + 1)
        lh = int(math.log2(half))
        first = s == 0

        def body(idx, carry, s=s, half=half, lh=lh, first=first):
            blk = idx >> lh
            j = idx & (half - 1)
            r0 = pl.multiple_of(((blk << (lh + 1)) + j) * ct, ct)
            r1 = pl.multiple_of(r0 + half * ct, ct)
            wr = tw_ref[0, s, j]
            wi = tw_ref[1, s, j]
            ar = zr[pl.ds(r0, ct), :]
            ai = zi[pl.ds(r0, ct), :]
            if not inverse:
                if first and half_input:
                    zr[pl.ds(r1, ct), :] = ar * wr - ai * wi
                    zi[pl.ds(r1, ct), :] = ar * wi + ai * wr
                else:
                    br = zr[pl.ds(r1, ct), :]
                    bi = zi[pl.ds(r1, ct), :]
                    dr = ar - br
                    di = ai - bi
                    zr[pl.ds(r0, ct), :] = ar + br
                    zi[pl.ds(r0, ct), :] = ai + bi
                    zr[pl.ds(r1, ct), :] = dr * wr - di * wi
                    zi[pl.ds(r1, ct), :] = dr * wi + di * wr
            else:
                br = zr[pl.ds(r1, ct), :]
                bi = zi[pl.ds(r1, ct), :]
                tr = br * wr + bi * wi
                ti = bi * wr - br * wi
                zr[pl.ds(r0, ct), :] = ar + tr
                zi[pl.ds(r0, ct), :] = ai + ti
                if not (first and half_output):
                    zr[pl.ds(r1, ct), :] = ar - tr
                    zi[pl.ds(r1, ct), :] = ai - ti
            return carry

        lax.fori_loop(0, n1 // 2, body, 0)


def _chunk_dft(xr, xi, w_ref):
    y = _dot(jnp.concatenate([xr, xi], axis=1), w_ref[...], precision=HIGHEST)
    return y[:, :FFT_CHUNK], y[:, FFT_CHUNK:]


def _fft_groups(n1, ct):
    g = min(8, n1)
    return n1 // g, g * ct


def _hy_spec_kernel(n1, ct, f_ref, tw_ref, twr_ref, twi_ref, wf_ref, fr_ref, fi_ref, zr, zi):
    for s in range(n1):
        zr[s * ct:(s + 1) * ct, :] = f_ref[:, s * FFT_CHUNK:(s + 1) * FFT_CHUNK]
    zi[...] = jnp.zeros(zi.shape, F32)
    _slab_fft(zr, zi, tw_ref, n1, ct, inverse=False)
    ngroups, gr = _fft_groups(n1, ct)

    def body(g, carry):
        r0 = pl.multiple_of(g * gr, gr)
        a = zr[pl.ds(r0, gr), :]
        b = zi[pl.ds(r0, gr), :]
        tr = twr_ref[pl.ds(r0, gr), :]
        ti = twi_ref[pl.ds(r0, gr), :]
        yr, yi = _chunk_dft(a * tr - b * ti, a * ti + b * tr, wf_ref)
        fr_ref[pl.ds(r0, gr), :] = yr
        fi_ref[pl.ds(r0, gr), :] = yi
        return carry

    lax.fori_loop(0, ngroups, body, 0)


def _hy_spec_call(filt_t):
    c, n = filt_t.shape
    n1 = n // FFT_CHUNK
    ct = HY_CT
    tw, twr, twi, wf, _ = _fft_consts(n1, ct)
    const2 = lambda i: (0, 0)
    out = jax.ShapeDtypeStruct((c // ct, n1 * ct, FFT_CHUNK), F32)
    ospec = pl.BlockSpec((None, n1 * ct, FFT_CHUNK), lambda i: (i, 0, 0))
    return pl.pallas_call(
        functools.partial(_hy_spec_kernel, n1, ct),
        grid=(c // ct,),
        in_specs=[pl.BlockSpec((ct, n), lambda i: (i, 0)),
                  pl.BlockSpec(memory_space=pltpu.SMEM),
                  pl.BlockSpec(twr.shape, const2), pl.BlockSpec(twi.shape, const2),
                  pl.BlockSpec(wf.shape, const2)],
        out_specs=[ospec, ospec],
        out_shape=[out, out],
        scratch_shapes=[pltpu.VMEM((n1 * ct, FFT_CHUNK), F32), pltpu.VMEM((n1 * ct, FFT_CHUNK), F32)],
        name="hyena_filter_spectrum",
        compiler_params=_cparams(("parallel",)),
    )(filt_t, jnp.asarray(tw), jnp.asarray(twr), jnp.asarray(twi), jnp.asarray(wf))


def _hy_conv_kernel(n1, ct, length, v_ref, x1_ref, x2_ref, cwv_ref, cw1_ref, cw2_ref, skip_ref,
                    fr_ref, fi_ref, tw_ref, twr_ref, twi_ref, wf_ref, wb_ref, o_ref,
                    zr, zi, u_scr, g_scr):
    lane = lax.broadcasted_iota(jnp.int32, (1, length), 1)

    def short_conv(x, cw_ref):
        xm = jnp.where(lane >= 1, pltpu.roll(x, 1, 1), 0.0)
        xp = jnp.where(lane <= length - 2, pltpu.roll(x, length - 1, 1), 0.0)
        return xm * cw_ref[:, 0:1] + x * cw_ref[:, 1:2] + xp * cw_ref[:, 2:3] + cw_ref[:, 3:4]

    for a in range(2):
        u_scr[a] = short_conv(v_ref[a], cwv_ref) * short_conv(x1_ref[a], cw1_ref)
        g_scr[a] = short_conv(x2_ref[a], cw2_ref)
    nz = length // FFT_CHUNK
    for s in range(nz):
        zr[s * ct:(s + 1) * ct, :] = u_scr[0, :, s * FFT_CHUNK:(s + 1) * FFT_CHUNK]
        zi[s * ct:(s + 1) * ct, :] = u_scr[1, :, s * FFT_CHUNK:(s + 1) * FFT_CHUNK]
    _slab_fft(zr, zi, tw_ref, n1, ct, inverse=False, half_input=True)
    ngroups, gr = _fft_groups(n1, ct)

    def body(g, carry):
        r0 = pl.multiple_of(g * gr, gr)
        a = zr[pl.ds(r0, gr), :]
        b = zi[pl.ds(r0, gr), :]
        tr = twr_ref[pl.ds(r0, gr), :]
        ti = twi_ref[pl.ds(r0, gr), :]
        yr, yi = _chunk_dft(a * tr - b * ti, a * ti + b * tr, wf_ref)
        fr = fr_ref[pl.ds(r0, gr), :]
        fi = fi_ref[pl.ds(r0, gr), :]
        qr, qi = _chunk_dft(yr * fr - yi * fi, yr * fi + yi * fr, wb_ref)
        zr[pl.ds(r0, gr), :] = qr * tr + qi * ti
        zi[pl.ds(r0, gr), :] = qi * tr - qr * ti
        return carry

    lax.fori_loop(0, ngroups, body, 0)
    _slab_fft(zr, zi, tw_ref, n1, ct, inverse=True, half_output=True)
    inv_n = 1.0 / (n1 * FFT_CHUNK)
    skip = skip_ref[...]
    for s in range(nz):
        sl = slice(s * FFT_CHUNK, (s + 1) * FFT_CHUNK)
        for a, z in ((0, zr), (1, zi)):
            y = z[s * ct:(s + 1) * ct, :] * inv_n
            u = u_scr[a, :, sl]
            o_ref[a, :, sl] = (g_scr[a, :, sl] * (y + u * skip)).astype(o_ref.dtype)


def _hy_conv_call(hy_t, spec_r, spec_i, cw, skip, length, lane_block, prev=None):
    _, bh, c3, t_all = hy_t.shape
    c = c3 // 3
    ct = HY_CT
    nct = c // ct
    n1 = 2 * length // FFT_CHUNK
    tw, twr, twi, wf, wb = _fft_consts(n1, ct)
    const2 = lambda p, i: (0, 0)
    xin = lambda k: pl.BlockSpec((2, None, ct, length), lambda p, i, k=k: (0, p, k * nct + i, lane_block))
    cwin = lambda k: pl.BlockSpec((ct, 4), lambda p, i, k=k: (k * nct + i, 0))
    fin = pl.BlockSpec((None, n1 * ct, FFT_CHUNK), lambda p, i: (i, 0, 0))
    in_specs = [xin(0), xin(1), xin(2), cwin(0), cwin(1), cwin(2),
                pl.BlockSpec((ct, 1), lambda p, i: (i, 0)),
                fin, fin,
                pl.BlockSpec(memory_space=pltpu.SMEM),
                pl.BlockSpec(twr.shape, const2), pl.BlockSpec(twi.shape, const2),
                pl.BlockSpec(wf.shape, const2), pl.BlockSpec(wb.shape, const2)]
    args = [hy_t, hy_t, hy_t, cw, cw, cw, skip, spec_r, spec_i, jnp.asarray(tw), jnp.asarray(twr),
            jnp.asarray(twi), jnp.asarray(wf), jnp.asarray(wb)]
    kern = functools.partial(_hy_conv_kernel, n1, ct, length)
    aliases = {}
    if prev is not None:
        in_specs.append(pl.BlockSpec(memory_space=pl.ANY))
        args.append(prev)
        aliases = {len(args) - 1: 0}
        kern = lambda *refs, kern=kern: kern(*refs[:14], *refs[15:])
    return pl.pallas_call(
        kern,
        grid=(bh, nct),
        in_specs=in_specs,
        out_specs=pl.BlockSpec((2, None, ct, length), lambda p, i: (0, p, i, lane_block)),
        out_shape=jax.ShapeDtypeStruct((2, bh, c, t_all), BF16),
        scratch_shapes=[pltpu.VMEM((n1 * ct, FFT_CHUNK), F32), pltpu.VMEM((n1 * ct, FFT_CHUNK), F32),
                        pltpu.VMEM((2, ct, length), F32), pltpu.VMEM((2, ct, length), F32)],
        input_output_aliases=aliases,
        name="hyena_conv_%d" % length,
        compiler_params=_cparams(("parallel", "parallel")),
    )(*args)


def _outproj_kernel(n_lat_tiles, ctx_row, x_ref, hy_ref, ga_ref, na_ref, w_ref, mod_ref, o_ref):
    b = pl.program_id(0)
    j = pl.program_id(1)
    d = x_ref.shape[-1]
    row = jnp.where(j < n_lat_tiles, b, ctx_row)
    hy = hy_ref[...].astype(F32).T.astype(BF16)
    mix = jnp.concatenate([hy, ga_ref[...], na_ref[...]], axis=1)
    o_ref[...] = x_ref[...] + _mod_row(mod_ref, row, 2, d) * _dot(mix, w_ref[...])


def _outproj_call(xa, hyo_t, gqa, na, w_out, mod_l, s_len, n_tok):
    bsz, _, d = xa.shape
    tm = TOK_TILE
    c = hyo_t.shape[1]
    const = lambda b, j: (0, 0)
    tok = lambda n: pl.BlockSpec((None, tm, n), lambda b, j: (b, j, 0))
    return pl.pallas_call(
        functools.partial(_outproj_kernel, s_len // tm, bsz),
        grid=(bsz, n_tok // tm),
        in_specs=[tok(d), pl.BlockSpec((None, c, tm), lambda b, j: (b, 0, j)),
                  tok(gqa.shape[-1]), tok(na.shape[-1]),
                  pl.BlockSpec(w_out.shape, const), pl.BlockSpec(mod_l.shape, const)],
        out_specs=tok(d),
        out_shape=jax.ShapeDtypeStruct((bsz, n_tok, d), F32),
        name="out_proj",
        compiler_params=_cparams(("parallel", "parallel")),
    )(xa, hyo_t, gqa, na, w_out, mod_l)


def _ffn_kernel(tiles_per_batch, n_lat_tiles, ctx_row, x_ref, mod_ref, g_ref, wg_ref, wu_ref, wo_ref,
                o_ref, h_scr, acc_scr):
    i = pl.program_id(0)
    j = pl.program_id(1)
    d = x_ref.shape[-1]
    groups = x_ref.shape[0] // TOK_TILE

    def mod_row(s):
        gi = i * groups + s
        jj = gi % tiles_per_batch
        return jnp.where(jj < n_lat_tiles, gi // tiles_per_batch, ctx_row)

    @pl.when(j == 0)
    def _():
        for s in range(groups):
            row = mod_row(s)
            sl = slice(s * TOK_TILE, (s + 1) * TOK_TILE)
            h = _norm_mod(x_ref[sl, :], g_ref[...], _mod_row(mod_ref, row, 3, d), _mod_row(mod_ref, row, 4, d))
            h_scr[sl, :] = h.astype(BF16)
        acc_scr[...] = jnp.zeros(acc_scr.shape, F32)

    h = h_scr[...]
    gate = _dot(h, wg_ref[...])
    up = _dot(h, wu_ref[...])
    act = (gate * jax.nn.sigmoid(gate) * up).astype(BF16)
    acc_scr[...] += _dot(act, wo_ref[...])

    @pl.when(j == pl.num_programs(1) - 1)
    def _():
        for s in range(groups):
            row = mod_row(s)
            sl = slice(s * TOK_TILE, (s + 1) * TOK_TILE)
            o_ref[sl, :] = x_ref[sl, :] + _mod_row(mod_ref, row, 5, d) * acc_scr[sl, :]


def _ffn_call(xa, mod_l, g, w_in, w_out, s_len):
    bsz, t_all, d = xa.shape
    f = w_out.shape[0]
    r = bsz * t_all
    tm = 1024 if r % 1024 == 0 else TOK_TILE
    th = 256
    nh = f // th
    const = lambda i, j: (0, 0)
    out = pl.pallas_call(
        functools.partial(_ffn_kernel, t_all // TOK_TILE, s_len // TOK_TILE, bsz),
        grid=(r // tm, nh),
        in_specs=[pl.BlockSpec((tm, d), lambda i, j: (i, 0)),
                  pl.BlockSpec(mod_l.shape, const),
                  pl.BlockSpec((1, d), const),
                  pl.BlockSpec((d, th), lambda i, j: (0, j)),
                  pl.BlockSpec((d, th), lambda i, j: (0, nh + j)),
                  pl.BlockSpec((th, d), lambda i, j: (j, 0))],
        out_specs=pl.BlockSpec((tm, d), lambda i, j: (i, 0)),
        out_shape=jax.ShapeDtypeStruct((r, d), F32),
        scratch_shapes=[pltpu.VMEM((tm, d), BF16), pltpu.VMEM((tm, d), F32)],
        name="swiglu_ffn",
        compiler_params=_cparams(("parallel", "arbitrary")),
    )(xa.reshape(r, d), mod_l, g, w_in, w_in, w_out)
    return out.reshape(bsz, t_all, d)


def _final_kernel(x_ref, g_ref, o_ref):
    x = x_ref[...]
    ms = jnp.mean(x * x, axis=-1, keepdims=True)
    o_ref[...] = x * lax.rsqrt(ms + NORM_EPS) * g_ref[...]


def _final_call(xa, g, s_len):
    bsz, _, d = xa.shape
    tm = TOK_TILE
    return pl.pallas_call(
        _final_kernel,
        grid=(bsz, s_len // tm),
        in_specs=[pl.BlockSpec((None, tm, d), lambda b, j: (b, j, 0)),
                  pl.BlockSpec((1, d), lambda b, j: (0, 0))],
        out_specs=pl.BlockSpec((None, tm, d), lambda b, j: (b, j, 0)),
        out_shape=jax.ShapeDtypeStruct((bsz, s_len, d), F32),
        name="final_norm",
        compiler_params=_cparams(("parallel", "parallel")),
    )(xa, g)


def _rope_tables(s_len, lc):
    pos = np.arange(s_len)
    row = (pos // GRID_W).astype(np.float64)
    col = (pos % GRID_W).astype(np.float64)
    n_f = HEAD_DIM // 4
    inv = (ROPE_THETA ** (-np.arange(n_f, dtype=np.float32) / n_f)).astype(np.float64)
    ang = np.concatenate([row[:, None] * inv, col[:, None] * inv], axis=-1).astype(np.float32).astype(np.float64)
    lane = np.arange(LANES)
    sign = np.where((lane % HEAD_DIM) < HEAD_DIM // 2, -1.0, 1.0)
    cos = np.cos(ang)[:, lane % (HEAD_DIM // 2)]
    sin = np.sin(ang)[:, lane % (HEAD_DIM // 2)] * sign[None, :]
    cos = np.concatenate([cos, np.ones((lc, LANES))], axis=0).astype(np.float32)
    sin = np.concatenate([sin, np.zeros((lc, LANES))], axis=0).astype(np.float32)
    return jnp.asarray(cos), jnp.asarray(sin)


def kernel(x, c, ctx, c_ctx, w_mod, b_mod, g_mix, g_ffn, w_in, w_out, hy_conv_w, hy_conv_b,
           hy_f_w1, hy_f_b1, hy_f_w2, hy_f_b2, hy_f_w3, hy_f_b3, hy_f_wout, hy_f_freq, hy_skip,
           qk_g_q, qk_g_k, na_rpb, w_ffn_in, w_ffn_out, g_final):
    bsz, s_len, d = x.shape
    lc = ctx.shape[1]
    depth = w_mod.shape[0]
    t_all = s_len + lc
    hyw = hy_skip.shape[1]
    hy_cols = 3 * hyw
    assert bsz % 2 == 0 and s_len % TOK_TILE == 0 and lc == TOK_TILE and s_len % lc == 0

    nb = -(-(bsz + 1) // 8) * 8
    cs = jnp.zeros((nb, d), F32).at[:bsz].set(c).at[bsz].set(c_ctx)
    mod = _mod_call(cs, w_mod, b_mod)

    cos_t, sin_t = _rope_tables(s_len, lc)
    tables = _na_bias_tables(na_rpb, s_len // GRID_W)
    w_hy_t = jnp.swapaxes(w_in[:, :, :hy_cols], 1, 2).astype(BF16)
    w_rest = w_in[:, :, hy_cols:].astype(BF16)
    w_out_b = w_out.astype(BF16)
    w_fi = w_ffn_in.astype(BF16)
    w_fo = w_ffn_out.astype(BF16)
    cw = jnp.concatenate([jnp.swapaxes(hy_conv_w, 1, 2), hy_conv_b[:, :, None]], axis=2)
    per = LANES // HEAD_DIM

    xa = jnp.concatenate([x, ctx], axis=1)
    for l in range(depth):
        last = l == depth - 1
        gq2 = jnp.tile(qk_g_q[l], per)[None]
        gk2 = jnp.tile(qk_g_k[l], per)[None]
        hy_t, q, k, v, nq, nk, nv = _inproj_call(xa, mod[l], g_mix[l][None], w_hy_t[l], w_rest[l],
                                                  gq2, gk2, cos_t, sin_t, s_len)
        n_tok = s_len if last else t_all
        gqa = _gqa_call(q, k, v, s_len, n_tok)
        na = _na_call(nq, nk, nv, tables[l], s_len, n_tok)

        fargs = (hy_f_w1[l], hy_f_b1[l], hy_f_w2[l], hy_f_b2[l], hy_f_w3[l], hy_f_b3[l],
                 hy_f_wout[l], hy_f_freq[l])
        hy_p = hy_t.reshape(2, bsz // 2, hy_cols, t_all)
        skip = hy_skip[l][:, None]
        fr, fi = _hy_spec_call(_hy_filter_call(s_len, *fargs))
        hyo = _hy_conv_call(hy_p, fr, fi, cw[l], skip, s_len, 0)
        if not last:
            frc, fic = _hy_spec_call(_hy_filter_call(lc, *fargs))
            hyo = _hy_conv_call(hy_p, frc, fic, cw[l], skip, lc, s_len // lc, prev=hyo)
        hyo = hyo.reshape(bsz, hyw, t_all)

        xa = _outproj_call(xa, hyo, gqa, na, w_out_b[l], mod[l], s_len, n_tok)
        xa = _ffn_call(xa, mod[l], g_ffn[l][None], w_fi[l], w_fo[l], s_len)
    return _final_call(xa, g_final[None], s_len)
```

```python
import functools
import math

import numpy as np
import jax
import jax.numpy as jnp
from jax import lax
from jax.experimental import pallas as pl
from jax.experimental.pallas import tpu as pltpu

F32 = jnp.float32
BF16 = jnp.bfloat16
HIGHEST = lax.Precision.HIGHEST

GRID_W = 64
HEAD_DIM = 64
GQA_HEADS = 8
GQA_KV_HEADS = 2
GQA_GROUP = GQA_HEADS // GQA_KV_HEADS
NA_HEADS = 4
NA_KH = 8
NA_KW = 16
ROPE_THETA = 10000.0
HY_EMB_DIM = 33
HY_DECAY_TARGET = 1e-2
HY_FAST_DECAY = 0.3
HY_SLOW_DECAY = 1.5
NORM_EPS = 1e-6
ATTN_SCALE = HEAD_DIM ** -0.5
NEG = -1e30
LOG2E = math.log2(math.e)

LANES = 128
TOK_TILE = 256
FFT_CHUNK = LANES
HY_CT = 64
NA_ROWS = TOK_TILE // GRID_W
NA_WIN = NA_ROWS + NA_KH
VMEM_LIMIT = 56 * 1024 * 1024


def _cparams(sem):
    return pltpu.CompilerParams(dimension_semantics=sem, vmem_limit_bytes=VMEM_LIMIT)


def _dot(a, b, **kw):
    return jnp.dot(a, b, preferred_element_type=F32, **kw)


def _dot_nt(a, b, **kw):
    return lax.dot_general(a, b, (((1,), (1,)), ((), ())), preferred_element_type=F32, **kw)


def _mod_kernel(cs_ref, w_ref, b_ref, o_ref):
    cs = cs_ref[...]
    s = cs * jax.nn.sigmoid(cs)
    o_ref[...] = _dot(s, w_ref[...], precision=HIGHEST) + b_ref[...]


def _mod_call(cs, w_mod, b_mod):
    depth, d, n = w_mod.shape
    nb = cs.shape[0]
    tn = 1024
    return pl.pallas_call(
        _mod_kernel,
        grid=(depth, n // tn),
        in_specs=[pl.BlockSpec((nb, d), lambda l, j: (0, 0)),
                  pl.BlockSpec((None, d, tn), lambda l, j: (l, 0, j)),
                  pl.BlockSpec((None, 1, tn), lambda l, j: (l, 0, j))],
        out_specs=pl.BlockSpec((None, nb, tn), lambda l, j: (l, 0, j)),
        out_shape=jax.ShapeDtypeStruct((depth, nb, n), F32),
        name="adaln_mod",
        compiler_params=_cparams(("parallel", "parallel")),
    )(cs, w_mod, b_mod.reshape(depth, 1, n))


def _mod_row(mod_ref, row, k, d):
    return mod_ref[pl.ds(row, 1), k * d:(k + 1) * d]


def _norm_mod(x, g, shift, scale):
    ms = jnp.mean(x * x, axis=-1, keepdims=True)
    y = x * lax.rsqrt(ms + NORM_EPS) * g
    return y * (1.0 + scale) + shift


def _inproj_kernel(n_lat_tiles, ctx_row, x_ref, mod_ref, g_ref, why_ref, wr_ref, gq_ref, gk_ref,
                   cos_ref, sin_ref, hy_ref, q_ref, k_ref, v_ref, nq_ref, nk_ref, nv_ref):
    b = pl.program_id(0)
    j = pl.program_id(1)
    d = x_ref.shape[-1]
    row = jnp.where(j < n_lat_tiles, b, ctx_row)
    h = _norm_mod(x_ref[...], g_ref[...], _mod_row(mod_ref, row, 0, d), _mod_row(mod_ref, row, 1, d))
    h = h.astype(BF16)
    hy_ref[...] = _dot_nt(why_ref[...], h)
    p = _dot(h, wr_ref[...])

    ri = lax.broadcasted_iota(jnp.int32, (LANES, LANES), 0) // HEAD_DIM
    ci = lax.broadcasted_iota(jnp.int32, (LANES, LANES), 1) // HEAD_DIM
    avg = jnp.where(ri == ci, 1.0 / HEAD_DIM, 0.0).astype(BF16)
    lane = lax.broadcasted_iota(jnp.int32, (1, LANES), 1)
    first_half = (lane % HEAD_DIM) < (HEAD_DIM // 2)
    cos = cos_ref[...]
    sin = sin_ref[...]

    def norm_rope(xc, g, scale):
        ms = _dot((xc * xc).astype(BF16), avg)
        xn = xc * lax.rsqrt(ms + NORM_EPS) * g
        partner = jnp.where(first_half, pltpu.roll(xn, LANES - HEAD_DIM // 2, 1),
                            pltpu.roll(xn, HEAD_DIM // 2, 1))
        return (xn * cos + partner * sin) * scale

    per = LANES // HEAD_DIM
    nq_chunks = GQA_HEADS // per
    for c in range(nq_chunks):
        y = norm_rope(p[:, c * LANES:(c + 1) * LANES], gq_ref[...], ATTN_SCALE * LOG2E).astype(BF16)
        for e in range(per):
            q_ref[c * per + e] = y[:, e * HEAD_DIM:(e + 1) * HEAD_DIM]
    off = nq_chunks * LANES
    y = norm_rope(p[:, off:off + LANES], gk_ref[...], 1.0).astype(BF16)
    for e in range(GQA_KV_HEADS):
        k_ref[e] = y[:, e * HEAD_DIM:(e + 1) * HEAD_DIM]
    off += LANES
    for e in range(GQA_KV_HEADS):
        v_ref[e] = p[:, off + e * HEAD_DIM:off + (e + 1) * HEAD_DIM].astype(BF16)
    off += LANES
    for ref, scale in ((nq_ref, ATTN_SCALE), (nk_ref, 1.0), (nv_ref, 1.0)):
        for e in range(NA_HEADS):
            ref[e] = (p[:, off + e * HEAD_DIM:off + (e + 1) * HEAD_DIM] * scale).astype(BF16)
        off += NA_HEADS * HEAD_DIM


def _inproj_call(xa, mod_l, g, w_hy_t, w_rest, gq2, gk2, cos_t, sin_t, s_len):
    bsz, t_all, d = xa.shape
    tm = TOK_TILE
    nb = mod_l.shape[0]
    hyc = w_hy_t.shape[0]
    nr = w_rest.shape[1]
    grid = (bsz, t_all // tm)
    const = lambda b, j: (0, 0)
    heads = lambda n: pl.BlockSpec((None, n, tm, HEAD_DIM), lambda b, j: (b, 0, j, 0))
    hshape = lambda n: jax.ShapeDtypeStruct((bsz, n, t_all, HEAD_DIM), BF16)
    return pl.pallas_call(
        functools.partial(_inproj_kernel, s_len // tm, bsz),
        grid=grid,
        in_specs=[pl.BlockSpec((None, tm, d), lambda b, j: (b, j, 0)),
                  pl.BlockSpec((nb, mod_l.shape[1]), const),
                  pl.BlockSpec((1, d), const),
                  pl.BlockSpec((hyc, d), const),
                  pl.BlockSpec((d, nr), const),
                  pl.BlockSpec((1, LANES), const),
                  pl.BlockSpec((1, LANES), const),
                  pl.BlockSpec((tm, LANES), lambda b, j: (j, 0)),
                  pl.BlockSpec((tm, LANES), lambda b, j: (j, 0))],
        out_specs=[pl.BlockSpec((None, hyc, tm), lambda b, j: (b, 0, j)),
                   heads(GQA_HEADS), heads(GQA_KV_HEADS), heads(GQA_KV_HEADS),
                   heads(NA_HEADS), heads(NA_HEADS), heads(NA_HEADS)],
        out_shape=[jax.ShapeDtypeStruct((bsz, hyc, t_all), F32),
                   hshape(GQA_HEADS), hshape(GQA_KV_HEADS), hshape(GQA_KV_HEADS),
                   hshape(NA_HEADS), hshape(NA_HEADS), hshape(NA_HEADS)],
        name="in_proj",
        compiler_params=_cparams(("parallel", "parallel")),
    )(xa, mod_l, g, w_hy_t, w_rest, gq2, gk2, cos_t, sin_t)


def _gqa_kernel(s_len, lc, tk, q_ref, k_ref, v_ref, o_ref, s_scr, sc_scr, m_scr, l_scr, acc_scr):
    i = pl.program_id(2)
    tm = q_ref.shape[1]
    rows = GQA_GROUP * tm
    q = q_ref[...].reshape(rows, HEAD_DIM)
    is_latent = i * tm < s_len
    n_chunks = s_len // tk
    m_scr[...] = jnp.full(m_scr.shape, NEG, F32)

    def scores(keys, dst):
        s = _dot_nt(q, keys)
        dst[...] = s
        m = m_scr[...]
        for g in range(s.shape[1] // LANES):
            m = jnp.maximum(m, s[:, g * LANES:(g + 1) * LANES])
        m_scr[...] = m

    @pl.when(is_latent)
    def _():
        for c in range(n_chunks):
            scores(k_ref[c * tk:(c + 1) * tk, :], s_scr.at[c])

    scores(k_ref[s_len:s_len + lc, :], sc_scr)
    m_scr[...] = jnp.broadcast_to(jnp.max(m_scr[...], axis=1, keepdims=True), m_scr.shape)
    l_scr[...] = jnp.zeros(l_scr.shape, F32)
    acc_scr[...] = jnp.zeros(acc_scr.shape, F32)

    def apply(src, values):
        s = src[...]
        m = m_scr[...]
        l = l_scr[...]
        ps = []
        for g in range(s.shape[1] // LANES):
            p = jnp.exp2(s[:, g * LANES:(g + 1) * LANES] - m)
            l = l + p
            ps.append(p.astype(BF16))
        l_scr[...] = l
        acc_scr[...] += _dot(jnp.concatenate(ps, axis=1), values)

    @pl.when(is_latent)
    def _():
        for c in range(n_chunks):
            apply(s_scr.at[c], v_ref[c * tk:(c + 1) * tk, :])

    apply(sc_scr, v_ref[s_len:s_len + lc, :])
    o = acc_scr[...] / jnp.sum(l_scr[...], axis=1, keepdims=True)
    for h in range(GQA_GROUP):
        o_ref[:, h * HEAD_DIM:(h + 1) * HEAD_DIM] = o[h * tm:(h + 1) * tm].astype(o_ref.dtype)


def _gqa_call(q, k, v, s_len, n_query):
    bsz, _, t_all, hd = q.shape
    tm = TOK_TILE
    tk = 2 * TOK_TILE
    lc = t_all - s_len
    q5 = q.reshape(bsz, GQA_KV_HEADS, GQA_GROUP, t_all, hd)
    return pl.pallas_call(
        functools.partial(_gqa_kernel, s_len, lc, tk),
        grid=(bsz, GQA_KV_HEADS, n_query // tm),
        in_specs=[pl.BlockSpec((None, None, GQA_GROUP, tm, hd), lambda b, g, i: (b, g, 0, i, 0)),
                  pl.BlockSpec((None, None, t_all, hd), lambda b, g, i: (b, g, 0, 0)),
                  pl.BlockSpec((None, None, t_all, hd), lambda b, g, i: (b, g, 0, 0))],
        out_specs=pl.BlockSpec((None, tm, GQA_GROUP * hd), lambda b, g, i: (b, i, g)),
        out_shape=jax.ShapeDtypeStruct((bsz, t_all, GQA_HEADS * hd), BF16),
        scratch_shapes=[pltpu.VMEM((s_len // tk, GQA_GROUP * tm, tk), F32),
                        pltpu.VMEM((GQA_GROUP * tm, lc), F32),
                        pltpu.VMEM((GQA_GROUP * tm, LANES), F32),
                        pltpu.VMEM((GQA_GROUP * tm, LANES), F32),
                        pltpu.VMEM((GQA_GROUP * tm, hd), F32)],
        name="gqa_attention",
        compiler_params=_cparams(("parallel", "parallel", "parallel")),
    )(q5, k, v)


def _na_bias_consts(rows):
    assert rows >= NA_WIN + NA_ROWS
    nrow, ncol = 2 * NA_KH - 1, 2 * NA_KW - 1
    rsel = np.zeros((3, NA_ROWS, NA_WIN, nrow), np.float32)
    mrow = np.zeros((3, NA_ROWS, NA_WIN), bool)
    for kind, r0 in enumerate((0, NA_ROWS, rows - NA_ROWS)):
        kr0 = int(np.clip(r0 - NA_KH // 2, 0, rows - NA_WIN))
        for qr in range(NA_ROWS):
            r = r0 + qr
            rs = int(np.clip(r - NA_KH // 2, 0, rows - NA_KH))
            for kr in range(NA_WIN):
                ka = kr0 + kr
                if rs <= ka < rs + NA_KH:
                    mrow[kind, qr, kr] = True
                    rsel[kind, qr, kr, ka - r + NA_KH - 1] = 1.0
    csel = np.zeros((GRID_W, GRID_W, ncol), np.float32)
    mcol = np.zeros((GRID_W, GRID_W), bool)
    for w in range(GRID_W):
        cs = int(np.clip(w - NA_KW // 2, 0, GRID_W - NA_KW))
        for kc in range(cs, cs + NA_KW):
            mcol[w, kc] = True
            csel[w, kc, kc - w + NA_KW - 1] = 1.0
    mask = mrow[:, :, None, :, None] & mcol[None, None, :, None, :]
    return rsel, csel, mask.reshape(3, TOK_TILE, NA_WIN * GRID_W)


def _na_bias_tables(na_rpb, rows):
    rsel, csel, mask = _na_bias_consts(rows)
    depth, heads = na_rpb.shape[:2]
    a = jnp.einsum("kqra,lhab->lkhqrb", jnp.asarray(rsel), na_rpb.astype(F32), precision=HIGHEST)
    t = jnp.einsum("lkhqrb,wcb->lkhqwrc", a, jnp.asarray(csel), precision=HIGHEST)
    t = t.reshape(depth, 3, heads, TOK_TILE, NA_WIN * GRID_W)
    return jnp.where(jnp.asarray(mask)[None, :, None], t, NEG)


def _softmax_pv(parts):
    m = None
    for s, _ in parts:
        mm = jnp.max(s, axis=1, keepdims=True)
        m = mm if m is None else jnp.maximum(m, mm)
    l = 0.0
    o = 0.0
    for s, v in parts:
        p = jnp.exp(s - m)
        l = l + jnp.sum(p, axis=1, keepdims=True)
        o = o + _dot(p.astype(BF16), v)
    return o / l


def _na_kernel(s_len, lc, rows, q_ref, k_ref, v_ref, tab_ref, o_ref):
    i = pl.program_id(1)
    nblk = s_len // TOK_TILE
    nloc = NA_WIN * GRID_W

    @pl.when(i < nblk)
    def _():
        kind = jnp.where(i == 0, 0, jnp.where(i == nblk - 1, 2, 1))
        kr0 = jnp.clip(i * NA_ROWS - NA_KH // 2, 0, rows - NA_WIN)
        ks = pl.multiple_of(kr0 * GRID_W, GRID_W)
        for h in range(NA_HEADS):
            q = q_ref[h]
            s_loc = _dot_nt(q, k_ref[h, pl.ds(ks, nloc), :]) + tab_ref[kind, h]
            s_ctx = _dot_nt(q, k_ref[h, pl.ds(s_len, lc), :])
            o = _softmax_pv([(s_loc, v_ref[h, pl.ds(ks, nloc), :]),
                             (s_ctx, v_ref[h, pl.ds(s_len, lc), :])])
            o_ref[:, h * HEAD_DIM:(h + 1) * HEAD_DIM] = o.astype(o_ref.dtype)

    @pl.when(i >= nblk)
    def _():
        for h in range(NA_HEADS):
            q = q_ref[h]
            s_ctx = _dot_nt(q, k_ref[h, pl.ds(s_len, lc), :])
            o = _softmax_pv([(s_ctx, v_ref[h, pl.ds(s_len, lc), :])])
            o_ref[:, h * HEAD_DIM:(h + 1) * HEAD_DIM] = o.astype(o_ref.dtype)


def _na_call(nq, nk, nv, table, s_len, n_query):
    bsz, heads, t_all, hd = nq.shape
    tm = TOK_TILE
    rows = s_len // GRID_W
    whole = pl.BlockSpec((None, heads, t_all, hd), lambda b, i: (b, 0, 0, 0))
    return pl.pallas_call(
        functools.partial(_na_kernel, s_len, t_all - s_len, rows),
        grid=(bsz, n_query // tm),
        in_specs=[pl.BlockSpec((None, heads, tm, hd), lambda b, i: (b, 0, i, 0)),
                  whole, whole,
                  pl.BlockSpec(table.shape, lambda b, i: (0, 0, 0, 0))],
        out_specs=pl.BlockSpec((None, tm, heads * hd), lambda b, i: (b, i, 0)),
        out_shape=jax.ShapeDtypeStruct((bsz, t_all, heads * hd), BF16),
        name="neighbourhood_attention",
        compiler_params=_cparams(("parallel", "parallel")),
    )(nq, nk, nv, table)


def _hy_filter_kernel(z_ref, t_ref, side_ref, w1_ref, b1_ref, w2_ref, b2_ref, w3_ref, b3_ref,
                      wo_ref, fr_ref, dl_ref, o_ref):
    om = fr_ref[...]
    h = jnp.sin(om * (_dot(z_ref[...], w1_ref[...], precision=HIGHEST) + b1_ref[...]))
    h = jnp.sin(om * (_dot(h, w2_ref[...], precision=HIGHEST) + b2_ref[...]))
    h = jnp.sin(om * (_dot(h, w3_ref[...], precision=HIGHEST) + b3_ref[...]))
    ht = _dot_nt(wo_ref[...], h, precision=HIGHEST)
    c = o_ref.shape[0]
    decay = jnp.exp(-t_ref[...] * jnp.abs(dl_ref[...]))
    side = side_ref[...]
    o_ref[...] = decay * (jnp.where(side > 0.5, ht[:c], 0.0) + jnp.where(side < -0.5, ht[c:], 0.0))


def _hy_filter_consts(length):
    n = np.arange(2 * length)
    pos = np.where(n < length, n, 2 * length - n)
    pos = np.where(n == length, 0, pos)
    side = np.where(n < length, 1.0, -1.0)
    side = np.where(n == length, 0.0, side)
    t = np.linspace(0.0, 1.0, length)[pos]
    bands = (HY_EMB_DIM - 1) // 2
    w = 2.0 * math.pi * pos / length
    fr = np.linspace(1e-4, bands - 1, bands)
    z = np.concatenate([t[:, None], np.cos(fr[None, :] * w[:, None]), -np.sin(fr[None, :] * w[:, None])], axis=-1)
    zp = np.zeros((2 * length, 64), np.float32)
    zp[:, :HY_EMB_DIM] = z
    return zp, t.astype(np.float32)[None, :], side.astype(np.float32)[None, :]


def _hy_filter_call(length, w1, b1, w2, b2, w3, b3, wout, freq):
    f = w2.shape[0]
    c = wout.shape[1] // 2
    zp, t, side = _hy_filter_consts(length)
    w1p = jnp.zeros((64, f), F32).at[:HY_EMB_DIM].set(w1)
    deltas = np.linspace(math.log(HY_DECAY_TARGET) / HY_SLOW_DECAY,
                         math.log(HY_DECAY_TARGET) / HY_FAST_DECAY, c).astype(np.float32)[:, None]
    args = (jnp.asarray(zp), jnp.asarray(t), jnp.asarray(side), w1p, b1[None], w2, b2[None], w3, b3[None],
            wout.T, freq[None], jnp.asarray(deltas))
    return pl.pallas_call(
        _hy_filter_kernel,
        in_specs=[pl.BlockSpec(a.shape, lambda: (0,) * a.ndim) for a in args],
        out_specs=pl.BlockSpec((c, 2 * length), lambda: (0, 0)),
        out_shape=jax.ShapeDtypeStruct((c, 2 * length), F32),
        grid=(),
        name="hyena_filter",
        compiler_params=pltpu.CompilerParams(vmem_limit_bytes=VMEM_LIMIT),
    )(*args)


def _bitrev(n):
    bits = int(math.log2(n))
    return np.array([int(format(i, "0%db" % bits)[::-1], 2) if bits else 0 for i in range(n)])


def _fft_consts(n1, ct):
    n = n1 * FFT_CHUNK
    stages = max(int(math.log2(n1)), 1)
    tw = np.zeros((2, stages, max(n1 // 2, 1)), np.float32)
    for s in range(int(math.log2(n1))):
        half = n1 >> (s + 1)
        ang = -2.0 * math.pi * np.arange(half) / (2 * half)
        tw[0, s, :half] = np.cos(ang)
        tw[1, s, :half] = np.sin(ang)
    k1 = _bitrev(n1)
    ang = -2.0 * math.pi * (np.arange(FFT_CHUNK)[None, :] * k1[:, None]) / n
    twr = np.repeat(np.cos(ang), ct, axis=0).astype(np.float32)
    twi = np.repeat(np.sin(ang), ct, axis=0).astype(np.float32)
    a = -2.0 * math.pi * np.outer(np.arange(FFT_CHUNK), np.arange(FFT_CHUNK)) / FFT_CHUNK
    wr, wi = np.cos(a), np.sin(a)
    wf = np.block([[wr, wi], [-wi, wr]]).astype(np.float32)
    wb = np.block([[wr, -wi], [wi, wr]]).astype(np.float32)
    return tw, twr, twi, _split_bf16(wf), _split_bf16(wb)


def _split_bf16(w):
    w = jnp.asarray(w, F32)
    hi = w.astype(BF16)
    return jnp.stack([hi, (w - hi.astype(F32)).astype(BF16)])


def _slab_fft(zr, zi, tw_ref, n1, ct, inverse, half_input=False, half_output=False):
    nst = int(math.log2(n1))
    order = range(nst - 1, -1, -1) if inverse else range(nst)
    for s in order:
        half = n1 >> (s + 1)
        lh = int(math.log2(half))
        first = s == 0

        def body(idx, carry, s=s, half=half, lh=lh, first=first):
            blk = idx >> lh
            j = idx & (half - 1)
            r0 = pl.multiple_of(((blk << (lh + 1)) + j) * ct, ct)
            r1 = pl.multiple_of(r0 + half * ct, ct)
            wr = tw_ref[0, s, j]
            wi = tw_ref[1, s, j]
            ar = zr[pl.ds(r0, ct), :]
            ai = zi[pl.ds(r0, ct), :]
            if not inverse:
                if first and half_input:
                    zr[pl.ds(r1, ct), :] = ar * wr - ai * wi
                    zi[pl.ds(r1, ct), :] = ar * wi + ai * wr
                else:
                    br = zr[pl.ds(r1, ct), :]
                    bi = zi[pl.ds(r1, ct), :]
                    dr = ar - br
                    di = ai - bi
                    zr[pl.ds(r0, ct), :] = ar + br
                    zi[pl.ds(r0, ct), :] = ai + bi
                    zr[pl.ds(r1, ct), :] = dr * wr - di * wi
                    zi[pl.ds(r1, ct), :] = dr * wi + di * wr
            else:
                br = zr[pl.ds(r1, ct), :]
                bi = zi[pl.ds(r1, ct), :]
                tr = br * wr + bi * wi
                ti = bi * wr - br * wi
                zr[pl.ds(r0, ct), :] = ar + tr
                zi[pl.ds(r0, ct), :] = ai + ti
                if not (first and half_output):
                    zr[pl.ds(r1, ct), :] = ar - tr
                    zi[pl.ds(r1, ct), :] = ai - ti
            return carry

        lax.fori_loop(0, n1 // 2, body, 0, unroll=2)


def _chunk_dft(xr, xi, w_ref):
    x = jnp.concatenate([xr, xi], axis=1)
    x_hi = x.astype(BF16)
    x_lo = (x - x_hi.astype(F32)).astype(BF16)
    y = _dot(x_hi, w_ref[0]) + (_dot(x_lo, w_ref[0]) + _dot(x_hi, w_ref[1]))
    return y[:, :FFT_CHUNK], y[:, FFT_CHUNK:]


def _fft_groups(n1, ct):
    g = min(8, n1)
    return n1 // g, g * ct


def _hy_spec_kernel(n1, ct, f_ref, tw_ref, twr_ref, twi_ref, wf_ref, fr_ref, fi_ref, zr, zi):
    for s in range(n1):
        zr[s * ct:(s + 1) * ct, :] = f_ref[:, s * FFT_CHUNK:(s + 1) * FFT_CHUNK]
    zi[...] = jnp.zeros(zi.shape, F32)
    _slab_fft(zr, zi, tw_ref, n1, ct, inverse=False)
    ngroups, gr = _fft_groups(n1, ct)

    def body(g, carry):
        r0 = pl.multiple_of(g * gr, gr)
        a = zr[pl.ds(r0, gr), :]
        b = zi[pl.ds(r0, gr), :]
        tr = twr_ref[pl.ds(r0, gr), :]
        ti = twi_ref[pl.ds(r0, gr), :]
        yr, yi = _chunk_dft(a * tr - b * ti, a * ti + b * tr, wf_ref)
        fr_ref[pl.ds(r0, gr), :] = yr
        fi_ref[pl.ds(r0, gr), :] = yi
        return carry

    lax.fori_loop(0, ngroups, body, 0, unroll=min(2, ngroups))


def _hy_spec_call(filt_t):
    c, n = filt_t.shape
    n1 = n // FFT_CHUNK
    ct = HY_CT
    tw, twr, twi, wf, _ = _fft_consts(n1, ct)
    const2 = lambda i: (0, 0)
    out = jax.ShapeDtypeStruct((c // ct, n1 * ct, FFT_CHUNK), F32)
    ospec = pl.BlockSpec((None, n1 * ct, FFT_CHUNK), lambda i: (i, 0, 0))
    return pl.pallas_call(
        functools.partial(_hy_spec_kernel, n1, ct),
        grid=(c // ct,),
        in_specs=[pl.BlockSpec((ct, n), lambda i: (i, 0)),
                  pl.BlockSpec(memory_space=pltpu.SMEM),
                  pl.BlockSpec(twr.shape, const2), pl.BlockSpec(twi.shape, const2),
                  pl.BlockSpec(wf.shape, lambda i: (0, 0, 0))],
        out_specs=[ospec, ospec],
        out_shape=[out, out],
        scratch_shapes=[pltpu.VMEM((n1 * ct, FFT_CHUNK), F32), pltpu.VMEM((n1 * ct, FFT_CHUNK), F32)],
        name="hyena_filter_spectrum",
        compiler_params=_cparams(("parallel",)),
    )(filt_t, jnp.asarray(tw), jnp.asarray(twr), jnp.asarray(twi), jnp.asarray(wf))


def _hy_conv_kernel(n1, ct, length, v_ref, x1_ref, x2_ref, cwv_ref, cw1_ref, cw2_ref, skip_ref,
                    fr_ref, fi_ref, tw_ref, twr_ref, twi_ref, wf_ref, wb_ref, o_ref,
                    zr, zi, u_scr, g_scr):
    lane = lax.broadcasted_iota(jnp.int32, (1, length), 1)

    def short_conv(x, cw_ref):
        xm = jnp.where(lane >= 1, pltpu.roll(x, 1, 1), 0.0)
        xp = jnp.where(lane <= length - 2, pltpu.roll(x, length - 1, 1), 0.0)
        return xm * cw_ref[:, 0:1] + x * cw_ref[:, 1:2] + xp * cw_ref[:, 2:3] + cw_ref[:, 3:4]

    for a in range(2):
        u_scr[a] = short_conv(v_ref[a], cwv_ref) * short_conv(x1_ref[a], cw1_ref)
        g_scr[a] = short_conv(x2_ref[a], cw2_ref)
    nz = length // FFT_CHUNK
    for s in range(nz):
        zr[s * ct:(s + 1) * ct, :] = u_scr[0, :, s * FFT_CHUNK:(s + 1) * FFT_CHUNK]
        zi[s * ct:(s + 1) * ct, :] = u_scr[1, :, s * FFT_CHUNK:(s + 1) * FFT_CHUNK]
    _slab_fft(zr, zi, tw_ref, n1, ct, inverse=False, half_input=True)
    ngroups, gr = _fft_groups(n1, ct)

    def body(g, carry):
        r0 = pl.multiple_of(g * gr, gr)
        a = zr[pl.ds(r0, gr), :]
        b = zi[pl.ds(r0, gr), :]
        tr = twr_ref[pl.ds(r0, gr), :]
        ti = twi_ref[pl.ds(r0, gr), :]
        yr, yi = _chunk_dft(a * tr - b * ti, a * ti + b * tr, wf_ref)
        fr = fr_ref[pl.ds(r0, gr), :]
        fi = fi_ref[pl.ds(r0, gr), :]
        qr, qi = _chunk_dft(yr * fr - yi * fi, yr * fi + yi * fr, wb_ref)
        zr[pl.ds(r0, gr), :] = qr * tr + qi * ti
        zi[pl.ds(r0, gr), :] = qi * tr - qr * ti
        return carry

    lax.fori_loop(0, ngroups, body, 0, unroll=min(2, ngroups))
    _slab_fft(zr, zi, tw_ref, n1, ct, inverse=True, half_output=True)
    inv_n = 1.0 / (n1 * FFT_CHUNK)
    skip = skip_ref[...]
    for s in range(nz):
        sl = slice(s * FFT_CHUNK, (s + 1) * FFT_CHUNK)
        for a, z in ((0, zr), (1, zi)):
            y = z[s * ct:(s + 1) * ct, :] * inv_n
            u = u_scr[a, :, sl]
            o_ref[a, :, sl] = (g_scr[a, :, sl] * (y + u * skip)).astype(o_ref.dtype)


def _hy_conv_call(hy_t, spec_r, spec_i, cw, skip, length, lane_block, prev=None):
    _, bh, c3, t_all = hy_t.shape
    c = c3 // 3
    ct = HY_CT
    nct = c // ct
    n1 = 2 * length // FFT_CHUNK
    tw, twr, twi, wf, wb = _fft_consts(n1, ct)
    const2 = lambda p, i: (0, 0)
    xin = lambda k: pl.BlockSpec((2, None, ct, length), lambda p, i, k=k: (0, p, k * nct + i, lane_block))
    cwin = lambda k: pl.BlockSpec((ct, 4), lambda p, i, k=k: (k * nct + i, 0))
    fin = pl.BlockSpec((None, n1 * ct, FFT_CHUNK), lambda p, i: (i, 0, 0))
    in_specs = [xin(0), xin(1), xin(2), cwin(0), cwin(1), cwin(2),
                pl.BlockSpec((ct, 1), lambda p, i: (i, 0)),
                fin, fin,
                pl.BlockSpec(memory_space=pltpu.SMEM),
                pl.BlockSpec(twr.shape, const2), pl.BlockSpec(twi.shape, const2),
                pl.BlockSpec(wf.shape, lambda p, i: (0, 0, 0)),
                pl.BlockSpec(wb.shape, lambda p, i: (0, 0, 0))]
    args = [hy_t, hy_t, hy_t, cw, cw, cw, skip, spec_r, spec_i, jnp.asarray(tw), jnp.asarray(twr),
            jnp.asarray(twi), jnp.asarray(wf), jnp.asarray(wb)]
    kern = functools.partial(_hy_conv_kernel, n1, ct, length)
    aliases = {}
    if prev is not None:
        in_specs.append(pl.BlockSpec(memory_space=pl.ANY))
        args.append(prev)
        aliases = {len(args) - 1: 0}
        kern = lambda *refs, kern=kern: kern(*refs[:14], *refs[15:])
    return pl.pallas_call(
        kern,
        grid=(bh, nct),
        in_specs=in_specs,
        out_specs=pl.BlockSpec((2, None, ct, length), lambda p, i: (0, p, i, lane_block)),
        out_shape=jax.ShapeDtypeStruct((2, bh, c, t_all), BF16),
        scratch_shapes=[pltpu.VMEM((n1 * ct, FFT_CHUNK), F32), pltpu.VMEM((n1 * ct, FFT_CHUNK), F32),
                        pltpu.VMEM((2, ct, length), F32), pltpu.VMEM((2, ct, length), F32)],
        input_output_aliases=aliases,
        name="hyena_conv_%d" % length,
        compiler_params=_cparams(("parallel", "parallel")),
    )(*args)


def _outproj_kernel(n_lat_tiles, ctx_row, x_ref, hy_ref, ga_ref, na_ref, w_ref, mod_ref, o_ref):
    b = pl.program_id(0)
    j = pl.program_id(1)
    d = x_ref.shape[-1]
    row = jnp.where(j < n_lat_tiles, b, ctx_row)
    hy = hy_ref[...].astype(F32).T.astype(BF16)
    mix = jnp.concatenate([hy, ga_ref[...], na_ref[...]], axis=1)
    o_ref[...] = x_ref[...] + _mod_row(mod_ref, row, 2, d) * _dot(mix, w_ref[...])


def _outproj_call(xa, hyo_t, gqa, na, w_out, mod_l, s_len, n_tok):
    bsz, _, d = xa.shape
    tm = TOK_TILE
    c = hyo_t.shape[1]
    const = lambda b, j: (0, 0)
    tok = lambda n: pl.BlockSpec((None, tm, n), lambda b, j: (b, j, 0))
    return pl.pallas_call(
        functools.partial(_outproj_kernel, s_len // tm, bsz),
        grid=(bsz, n_tok // tm),
        in_specs=[tok(d), pl.BlockSpec((None, c, tm), lambda b, j: (b, 0, j)),
                  tok(gqa.shape[-1]), tok(na.shape[-1]),
                  pl.BlockSpec(w_out.shape, const), pl.BlockSpec(mod_l.shape, const)],
        out_specs=tok(d),
        out_shape=jax.ShapeDtypeStruct((bsz, n_tok, d), F32),
        name="out_proj",
        compiler_params=_cparams(("parallel", "parallel")),
    )(xa, hyo_t, gqa, na, w_out, mod_l)


def _ffn_kernel(tiles_per_batch, n_lat_tiles, ctx_row, th, x_ref, mod_ref, g_ref, wi_ref, wo_ref,
                o_ref, h_scr, act_scr):
    i = pl.program_id(0)
    d = x_ref.shape[-1]
    f = wo_ref.shape[0]
    groups = x_ref.shape[0] // TOK_TILE

    def mod_row(s):
        gi = i * groups + s
        jj = gi % tiles_per_batch
        return jnp.where(jj < n_lat_tiles, gi // tiles_per_batch, ctx_row)

    for s in range(groups):
        row = mod_row(s)
        sl = slice(s * TOK_TILE, (s + 1) * TOK_TILE)
        h = _norm_mod(x_ref[sl, :], g_ref[...], _mod_row(mod_ref, row, 3, d), _mod_row(mod_ref, row, 4, d))
        h_scr[sl, :] = h.astype(BF16)
    h = h_scr[...]
    for j in range(f // th):
        gate = _dot(h, wi_ref[:, j * th:(j + 1) * th])
        up = _dot(h, wi_ref[:, f + j * th:f + (j + 1) * th])
        act_scr[:, j * th:(j + 1) * th] = (gate * jax.nn.sigmoid(gate) * up).astype(BF16)
    y = _dot(act_scr[...], wo_ref[...])
    for s in range(groups):
        row = mod_row(s)
        sl = slice(s * TOK_TILE, (s + 1) * TOK_TILE)
        o_ref[sl, :] = x_ref[sl, :] + _mod_row(mod_ref, row, 5, d) * y[sl, :]


def _ffn_call(xa, mod_l, g, w_in, w_out, s_len):
    bsz, t_all, d = xa.shape
    f = w_out.shape[0]
    r = bsz * t_all
    tm = 1024 if r % 1024 == 0 else TOK_TILE
    th = 256
    const = lambda i: (0, 0)
    resident = lambda a: pl.BlockSpec(a.shape, const, pipeline_mode=pl.Buffered(1))
    out = pl.pallas_call(
        functools.partial(_ffn_kernel, t_all // TOK_TILE, s_len // TOK_TILE, bsz, th),
        grid=(r // tm,),
        in_specs=[pl.BlockSpec((tm, d), lambda i: (i, 0)),
                  pl.BlockSpec(mod_l.shape, const),
                  pl.BlockSpec((1, d), const),
                  resident(w_in), resident(w_out)],
        out_specs=pl.BlockSpec((tm, d), lambda i: (i, 0)),
        out_shape=jax.ShapeDtypeStruct((r, d), F32),
        scratch_shapes=[pltpu.VMEM((tm, d), BF16), pltpu.VMEM((tm, f), BF16)],
        name="swiglu_ffn",
        compiler_params=_cparams(("parallel",)),
    )(xa.reshape(r, d), mod_l, g, w_in, w_out)
    return out.reshape(bsz, t_all, d)


def _final_kernel(x_ref, g_ref, o_ref):
    x = x_ref[...]
    ms = jnp.mean(x * x, axis=-1, keepdims=True)
    o_ref[...] = x * lax.rsqrt(ms + NORM_EPS) * g_ref[...]


def _final_call(xa, g, s_len):
    bsz, _, d = xa.shape
    tm = TOK_TILE
    return pl.pallas_call(
        _final_kernel,
        grid=(bsz, s_len // tm),
        in_specs=[pl.BlockSpec((None, tm, d), lambda b, j: (b, j, 0)),
                  pl.BlockSpec((1, d), lambda b, j: (0, 0))],
        out_specs=pl.BlockSpec((None, tm, d), lambda b, j: (b, j, 0)),
        out_shape=jax.ShapeDtypeStruct((bsz, s_len, d), F32),
        name="final_norm",
        compiler_params=_cparams(("parallel", "parallel")),
    )(xa, g)


def _rope_tables(s_len, lc):
    pos = np.arange(s_len)
    row = (pos // GRID_W).astype(np.float64)
    col = (pos % GRID_W).astype(np.float64)
    n_f = HEAD_DIM // 4
    inv = (ROPE_THETA ** (-np.arange(n_f, dtype=np.float32) / n_f)).astype(np.float64)
    ang = np.concatenate([row[:, None] * inv, col[:, None] * inv], axis=-1).astype(np.float32).astype(np.float64)
    lane = np.arange(LANES)
    sign = np.where((lane % HEAD_DIM) < HEAD_DIM // 2, -1.0, 1.0)
    cos = np.cos(ang)[:, lane % (HEAD_DIM // 2)]
    sin = np.sin(ang)[:, lane % (HEAD_DIM // 2)] * sign[None, :]
    cos = np.concatenate([cos, np.ones((lc, LANES))], axis=0).astype(np.float32)
    sin = np.concatenate([sin, np.zeros((lc, LANES))], axis=0).astype(np.float32)
    return jnp.asarray(cos), jnp.asarray(sin)


def kernel(x, c, ctx, c_ctx, w_mod, b_mod, g_mix, g_ffn, w_in, w_out, hy_conv_w, hy_conv_b,
           hy_f_w1, hy_f_b1, hy_f_w2, hy_f_b2, hy_f_w3, hy_f_b3, hy_f_wout, hy_f_freq, hy_skip,
           qk_g_q, qk_g_k, na_rpb, w_ffn_in, w_ffn_out, g_final):
    bsz, s_len, d = x.shape
    lc = ctx.shape[1]
    depth = w_mod.shape[0]
    t_all = s_len + lc
    hyw = hy_skip.shape[1]
    hy_cols = 3 * hyw
    assert bsz % 2 == 0 and s_len % TOK_TILE == 0 and lc == TOK_TILE and s_len % lc == 0

    nb = -(-(bsz + 1) // 8) * 8
    cs = jnp.zeros((nb, d), F32).at[:bsz].set(c).at[bsz].set(c_ctx)
    mod = _mod_call(cs, w_mod, b_mod)

    cos_t, sin_t = _rope_tables(s_len, lc)
    tables = _na_bias_tables(na_rpb, s_len // GRID_W)
    w_hy_t = jnp.swapaxes(w_in[:, :, :hy_cols], 1, 2).astype(BF16)
    w_rest = w_in[:, :, hy_cols:].astype(BF16)
    w_out_b = w_out.astype(BF16)
    w_fi = w_ffn_in.astype(BF16)
    w_fo = w_ffn_out.astype(BF16)
    cw = jnp.concatenate([jnp.swapaxes(hy_conv_w, 1, 2), hy_conv_b[:, :, None]], axis=2)
    per = LANES // HEAD_DIM

    xa = jnp.concatenate([x, ctx], axis=1)
    for l in range(depth):
        last = l == depth - 1
        gq2 = jnp.tile(qk_g_q[l], per)[None]
        gk2 = jnp.tile(qk_g_k[l], per)[None]
        hy_t, q, k, v, nq, nk, nv = _inproj_call(xa, mod[l], g_mix[l][None], w_hy_t[l], w_rest[l],
                                                  gq2, gk2, cos_t, sin_t, s_len)
        n_tok = s_len if last else t_all
        gqa = _gqa_call(q, k, v, s_len, n_tok)
        na = _na_call(nq, nk, nv, tables[l], s_len, n_tok)

        fargs = (hy_f_w1[l], hy_f_b1[l], hy_f_w2[l], hy_f_b2[l], hy_f_w3[l], hy_f_b3[l],
                 hy_f_wout[l], hy_f_freq[l])
        hy_p = hy_t.reshape(2, bsz // 2, hy_cols, t_all)
        skip = hy_skip[l][:, None]
        fr, fi = _hy_spec_call(_hy_filter_call(s_len, *fargs))
        hyo = _hy_conv_call(hy_p, fr, fi, cw[l], skip, s_len, 0)
        if not last:
            frc, fic = _hy_spec_call(_hy_filter_call(lc, *fargs))
            hyo = _hy_conv_call(hy_p, frc, fic, cw[l], skip, lc, s_len // lc, prev=hyo)
        hyo = hyo.reshape(bsz, hyw, t_all)

        xa = _outproj_call(xa, hyo, gqa, na, w_out_b[l], mod[l], s_len, n_tok)
        xa = _ffn_call(xa, mod[l], g_ffn[l][None], w_fi[l], w_fo[l], s_len)
    return _final_call(xa, g_final[None], s_len)
```

```python
import functools
import math

import numpy as np
import jax
import jax.numpy as jnp
from jax import lax
from jax.experimental import pallas as pl
from jax.experimental.pallas import tpu as pltpu

F32 = jnp.float32
BF16 = jnp.bfloat16
HIGHEST = lax.Precision.HIGHEST

GRID_W = 64
HEAD_DIM = 64
GQA_HEADS = 8
GQA_KV_HEADS = 2
GQA_GROUP = GQA_HEADS // GQA_KV_HEADS
NA_HEADS = 4
NA_KH = 8
NA_KW = 16
ROPE_THETA = 10000.0
HY_EMB_DIM = 33
HY_DECAY_TARGET = 1e-2
HY_FAST_DECAY = 0.3
HY_SLOW_DECAY = 1.5
NORM_EPS = 1e-6
ATTN_SCALE = HEAD_DIM ** -0.5
NEG = -1e30
LOG2E = math.log2(math.e)

LANES = 128
TOK_TILE = 256
FFT_CHUNK = LANES
HY_CT = 64
NA_ROWS = TOK_TILE // GRID_W
NA_WIN = NA_ROWS + NA_KH
VMEM_LIMIT = 56 * 1024 * 1024


def _cparams(sem):
    return pltpu.CompilerParams(dimension_semantics=sem, vmem_limit_bytes=VMEM_LIMIT)


def _dot(a, b, **kw):
    return jnp.dot(a, b, preferred_element_type=F32, **kw)


def _dot_nt(a, b, **kw):
    return lax.dot_general(a, b, (((1,), (1,)), ((), ())), preferred_element_type=F32, **kw)


def _mod_kernel(cs_ref, w_ref, b_ref, o_ref):
    cs = cs_ref[...]
    s = cs * jax.nn.sigmoid(cs)
    o_ref[...] = _dot(s, w_ref[...], precision=HIGHEST) + b_ref[...]


def _mod_call(cs, w_mod, b_mod):
    depth, d, n = w_mod.shape
    nb = cs.shape[0]
    tn = 1024
    return pl.pallas_call(
        _mod_kernel,
        grid=(depth, n // tn),
        in_specs=[pl.BlockSpec((nb, d), lambda l, j: (0, 0)),
                  pl.BlockSpec((None, d, tn), lambda l, j: (l, 0, j)),
                  pl.BlockSpec((None, 1, tn), lambda l, j: (l, 0, j))],
        out_specs=pl.BlockSpec((None, nb, tn), lambda l, j: (l, 0, j)),
        out_shape=jax.ShapeDtypeStruct((depth, nb, n), F32),
        name="adaln_mod",
        compiler_params=_cparams(("parallel", "parallel")),
    )(cs, w_mod, b_mod.reshape(depth, 1, n))


def _mod_row(mod_ref, row, k, d):
    return mod_ref[pl.ds(row, 1), k * d:(k + 1) * d]


def _norm_mod(x, g, shift, scale):
    ms = jnp.mean(x * x, axis=-1, keepdims=True)
    y = x * lax.rsqrt(ms + NORM_EPS) * g
    return y * (1.0 + scale) + shift


def _batch_group(bsz):
    return 4 if bsz % 4 == 0 else 2


def _inproj_kernel(n_lat_tiles, ctx_row, x_ref, mod_ref, g_ref, why_ref, wr_ref, gq_ref, gk_ref,
                   cos_ref, sin_ref, hy_ref, q_ref, k_ref, v_ref, nq_ref, nk_ref, nv_ref, h_scr):
    j = pl.program_id(1)
    nbat, tm, d = x_ref.shape
    for s in range(nbat):
        row = jnp.where(j < n_lat_tiles, pl.program_id(0) * nbat + s, ctx_row)
        h = _norm_mod(x_ref[s], g_ref[...], _mod_row(mod_ref, row, 0, d), _mod_row(mod_ref, row, 1, d))
        h_scr[s * tm:(s + 1) * tm, :] = h.astype(BF16)
    h = h_scr[...]
    hy = _dot_nt(why_ref[...], h)
    p_all = _dot(h, wr_ref[...])
    for s in range(nbat):
        hy_ref[s] = hy[:, s * tm:(s + 1) * tm]
        _inproj_heads(p_all[s * tm:(s + 1) * tm], gq_ref, gk_ref, cos_ref, sin_ref,
                      q_ref.at[s], k_ref.at[s], v_ref.at[s], nq_ref.at[s], nk_ref.at[s], nv_ref.at[s])


def _inproj_heads(p, gq_ref, gk_ref, cos_ref, sin_ref, q_ref, k_ref, v_ref, nq_ref, nk_ref, nv_ref):
    ri = lax.broadcasted_iota(jnp.int32, (LANES, LANES), 0) // HEAD_DIM
    ci = lax.broadcasted_iota(jnp.int32, (LANES, LANES), 1) // HEAD_DIM
    avg = jnp.where(ri == ci, 1.0 / HEAD_DIM, 0.0).astype(BF16)
    lane = lax.broadcasted_iota(jnp.int32, (1, LANES), 1)
    first_half = (lane % HEAD_DIM) < (HEAD_DIM // 2)
    cos = cos_ref[...]
    sin = sin_ref[...]

    def norm_rope(xc, g, scale):
        ms = _dot((xc * xc).astype(BF16), avg)
        xn = xc * lax.rsqrt(ms + NORM_EPS) * g
        partner = jnp.where(first_half, pltpu.roll(xn, LANES - HEAD_DIM // 2, 1),
                            pltpu.roll(xn, HEAD_DIM // 2, 1))
        return (xn * cos + partner * sin) * scale

    per = LANES // HEAD_DIM
    nq_chunks = GQA_HEADS // per
    for c in range(nq_chunks):
        y = norm_rope(p[:, c * LANES:(c + 1) * LANES], gq_ref[...], ATTN_SCALE * LOG2E).astype(BF16)
        for e in range(per):
            q_ref[c * per + e] = y[:, e * HEAD_DIM:(e + 1) * HEAD_DIM]
    off = nq_chunks * LANES
    y = norm_rope(p[:, off:off + LANES], gk_ref[...], 1.0).astype(BF16)
    for e in range(GQA_KV_HEADS):
        k_ref[e] = y[:, e * HEAD_DIM:(e + 1) * HEAD_DIM]
    off += LANES
    for e in range(GQA_KV_HEADS):
        v_ref[e] = p[:, off + e * HEAD_DIM:off + (e + 1) * HEAD_DIM].astype(BF16)
    off += LANES
    for ref, scale in ((nq_ref, ATTN_SCALE), (nk_ref, 1.0), (nv_ref, 1.0)):
        for e in range(NA_HEADS):
            ref[e] = (p[:, off + e * HEAD_DIM:off + (e + 1) * HEAD_DIM] * scale).astype(BF16)
        off += NA_HEADS * HEAD_DIM


def _inproj_call(xa, mod_l, g, w_hy_t, w_rest, gq2, gk2, cos_t, sin_t, s_len):
    bsz, t_all, d = xa.shape
    tm = TOK_TILE
    nbat = _batch_group(bsz)
    nb = mod_l.shape[0]
    hyc = w_hy_t.shape[0]
    nr = w_rest.shape[1]
    grid = (bsz // nbat, t_all // tm)
    const = lambda b, j: (0, 0)
    resident = lambda shape: pl.BlockSpec(shape, const, pipeline_mode=pl.Buffered(1))
    heads = lambda n: pl.BlockSpec((nbat, n, tm, HEAD_DIM), lambda b, j: (b, 0, j, 0))
    hshape = lambda n: jax.ShapeDtypeStruct((bsz, n, t_all, HEAD_DIM), BF16)
    return pl.pallas_call(
        functools.partial(_inproj_kernel, s_len // tm, bsz),
        grid=grid,
        in_specs=[pl.BlockSpec((nbat, tm, d), lambda b, j: (b, j, 0)),
                  pl.BlockSpec((nb, mod_l.shape[1]), const),
                  pl.BlockSpec((1, d), const),
                  resident((hyc, d)),
                  resident((d, nr)),
                  pl.BlockSpec((1, LANES), const),
                  pl.BlockSpec((1, LANES), const),
                  pl.BlockSpec((tm, LANES), lambda b, j: (j, 0)),
                  pl.BlockSpec((tm, LANES), lambda b, j: (j, 0))],
        out_specs=[pl.BlockSpec((nbat, hyc, tm), lambda b, j: (b, 0, j)),
                   heads(GQA_HEADS), heads(GQA_KV_HEADS), heads(GQA_KV_HEADS),
                   heads(NA_HEADS), heads(NA_HEADS), heads(NA_HEADS)],
        out_shape=[jax.ShapeDtypeStruct((bsz, hyc, t_all), F32),
                   hshape(GQA_HEADS), hshape(GQA_KV_HEADS), hshape(GQA_KV_HEADS),
                   hshape(NA_HEADS), hshape(NA_HEADS), hshape(NA_HEADS)],
        scratch_shapes=[pltpu.VMEM((nbat * tm, d), BF16)],
        name="in_proj",
        compiler_params=_cparams(("parallel", "parallel")),
    )(xa, mod_l, g, w_hy_t, w_rest, gq2, gk2, cos_t, sin_t)


def _gqa_kernel(s_len, lc, tk, q_ref, k_ref, v_ref, o_ref, s_scr, sc_scr, m_scr, l_scr, acc_scr):
    i = pl.program_id(2)
    tm = q_ref.shape[1]
    rows = GQA_GROUP * tm
    q = q_ref[...].reshape(rows, HEAD_DIM)
    is_latent = i * tm < s_len
    n_chunks = s_len // tk
    m_scr[...] = jnp.full(m_scr.shape, NEG, F32)

    def scores(keys, dst):
        s = _dot_nt(q, keys)
        dst[...] = s
        m = m_scr[...]
        for g in range(s.shape[1] // LANES):
            m = jnp.maximum(m, s[:, g * LANES:(g + 1) * LANES])
        m_scr[...] = m

    @pl.when(is_latent)
    def _():
        for c in range(n_chunks):
            scores(k_ref[c * tk:(c + 1) * tk, :], s_scr.at[c])

    scores(k_ref[s_len:s_len + lc, :], sc_scr)
    m_scr[...] = jnp.broadcast_to(jnp.max(m_scr[...], axis=1, keepdims=True), m_scr.shape)
    l_scr[...] = jnp.zeros(l_scr.shape, F32)
    acc_scr[...] = jnp.zeros(acc_scr.shape, F32)

    def apply(src, values):
        s = src[...]
        m = m_scr[...]
        l = l_scr[...]
        ps = []
        for g in range(s.shape[1] // LANES):
            p = jnp.exp2(s[:, g * LANES:(g + 1) * LANES] - m)
            l = l + p
            ps.append(p.astype(BF16))
        l_scr[...] = l
        acc_scr[...] += _dot(jnp.concatenate(ps, axis=1), values)

    @pl.when(is_latent)
    def _():
        for c in range(n_chunks):
            apply(s_scr.at[c], v_ref[c * tk:(c + 1) * tk, :])

    apply(sc_scr, v_ref[s_len:s_len + lc, :])
    o = acc_scr[...] / jnp.sum(l_scr[...], axis=1, keepdims=True)
    for h in range(GQA_GROUP):
        o_ref[:, h * HEAD_DIM:(h + 1) * HEAD_DIM] = o[h * tm:(h + 1) * tm].astype(o_ref.dtype)


def _gqa_call(q, k, v, s_len, n_query):
    bsz, _, t_all, hd = q.shape
    tm = TOK_TILE
    tk = 2 * TOK_TILE
    lc = t_all - s_len
    q5 = q.reshape(bsz, GQA_KV_HEADS, GQA_GROUP, t_all, hd)
    return pl.pallas_call(
        functools.partial(_gqa_kernel, s_len, lc, tk),
        grid=(bsz, GQA_KV_HEADS, n_query // tm),
        in_specs=[pl.BlockSpec((None, None, GQA_GROUP, tm, hd), lambda b, g, i: (b, g, 0, i, 0)),
                  pl.BlockSpec((None, None, t_all, hd), lambda b, g, i: (b, g, 0, 0)),
                  pl.BlockSpec((None, None, t_all, hd), lambda b, g, i: (b, g, 0, 0))],
        out_specs=pl.BlockSpec((None, tm, GQA_GROUP * hd), lambda b, g, i: (b, i, g)),
        out_shape=jax.ShapeDtypeStruct((bsz, t_all, GQA_HEADS * hd), BF16),
        scratch_shapes=[pltpu.VMEM((s_len // tk, GQA_GROUP * tm, tk), F32),
                        pltpu.VMEM((GQA_GROUP * tm, lc), F32),
                        pltpu.VMEM((GQA_GROUP * tm, LANES), F32),
                        pltpu.VMEM((GQA_GROUP * tm, LANES), F32),
                        pltpu.VMEM((GQA_GROUP * tm, hd), F32)],
        name="gqa_attention",
        compiler_params=_cparams(("parallel", "parallel", "parallel")),
    )(q5, k, v)


def _na_bias_consts(rows):
    assert rows >= NA_WIN + NA_ROWS
    nrow, ncol = 2 * NA_KH - 1, 2 * NA_KW - 1
    rsel = np.zeros((3, NA_ROWS, NA_WIN, nrow), np.float32)
    mrow = np.zeros((3, NA_ROWS, NA_WIN), bool)
    for kind, r0 in enumerate((0, NA_ROWS, rows - NA_ROWS)):
        kr0 = int(np.clip(r0 - NA_KH // 2, 0, rows - NA_WIN))
        for qr in range(NA_ROWS):
            r = r0 + qr
            rs = int(np.clip(r - NA_KH // 2, 0, rows - NA_KH))
            for kr in range(NA_WIN):
                ka = kr0 + kr
                if rs <= ka < rs + NA_KH:
                    mrow[kind, qr, kr] = True
                    rsel[kind, qr, kr, ka - r + NA_KH - 1] = 1.0
    csel = np.zeros((GRID_W, GRID_W, ncol), np.float32)
    mcol = np.zeros((GRID_W, GRID_W), bool)
    for w in range(GRID_W):
        cs = int(np.clip(w - NA_KW // 2, 0, GRID_W - NA_KW))
        for kc in range(cs, cs + NA_KW):
            mcol[w, kc] = True
            csel[w, kc, kc - w + NA_KW - 1] = 1.0
    mask = mrow[:, :, None, :, None] & mcol[None, None, :, None, :]
    return rsel, csel, mask.reshape(3, TOK_TILE, NA_WIN * GRID_W)


def _na_bias_tables(na_rpb, rows):
    rsel, csel, mask = _na_bias_consts(rows)
    depth, heads = na_rpb.shape[:2]
    a = jnp.einsum("kqra,lhab->lkhqrb", jnp.asarray(rsel), na_rpb.astype(F32), precision=HIGHEST)
    t = jnp.einsum("lkhqrb,wcb->lkhqwrc", a, jnp.asarray(csel), precision=HIGHEST)
    t = t.reshape(depth, 3, heads, TOK_TILE, NA_WIN * GRID_W)
    return jnp.where(jnp.asarray(mask)[None, :, None], t, NEG)


def _softmax_pv(parts):
    m = None
    for s, _ in parts:
        mm = jnp.max(s, axis=1, keepdims=True)
        m = mm if m is None else jnp.maximum(m, mm)
    l = 0.0
    o = 0.0
    for s, v in parts:
        p = jnp.exp(s - m)
        l = l + jnp.sum(p, axis=1, keepdims=True)
        o = o + _dot(p.astype(BF16), v)
    return o / l


def _na_kernel(s_len, lc, rows, q_ref, k_ref, v_ref, tab_ref, o_ref):
    i = pl.program_id(1)
    nblk = s_len // TOK_TILE
    nloc = NA_WIN * GRID_W

    @pl.when(i < nblk)
    def _():
        kind = jnp.where(i == 0, 0, jnp.where(i == nblk - 1, 2, 1))
        kr0 = jnp.clip(i * NA_ROWS - NA_KH // 2, 0, rows - NA_WIN)
        ks = pl.multiple_of(kr0 * GRID_W, GRID_W)
        for h in range(NA_HEADS):
            q = q_ref[h]
            s_loc = _dot_nt(q, k_ref[h, pl.ds(ks, nloc), :]) + tab_ref[kind, h]
            s_ctx = _dot_nt(q, k_ref[h, pl.ds(s_len, lc), :])
            o = _softmax_pv([(s_loc, v_ref[h, pl.ds(ks, nloc), :]),
                             (s_ctx, v_ref[h, pl.ds(s_len, lc), :])])
            o_ref[:, h * HEAD_DIM:(h + 1) * HEAD_DIM] = o.astype(o_ref.dtype)

    @pl.when(i >= nblk)
    def _():
        for h in range(NA_HEADS):
            q = q_ref[h]
            s_ctx = _dot_nt(q, k_ref[h, pl.ds(s_len, lc), :])
            o = _softmax_pv([(s_ctx, v_ref[h, pl.ds(s_len, lc), :])])
            o_ref[:, h * HEAD_DIM:(h + 1) * HEAD_DIM] = o.astype(o_ref.dtype)


def _na_call(nq, nk, nv, table, s_len, n_query):
    bsz, heads, t_all, hd = nq.shape
    tm = TOK_TILE
    rows = s_len // GRID_W
    whole = pl.BlockSpec((None, heads, t_all, hd), lambda b, i: (b, 0, 0, 0))
    return pl.pallas_call(
        functools.partial(_na_kernel, s_len, t_all - s_len, rows),
        grid=(bsz, n_query // tm),
        in_specs=[pl.BlockSpec((None, heads, tm, hd), lambda b, i: (b, 0, i, 0)),
                  whole, whole,
                  pl.BlockSpec(table.shape, lambda b, i: (0, 0, 0, 0))],
        out_specs=pl.BlockSpec((None, tm, heads * hd), lambda b, i: (b, i, 0)),
        out_shape=jax.ShapeDtypeStruct((bsz, t_all, heads * hd), BF16),
        name="neighbourhood_attention",
        compiler_params=_cparams(("parallel", "parallel")),
    )(nq, nk, nv, table)


def _hy_filter_kernel(z_ref, t_ref, side_ref, w1_ref, b1_ref, w2_ref, b2_ref, w3_ref, b3_ref,
                      wo_ref, fr_ref, dl_ref, o_ref):
    om = fr_ref[...]
    h = jnp.sin(om * (_dot(z_ref[...], w1_ref[...], precision=HIGHEST) + b1_ref[...]))
    h = jnp.sin(om * (_dot(h, w2_ref[...], precision=HIGHEST) + b2_ref[...]))
    h = jnp.sin(om * (_dot(h, w3_ref[...], precision=HIGHEST) + b3_ref[...]))
    ht = _dot_nt(wo_ref[...], h, precision=HIGHEST)
    c = o_ref.shape[0]
    decay = jnp.exp(-t_ref[...] * jnp.abs(dl_ref[...]))
    side = side_ref[...]
    o_ref[...] = decay * (jnp.where(side > 0.5, ht[:c], 0.0) + jnp.where(side < -0.5, ht[c:], 0.0))


def _hy_filter_consts(length):
    n = np.arange(2 * length)
    pos = np.where(n < length, n, 2 * length - n)
    pos = np.where(n == length, 0, pos)
    side = np.where(n < length, 1.0, -1.0)
    side = np.where(n == length, 0.0, side)
    t = np.linspace(0.0, 1.0, length)[pos]
    bands = (HY_EMB_DIM - 1) // 2
    w = 2.0 * math.pi * pos / length
    fr = np.linspace(1e-4, bands - 1, bands)
    z = np.concatenate([t[:, None], np.cos(fr[None, :] * w[:, None]), -np.sin(fr[None, :] * w[:, None])], axis=-1)
    zp = np.zeros((2 * length, 64), np.float32)
    zp[:, :HY_EMB_DIM] = z
    return zp, t.astype(np.float32)[None, :], side.astype(np.float32)[None, :]


def _hy_filter_call(length, w1, b1, w2, b2, w3, b3, wout, freq):
    f = w2.shape[0]
    c = wout.shape[1] // 2
    zp, t, side = _hy_filter_consts(length)
    w1p = jnp.zeros((64, f), F32).at[:HY_EMB_DIM].set(w1)
    deltas = np.linspace(math.log(HY_DECAY_TARGET) / HY_SLOW_DECAY,
                         math.log(HY_DECAY_TARGET) / HY_FAST_DECAY, c).astype(np.float32)[:, None]
    args = (jnp.asarray(zp), jnp.asarray(t), jnp.asarray(side), w1p, b1[None], w2, b2[None], w3, b3[None],
            wout.T, freq[None], jnp.asarray(deltas))
    return pl.pallas_call(
        _hy_filter_kernel,
        in_specs=[pl.BlockSpec(a.shape, lambda: (0,) * a.ndim) for a in args],
        out_specs=pl.BlockSpec((c, 2 * length), lambda: (0, 0)),
        out_shape=jax.ShapeDtypeStruct((c, 2 * length), F32),
        grid=(),
        name="hyena_filter",
        compiler_params=pltpu.CompilerParams(vmem_limit_bytes=VMEM_LIMIT),
    )(*args)


def _bitrev(n):
    bits = int(math.log2(n))
    return np.array([int(format(i, "0%db" % bits)[::-1], 2) if bits else 0 for i in range(n)])


def _fft_consts(n1, ct):
    n = n1 * FFT_CHUNK
    stages = max(int(math.log2(n1)), 1)
    tw = np.zeros((2, stages, max(n1 // 2, 1)), np.float32)
    for s in range(int(math.log2(n1))):
        half = n1 >> (s + 1)
        ang = -2.0 * math.pi * np.arange(half) / (2 * half)
        tw[0, s, :half] = np.cos(ang)
        tw[1, s, :half] = np.sin(ang)
    k1 = _bitrev(n1)
    ang = -2.0 * math.pi * (np.arange(FFT_CHUNK)[None, :] * k1[:, None]) / n
    twr = np.repeat(np.cos(ang), ct, axis=0).astype(np.float32)
    twi = np.repeat(np.sin(ang), ct, axis=0).astype(np.float32)
    a = -2.0 * math.pi * np.outer(np.arange(FFT_CHUNK), np.arange(FFT_CHUNK)) / FFT_CHUNK
    wr, wi = np.cos(a), np.sin(a)
    wf = np.block([[wr, wi], [-wi, wr]]).astype(np.float32)
    wb = np.block([[wr, -wi], [wi, wr]]).astype(np.float32)
    return tw, twr, twi, _split_bf16(wf), _split_bf16(wb)


def _split_bf16(w):
    w = jnp.asarray(w, F32)
    hi = w.astype(BF16)
    return jnp.stack([hi, (w - hi.astype(F32)).astype(BF16)])


def _slab_fft(zr, zi, tw_ref, n1, ct, inverse, half_input=False, half_output=False):
    nst = int(math.log2(n1))
    order = range(nst - 1, -1, -1) if inverse else range(nst)
    for s in order:
        half = n1 >> (s + 1)
        lh = int(math.log2(half))
        first = s == 0

        def body(idx, carry, s=s, half=half, lh=lh, first=first):
            blk = idx >> lh
            j = idx & (half - 1)
            r0 = pl.multiple_of(((blk << (lh + 1)) + j) * ct, ct)
            r1 = pl.multiple_of(r0 + half * ct, ct)
            wr = tw_ref[0, s, j]
            wi = tw_ref[1, s, j]
            ar = zr[pl.ds(r0, ct), :]
            ai = zi[pl.ds(r0, ct), :]
            if not inverse:
                if first and half_input:
                    zr[pl.ds(r1, ct), :] = ar * wr - ai * wi
                    zi[pl.ds(r1, ct), :] = ar * wi + ai * wr
                else:
                    br = zr[pl.ds(r1, ct), :]
                    bi = zi[pl.ds(r1, ct), :]
                    dr = ar - br
                    di = ai - bi
                    zr[pl.ds(r0, ct), :] = ar + br
                    zi[pl.ds(r0, ct), :] = ai + bi
                    zr[pl.ds(r1, ct), :] = dr * wr - di * wi
                    zi[pl.ds(r1, ct), :] = dr * wi + di * wr
            else:
                br = zr[pl.ds(r1, ct), :]
                bi = zi[pl.ds(r1, ct), :]
                tr = br * wr + bi * wi
                ti = bi * wr - br * wi
                zr[pl.ds(r0, ct), :] = ar + tr
                zi[pl.ds(r0, ct), :] = ai + ti
                if not (first and half_output):
                    zr[pl.ds(r1, ct), :] = ar - tr
                    zi[pl.ds(r1, ct), :] = ai - ti
            return carry

        lax.fori_loop(0, n1 // 2, body, 0, unroll=2)


def _chunk_dft(xr, xi, w_ref):
    x = jnp.concatenate([xr, xi], axis=1)
    x_hi = x.astype(BF16)
    x_lo = (x - x_hi.astype(F32)).astype(BF16)
    y = _dot(x_hi, w_ref[0]) + (_dot(x_lo, w_ref[0]) + _dot(x_hi, w_ref[1]))
    return y[:, :FFT_CHUNK], y[:, FFT_CHUNK:]


def _fft_groups(n1, ct):
    g = min(8, n1)
    return n1 // g, g * ct


def _hy_spec_kernel(n1, ct, f_ref, tw_ref, twr_ref, twi_ref, wf_ref, fr_ref, fi_ref, zr, zi):
    for s in range(n1):
        zr[s * ct:(s + 1) * ct, :] = f_ref[:, s * FFT_CHUNK:(s + 1) * FFT_CHUNK]
    zi[...] = jnp.zeros(zi.shape, F32)
    _slab_fft(zr, zi, tw_ref, n1, ct, inverse=False)
    ngroups, gr = _fft_groups(n1, ct)

    def body(g, carry):
        r0 = pl.multiple_of(g * gr, gr)
        a = zr[pl.ds(r0, gr), :]
        b = zi[pl.ds(r0, gr), :]
        tr = twr_ref[pl.ds(r0, gr), :]
        ti = twi_ref[pl.ds(r0, gr), :]
        yr, yi = _chunk_dft(a * tr - b * ti, a * ti + b * tr, wf_ref)
        fr_ref[pl.ds(r0, gr), :] = yr
        fi_ref[pl.ds(r0, gr), :] = yi
        return carry

    lax.fori_loop(0, ngroups, body, 0, unroll=min(2, ngroups))


def _hy_spec_call(filt_t):
    c, n = filt_t.shape
    n1 = n // FFT_CHUNK
    ct = HY_CT
    tw, twr, twi, wf, _ = _fft_consts(n1, ct)
    const2 = lambda i: (0, 0)
    out = jax.ShapeDtypeStruct((c // ct, n1 * ct, FFT_CHUNK), F32)
    ospec = pl.BlockSpec((None, n1 * ct, FFT_CHUNK), lambda i: (i, 0, 0))
    return pl.pallas_call(
        functools.partial(_hy_spec_kernel, n1, ct),
        grid=(c // ct,),
        in_specs=[pl.BlockSpec((ct, n), lambda i: (i, 0)),
                  pl.BlockSpec(memory_space=pltpu.SMEM),
                  pl.BlockSpec(twr.shape, const2), pl.BlockSpec(twi.shape, const2),
                  pl.BlockSpec(wf.shape, lambda i: (0, 0, 0))],
        out_specs=[ospec, ospec],
        out_shape=[out, out],
        scratch_shapes=[pltpu.VMEM((n1 * ct, FFT_CHUNK), F32), pltpu.VMEM((n1 * ct, FFT_CHUNK), F32)],
        name="hyena_filter_spectrum",
        compiler_params=_cparams(("parallel",)),
    )(filt_t, jnp.asarray(tw), jnp.asarray(twr), jnp.asarray(twi), jnp.asarray(wf))


def _hy_conv_kernel(n1, ct, length, v_ref, x1_ref, x2_ref, cwv_ref, cw1_ref, cw2_ref, skip_ref,
                    fr_ref, fi_ref, tw_ref, twr_ref, twi_ref, wf_ref, wb_ref, o_ref,
                    zr, zi, u_scr, g_scr):
    lane = lax.broadcasted_iota(jnp.int32, (1, length), 1)

    def short_conv(x, cw_ref):
        xm = jnp.where(lane >= 1, pltpu.roll(x, 1, 1), 0.0)
        xp = jnp.where(lane <= length - 2, pltpu.roll(x, length - 1, 1), 0.0)
        return xm * cw_ref[:, 0:1] + x * cw_ref[:, 1:2] + xp * cw_ref[:, 2:3] + cw_ref[:, 3:4]

    for a in range(2):
        u_scr[a] = short_conv(v_ref[a], cwv_ref) * short_conv(x1_ref[a], cw1_ref)
        g_scr[a] = short_conv(x2_ref[a], cw2_ref)
    nz = length // FFT_CHUNK
    for s in range(nz):
        zr[s * ct:(s + 1) * ct, :] = u_scr[0, :, s * FFT_CHUNK:(s + 1) * FFT_CHUNK]
        zi[s * ct:(s + 1) * ct, :] = u_scr[1, :, s * FFT_CHUNK:(s + 1) * FFT_CHUNK]
    _slab_fft(zr, zi, tw_ref, n1, ct, inverse=False, half_input=True)
    ngroups, gr = _fft_groups(n1, ct)

    def body(g, carry):
        r0 = pl.multiple_of(g * gr, gr)
        a = zr[pl.ds(r0, gr), :]
        b = zi[pl.ds(r0, gr), :]
        tr = twr_ref[pl.ds(r0, gr), :]
        ti = twi_ref[pl.ds(r0, gr), :]
        yr, yi = _chunk_dft(a * tr - b * ti, a * ti + b * tr, wf_ref)
        fr = fr_ref[pl.ds(r0, gr), :]
        fi = fi_ref[pl.ds(r0, gr), :]
        qr, qi = _chunk_dft(yr * fr - yi * fi, yr * fi + yi * fr, wb_ref)
        zr[pl.ds(r0, gr), :] = qr * tr + qi * ti
        zi[pl.ds(r0, gr), :] = qi * tr - qr * ti
        return carry

    lax.fori_loop(0, ngroups, body, 0, unroll=min(2, ngroups))
    _slab_fft(zr, zi, tw_ref, n1, ct, inverse=True, half_output=True)
    inv_n = 1.0 / (n1 * FFT_CHUNK)
    skip = skip_ref[...]
    for s in range(nz):
        sl = slice(s * FFT_CHUNK, (s + 1) * FFT_CHUNK)
        for a, z in ((0, zr), (1, zi)):
            y = z[s * ct:(s + 1) * ct, :] * inv_n
            u = u_scr[a, :, sl]
            o_ref[a, :, sl] = (g_scr[a, :, sl] * (y + u * skip)).astype(o_ref.dtype)


def _hy_conv_call(hy_t, spec_r, spec_i, cw, skip, length, lane_block, prev=None):
    _, bh, c3, t_all = hy_t.shape
    c = c3 // 3
    ct = HY_CT
    nct = c // ct
    n1 = 2 * length // FFT_CHUNK
    tw, twr, twi, wf, wb = _fft_consts(n1, ct)
    const2 = lambda p, i: (0, 0)
    xin = lambda k: pl.BlockSpec((2, None, ct, length), lambda p, i, k=k: (0, p, k * nct + i, lane_block))
    cwin = lambda k: pl.BlockSpec((ct, 4), lambda p, i, k=k: (k * nct + i, 0))
    fin = pl.BlockSpec((None, n1 * ct, FFT_CHUNK), lambda p, i: (i, 0, 0))
    in_specs = [xin(0), xin(1), xin(2), cwin(0), cwin(1), cwin(2),
                pl.BlockSpec((ct, 1), lambda p, i: (i, 0)),
                fin, fin,
                pl.BlockSpec(memory_space=pltpu.SMEM),
                pl.BlockSpec(twr.shape, const2), pl.BlockSpec(twi.shape, const2),
                pl.BlockSpec(wf.shape, lambda p, i: (0, 0, 0)),
                pl.BlockSpec(wb.shape, lambda p, i: (0, 0, 0))]
    args = [hy_t, hy_t, hy_t, cw, cw, cw, skip, spec_r, spec_i, jnp.asarray(tw), jnp.asarray(twr),
            jnp.asarray(twi), jnp.asarray(wf), jnp.asarray(wb)]
    kern = functools.partial(_hy_conv_kernel, n1, ct, length)
    aliases = {}
    if prev is not None:
        in_specs.append(pl.BlockSpec(memory_space=pl.ANY))
        args.append(prev)
        aliases = {len(args) - 1: 0}
        kern = lambda *refs, kern=kern: kern(*refs[:14], *refs[15:])
    return pl.pallas_call(
        kern,
        grid=(bh, nct),
        in_specs=in_specs,
        out_specs=pl.BlockSpec((2, None, ct, length), lambda p, i: (0, p, i, lane_block)),
        out_shape=jax.ShapeDtypeStruct((2, bh, c, t_all), BF16),
        scratch_shapes=[pltpu.VMEM((n1 * ct, FFT_CHUNK), F32), pltpu.VMEM((n1 * ct, FFT_CHUNK), F32),
                        pltpu.VMEM((2, ct, length), F32), pltpu.VMEM((2, ct, length), F32)],
        input_output_aliases=aliases,
        name="hyena_conv_%d" % length,
        compiler_params=_cparams(("parallel", "parallel")),
    )(*args)


def _outproj_kernel(n_lat_tiles, ctx_row, x_ref, hy_ref, ga_ref, na_ref, w_ref, mod_ref, o_ref):
    j = pl.program_id(1)
    nbat, tm, d = x_ref.shape
    mix = []
    for s in range(nbat):
        hy = hy_ref[s].astype(F32).T.astype(BF16)
        mix.append(jnp.concatenate([hy, ga_ref[s], na_ref[s]], axis=1))
    y = _dot(jnp.concatenate(mix, axis=0), w_ref[...])
    for s in range(nbat):
        row = jnp.where(j < n_lat_tiles, pl.program_id(0) * nbat + s, ctx_row)
        o_ref[s] = x_ref[s] + _mod_row(mod_ref, row, 2, d) * y[s * tm:(s + 1) * tm]


def _outproj_call(xa, hyo_t, gqa, na, w_out, mod_l, s_len, n_tok):
    bsz, _, d = xa.shape
    tm = TOK_TILE
    nbat = _batch_group(bsz)
    c = hyo_t.shape[1]
    const = lambda b, j: (0, 0)
    tok = lambda n: pl.BlockSpec((nbat, tm, n), lambda b, j: (b, j, 0))
    return pl.pallas_call(
        functools.partial(_outproj_kernel, s_len // tm, bsz),
        grid=(bsz // nbat, n_tok // tm),
        in_specs=[tok(d), pl.BlockSpec((nbat, c, tm), lambda b, j: (b, 0, j)),
                  tok(gqa.shape[-1]), tok(na.shape[-1]),
                  pl.BlockSpec(w_out.shape, const), pl.BlockSpec(mod_l.shape, const)],
        out_specs=tok(d),
        out_shape=jax.ShapeDtypeStruct((bsz, n_tok, d), F32),
        name="out_proj",
        compiler_params=_cparams(("parallel", "parallel")),
    )(xa, hyo_t, gqa, na, w_out, mod_l)


def _ffn_kernel(tiles_per_batch, n_lat_tiles, ctx_row, th, x_ref, mod_ref, g_ref, wi_ref, wo_ref,
                o_ref, h_scr, act_scr):
    i = pl.program_id(0)
    d = x_ref.shape[-1]
    f = wo_ref.shape[0]
    groups = x_ref.shape[0] // TOK_TILE

    def mod_row(s):
        gi = i * groups + s
        jj = gi % tiles_per_batch
        return jnp.where(jj < n_lat_tiles, gi // tiles_per_batch, ctx_row)

    for s in range(groups):
        row = mod_row(s)
        sl = slice(s * TOK_TILE, (s + 1) * TOK_TILE)
        h = _norm_mod(x_ref[sl, :], g_ref[...], _mod_row(mod_ref, row, 3, d), _mod_row(mod_ref, row, 4, d))
        h_scr[sl, :] = h.astype(BF16)
    h = h_scr[...]
    for j in range(f // th):
        gate = _dot(h, wi_ref[:, j * th:(j + 1) * th])
        up = _dot(h, wi_ref[:, f + j * th:f + (j + 1) * th])
        act_scr[:, j * th:(j + 1) * th] = (gate * jax.nn.sigmoid(gate) * up).astype(BF16)
    y = _dot(act_scr[...], wo_ref[...])
    for s in range(groups):
        row = mod_row(s)
        sl = slice(s * TOK_TILE, (s + 1) * TOK_TILE)
        o_ref[sl, :] = x_ref[sl, :] + _mod_row(mod_ref, row, 5, d) * y[sl, :]


def _ffn_call(xa, mod_l, g, w_in, w_out, s_len):
    bsz, t_all, d = xa.shape
    f = w_out.shape[0]
    r = bsz * t_all
    tm = 1024 if r % 1024 == 0 else TOK_TILE
    th = 256
    const = lambda i: (0, 0)
    resident = lambda a: pl.BlockSpec(a.shape, const, pipeline_mode=pl.Buffered(1))
    out = pl.pallas_call(
        functools.partial(_ffn_kernel, t_all // TOK_TILE, s_len // TOK_TILE, bsz, th),
        grid=(r // tm,),
        in_specs=[pl.BlockSpec((tm, d), lambda i: (i, 0)),
                  pl.BlockSpec(mod_l.shape, const),
                  pl.BlockSpec((1, d), const),
                  resident(w_in), resident(w_out)],
        out_specs=pl.BlockSpec((tm, d), lambda i: (i, 0)),
        out_shape=jax.ShapeDtypeStruct((r, d), F32),
        scratch_shapes=[pltpu.VMEM((tm, d), BF16), pltpu.VMEM((tm, f), BF16)],
        name="swiglu_ffn",
        compiler_params=_cparams(("parallel",)),
    )(xa.reshape(r, d), mod_l, g, w_in, w_out)
    return out.reshape(bsz, t_all, d)


def _final_kernel(x_ref, g_ref, o_ref):
    x = x_ref[...]
    ms = jnp.mean(x * x, axis=-1, keepdims=True)
    o_ref[...] = x * lax.rsqrt(ms + NORM_EPS) * g_ref[...]


def _final_call(xa, g, s_len):
    bsz, _, d = xa.shape
    tm = TOK_TILE
    return pl.pallas_call(
        _final_kernel,
        grid=(bsz, s_len // tm),
        in_specs=[pl.BlockSpec((None, tm, d), lambda b, j: (b, j, 0)),
                  pl.BlockSpec((1, d), lambda b, j: (0, 0))],
        out_specs=pl.BlockSpec((None, tm, d), lambda b, j: (b, j, 0)),
        out_shape=jax.ShapeDtypeStruct((bsz, s_len, d), F32),
        name="final_norm",
        compiler_params=_cparams(("parallel", "parallel")),
    )(xa, g)


def _rope_tables(s_len, lc):
    pos = np.arange(s_len)
    row = (pos // GRID_W).astype(np.float64)
    col = (pos % GRID_W).astype(np.float64)
    n_f = HEAD_DIM // 4
    inv = (ROPE_THETA ** (-np.arange(n_f, dtype=np.float32) / n_f)).astype(np.float64)
    ang = np.concatenate([row[:, None] * inv, col[:, None] * inv], axis=-1).astype(np.float32).astype(np.float64)
    lane = np.arange(LANES)
    sign = np.where((lane % HEAD_DIM) < HEAD_DIM // 2, -1.0, 1.0)
    cos = np.cos(ang)[:, lane % (HEAD_DIM // 2)]
    sin = np.sin(ang)[:, lane % (HEAD_DIM // 2)] * sign[None, :]
    cos = np.concatenate([cos, np.ones((lc, LANES))], axis=0).astype(np.float32)
    sin = np.concatenate([sin, np.zeros((lc, LANES))], axis=0).astype(np.float32)
    return jnp.asarray(cos), jnp.asarray(sin)


def kernel(x, c, ctx, c_ctx, w_mod, b_mod, g_mix, g_ffn, w_in, w_out, hy_conv_w, hy_conv_b,
           hy_f_w1, hy_f_b1, hy_f_w2, hy_f_b2, hy_f_w3, hy_f_b3, hy_f_wout, hy_f_freq, hy_skip,
           qk_g_q, qk_g_k, na_rpb, w_ffn_in, w_ffn_out, g_final):
    bsz, s_len, d = x.shape
    lc = ctx.shape[1]
    depth = w_mod.shape[0]
    t_all = s_len + lc
    hyw = hy_skip.shape[1]
    hy_cols = 3 * hyw
    assert bsz % 2 == 0 and s_len % TOK_TILE == 0 and lc == TOK_TILE and s_len % lc == 0

    nb = -(-(bsz + 1) // 8) * 8
    cs = jnp.zeros((nb, d), F32).at[:bsz].set(c).at[bsz].set(c_ctx)
    mod = _mod_call(cs, w_mod, b_mod)

    cos_t, sin_t = _rope_tables(s_len, lc)
    tables = _na_bias_tables(na_rpb, s_len // GRID_W)
    w_hy_t = jnp.swapaxes(w_in[:, :, :hy_cols], 1, 2).astype(BF16)
    w_rest = w_in[:, :, hy_cols:].astype(BF16)
    w_out_b = w_out.astype(BF16)
    w_fi = w_ffn_in.astype(BF16)
    w_fo = w_ffn_out.astype(BF16)
    cw = jnp.concatenate([jnp.swapaxes(hy_conv_w, 1, 2), hy_conv_b[:, :, None]], axis=2)
    per = LANES // HEAD_DIM

    xa = jnp.concatenate([x, ctx], axis=1)
    for l in range(depth):
        last = l == depth - 1
        gq2 = jnp.tile(qk_g_q[l], per)[None]
        gk2 = jnp.tile(qk_g_k[l], per)[None]
        hy_t, q, k, v, nq, nk, nv = _inproj_call(xa, mod[l], g_mix[l][None], w_hy_t[l], w_rest[l],
                                                  gq2, gk2, cos_t, sin_t, s_len)
        n_tok = s_len if last else t_all
        gqa = _gqa_call(q, k, v, s_len, n_tok)
        na = _na_call(nq, nk, nv, tables[l], s_len, n_tok)

        fargs = (hy_f_w1[l], hy_f_b1[l], hy_f_w2[l], hy_f_b2[l], hy_f_w3[l], hy_f_b3[l],
                 hy_f_wout[l], hy_f_freq[l])
        hy_p = hy_t.reshape(2, bsz // 2, hy_cols, t_all)
        skip = hy_skip[l][:, None]
        fr, fi = _hy_spec_call(_hy_filter_call(s_len, *fargs))
        hyo = _hy_conv_call(hy_p, fr, fi, cw[l], skip, s_len, 0)
        if not last:
            frc, fic = _hy_spec_call(_hy_filter_call(lc, *fargs))
            hyo = _hy_conv_call(hy_p, frc, fic, cw[l], skip, lc, s_len // lc, prev=hyo)
        hyo = hyo.reshape(bsz, hyw, t_all)

        xa = _outproj_call(xa, hyo, gqa, na, w_out_b[l], mod[l], s_len, n_tok)
        xa = _ffn_call(xa, mod[l], g_ffn[l][None], w_fi[l], w_fo[l], s_len)
    return _final_call(xa, g_final[None], s_len)
```

```python
import functools
import math

import numpy as np
import jax
import jax.numpy as jnp
from jax import lax
from jax.experimental import pallas as pl
from jax.experimental.pallas import tpu as pltpu

F32 = jnp.float32
BF16 = jnp.bfloat16
HIGHEST = lax.Precision.HIGHEST

GRID_W = 64
HEAD_DIM = 64
GQA_HEADS = 8
GQA_KV_HEADS = 2
GQA_GROUP = GQA_HEADS // GQA_KV_HEADS
NA_HEADS = 4
NA_KH = 8
NA_KW = 16
ROPE_THETA = 10000.0
HY_EMB_DIM = 33
HY_DECAY_TARGET = 1e-2
HY_FAST_DECAY = 0.3
HY_SLOW_DECAY = 1.5
NORM_EPS = 1e-6
ATTN_SCALE = HEAD_DIM ** -0.5
NEG = -1e30
LOG2E = math.log2(math.e)

LANES = 128
TOK_TILE = 256
FFT_CHUNK = LANES
HY_CT = 64
NA_ROWS = TOK_TILE // GRID_W
NA_WIN = NA_ROWS + NA_KH
VMEM_LIMIT = 56 * 1024 * 1024


def _cparams(sem):
    return pltpu.CompilerParams(dimension_semantics=sem, vmem_limit_bytes=VMEM_LIMIT)


def _dot(a, b, **kw):
    return jnp.dot(a, b, preferred_element_type=F32, **kw)


def _dot_nt(a, b, **kw):
    return lax.dot_general(a, b, (((1,), (1,)), ((), ())), preferred_element_type=F32, **kw)


def _mod_kernel(cs_ref, w_ref, b_ref, o_ref):
    cs = cs_ref[...]
    s = cs * jax.nn.sigmoid(cs)
    o_ref[...] = _dot(s, w_ref[...], precision=HIGHEST) + b_ref[...]


def _mod_call(cs, w_mod, b_mod):
    depth, d, n = w_mod.shape
    nb = cs.shape[0]
    tn = 1024
    return pl.pallas_call(
        _mod_kernel,
        grid=(depth, n // tn),
        in_specs=[pl.BlockSpec((nb, d), lambda l, j: (0, 0)),
                  pl.BlockSpec((None, d, tn), lambda l, j: (l, 0, j)),
                  pl.BlockSpec((None, 1, tn), lambda l, j: (l, 0, j))],
        out_specs=pl.BlockSpec((None, nb, tn), lambda l, j: (l, 0, j)),
        out_shape=jax.ShapeDtypeStruct((depth, nb, n), F32),
        name="adaln_mod",
        compiler_params=_cparams(("parallel", "parallel")),
    )(cs, w_mod, b_mod.reshape(depth, 1, n))


def _mod_row(mod_ref, row, k, d):
    return mod_ref[pl.ds(row, 1), k * d:(k + 1) * d]


def _norm_mod(x, g, shift, scale):
    ms = jnp.mean(x * x, axis=-1, keepdims=True)
    y = x * lax.rsqrt(ms + NORM_EPS) * g
    return y * (1.0 + scale) + shift


def _batch_group(bsz):
    return 4 if bsz % 4 == 0 else 2


def _inproj_kernel(n_lat_tiles, ctx_row, hyc, x_ref, mod_ref, g_ref, wcm_ref, wtm_ref, gq_ref, gk_ref,
                   cosc_ref, sinc_ref, cos_ref, sin_ref,
                   hy_ref, q_ref, k_ref, v_ref, nq_ref, nk_ref, nv_ref, h_scr):
    j = pl.program_id(1)
    nbat, tm, d = x_ref.shape
    for s in range(nbat):
        row = jnp.where(j < n_lat_tiles, pl.program_id(0) * nbat + s, ctx_row)
        h = _norm_mod(x_ref[s], g_ref[...], _mod_row(mod_ref, row, 0, d), _mod_row(mod_ref, row, 1, d))
        h_scr[s * tm:(s + 1) * tm, :] = h.astype(BF16)
    h = h_scr[...]
    cm = _dot_nt(wcm_ref[...], h)
    tmj = _dot(h, wtm_ref[...])

    half = HEAD_DIM // 2
    reps = cm.shape[1] // LANES
    gq = jnp.concatenate([gq_ref[...]] * reps, axis=1)
    cosc = jnp.concatenate([cosc_ref[...]] * nbat, axis=1)
    sinc = jnp.concatenate([sinc_ref[...]] * nbat, axis=1)
    for hh in range(GQA_HEADS):
        x = cm[hyc + hh * HEAD_DIM:hyc + (hh + 1) * HEAD_DIM, :]
        ms = jnp.mean(x * x, axis=0, keepdims=True)
        xn = x * lax.rsqrt(ms + NORM_EPS) * gq
        x1, x2 = xn[:half], xn[half:]
        y = jnp.concatenate([x1 * cosc - x2 * sinc, x1 * sinc + x2 * cosc], axis=0)
        y = (y * (ATTN_SCALE * LOG2E)).astype(BF16)
        for s in range(nbat):
            q_ref[s, hh * HEAD_DIM:(hh + 1) * HEAD_DIM, :] = y[:, s * tm:(s + 1) * tm]
    voff = hyc + GQA_HEADS * HEAD_DIM
    for s in range(nbat):
        hy_ref[s] = cm[:hyc, s * tm:(s + 1) * tm]
        v_ref[s] = cm[voff:, s * tm:(s + 1) * tm].astype(BF16)
        _inproj_token_major(tmj[s * tm:(s + 1) * tm], gk_ref, cos_ref, sin_ref,
                            k_ref.at[s], nq_ref.at[s], nk_ref.at[s], nv_ref.at[s])


def _inproj_token_major(p, gk_ref, cos_ref, sin_ref, k_ref, nq_ref, nk_ref, nv_ref):
    ri = lax.broadcasted_iota(jnp.int32, (LANES, LANES), 0) // HEAD_DIM
    ci = lax.broadcasted_iota(jnp.int32, (LANES, LANES), 1) // HEAD_DIM
    avg = jnp.where(ri == ci, 1.0 / HEAD_DIM, 0.0).astype(BF16)
    lane = lax.broadcasted_iota(jnp.int32, (1, LANES), 1)
    first_half = (lane % HEAD_DIM) < (HEAD_DIM // 2)
    xc = p[:, :LANES]
    ms = _dot((xc * xc).astype(BF16), avg)
    xn = xc * lax.rsqrt(ms + NORM_EPS) * gk_ref[...]
    partner = jnp.where(first_half, pltpu.roll(xn, LANES - HEAD_DIM // 2, 1),
                        pltpu.roll(xn, HEAD_DIM // 2, 1))
    y = (xn * cos_ref[...] + partner * sin_ref[...]).astype(BF16)
    for e in range(GQA_KV_HEADS):
        k_ref[e] = y[:, e * HEAD_DIM:(e + 1) * HEAD_DIM]
    off = LANES
    for ref, scale in ((nq_ref, ATTN_SCALE), (nk_ref, 1.0), (nv_ref, 1.0)):
        for e in range(NA_HEADS):
            ref[e] = (p[:, off + e * HEAD_DIM:off + (e + 1) * HEAD_DIM] * scale).astype(BF16)
        off += NA_HEADS * HEAD_DIM


def _inproj_call(xa, mod_l, g, w_cm, w_tm, gq_b, gk2, rope, hyc, s_len):
    bsz, t_all, d = xa.shape
    tm = TOK_TILE
    nbat = _batch_group(bsz)
    cosc, sinc, cos_t, sin_t = rope
    grid = (bsz // nbat, t_all // tm)
    const = lambda b, j: (0, 0)
    resident = lambda a: pl.BlockSpec(a.shape, const, pipeline_mode=pl.Buffered(1))
    heads = lambda n: pl.BlockSpec((nbat, n, tm, HEAD_DIM), lambda b, j: (b, 0, j, 0))
    hshape = lambda n: jax.ShapeDtypeStruct((bsz, n, t_all, HEAD_DIM), BF16)
    chan = lambda n: pl.BlockSpec((nbat, n, tm), lambda b, j: (b, 0, j))
    qc = GQA_HEADS * HEAD_DIM
    vc = GQA_KV_HEADS * HEAD_DIM
    return pl.pallas_call(
        functools.partial(_inproj_kernel, s_len // tm, bsz, hyc),
        grid=grid,
        in_specs=[pl.BlockSpec((nbat, tm, d), lambda b, j: (b, j, 0)),
                  pl.BlockSpec(mod_l.shape, const),
                  pl.BlockSpec((1, d), const),
                  resident(w_cm), resident(w_tm),
                  pl.BlockSpec(gq_b.shape, const),
                  pl.BlockSpec((1, LANES), const),
                  pl.BlockSpec((HEAD_DIM // 2, tm), lambda b, j: (0, j)),
                  pl.BlockSpec((HEAD_DIM // 2, tm), lambda b, j: (0, j)),
                  pl.BlockSpec((tm, LANES), lambda b, j: (j, 0)),
                  pl.BlockSpec((tm, LANES), lambda b, j: (j, 0))],
        out_specs=[chan(hyc), chan(qc), heads(GQA_KV_HEADS), chan(vc),
                   heads(NA_HEADS), heads(NA_HEADS), heads(NA_HEADS)],
        out_shape=[jax.ShapeDtypeStruct((bsz, hyc, t_all), F32),
                   jax.ShapeDtypeStruct((bsz, qc, t_all), BF16),
                   hshape(GQA_KV_HEADS),
                   jax.ShapeDtypeStruct((bsz, vc, t_all), BF16),
                   hshape(NA_HEADS), hshape(NA_HEADS), hshape(NA_HEADS)],
        scratch_shapes=[pltpu.VMEM((nbat * tm, d), BF16)],
        name="in_proj",
        compiler_params=_cparams(("parallel", "parallel")),
    )(xa, mod_l, g, w_cm, w_tm, gq_b, gk2, cosc, sinc, cos_t, sin_t)


def _gqa_kernel(s_len, lc, tk, q_ref, k_ref, vt_ref, o_ref, s_scr):
    i = pl.program_id(2)
    tm = q_ref.shape[1]
    sub = 8

    def scores(h, r0, nrow, m):
        s = _dot(k_ref[r0:r0 + nrow, :], q_ref[h * HEAD_DIM:(h + 1) * HEAD_DIM, :])
        s_scr[h, r0:r0 + nrow, :] = s
        return jnp.maximum(m, jnp.max(s.reshape(nrow // sub, sub, tm), axis=0))

    def apply(h, r0, nrow, m, l, acc):
        p = jnp.exp2(s_scr[h, r0:r0 + nrow, :].reshape(nrow // sub, sub, tm) - m[None])
        l = l + jnp.sum(p, axis=0)
        acc = acc + _dot(vt_ref[:, r0:r0 + nrow], p.reshape(nrow, tm).astype(BF16))
        return l, acc

    def run(chunks):
        maxes = [None] * GQA_GROUP
        for stage in range(GQA_GROUP + 1):
            h1, h2 = stage, stage - 1
            if h1 < GQA_GROUP:
                m1 = jnp.full((sub, tm), NEG, F32)
            if h2 >= 0:
                m2 = jnp.broadcast_to(jnp.max(maxes[h2], axis=0, keepdims=True), (sub, tm))
                l = jnp.zeros((sub, tm), F32)
                acc = jnp.zeros((HEAD_DIM, tm), F32)
            for r0, nrow in chunks:
                if h1 < GQA_GROUP:
                    m1 = scores(h1, r0, nrow, m1)
                if h2 >= 0:
                    l, acc = apply(h2, r0, nrow, m2, l, acc)
            if h1 < GQA_GROUP:
                maxes[h1] = m1
            if h2 >= 0:
                o = acc / jnp.sum(l, axis=0, keepdims=True)
                o_ref[h2 * HEAD_DIM:(h2 + 1) * HEAD_DIM, :] = o.astype(o_ref.dtype)

    ctx_chunk = (s_len, lc)

    @pl.when(i * tm < s_len)
    def _():
        run([(c * tk, tk) for c in range(s_len // tk)] + [ctx_chunk])

    @pl.when(i * tm >= s_len)
    def _():
        run([ctx_chunk])


def _gqa_call(qt, k, vt, s_len, n_query):
    bsz, qc, t_all = qt.shape
    hd = HEAD_DIM
    tm = TOK_TILE
    tk = 2 * TOK_TILE
    lc = t_all - s_len
    return pl.pallas_call(
        functools.partial(_gqa_kernel, s_len, lc, tk),
        grid=(bsz, GQA_KV_HEADS, n_query // tm),
        in_specs=[pl.BlockSpec((None, GQA_GROUP * hd, tm), lambda b, g, i: (b, g, i)),
                  pl.BlockSpec((None, None, t_all, hd), lambda b, g, i: (b, g, 0, 0)),
                  pl.BlockSpec((None, hd, t_all), lambda b, g, i: (b, g, 0))],
        out_specs=pl.BlockSpec((None, GQA_GROUP * hd, tm), lambda b, g, i: (b, g, i)),
        out_shape=jax.ShapeDtypeStruct((bsz, qc, t_all), BF16),
        scratch_shapes=[pltpu.VMEM((GQA_GROUP, t_all, tm), F32)],
        name="gqa_attention",
        compiler_params=_cparams(("parallel", "parallel", "parallel")),
    )(qt, k, vt)


def _na_bias_consts(rows):
    assert rows >= NA_WIN + NA_ROWS
    nrow, ncol = 2 * NA_KH - 1, 2 * NA_KW - 1
    rsel = np.zeros((3, NA_ROWS, NA_WIN, nrow), np.float32)
    mrow = np.zeros((3, NA_ROWS, NA_WIN), bool)
    for kind, r0 in enumerate((0, NA_ROWS, rows - NA_ROWS)):
        kr0 = int(np.clip(r0 - NA_KH // 2, 0, rows - NA_WIN))
        for qr in range(NA_ROWS):
            r = r0 + qr
            rs = int(np.clip(r - NA_KH // 2, 0, rows - NA_KH))
            for kr in range(NA_WIN):
                ka = kr0 + kr
                if rs <= ka < rs + NA_KH:
                    mrow[kind, qr, kr] = True
                    rsel[kind, qr, kr, ka - r + NA_KH - 1] = 1.0
    csel = np.zeros((GRID_W, GRID_W, ncol), np.float32)
    mcol = np.zeros((GRID_W, GRID_W), bool)
    for w in range(GRID_W):
        cs = int(np.clip(w - NA_KW // 2, 0, GRID_W - NA_KW))
        for kc in range(cs, cs + NA_KW):
            mcol[w, kc] = True
            csel[w, kc, kc - w + NA_KW - 1] = 1.0
    mask = mrow[:, :, None, :, None] & mcol[None, None, :, None, :]
    return rsel, csel, mask.reshape(3, TOK_TILE, NA_WIN * GRID_W)


def _na_bias_tables(na_rpb, rows):
    rsel, csel, mask = _na_bias_consts(rows)
    depth, heads = na_rpb.shape[:2]
    a = jnp.einsum("kqra,lhab->lkhqrb", jnp.asarray(rsel), na_rpb.astype(F32), precision=HIGHEST)
    t = jnp.einsum("lkhqrb,wcb->lkhqwrc", a, jnp.asarray(csel), precision=HIGHEST)
    t = t.reshape(depth, 3, heads, TOK_TILE, NA_WIN * GRID_W)
    return jnp.where(jnp.asarray(mask)[None, :, None], t, NEG)


def _softmax_pv(parts):
    m = None
    for s, _ in parts:
        mm = jnp.max(s, axis=1, keepdims=True)
        m = mm if m is None else jnp.maximum(m, mm)
    l = 0.0
    o = 0.0
    for s, v in parts:
        p = jnp.exp(s - m)
        l = l + jnp.sum(p, axis=1, keepdims=True)
        o = o + _dot(p.astype(BF16), v)
    return o / l


def _na_kernel(s_len, lc, rows, q_ref, k_ref, v_ref, tab_ref, o_ref):
    i = pl.program_id(1)
    nblk = s_len // TOK_TILE
    nloc = NA_WIN * GRID_W

    @pl.when(i < nblk)
    def _():
        kind = jnp.where(i == 0, 0, jnp.where(i == nblk - 1, 2, 1))
        kr0 = jnp.clip(i * NA_ROWS - NA_KH // 2, 0, rows - NA_WIN)
        ks = pl.multiple_of(kr0 * GRID_W, GRID_W)
        for h in range(NA_HEADS):
            q = q_ref[h]
            s_loc = _dot_nt(q, k_ref[h, pl.ds(ks, nloc), :]) + tab_ref[kind, h]
            s_ctx = _dot_nt(q, k_ref[h, pl.ds(s_len, lc), :])
            o = _softmax_pv([(s_loc, v_ref[h, pl.ds(ks, nloc), :]),
                             (s_ctx, v_ref[h, pl.ds(s_len, lc), :])])
            o_ref[:, h * HEAD_DIM:(h + 1) * HEAD_DIM] = o.astype(o_ref.dtype)

    @pl.when(i >= nblk)
    def _():
        for h in range(NA_HEADS):
            q = q_ref[h]
            s_ctx = _dot_nt(q, k_ref[h, pl.ds(s_len, lc), :])
            o = _softmax_pv([(s_ctx, v_ref[h, pl.ds(s_len, lc), :])])
            o_ref[:, h * HEAD_DIM:(h + 1) * HEAD_DIM] = o.astype(o_ref.dtype)


def _na_call(nq, nk, nv, table, s_len, n_query):
    bsz, heads, t_all, hd = nq.shape
    tm = TOK_TILE
    rows = s_len // GRID_W
    whole = pl.BlockSpec((None, heads, t_all, hd), lambda b, i: (b, 0, 0, 0))
    return pl.pallas_call(
        functools.partial(_na_kernel, s_len, t_all - s_len, rows),
        grid=(bsz, n_query // tm),
        in_specs=[pl.BlockSpec((None, heads, tm, hd), lambda b, i: (b, 0, i, 0)),
                  whole, whole,
                  pl.BlockSpec(table.shape, lambda b, i: (0, 0, 0, 0))],
        out_specs=pl.BlockSpec((None, tm, heads * hd), lambda b, i: (b, i, 0)),
        out_shape=jax.ShapeDtypeStruct((bsz, t_all, heads * hd), BF16),
        name="neighbourhood_attention",
        compiler_params=_cparams(("parallel", "parallel")),
    )(nq, nk, nv, table)


def _hy_filter_kernel(z_ref, t_ref, side_ref, w1_ref, b1_ref, w2_ref, b2_ref, w3_ref, b3_ref,
                      wo_ref, fr_ref, dl_ref, o_ref):
    om = fr_ref[...]
    h = jnp.sin(om * (_dot(z_ref[...], w1_ref[...], precision=HIGHEST) + b1_ref[...]))
    h = jnp.sin(om * (_dot(h, w2_ref[...], precision=HIGHEST) + b2_ref[...]))
    h = jnp.sin(om * (_dot(h, w3_ref[...], precision=HIGHEST) + b3_ref[...]))
    ht = _dot_nt(wo_ref[...], h, precision=HIGHEST)
    c = o_ref.shape[0]
    decay = jnp.exp(-t_ref[...] * jnp.abs(dl_ref[...]))
    side = side_ref[...]
    o_ref[...] = decay * (jnp.where(side > 0.5, ht[:c], 0.0) + jnp.where(side < -0.5, ht[c:], 0.0))


def _hy_filter_consts(length):
    n = np.arange(2 * length)
    pos = np.where(n < length, n, 2 * length - n)
    pos = np.where(n == length, 0, pos)
    side = np.where(n < length, 1.0, -1.0)
    side = np.where(n == length, 0.0, side)
    t = np.linspace(0.0, 1.0, length)[pos]
    bands = (HY_EMB_DIM - 1) // 2
    w = 2.0 * math.pi * pos / length
    fr = np.linspace(1e-4, bands - 1, bands)
    z = np.concatenate([t[:, None], np.cos(fr[None, :] * w[:, None]), -np.sin(fr[None, :] * w[:, None])], axis=-1)
    zp = np.zeros((2 * length, 64), np.float32)
    zp[:, :HY_EMB_DIM] = z
    return zp, t.astype(np.float32)[None, :], side.astype(np.float32)[None, :]


def _hy_filter_call(length, w1, b1, w2, b2, w3, b3, wout, freq):
    f = w2.shape[0]
    c = wout.shape[1] // 2
    zp, t, side = _hy_filter_consts(length)
    w1p = jnp.zeros((64, f), F32).at[:HY_EMB_DIM].set(w1)
    deltas = np.linspace(math.log(HY_DECAY_TARGET) / HY_SLOW_DECAY,
                         math.log(HY_DECAY_TARGET) / HY_FAST_DECAY, c).astype(np.float32)[:, None]
    args = (jnp.asarray(zp), jnp.asarray(t), jnp.asarray(side), w1p, b1[None], w2, b2[None], w3, b3[None],
            wout.T, freq[None], jnp.asarray(deltas))
    return pl.pallas_call(
        _hy_filter_kernel,
        in_specs=[pl.BlockSpec(a.shape, lambda: (0,) * a.ndim) for a in args],
        out_specs=pl.BlockSpec((c, 2 * length), lambda: (0, 0)),
        out_shape=jax.ShapeDtypeStruct((c, 2 * length), F32),
        grid=(),
        name="hyena_filter",
        compiler_params=pltpu.CompilerParams(vmem_limit_bytes=VMEM_LIMIT),
    )(*args)


def _bitrev(n):
    bits = int(math.log2(n))
    return np.array([int(format(i, "0%db" % bits)[::-1], 2) if bits else 0 for i in range(n)])


def _fft_consts(n1, ct):
    n = n1 * FFT_CHUNK
    stages = max(int(math.log2(n1)), 1)
    tw = np.zeros((2, stages, max(n1 // 2, 1)), np.float32)
    for s in range(int(math.log2(n1))):
        half = n1 >> (s + 1)
        ang = -2.0 * math.pi * np.arange(half) / (2 * half)
        tw[0, s, :half] = np.cos(ang)
        tw[1, s, :half] = np.sin(ang)
    k1 = _bitrev(n1)
    ang = -2.0 * math.pi * (np.arange(FFT_CHUNK)[None, :] * k1[:, None]) / n
    twr = np.repeat(np.cos(ang), ct, axis=0).astype(np.float32)
    twi = np.repeat(np.sin(ang), ct, axis=0).astype(np.float32)
    a = -2.0 * math.pi * np.outer(np.arange(FFT_CHUNK), np.arange(FFT_CHUNK)) / FFT_CHUNK
    wr, wi = np.cos(a), np.sin(a)
    wf = np.block([[wr, wi], [-wi, wr]]).astype(np.float32)
    wb = np.block([[wr, -wi], [wi, wr]]).astype(np.float32)
    return tw, twr, twi, _split_bf16(wf), _split_bf16(wb)


def _split_bf16(w):
    w = jnp.asarray(w, F32)
    hi = w.astype(BF16)
    return jnp.stack([hi, (w - hi.astype(F32)).astype(BF16)])


def _slab_fft(zr, zi, tw_ref, n1, ct, inverse, half_input=False, half_output=False):
    nst = int(math.log2(n1))
    order = range(nst - 1, -1, -1) if inverse else range(nst)
    for s in order:
        half = n1 >> (s + 1)
        lh = int(math.log2(half))
        first = s == 0

        def body(idx, carry, s=s, half=half, lh=lh, first=first):
            blk = idx >> lh
            j = idx & (half - 1)
            r0 = pl.multiple_of(((blk << (lh + 1)) + j) * ct, ct)
            r1 = pl.multiple_of(r0 + half * ct, ct)
            wr = tw_ref[0, s, j]
            wi = tw_ref[1, s, j]
            ar = zr[pl.ds(r0, ct), :]
            ai = zi[pl.ds(r0, ct), :]
            if not inverse:
                if first and half_input:
                    zr[pl.ds(r1, ct), :] = ar * wr - ai * wi
                    zi[pl.ds(r1, ct), :] = ar * wi + ai * wr
                else:
                    br = zr[pl.ds(r1, ct), :]
                    bi = zi[pl.ds(r1, ct), :]
                    dr = ar - br
                    di = ai - bi
                    zr[pl.ds(r0, ct), :] = ar + br
                    zi[pl.ds(r0, ct), :] = ai + bi
                    zr[pl.ds(r1, ct), :] = dr * wr - di * wi
                    zi[pl.ds(r1, ct), :] = dr * wi + di * wr
            else:
                br = zr[pl.ds(r1, ct), :]
                bi = zi[pl.ds(r1, ct), :]
                tr = br * wr + bi * wi
                ti = bi * wr - br * wi
                zr[pl.ds(r0, ct), :] = ar + tr
                zi[pl.ds(r0, ct), :] = ai + ti
                if not (first and half_output):
                    zr[pl.ds(r1, ct), :] = ar - tr
                    zi[pl.ds(r1, ct), :] = ai - ti
            return carry

        lax.fori_loop(0, n1 // 2, body, 0, unroll=2)


def _chunk_dft(xr, xi, w_ref):
    x = jnp.concatenate([xr, xi], axis=1)
    x_hi = x.astype(BF16)
    x_lo = (x - x_hi.astype(F32)).astype(BF16)
    y = _dot(x_hi, w_ref[0]) + (_dot(x_lo, w_ref[0]) + _dot(x_hi, w_ref[1]))
    return y[:, :FFT_CHUNK], y[:, FFT_CHUNK:]


def _fft_groups(n1, ct):
    g = min(8, n1)
    return n1 // g, g * ct


def _hy_spec_kernel(n1, ct, f_ref, tw_ref, twr_ref, twi_ref, wf_ref, fr_ref, fi_ref, zr, zi):
    for s in range(n1):
        zr[s * ct:(s + 1) * ct, :] = f_ref[:, s * FFT_CHUNK:(s + 1) * FFT_CHUNK]
    zi[...] = jnp.zeros(zi.shape, F32)
    _slab_fft(zr, zi, tw_ref, n1, ct, inverse=False)
    ngroups, gr = _fft_groups(n1, ct)

    def body(g, carry):
        r0 = pl.multiple_of(g * gr, gr)
        a = zr[pl.ds(r0, gr), :]
        b = zi[pl.ds(r0, gr), :]
        tr = twr_ref[pl.ds(r0, gr), :]
        ti = twi_ref[pl.ds(r0, gr), :]
        yr, yi = _chunk_dft(a * tr - b * ti, a * ti + b * tr, wf_ref)
        fr_ref[pl.ds(r0, gr), :] = yr
        fi_ref[pl.ds(r0, gr), :] = yi
        return carry

    lax.fori_loop(0, ngroups, body, 0, unroll=min(2, ngroups))


def _hy_spec_call(filt_t):
    c, n = filt_t.shape
    n1 = n // FFT_CHUNK
    ct = HY_CT
    tw, twr, twi, wf, _ = _fft_consts(n1, ct)
    const2 = lambda i: (0, 0)
    out = jax.ShapeDtypeStruct((c // ct, n1 * ct, FFT_CHUNK), F32)
    ospec = pl.BlockSpec((None, n1 * ct, FFT_CHUNK), lambda i: (i, 0, 0))
    return pl.pallas_call(
        functools.partial(_hy_spec_kernel, n1, ct),
        grid=(c // ct,),
        in_specs=[pl.BlockSpec((ct, n), lambda i: (i, 0)),
                  pl.BlockSpec(memory_space=pltpu.SMEM),
                  pl.BlockSpec(twr.shape, const2), pl.BlockSpec(twi.shape, const2),
                  pl.BlockSpec(wf.shape, lambda i: (0, 0, 0))],
        out_specs=[ospec, ospec],
        out_shape=[out, out],
        scratch_shapes=[pltpu.VMEM((n1 * ct, FFT_CHUNK), F32), pltpu.VMEM((n1 * ct, FFT_CHUNK), F32)],
        name="hyena_filter_spectrum",
        compiler_params=_cparams(("parallel",)),
    )(filt_t, jnp.asarray(tw), jnp.asarray(twr), jnp.asarray(twi), jnp.asarray(wf))


def _hy_conv_kernel(n1, ct, length, v_ref, x1_ref, x2_ref, cwv_ref, cw1_ref, cw2_ref, skip_ref,
                    fr_ref, fi_ref, tw_ref, twr_ref, twi_ref, wf_ref, wb_ref, o_ref,
                    zr, zi, u_scr, g_scr):
    lane = lax.broadcasted_iota(jnp.int32, (1, length), 1)

    def short_conv(x, cw_ref):
        xm = jnp.where(lane >= 1, pltpu.roll(x, 1, 1), 0.0)
        xp = jnp.where(lane <= length - 2, pltpu.roll(x, length - 1, 1), 0.0)
        return xm * cw_ref[:, 0:1] + x * cw_ref[:, 1:2] + xp * cw_ref[:, 2:3] + cw_ref[:, 3:4]

    for a in range(2):
        u_scr[a] = short_conv(v_ref[a], cwv_ref) * short_conv(x1_ref[a], cw1_ref)
        g_scr[a] = short_conv(x2_ref[a], cw2_ref)
    nz = length // FFT_CHUNK
    for s in range(nz):
        zr[s * ct:(s + 1) * ct, :] = u_scr[0, :, s * FFT_CHUNK:(s + 1) * FFT_CHUNK]
        zi[s * ct:(s + 1) * ct, :] = u_scr[1, :, s * FFT_CHUNK:(s + 1) * FFT_CHUNK]
    _slab_fft(zr, zi, tw_ref, n1, ct, inverse=False, half_input=True)
    ngroups, gr = _fft_groups(n1, ct)

    def body(g, carry):
        r0 = pl.multiple_of(g * gr, gr)
        a = zr[pl.ds(r0, gr), :]
        b = zi[pl.ds(r0, gr), :]
        tr = twr_ref[pl.ds(r0, gr), :]
        ti = twi_ref[pl.ds(r0, gr), :]
        yr, yi = _chunk_dft(a * tr - b * ti, a * ti + b * tr, wf_ref)
        fr = fr_ref[pl.ds(r0, gr), :]
        fi = fi_ref[pl.ds(r0, gr), :]
        qr, qi = _chunk_dft(yr * fr - yi * fi, yr * fi + yi * fr, wb_ref)
        zr[pl.ds(r0, gr), :] = qr * tr + qi * ti
        zi[pl.ds(r0, gr), :] = qi * tr - qr * ti
        return carry

    lax.fori_loop(0, ngroups, body, 0, unroll=min(2, ngroups))
    _slab_fft(zr, zi, tw_ref, n1, ct, inverse=True, half_output=True)
    inv_n = 1.0 / (n1 * FFT_CHUNK)
    skip = skip_ref[...]
    for s in range(nz):
        sl = slice(s * FFT_CHUNK, (s + 1) * FFT_CHUNK)
        for a, z in ((0, zr), (1, zi)):
            y = z[s * ct:(s + 1) * ct, :] * inv_n
            u = u_scr[a, :, sl]
            o_ref[a, :, sl] = (g_scr[a, :, sl] * (y + u * skip)).astype(o_ref.dtype)


def _hy_conv_call(hy_t, spec_r, spec_i, cw, skip, length, lane_block, prev=None):
    _, bh, c3, t_all = hy_t.shape
    c = c3 // 3
    ct = HY_CT
    nct = c // ct
    n1 = 2 * length // FFT_CHUNK
    tw, twr, twi, wf, wb = _fft_consts(n1, ct)
    const2 = lambda p, i: (0, 0)
    xin = lambda k: pl.BlockSpec((2, None, ct, length), lambda p, i, k=k: (0, p, k * nct + i, lane_block))
    cwin = lambda k: pl.BlockSpec((ct, 4), lambda p, i, k=k: (k * nct + i, 0))
    fin = pl.BlockSpec((None, n1 * ct, FFT_CHUNK), lambda p, i: (i, 0, 0))
    in_specs = [xin(0), xin(1), xin(2), cwin(0), cwin(1), cwin(2),
                pl.BlockSpec((ct, 1), lambda p, i: (i, 0)),
                fin, fin,
                pl.BlockSpec(memory_space=pltpu.SMEM),
                pl.BlockSpec(twr.shape, const2), pl.BlockSpec(twi.shape, const2),
                pl.BlockSpec(wf.shape, lambda p, i: (0, 0, 0)),
                pl.BlockSpec(wb.shape, lambda p, i: (0, 0, 0))]
    args = [hy_t, hy_t, hy_t, cw, cw, cw, skip, spec_r, spec_i, jnp.asarray(tw), jnp.asarray(twr),
            jnp.asarray(twi), jnp.asarray(wf), jnp.asarray(wb)]
    kern = functools.partial(_hy_conv_kernel, n1, ct, length)
    aliases = {}
    if prev is not None:
        in_specs.append(pl.BlockSpec(memory_space=pl.ANY))
        args.append(prev)
        aliases = {len(args) - 1: 0}
        kern = lambda *refs, kern=kern: kern(*refs[:14], *refs[15:])
    return pl.pallas_call(
        kern,
        grid=(bh, nct),
        in_specs=in_specs,
        out_specs=pl.BlockSpec((2, None, ct, length), lambda p, i: (0, p, i, lane_block)),
        out_shape=jax.ShapeDtypeStruct((2, bh, c, t_all), BF16),
        scratch_shapes=[pltpu.VMEM((n1 * ct, FFT_CHUNK), F32), pltpu.VMEM((n1 * ct, FFT_CHUNK), F32),
                        pltpu.VMEM((2, ct, length), F32), pltpu.VMEM((2, ct, length), F32)],
        input_output_aliases=aliases,
        name="hyena_conv_%d" % length,
        compiler_params=_cparams(("parallel", "parallel")),
    )(*args)


def _outproj_kernel(n_lat_tiles, ctx_row, x_ref, hy_ref, ga_ref, na_ref, w_ref, mod_ref, o_ref):
    j = pl.program_id(1)
    nbat, tm, d = x_ref.shape
    mix = []
    for s in range(nbat):
        cm = jnp.concatenate([hy_ref[s], ga_ref[s]], axis=0)
        mix.append(jnp.concatenate([cm.astype(F32).T.astype(BF16), na_ref[s]], axis=1))
    y = _dot(jnp.concatenate(mix, axis=0), w_ref[...])
    for s in range(nbat):
        row = jnp.where(j < n_lat_tiles, pl.program_id(0) * nbat + s, ctx_row)
        o_ref[s] = x_ref[s] + _mod_row(mod_ref, row, 2, d) * y[s * tm:(s + 1) * tm]


def _outproj_call(xa, hyo_t, gqa, na, w_out, mod_l, s_len, n_tok):
    bsz, _, d = xa.shape
    tm = TOK_TILE
    nbat = _batch_group(bsz)
    const = lambda b, j: (0, 0)
    tok = lambda n: pl.BlockSpec((nbat, tm, n), lambda b, j: (b, j, 0))
    chan = lambda n: pl.BlockSpec((nbat, n, tm), lambda b, j: (b, 0, j))
    return pl.pallas_call(
        functools.partial(_outproj_kernel, s_len // tm, bsz),
        grid=(bsz // nbat, n_tok // tm),
        in_specs=[tok(d), chan(hyo_t.shape[1]), chan(gqa.shape[1]), tok(na.shape[-1]),
                  pl.BlockSpec(w_out.shape, const), pl.BlockSpec(mod_l.shape, const)],
        out_specs=tok(d),
        out_shape=jax.ShapeDtypeStruct((bsz, n_tok, d), F32),
        name="out_proj",
        compiler_params=_cparams(("parallel", "parallel")),
    )(xa, hyo_t, gqa, na, w_out, mod_l)


def _ffn_kernel(tiles_per_batch, n_lat_tiles, ctx_row, th, x_ref, mod_ref, g_ref, wi_ref, wo_ref,
                o_ref, h_scr, act_scr):
    i = pl.program_id(0)
    d = x_ref.shape[-1]
    f = wo_ref.shape[0]
    groups = x_ref.shape[0] // TOK_TILE

    def mod_row(s):
        gi = i * groups + s
        jj = gi % tiles_per_batch
        return jnp.where(jj < n_lat_tiles, gi // tiles_per_batch, ctx_row)

    for s in range(groups):
        row = mod_row(s)
        sl = slice(s * TOK_TILE, (s + 1) * TOK_TILE)
        h = _norm_mod(x_ref[sl, :], g_ref[...], _mod_row(mod_ref, row, 3, d), _mod_row(mod_ref, row, 4, d))
        h_scr[sl, :] = h.astype(BF16)
    h = h_scr[...]
    for j in range(f // th):
        gate = _dot(h, wi_ref[:, j * th:(j + 1) * th])
        up = _dot(h, wi_ref[:, f + j * th:f + (j + 1) * th])
        act_scr[:, j * th:(j + 1) * th] = (gate * jax.nn.sigmoid(gate) * up).astype(BF16)
    y = _dot(act_scr[...], wo_ref[...])
    for s in range(groups):
        row = mod_row(s)
        sl = slice(s * TOK_TILE, (s + 1) * TOK_TILE)
        o_ref[sl, :] = x_ref[sl, :] + _mod_row(mod_ref, row, 5, d) * y[sl, :]


def _ffn_call(xa, mod_l, g, w_in, w_out, s_len):
    bsz, t_all, d = xa.shape
    f = w_out.shape[0]
    r = bsz * t_all
    tm = 1024 if r % 1024 == 0 else TOK_TILE
    th = 256
    const = lambda i: (0, 0)
    resident = lambda a: pl.BlockSpec(a.shape, const, pipeline_mode=pl.Buffered(1))
    out = pl.pallas_call(
        functools.partial(_ffn_kernel, t_all // TOK_TILE, s_len // TOK_TILE, bsz, th),
        grid=(r // tm,),
        in_specs=[pl.BlockSpec((tm, d), lambda i: (i, 0)),
                  pl.BlockSpec(mod_l.shape, const),
                  pl.BlockSpec((1, d), const),
                  resident(w_in), resident(w_out)],
        out_specs=pl.BlockSpec((tm, d), lambda i: (i, 0)),
        out_shape=jax.ShapeDtypeStruct((r, d), F32),
        scratch_shapes=[pltpu.VMEM((tm, d), BF16), pltpu.VMEM((tm, f), BF16)],
        name="swiglu_ffn",
        compiler_params=_cparams(("parallel",)),
    )(xa.reshape(r, d), mod_l, g, w_in, w_out)
    return out.reshape(bsz, t_all, d)


def _final_kernel(x_ref, g_ref, o_ref):
    x = x_ref[...]
    ms = jnp.mean(x * x, axis=-1, keepdims=True)
    o_ref[...] = x * lax.rsqrt(ms + NORM_EPS) * g_ref[...]


def _final_call(xa, g, s_len):
    bsz, _, d = xa.shape
    tm = TOK_TILE
    return pl.pallas_call(
        _final_kernel,
        grid=(bsz, s_len // tm),
        in_specs=[pl.BlockSpec((None, tm, d), lambda b, j: (b, j, 0)),
                  pl.BlockSpec((1, d), lambda b, j: (0, 0))],
        out_specs=pl.BlockSpec((None, tm, d), lambda b, j: (b, j, 0)),
        out_shape=jax.ShapeDtypeStruct((bsz, s_len, d), F32),
        name="final_norm",
        compiler_params=_cparams(("parallel", "parallel")),
    )(xa, g)


def _rope_tables(s_len, lc):
    pos = np.arange(s_len)
    row = (pos // GRID_W).astype(np.float64)
    col = (pos % GRID_W).astype(np.float64)
    n_f = HEAD_DIM // 4
    inv = (ROPE_THETA ** (-np.arange(n_f, dtype=np.float32) / n_f)).astype(np.float64)
    ang = np.concatenate([row[:, None] * inv, col[:, None] * inv], axis=-1).astype(np.float32).astype(np.float64)
    lane = np.arange(LANES)
    sign = np.where((lane % HEAD_DIM) < HEAD_DIM // 2, -1.0, 1.0)
    cos = np.cos(ang)[:, lane % (HEAD_DIM // 2)]
    sin = np.sin(ang)[:, lane % (HEAD_DIM // 2)] * sign[None, :]
    cos = np.concatenate([cos, np.ones((lc, LANES))], axis=0).astype(np.float32)
    sin = np.concatenate([sin, np.zeros((lc, LANES))], axis=0).astype(np.float32)
    cosc = np.concatenate([np.cos(ang).T, np.ones((HEAD_DIM // 2, lc))], axis=1).astype(np.float32)
    sinc = np.concatenate([np.sin(ang).T, np.zeros((HEAD_DIM // 2, lc))], axis=1).astype(np.float32)
    return jnp.asarray(cosc), jnp.asarray(sinc), jnp.asarray(cos), jnp.asarray(sin)


def kernel(x, c, ctx, c_ctx, w_mod, b_mod, g_mix, g_ffn, w_in, w_out, hy_conv_w, hy_conv_b,
           hy_f_w1, hy_f_b1, hy_f_w2, hy_f_b2, hy_f_w3, hy_f_b3, hy_f_wout, hy_f_freq, hy_skip,
           qk_g_q, qk_g_k, na_rpb, w_ffn_in, w_ffn_out, g_final):
    bsz, s_len, d = x.shape
    lc = ctx.shape[1]
    depth = w_mod.shape[0]
    t_all = s_len + lc
    hyw = hy_skip.shape[1]
    hy_cols = 3 * hyw
    assert bsz % 2 == 0 and s_len % TOK_TILE == 0 and lc == TOK_TILE and s_len % lc == 0

    nb = -(-(bsz + 1) // 8) * 8
    cs = jnp.zeros((nb, d), F32).at[:bsz].set(c).at[bsz].set(c_ctx)
    mod = _mod_call(cs, w_mod, b_mod)

    rope = _rope_tables(s_len, lc)
    tables = _na_bias_tables(na_rpb, s_len // GRID_W)
    q0 = hy_cols
    k0 = q0 + GQA_HEADS * HEAD_DIM
    v0 = k0 + GQA_KV_HEADS * HEAD_DIM
    n0 = v0 + GQA_KV_HEADS * HEAD_DIM
    w_cm = jnp.swapaxes(jnp.concatenate([w_in[:, :, :k0], w_in[:, :, v0:n0]], axis=2), 1, 2).astype(BF16)
    w_tm = jnp.concatenate([w_in[:, :, k0:v0], w_in[:, :, n0:]], axis=2).astype(BF16)
    w_out_b = w_out.astype(BF16)
    w_fi = w_ffn_in.astype(BF16)
    w_fo = w_ffn_out.astype(BF16)
    cw = jnp.concatenate([jnp.swapaxes(hy_conv_w, 1, 2), hy_conv_b[:, :, None]], axis=2)
    per = LANES // HEAD_DIM

    xa = jnp.concatenate([x, ctx], axis=1)
    for l in range(depth):
        last = l == depth - 1
        gq_b = jnp.broadcast_to(qk_g_q[l][:, None], (HEAD_DIM, LANES))
        gk2 = jnp.tile(qk_g_k[l], per)[None]
        hy_t, q, k, v, nq, nk, nv = _inproj_call(xa, mod[l], g_mix[l][None], w_cm[l], w_tm[l],
                                                  gq_b, gk2, rope, hy_cols, s_len)
        n_tok = s_len if last else t_all
        gqa = _gqa_call(q, k, v, s_len, n_tok)
        na = _na_call(nq, nk, nv, tables[l], s_len, n_tok)

        fargs = (hy_f_w1[l], hy_f_b1[l], hy_f_w2[l], hy_f_b2[l], hy_f_w3[l], hy_f_b3[l],
                 hy_f_wout[l], hy_f_freq[l])
        hy_p = hy_t.reshape(2, bsz // 2, hy_cols, t_all)
        skip = hy_skip[l][:, None]
        fr, fi = _hy_spec_call(_hy_filter_call(s_len, *fargs))
        hyo = _hy_conv_call(hy_p, fr, fi, cw[l], skip, s_len, 0)
        if not last:
            frc, fic = _hy_spec_call(_hy_filter_call(lc, *fargs))
            hyo = _hy_conv_call(hy_p, frc, fic, cw[l], skip, lc, s_len // lc, prev=hyo)
        hyo = hyo.reshape(bsz, hyw, t_all)

        xa = _outproj_call(xa, hyo, gqa, na, w_out_b[l], mod[l], s_len, n_tok)
        xa = _ffn_call(xa, mod[l], g_ffn[l][None], w_fi[l], w_fo[l], s_len)
    return _final_call(xa, g_final[None], s_len)
```

```python
import functools
import math

import numpy as np
import jax
import jax.numpy as jnp
from jax import lax
from jax.experimental import pallas as pl
from jax.experimental.pallas import tpu as pltpu

F32 = jnp.float32
BF16 = jnp.bfloat16
HIGHEST = lax.Precision.HIGHEST

GRID_W = 64
HEAD_DIM = 64
GQA_HEADS = 8
GQA_KV_HEADS = 2
GQA_GROUP = GQA_HEADS // GQA_KV_HEADS
NA_HEADS = 4
NA_KH = 8
NA_KW = 16
ROPE_THETA = 10000.0
HY_EMB_DIM = 33
HY_DECAY_TARGET = 1e-2
HY_FAST_DECAY = 0.3
HY_SLOW_DECAY = 1.5
NORM_EPS = 1e-6
ATTN_SCALE = HEAD_DIM ** -0.5
NEG = -1e30
LOG2E = math.log2(math.e)

LANES = 128
TOK_TILE = 256
FFT_CHUNK = LANES
HY_CT = 64
NA_ROWS = TOK_TILE // GRID_W
NA_WIN = NA_ROWS + NA_KH
VMEM_LIMIT = 56 * 1024 * 1024


def _cparams(sem):
    return pltpu.CompilerParams(dimension_semantics=sem, vmem_limit_bytes=VMEM_LIMIT)


def _dot(a, b, **kw):
    return jnp.dot(a, b, preferred_element_type=F32, **kw)


def _dot_nt(a, b, **kw):
    return lax.dot_general(a, b, (((1,), (1,)), ((), ())), preferred_element_type=F32, **kw)


def _mod_kernel(cs_ref, w_ref, b_ref, o_ref):
    cs = cs_ref[...]
    s = cs * jax.nn.sigmoid(cs)
    o_ref[...] = _dot(s, w_ref[...], precision=HIGHEST) + b_ref[...]


def _mod_call(cs, w_mod, b_mod):
    depth, d, n = w_mod.shape
    nb = cs.shape[0]
    tn = 1024
    return pl.pallas_call(
        _mod_kernel,
        grid=(depth, n // tn),
        in_specs=[pl.BlockSpec((nb, d), lambda l, j: (0, 0)),
                  pl.BlockSpec((None, d, tn), lambda l, j: (l, 0, j)),
                  pl.BlockSpec((None, 1, tn), lambda l, j: (l, 0, j))],
        out_specs=pl.BlockSpec((None, nb, tn), lambda l, j: (l, 0, j)),
        out_shape=jax.ShapeDtypeStruct((depth, nb, n), F32),
        name="adaln_mod",
        compiler_params=_cparams(("parallel", "parallel")),
    )(cs, w_mod, b_mod.reshape(depth, 1, n))


def _mod_row(mod_ref, row, k, d):
    return mod_ref[pl.ds(row, 1), k * d:(k + 1) * d]


def _norm_mod(x, g, shift, scale):
    ms = jnp.mean(x * x, axis=-1, keepdims=True)
    y = x * lax.rsqrt(ms + NORM_EPS) * g
    return y * (1.0 + scale) + shift


def _batch_group(bsz):
    return 4 if bsz % 4 == 0 else 2


def _inproj_kernel(n_lat_tiles, ctx_row, hyc, x_ref, mod_ref, g_ref, wcm_ref, wtm_ref, gq_ref, gk_ref,
                   cosc_ref, sinc_ref, cos_ref, sin_ref,
                   hy_ref, q_ref, k_ref, v_ref, nq_ref, nk_ref, nv_ref, h_scr):
    j = pl.program_id(1)
    nbat, tm, d = x_ref.shape
    for s in range(nbat):
        row = jnp.where(j < n_lat_tiles, pl.program_id(0) * nbat + s, ctx_row)
        h = _norm_mod(x_ref[s], g_ref[...], _mod_row(mod_ref, row, 0, d), _mod_row(mod_ref, row, 1, d))
        h_scr[s * tm:(s + 1) * tm, :] = h.astype(BF16)
    h = h_scr[...]
    cm = _dot_nt(wcm_ref[...], h)
    tmj = _dot(h, wtm_ref[...])

    half = HEAD_DIM // 2
    reps = cm.shape[1] // LANES
    gq = jnp.concatenate([gq_ref[...]] * reps, axis=1)
    cosc = jnp.concatenate([cosc_ref[...]] * nbat, axis=1)
    sinc = jnp.concatenate([sinc_ref[...]] * nbat, axis=1)
    for hh in range(GQA_HEADS):
        x = cm[hyc + hh * HEAD_DIM:hyc + (hh + 1) * HEAD_DIM, :]
        ms = jnp.mean(x * x, axis=0, keepdims=True)
        xn = x * lax.rsqrt(ms + NORM_EPS) * gq
        x1, x2 = xn[:half], xn[half:]
        y = jnp.concatenate([x1 * cosc - x2 * sinc, x1 * sinc + x2 * cosc], axis=0)
        y = (y * (ATTN_SCALE * LOG2E)).astype(BF16)
        for s in range(nbat):
            q_ref[s, hh * HEAD_DIM:(hh + 1) * HEAD_DIM, :] = y[:, s * tm:(s + 1) * tm]
    voff = hyc + GQA_HEADS * HEAD_DIM
    nqoff = voff + GQA_KV_HEADS * HEAD_DIM
    nvoff = nqoff + NA_HEADS * HEAD_DIM
    for s in range(nbat):
        cols = slice(s * tm, (s + 1) * tm)
        hy_ref[s] = cm[:hyc, cols]
        v_ref[s] = cm[voff:nqoff, cols].astype(BF16)
        nq_ref[s] = (cm[nqoff:nvoff, cols] * (ATTN_SCALE * LOG2E)).astype(BF16)
        for e in range(NA_HEADS):
            nv_ref[s, e] = cm[nvoff + e * HEAD_DIM:nvoff + (e + 1) * HEAD_DIM, cols].astype(BF16)
        _inproj_token_major(tmj[s * tm:(s + 1) * tm], gk_ref, cos_ref, sin_ref, k_ref.at[s], nk_ref.at[s])


def _inproj_token_major(p, gk_ref, cos_ref, sin_ref, k_ref, nk_ref):
    ri = lax.broadcasted_iota(jnp.int32, (LANES, LANES), 0) // HEAD_DIM
    ci = lax.broadcasted_iota(jnp.int32, (LANES, LANES), 1) // HEAD_DIM
    avg = jnp.where(ri == ci, 1.0 / HEAD_DIM, 0.0).astype(BF16)
    lane = lax.broadcasted_iota(jnp.int32, (1, LANES), 1)
    first_half = (lane % HEAD_DIM) < (HEAD_DIM // 2)
    xc = p[:, :LANES]
    ms = _dot((xc * xc).astype(BF16), avg)
    xn = xc * lax.rsqrt(ms + NORM_EPS) * gk_ref[...]
    partner = jnp.where(first_half, pltpu.roll(xn, LANES - HEAD_DIM // 2, 1),
                        pltpu.roll(xn, HEAD_DIM // 2, 1))
    y = (xn * cos_ref[...] + partner * sin_ref[...]).astype(BF16)
    for e in range(GQA_KV_HEADS):
        k_ref[e] = y[:, e * HEAD_DIM:(e + 1) * HEAD_DIM]
    for e in range(NA_HEADS):
        nk_ref[e] = p[:, LANES + e * HEAD_DIM:LANES + (e + 1) * HEAD_DIM].astype(BF16)


def _inproj_call(xa, mod_l, g, w_cm, w_tm, gq_b, gk2, rope, hyc, s_len):
    bsz, t_all, d = xa.shape
    tm = TOK_TILE
    nbat = _batch_group(bsz)
    cosc, sinc, cos_t, sin_t = rope
    grid = (bsz // nbat, t_all // tm)
    const = lambda b, j: (0, 0)
    resident = lambda a: pl.BlockSpec(a.shape, const, pipeline_mode=pl.Buffered(1))
    heads = lambda n: pl.BlockSpec((nbat, n, tm, HEAD_DIM), lambda b, j: (b, 0, j, 0))
    hshape = lambda n: jax.ShapeDtypeStruct((bsz, n, t_all, HEAD_DIM), BF16)
    chan = lambda n: pl.BlockSpec((nbat, n, tm), lambda b, j: (b, 0, j))
    qc = GQA_HEADS * HEAD_DIM
    vc = GQA_KV_HEADS * HEAD_DIM
    nc = NA_HEADS * HEAD_DIM
    return pl.pallas_call(
        functools.partial(_inproj_kernel, s_len // tm, bsz, hyc),
        grid=grid,
        in_specs=[pl.BlockSpec((nbat, tm, d), lambda b, j: (b, j, 0)),
                  pl.BlockSpec(mod_l.shape, const),
                  pl.BlockSpec((1, d), const),
                  resident(w_cm), resident(w_tm),
                  pl.BlockSpec(gq_b.shape, const),
                  pl.BlockSpec((1, LANES), const),
                  pl.BlockSpec((HEAD_DIM // 2, tm), lambda b, j: (0, j)),
                  pl.BlockSpec((HEAD_DIM // 2, tm), lambda b, j: (0, j)),
                  pl.BlockSpec((tm, LANES), lambda b, j: (j, 0)),
                  pl.BlockSpec((tm, LANES), lambda b, j: (j, 0))],
        out_specs=[chan(hyc), chan(qc), heads(GQA_KV_HEADS), chan(vc),
                   chan(nc), heads(NA_HEADS),
                   pl.BlockSpec((nbat, NA_HEADS, None, HEAD_DIM, tm), lambda b, j: (b, 0, j, 0, 0))],
        out_shape=[jax.ShapeDtypeStruct((bsz, hyc, t_all), F32),
                   jax.ShapeDtypeStruct((bsz, qc, t_all), BF16),
                   hshape(GQA_KV_HEADS),
                   jax.ShapeDtypeStruct((bsz, vc, t_all), BF16),
                   jax.ShapeDtypeStruct((bsz, nc, t_all), BF16),
                   hshape(NA_HEADS),
                   jax.ShapeDtypeStruct((bsz, NA_HEADS, t_all // tm, HEAD_DIM, tm), BF16)],
        scratch_shapes=[pltpu.VMEM((nbat * tm, d), BF16)],
        name="in_proj",
        compiler_params=_cparams(("parallel", "parallel")),
    )(xa, mod_l, g, w_cm, w_tm, gq_b, gk2, cosc, sinc, cos_t, sin_t)


def _gqa_kernel(s_len, lc, tk, q_ref, k_ref, vt_ref, o_ref, s_scr):
    i = pl.program_id(2)
    tm = q_ref.shape[1]
    sub = 8

    def scores(h, r0, nrow, m):
        s = _dot(k_ref[r0:r0 + nrow, :], q_ref[h * HEAD_DIM:(h + 1) * HEAD_DIM, :])
        s_scr[h, r0:r0 + nrow, :] = s
        return jnp.maximum(m, jnp.max(s.reshape(nrow // sub, sub, tm), axis=0))

    def apply(h, r0, nrow, m, l, acc):
        p = jnp.exp2(s_scr[h, r0:r0 + nrow, :].reshape(nrow // sub, sub, tm) - m[None])
        l = l + jnp.sum(p, axis=0)
        pt = p.reshape(nrow, tm).astype(BF16)
        for k0 in range(0, nrow, TOK_TILE):
            acc = acc + _dot(vt_ref[:, r0 + k0:r0 + k0 + TOK_TILE], pt[k0:k0 + TOK_TILE])
        return l, acc

    def run(chunks):
        maxes = [None] * GQA_GROUP
        for stage in range(GQA_GROUP + 1):
            h1, h2 = stage, stage - 1
            if h1 < GQA_GROUP:
                m1 = jnp.full((sub, tm), NEG, F32)
            if h2 >= 0:
                m2 = jnp.broadcast_to(jnp.max(maxes[h2], axis=0, keepdims=True), (sub, tm))
                l = jnp.zeros((sub, tm), F32)
                acc = jnp.zeros((HEAD_DIM, tm), F32)
            for r0, nrow in chunks:
                if h1 < GQA_GROUP:
                    m1 = scores(h1, r0, nrow, m1)
                if h2 >= 0:
                    l, acc = apply(h2, r0, nrow, m2, l, acc)
            if h1 < GQA_GROUP:
                maxes[h1] = m1
            if h2 >= 0:
                o = acc / jnp.sum(l, axis=0, keepdims=True)
                o_ref[h2 * HEAD_DIM:(h2 + 1) * HEAD_DIM, :] = o.astype(o_ref.dtype)

    ctx_chunk = (s_len, lc)

    @pl.when(i * tm < s_len)
    def _():
        run([(c * tk, tk) for c in range(s_len // tk)] + [ctx_chunk])

    @pl.when(i * tm >= s_len)
    def _():
        run([ctx_chunk])


def _gqa_call(qt, k, vt, s_len, n_query):
    bsz, qc, t_all = qt.shape
    hd = HEAD_DIM
    tm = TOK_TILE
    tk = 2 * TOK_TILE
    lc = t_all - s_len
    return pl.pallas_call(
        functools.partial(_gqa_kernel, s_len, lc, tk),
        grid=(bsz, GQA_KV_HEADS, n_query // tm),
        in_specs=[pl.BlockSpec((None, GQA_GROUP * hd, tm), lambda b, g, i: (b, g, i)),
                  pl.BlockSpec((None, None, t_all, hd), lambda b, g, i: (b, g, 0, 0)),
                  pl.BlockSpec((None, hd, t_all), lambda b, g, i: (b, g, 0))],
        out_specs=pl.BlockSpec((None, GQA_GROUP * hd, tm), lambda b, g, i: (b, g, i)),
        out_shape=jax.ShapeDtypeStruct((bsz, qc, t_all), BF16),
        scratch_shapes=[pltpu.VMEM((GQA_GROUP, t_all, tm), F32)],
        name="gqa_attention",
        compiler_params=_cparams(("parallel", "parallel", "parallel")),
    )(qt, k, vt)


def _na_bias_consts(rows):
    assert rows >= NA_WIN + NA_ROWS
    nrow, ncol = 2 * NA_KH - 1, 2 * NA_KW - 1
    rsel = np.zeros((3, NA_ROWS, NA_WIN, nrow), np.float32)
    mrow = np.zeros((3, NA_ROWS, NA_WIN), bool)
    for kind, r0 in enumerate((0, NA_ROWS, rows - NA_ROWS)):
        kr0 = int(np.clip(r0 - NA_KH // 2, 0, rows - NA_WIN))
        for qr in range(NA_ROWS):
            r = r0 + qr
            rs = int(np.clip(r - NA_KH // 2, 0, rows - NA_KH))
            for kr in range(NA_WIN):
                ka = kr0 + kr
                if rs <= ka < rs + NA_KH:
                    mrow[kind, qr, kr] = True
                    rsel[kind, qr, kr, ka - r + NA_KH - 1] = 1.0
    csel = np.zeros((GRID_W, GRID_W, ncol), np.float32)
    mcol = np.zeros((GRID_W, GRID_W), bool)
    for w in range(GRID_W):
        cs = int(np.clip(w - NA_KW // 2, 0, GRID_W - NA_KW))
        for kc in range(cs, cs + NA_KW):
            mcol[w, kc] = True
            csel[w, kc, kc - w + NA_KW - 1] = 1.0
    mask = mrow[:, :, None, :, None] & mcol[None, None, :, None, :]
    return rsel, csel, mask.reshape(3, TOK_TILE, NA_WIN * GRID_W)


def _na_bias_tables(na_rpb, rows):
    rsel, csel, mask = _na_bias_consts(rows)
    depth, heads = na_rpb.shape[:2]
    a = jnp.einsum("kqra,lhab->lkhqrb", jnp.asarray(rsel), na_rpb.astype(F32), precision=HIGHEST)
    t = jnp.einsum("lkhqrb,wcb->lkhrcqw", a, jnp.asarray(csel), precision=HIGHEST)
    t = t.reshape(depth, 3, heads, NA_WIN * GRID_W, TOK_TILE) * LOG2E
    mask_t = np.swapaxes(mask, 1, 2)
    return jnp.where(jnp.asarray(mask_t)[None, :, None], t, NEG)


def _na_kernel(s_len, lc, rows, q_ref, k_ref, vt_ref, tab_ref, o_ref):
    i = pl.program_id(1)
    tm = q_ref.shape[1]
    nblk = s_len // tm
    sub = 8

    def attend(h, parts):
        qh = q_ref[h * HEAD_DIM:(h + 1) * HEAD_DIM, :]
        scores = []
        m = None
        for r0, _, nrow, bias in parts:
            s = _dot(k_ref[h, pl.ds(r0, nrow), :], qh)
            if bias is not None:
                s = s + bias
            scores.append(s)
            mm = jnp.max(s.reshape(nrow // sub, sub, tm), axis=0)
            m = mm if m is None else jnp.maximum(m, mm)
        m = jnp.broadcast_to(jnp.max(m, axis=0, keepdims=True), (sub, tm))
        l = jnp.zeros((sub, tm), F32)
        acc = jnp.zeros((HEAD_DIM, tm), F32)
        for (_, c0, nrow, _), s in zip(parts, scores):
            p = jnp.exp2(s.reshape(nrow // sub, sub, tm) - m[None])
            l = l + jnp.sum(p, axis=0)
            pt = p.reshape(nrow, tm).astype(BF16)
            for c in range(nrow // tm):
                acc = acc + _dot(vt_ref[h, c0 + c], pt[c * tm:(c + 1) * tm])
        o = acc / jnp.sum(l, axis=0, keepdims=True)
        o_ref[h * HEAD_DIM:(h + 1) * HEAD_DIM, :] = o.astype(o_ref.dtype)

    ctx_part = (s_len, s_len // tm, lc, None)

    @pl.when(i < nblk)
    def _():
        kind = jnp.where(i == 0, 0, jnp.where(i == nblk - 1, 2, 1))
        c0 = jnp.clip(i - (NA_KH // 2) // NA_ROWS, 0, (rows - NA_WIN) // NA_ROWS)
        ks = pl.multiple_of(c0 * tm, tm)
        for h in range(NA_HEADS):
            attend(h, [(ks, c0, NA_WIN * GRID_W, tab_ref[kind, h]), ctx_part])

    @pl.when(i >= nblk)
    def _():
        for h in range(NA_HEADS):
            attend(h, [ctx_part])


def _na_call(nqt, nk, nvt, table, s_len, n_query):
    bsz, heads, t_all, hd = nk.shape
    tm = TOK_TILE
    rows = s_len // GRID_W
    assert NA_ROWS * GRID_W == tm and (NA_KH // 2) % NA_ROWS == 0 and (rows - NA_WIN) % NA_ROWS == 0
    return pl.pallas_call(
        functools.partial(_na_kernel, s_len, t_all - s_len, rows),
        grid=(bsz, n_query // tm),
        in_specs=[pl.BlockSpec((None, heads * hd, tm), lambda b, i: (b, 0, i)),
                  pl.BlockSpec((None, heads, t_all, hd), lambda b, i: (b, 0, 0, 0)),
                  pl.BlockSpec((None,) + nvt.shape[1:], lambda b, i: (b, 0, 0, 0, 0)),
                  pl.BlockSpec(table.shape, lambda b, i: (0, 0, 0, 0), pipeline_mode=pl.Buffered(1))],
        out_specs=pl.BlockSpec((None, heads * hd, tm), lambda b, i: (b, 0, i)),
        out_shape=jax.ShapeDtypeStruct((bsz, heads * hd, t_all), BF16),
        name="neighbourhood_attention",
        compiler_params=_cparams(("parallel", "parallel")),
    )(nqt, nk, nvt, table)


def _hy_filter_kernel(z_ref, t_ref, side_ref, w1_ref, b1_ref, w2_ref, b2_ref, w3_ref, b3_ref,
                      wo_ref, fr_ref, dl_ref, o_ref):
    om = fr_ref[...]
    h = jnp.sin(om * (_dot(z_ref[...], w1_ref[...], precision=HIGHEST) + b1_ref[...]))
    h = jnp.sin(om * (_dot(h, w2_ref[...], precision=HIGHEST) + b2_ref[...]))
    h = jnp.sin(om * (_dot(h, w3_ref[...], precision=HIGHEST) + b3_ref[...]))
    ht = _dot_nt(wo_ref[...], h, precision=HIGHEST)
    c = o_ref.shape[0]
    decay = jnp.exp(-t_ref[...] * jnp.abs(dl_ref[...]))
    side = side_ref[...]
    o_ref[...] = decay * (jnp.where(side > 0.5, ht[:c], 0.0) + jnp.where(side < -0.5, ht[c:], 0.0))


def _hy_filter_consts(length):
    n = np.arange(2 * length)
    pos = np.where(n < length, n, 2 * length - n)
    pos = np.where(n == length, 0, pos)
    side = np.where(n < length, 1.0, -1.0)
    side = np.where(n == length, 0.0, side)
    t = np.linspace(0.0, 1.0, length)[pos]
    bands = (HY_EMB_DIM - 1) // 2
    w = 2.0 * math.pi * pos / length
    fr = np.linspace(1e-4, bands - 1, bands)
    z = np.concatenate([t[:, None], np.cos(fr[None, :] * w[:, None]), -np.sin(fr[None, :] * w[:, None])], axis=-1)
    zp = np.zeros((2 * length, 64), np.float32)
    zp[:, :HY_EMB_DIM] = z
    return zp, t.astype(np.float32)[None, :], side.astype(np.float32)[None, :]


def _hy_filter_call(length, w1, b1, w2, b2, w3, b3, wout, freq):
    f = w2.shape[0]
    c = wout.shape[1] // 2
    zp, t, side = _hy_filter_consts(length)
    w1p = jnp.zeros((64, f), F32).at[:HY_EMB_DIM].set(w1)
    deltas = np.linspace(math.log(HY_DECAY_TARGET) / HY_SLOW_DECAY,
                         math.log(HY_DECAY_TARGET) / HY_FAST_DECAY, c).astype(np.float32)[:, None]
    args = (jnp.asarray(zp), jnp.asarray(t), jnp.asarray(side), w1p, b1[None], w2, b2[None], w3, b3[None],
            wout.T, freq[None], jnp.asarray(deltas))
    return pl.pallas_call(
        _hy_filter_kernel,
        in_specs=[pl.BlockSpec(a.shape, lambda: (0,) * a.ndim) for a in args],
        out_specs=pl.BlockSpec((c, 2 * length), lambda: (0, 0)),
        out_shape=jax.ShapeDtypeStruct((c, 2 * length), F32),
        grid=(),
        name="hyena_filter",
        compiler_params=pltpu.CompilerParams(vmem_limit_bytes=VMEM_LIMIT),
    )(*args)


def _bitrev(n):
    bits = int(math.log2(n))
    return np.array([int(format(i, "0%db" % bits)[::-1], 2) if bits else 0 for i in range(n)])


def _fft_consts(n1, ct):
    n = n1 * FFT_CHUNK
    stages = max(int(math.log2(n1)), 1)
    tw = np.zeros((2, stages, max(n1 // 2, 1)), np.float32)
    for s in range(int(math.log2(n1))):
        half = n1 >> (s + 1)
        ang = -2.0 * math.pi * np.arange(half) / (2 * half)
        tw[0, s, :half] = np.cos(ang)
        tw[1, s, :half] = np.sin(ang)
    k1 = _bitrev(n1)
    ang = -2.0 * math.pi * (np.arange(FFT_CHUNK)[None, :] * k1[:, None]) / n
    twr = np.repeat(np.cos(ang), ct, axis=0).astype(np.float32)
    twi = np.repeat(np.sin(ang), ct, axis=0).astype(np.float32)
    a = -2.0 * math.pi * np.outer(np.arange(FFT_CHUNK), np.arange(FFT_CHUNK)) / FFT_CHUNK
    wr, wi = np.cos(a), np.sin(a)
    wf = np.block([[wr, wi], [-wi, wr]]).astype(np.float32)
    wb = np.block([[wr, -wi], [wi, wr]]).astype(np.float32)
    return tw, twr, twi, _split_bf16(wf), _split_bf16(wb)


def _split_bf16(w):
    w = jnp.asarray(w, F32)
    hi = w.astype(BF16)
    return jnp.stack([hi, (w - hi.astype(F32)).astype(BF16)])


def _slab_fft(zr, zi, tw_ref, n1, ct, inverse, half_input=False, half_output=False):
    nst = int(math.log2(n1))
    order = range(nst - 1, -1, -1) if inverse else range(nst)
    for s in order:
        half = n1 >> (s + 1)
        lh = int(math.log2(half))
        first = s == 0

        def body(idx, carry, s=s, half=half, lh=lh, first=first):
            blk = idx >> lh
            j = idx & (half - 1)
            r0 = pl.multiple_of(((blk << (lh + 1)) + j) * ct, ct)
            r1 = pl.multiple_of(r0 + half * ct, ct)
            wr = tw_ref[0, s, j]
            wi = tw_ref[1, s, j]
            ar = zr[pl.ds(r0, ct), :]
            ai = zi[pl.ds(r0, ct), :]
            if not inverse:
                if first and half_input:
                    zr[pl.ds(r1, ct), :] = ar * wr - ai * wi
                    zi[pl.ds(r1, ct), :] = ar * wi + ai * wr
                else:
                    br = zr[pl.ds(r1, ct), :]
                    bi = zi[pl.ds(r1, ct), :]
                    dr = ar - br
                    di = ai - bi
                    zr[pl.ds(r0, ct), :] = ar + br
                    zi[pl.ds(r0, ct), :] = ai + bi
                    zr[pl.ds(r1, ct), :] = dr * wr - di * wi
                    zi[pl.ds(r1, ct), :] = dr * wi + di * wr
            else:
                br = zr[pl.ds(r1, ct), :]
                bi = zi[pl.ds(r1, ct), :]
                tr = br * wr + bi * wi
                ti = bi * wr - br * wi
                zr[pl.ds(r0, ct), :] = ar + tr
                zi[pl.ds(r0, ct), :] = ai + ti
                if not (first and half_output):
                    zr[pl.ds(r1, ct), :] = ar - tr
                    zi[pl.ds(r1, ct), :] = ai - ti
            return carry

        lax.fori_loop(0, n1 // 2, body, 0, unroll=2)


def _chunk_dft(xr, xi, w_ref):
    x = jnp.concatenate([xr, xi], axis=1)
    x_hi = x.astype(BF16)
    x_lo = (x - x_hi.astype(F32)).astype(BF16)
    y = _dot(x_hi, w_ref[0]) + (_dot(x_lo, w_ref[0]) + _dot(x_hi, w_ref[1]))
    return y[:, :FFT_CHUNK], y[:, FFT_CHUNK:]


def _fft_groups(n1, ct):
    g = min(8, n1)
    return n1 // g, g * ct


def _hy_spec_kernel(n1, ct, f_ref, tw_ref, twr_ref, twi_ref, wf_ref, fr_ref, fi_ref, zr, zi):
    for s in range(n1):
        zr[s * ct:(s + 1) * ct, :] = f_ref[:, s * FFT_CHUNK:(s + 1) * FFT_CHUNK]
    zi[...] = jnp.zeros(zi.shape, F32)
    _slab_fft(zr, zi, tw_ref, n1, ct, inverse=False)
    ngroups, gr = _fft_groups(n1, ct)

    def body(g, carry):
        r0 = pl.multiple_of(g * gr, gr)
        a = zr[pl.ds(r0, gr), :]
        b = zi[pl.ds(r0, gr), :]
        tr = twr_ref[pl.ds(r0, gr), :]
        ti = twi_ref[pl.ds(r0, gr), :]
        yr, yi = _chunk_dft(a * tr - b * ti, a * ti + b * tr, wf_ref)
        fr_ref[pl.ds(r0, gr), :] = yr
        fi_ref[pl.ds(r0, gr), :] = yi
        return carry

    lax.fori_loop(0, ngroups, body, 0, unroll=min(2, ngroups))


def _hy_spec_call(filt_t):
    c, n = filt_t.shape
    n1 = n // FFT_CHUNK
    ct = HY_CT
    tw, twr, twi, wf, _ = _fft_consts(n1, ct)
    const2 = lambda i: (0, 0)
    out = jax.ShapeDtypeStruct((c // ct, n1 * ct, FFT_CHUNK), F32)
    ospec = pl.BlockSpec((None, n1 * ct, FFT_CHUNK), lambda i: (i, 0, 0))
    return pl.pallas_call(
        functools.partial(_hy_spec_kernel, n1, ct),
        grid=(c // ct,),
        in_specs=[pl.BlockSpec((ct, n), lambda i: (i, 0)),
                  pl.BlockSpec(memory_space=pltpu.SMEM),
                  pl.BlockSpec(twr.shape, const2), pl.BlockSpec(twi.shape, const2),
                  pl.BlockSpec(wf.shape, lambda i: (0, 0, 0))],
        out_specs=[ospec, ospec],
        out_shape=[out, out],
        scratch_shapes=[pltpu.VMEM((n1 * ct, FFT_CHUNK), F32), pltpu.VMEM((n1 * ct, FFT_CHUNK), F32)],
        name="hyena_filter_spectrum",
        compiler_params=_cparams(("parallel",)),
    )(filt_t, jnp.asarray(tw), jnp.asarray(twr), jnp.asarray(twi), jnp.asarray(wf))


def _hy_conv_kernel(n1, ct, length, v_ref, x1_ref, x2_ref, cwv_ref, cw1_ref, cw2_ref, skip_ref,
                    fr_ref, fi_ref, tw_ref, twr_ref, twi_ref, wf_ref, wb_ref, o_ref,
                    zr, zi, u_scr, g_scr):
    lane = lax.broadcasted_iota(jnp.int32, (1, length), 1)

    def short_conv(x, cw_ref):
        xm = jnp.where(lane >= 1, pltpu.roll(x, 1, 1), 0.0)
        xp = jnp.where(lane <= length - 2, pltpu.roll(x, length - 1, 1), 0.0)
        return xm * cw_ref[:, 0:1] + x * cw_ref[:, 1:2] + xp * cw_ref[:, 2:3] + cw_ref[:, 3:4]

    for a in range(2):
        u_scr[a] = short_conv(v_ref[a], cwv_ref) * short_conv(x1_ref[a], cw1_ref)
        g_scr[a] = short_conv(x2_ref[a], cw2_ref)
    nz = length // FFT_CHUNK
    for s in range(nz):
        zr[s * ct:(s + 1) * ct, :] = u_scr[0, :, s * FFT_CHUNK:(s + 1) * FFT_CHUNK]
        zi[s * ct:(s + 1) * ct, :] = u_scr[1, :, s * FFT_CHUNK:(s + 1) * FFT_CHUNK]
    _slab_fft(zr, zi, tw_ref, n1, ct, inverse=False, half_input=True)
    ngroups, gr = _fft_groups(n1, ct)

    def body(g, carry):
        r0 = pl.multiple_of(g * gr, gr)
        a = zr[pl.ds(r0, gr), :]
        b = zi[pl.ds(r0, gr), :]
        tr = twr_ref[pl.ds(r0, gr), :]
        ti = twi_ref[pl.ds(r0, gr), :]
        yr, yi = _chunk_dft(a * tr - b * ti, a * ti + b * tr, wf_ref)
        fr = fr_ref[pl.ds(r0, gr), :]
        fi = fi_ref[pl.ds(r0, gr), :]
        qr, qi = _chunk_dft(yr * fr - yi * fi, yr * fi + yi * fr, wb_ref)
        zr[pl.ds(r0, gr), :] = qr * tr + qi * ti
        zi[pl.ds(r0, gr), :] = qi * tr - qr * ti
        return carry

    lax.fori_loop(0, ngroups, body, 0, unroll=min(2, ngroups))
    _slab_fft(zr, zi, tw_ref, n1, ct, inverse=True, half_output=True)
    inv_n = 1.0 / (n1 * FFT_CHUNK)
    skip = skip_ref[...]
    for s in range(nz):
        sl = slice(s * FFT_CHUNK, (s + 1) * FFT_CHUNK)
        for a, z in ((0, zr), (1, zi)):
            y = z[s * ct:(s + 1) * ct, :] * inv_n
            u = u_scr[a, :, sl]
            o_ref[a, :, sl] = (g_scr[a, :, sl] * (y + u * skip)).astype(o_ref.dtype)


def _hy_conv_call(hy_t, spec_r, spec_i, cw, skip, length, lane_block, prev=None):
    _, bh, c3, t_all = hy_t.shape
    c = c3 // 3
    ct = HY_CT
    nct = c // ct
    n1 = 2 * length // FFT_CHUNK
    tw, twr, twi, wf, wb = _fft_consts(n1, ct)
    const2 = lambda p, i: (0, 0)
    xin = lambda k: pl.BlockSpec((2, None, ct, length), lambda p, i, k=k: (0, p, k * nct + i, lane_block))
    cwin = lambda k: pl.BlockSpec((ct, 4), lambda p, i, k=k: (k * nct + i, 0))
    fin = pl.BlockSpec((None, n1 * ct, FFT_CHUNK), lambda p, i: (i, 0, 0))
    in_specs = [xin(0), xin(1), xin(2), cwin(0), cwin(1), cwin(2),
                pl.BlockSpec((ct, 1), lambda p, i: (i, 0)),
                fin, fin,
                pl.BlockSpec(memory_space=pltpu.SMEM),
                pl.BlockSpec(twr.shape, const2), pl.BlockSpec(twi.shape, const2),
                pl.BlockSpec(wf.shape, lambda p, i: (0, 0, 0)),
                pl.BlockSpec(wb.shape, lambda p, i: (0, 0, 0))]
    args = [hy_t, hy_t, hy_t, cw, cw, cw, skip, spec_r, spec_i, jnp.asarray(tw), jnp.asarray(twr),
            jnp.asarray(twi), jnp.asarray(wf), jnp.asarray(wb)]
    kern = functools.partial(_hy_conv_kernel, n1, ct, length)
    aliases = {}
    if prev is not None:
        in_specs.append(pl.BlockSpec(memory_space=pl.ANY))
        args.append(prev)
        aliases = {len(args) - 1: 0}
        kern = lambda *refs, kern=kern: kern(*refs[:14], *refs[15:])
    return pl.pallas_call(
        kern,
        grid=(bh, nct),
        in_specs=in_specs,
        out_specs=pl.BlockSpec((2, None, ct, length), lambda p, i: (0, p, i, lane_block)),
        out_shape=jax.ShapeDtypeStruct((2, bh, c, t_all), BF16),
        scratch_shapes=[pltpu.VMEM((n1 * ct, FFT_CHUNK), F32), pltpu.VMEM((n1 * ct, FFT_CHUNK), F32),
                        pltpu.VMEM((2, ct, length), F32), pltpu.VMEM((2, ct, length), F32)],
        input_output_aliases=aliases,
        name="hyena_conv_%d" % length,
        compiler_params=_cparams(("parallel", "parallel")),
    )(*args)


def _outproj_kernel(n_lat_tiles, ctx_row, x_ref, hy_ref, ga_ref, na_ref, w_ref, mod_ref, o_ref):
    j = pl.program_id(1)
    nbat, tm, d = x_ref.shape
    mix = []
    for s in range(nbat):
        cm = jnp.concatenate([hy_ref[s], ga_ref[s], na_ref[s]], axis=0)
        mix.append(cm.astype(F32).T.astype(BF16))
    y = _dot(jnp.concatenate(mix, axis=0), w_ref[...])
    for s in range(nbat):
        row = jnp.where(j < n_lat_tiles, pl.program_id(0) * nbat + s, ctx_row)
        o_ref[s] = x_ref[s] + _mod_row(mod_ref, row, 2, d) * y[s * tm:(s + 1) * tm]


def _outproj_call(xa, hyo_t, gqa, na, w_out, mod_l, s_len, n_tok):
    bsz, _, d = xa.shape
    tm = TOK_TILE
    nbat = _batch_group(bsz)
    const = lambda b, j: (0, 0)
    tok = lambda n: pl.BlockSpec((nbat, tm, n), lambda b, j: (b, j, 0))
    chan = lambda n: pl.BlockSpec((nbat, n, tm), lambda b, j: (b, 0, j))
    return pl.pallas_call(
        functools.partial(_outproj_kernel, s_len // tm, bsz),
        grid=(bsz // nbat, n_tok // tm),
        in_specs=[tok(d), chan(hyo_t.shape[1]), chan(gqa.shape[1]), chan(na.shape[1]),
                  pl.BlockSpec(w_out.shape, const), pl.BlockSpec(mod_l.shape, const)],
        out_specs=tok(d),
        out_shape=jax.ShapeDtypeStruct((bsz, n_tok, d), F32),
        name="out_proj",
        compiler_params=_cparams(("parallel", "parallel")),
    )(xa, hyo_t, gqa, na, w_out, mod_l)


def _ffn_kernel(tiles_per_batch, n_lat_tiles, ctx_row, th, x_ref, mod_ref, g_ref, wi_ref, wo_ref,
                gf_ref, o_ref, h_scr, act_scr):
    i = pl.program_id(0)
    d = x_ref.shape[-1]
    f = wo_ref.shape[0]
    groups = x_ref.shape[0] // TOK_TILE

    def mod_row(s):
        gi = i * groups + s
        jj = gi % tiles_per_batch
        return jnp.where(jj < n_lat_tiles, gi // tiles_per_batch, ctx_row)

    for s in range(groups):
        row = mod_row(s)
        sl = slice(s * TOK_TILE, (s + 1) * TOK_TILE)
        h = _norm_mod(x_ref[sl, :], g_ref[...], _mod_row(mod_ref, row, 3, d), _mod_row(mod_ref, row, 4, d))
        h_scr[sl, :] = h.astype(BF16)
    h = h_scr[...]
    for j in range(f // th):
        gate = _dot(h, wi_ref[:, j * th:(j + 1) * th])
        up = _dot(h, wi_ref[:, f + j * th:f + (j + 1) * th])
        act_scr[:, j * th:(j + 1) * th] = (gate * jax.nn.sigmoid(gate) * up).astype(BF16)
    y = _dot(act_scr[...], wo_ref[...])
    for s in range(groups):
        row = mod_row(s)
        sl = slice(s * TOK_TILE, (s + 1) * TOK_TILE)
        out = x_ref[sl, :] + _mod_row(mod_ref, row, 5, d) * y[sl, :]
        if gf_ref is not None:
            ms = jnp.mean(out * out, axis=-1, keepdims=True)
            out = out * lax.rsqrt(ms + NORM_EPS) * gf_ref[...]
        o_ref[sl, :] = out


def _ffn_call(xa, mod_l, g, w_in, w_out, s_len, g_final=None):
    bsz, t_all, d = xa.shape
    f = w_out.shape[0]
    r = bsz * t_all
    tm = 1024 if r % 1024 == 0 else TOK_TILE
    th = 256
    const = lambda i: (0, 0)
    resident = lambda a: pl.BlockSpec(a.shape, const, pipeline_mode=pl.Buffered(1))
    in_specs = [pl.BlockSpec((tm, d), lambda i: (i, 0)),
                pl.BlockSpec(mod_l.shape, const),
                pl.BlockSpec((1, d), const),
                resident(w_in), resident(w_out)]
    args = [xa.reshape(r, d), mod_l, g, w_in, w_out]
    kern = functools.partial(_ffn_kernel, t_all // TOK_TILE, s_len // TOK_TILE, bsz, th)
    if g_final is None:
        kern = functools.partial(_ffn_no_final, kern)
    else:
        in_specs.append(pl.BlockSpec((1, d), const))
        args.append(g_final)
    out = pl.pallas_call(
        kern,
        grid=(r // tm,),
        in_specs=in_specs,
        out_specs=pl.BlockSpec((tm, d), lambda i: (i, 0)),
        out_shape=jax.ShapeDtypeStruct((r, d), F32),
        scratch_shapes=[pltpu.VMEM((tm, d), BF16), pltpu.VMEM((tm, f), BF16)],
        name="swiglu_ffn",
        compiler_params=_cparams(("parallel",)),
    )(*args)
    return out.reshape(bsz, t_all, d)


def _ffn_no_final(kern, x_ref, mod_ref, g_ref, wi_ref, wo_ref, o_ref, h_scr, act_scr):
    kern(x_ref, mod_ref, g_ref, wi_ref, wo_ref, None, o_ref, h_scr, act_scr)


def _rope_tables(s_len, lc):
    pos = np.arange(s_len)
    row = (pos // GRID_W).astype(np.float64)
    col = (pos % GRID_W).astype(np.float64)
    n_f = HEAD_DIM // 4
    inv = (ROPE_THETA ** (-np.arange(n_f, dtype=np.float32) / n_f)).astype(np.float64)
    ang = np.concatenate([row[:, None] * inv, col[:, None] * inv], axis=-1).astype(np.float32).astype(np.float64)
    lane = np.arange(LANES)
    sign = np.where((lane % HEAD_DIM) < HEAD_DIM // 2, -1.0, 1.0)
    cos = np.cos(ang)[:, lane % (HEAD_DIM // 2)]
    sin = np.sin(ang)[:, lane % (HEAD_DIM // 2)] * sign[None, :]
    cos = np.concatenate([cos, np.ones((lc, LANES))], axis=0).astype(np.float32)
    sin = np.concatenate([sin, np.zeros((lc, LANES))], axis=0).astype(np.float32)
    cosc = np.concatenate([np.cos(ang).T, np.ones((HEAD_DIM // 2, lc))], axis=1).astype(np.float32)
    sinc = np.concatenate([np.sin(ang).T, np.zeros((HEAD_DIM // 2, lc))], axis=1).astype(np.float32)
    return jnp.asarray(cosc), jnp.asarray(sinc), jnp.asarray(cos), jnp.asarray(sin)


def kernel(x, c, ctx, c_ctx, w_mod, b_mod, g_mix, g_ffn, w_in, w_out, hy_conv_w, hy_conv_b,
           hy_f_w1, hy_f_b1, hy_f_w2, hy_f_b2, hy_f_w3, hy_f_b3, hy_f_wout, hy_f_freq, hy_skip,
           qk_g_q, qk_g_k, na_rpb, w_ffn_in, w_ffn_out, g_final):
    bsz, s_len, d = x.shape
    lc = ctx.shape[1]
    depth = w_mod.shape[0]
    t_all = s_len + lc
    hyw = hy_skip.shape[1]
    hy_cols = 3 * hyw
    assert bsz % 2 == 0 and s_len % TOK_TILE == 0 and lc == TOK_TILE and s_len % lc == 0

    nb = -(-(bsz + 1) // 8) * 8
    cs = jnp.zeros((nb, d), F32).at[:bsz].set(c).at[bsz].set(c_ctx)
    mod = _mod_call(cs, w_mod, b_mod)

    rope = _rope_tables(s_len, lc)
    tables = _na_bias_tables(na_rpb, s_len // GRID_W)
    q0 = hy_cols
    k0 = q0 + GQA_HEADS * HEAD_DIM
    v0 = k0 + GQA_KV_HEADS * HEAD_DIM
    nq0 = v0 + GQA_KV_HEADS * HEAD_DIM
    nk0 = nq0 + NA_HEADS * HEAD_DIM
    nv0 = nk0 + NA_HEADS * HEAD_DIM
    w_cm = jnp.concatenate([w_in[:, :, :k0], w_in[:, :, v0:nk0], w_in[:, :, nv0:]], axis=2)
    w_cm = jnp.swapaxes(w_cm, 1, 2).astype(BF16)
    w_tm = jnp.concatenate([w_in[:, :, k0:v0], w_in[:, :, nk0:nv0]], axis=2).astype(BF16)
    w_out_b = w_out.astype(BF16)
    w_fi = w_ffn_in.astype(BF16)
    w_fo = w_ffn_out.astype(BF16)
    cw = jnp.concatenate([jnp.swapaxes(hy_conv_w, 1, 2), hy_conv_b[:, :, None]], axis=2)
    per = LANES // HEAD_DIM

    xa = jnp.concatenate([x, ctx], axis=1)
    for l in range(depth):
        last = l == depth - 1
        gq_b = jnp.broadcast_to(qk_g_q[l][:, None], (HEAD_DIM, LANES))
        gk2 = jnp.tile(qk_g_k[l], per)[None]
        hy_t, q, k, v, nq, nk, nv = _inproj_call(xa, mod[l], g_mix[l][None], w_cm[l], w_tm[l],
                                                  gq_b, gk2, rope, hy_cols, s_len)
        n_tok = s_len if last else t_all
        gqa = _gqa_call(q, k, v, s_len, n_tok)
        na = _na_call(nq, nk, nv, tables[l], s_len, n_tok)

        fargs = (hy_f_w1[l], hy_f_b1[l], hy_f_w2[l], hy_f_b2[l], hy_f_w3[l], hy_f_b3[l],
                 hy_f_wout[l], hy_f_freq[l])
        hy_p = hy_t.reshape(2, bsz // 2, hy_cols, t_all)
        skip = hy_skip[l][:, None]
        fr, fi = _hy_spec_call(_hy_filter_call(s_len, *fargs))
        hyo = _hy_conv_call(hy_p, fr, fi, cw[l], skip, s_len, 0)
        if not last:
            frc, fic = _hy_spec_call(_hy_filter_call(lc, *fargs))
            hyo = _hy_conv_call(hy_p, frc, fic, cw[l], skip, lc, s_len // lc, prev=hyo)
        hyo = hyo.reshape(bsz, hyw, t_all)

        xa = _outproj_call(xa, hyo, gqa, na, w_out_b[l], mod[l], s_len, n_tok)
        xa = _ffn_call(xa, mod[l], g_ffn[l][None], w_fi[l], w_fo[l], s_len,
                       g_final[None] if last else None)
    return xa
```

```python
import functools
import math

import numpy as np
import jax
import jax.numpy as jnp
from jax import lax
from jax.experimental import pallas as pl
from jax.experimental.pallas import tpu as pltpu

F32 = jnp.float32
BF16 = jnp.bfloat16
HIGHEST = lax.Precision.HIGHEST

GRID_W = 64
HEAD_DIM = 64
GQA_HEADS = 8
GQA_KV_HEADS = 2
GQA_GROUP = GQA_HEADS // GQA_KV_HEADS
GQA_PAIR = 2
NA_HEADS = 4
NA_KH = 8
NA_KW = 16
ROPE_THETA = 10000.0
HY_EMB_DIM = 33
HY_DECAY_TARGET = 1e-2
HY_FAST_DECAY = 0.3
HY_SLOW_DECAY = 1.5
NORM_EPS = 1e-6
ATTN_SCALE = HEAD_DIM ** -0.5
NEG = -1e30
LOG2E = math.log2(math.e)

LANES = 128
TOK_TILE = 256
FFT_CHUNK = LANES
HY_CT = 64
NA_ROWS = TOK_TILE // GRID_W
NA_WIN = NA_ROWS + NA_KH
VMEM_LIMIT = 56 * 1024 * 1024


def _cparams(sem):
    return pltpu.CompilerParams(dimension_semantics=sem, vmem_limit_bytes=VMEM_LIMIT)


def _dot(a, b, **kw):
    return jnp.dot(a, b, preferred_element_type=F32, **kw)


def _dot_nt(a, b, **kw):
    return lax.dot_general(a, b, (((1,), (1,)), ((), ())), preferred_element_type=F32, **kw)


def _mod_kernel(cs_ref, w_ref, b_ref, o_ref):
    cs = cs_ref[...]
    s = cs * jax.nn.sigmoid(cs)
    o_ref[...] = _dot(s, w_ref[...], precision=HIGHEST) + b_ref[...]


def _mod_call(cs, w_mod, b_mod):
    depth, d, n = w_mod.shape
    nb = cs.shape[0]
    tn = 1024
    return pl.pallas_call(
        _mod_kernel,
        grid=(depth, n // tn),
        in_specs=[pl.BlockSpec((nb, d), lambda l, j: (0, 0)),
                  pl.BlockSpec((None, d, tn), lambda l, j: (l, 0, j)),
                  pl.BlockSpec((None, 1, tn), lambda l, j: (l, 0, j))],
        out_specs=pl.BlockSpec((None, nb, tn), lambda l, j: (l, 0, j)),
        out_shape=jax.ShapeDtypeStruct((depth, nb, n), F32),
        name="adaln_mod",
        compiler_params=_cparams(("parallel", "parallel")),
    )(cs, w_mod, b_mod.reshape(depth, 1, n))


def _mod_row(mod_ref, row, k, d):
    return mod_ref[pl.ds(row, 1), k * d:(k + 1) * d]


def _norm_mod(x, g, shift, scale):
    ms = jnp.mean(x * x, axis=-1, keepdims=True)
    y = x * lax.rsqrt(ms + NORM_EPS) * g
    return y * (1.0 + scale) + shift


def _batch_group(bsz):
    return 4 if bsz % 4 == 0 else 2


def _inproj_kernel(n_lat_tiles, ctx_row, hyc, x_ref, mod_ref, g_ref, wcm_ref, wtm_ref, gq_ref, gk_ref,
                   cosc_ref, sinc_ref, cos_ref, sin_ref,
                   hy_ref, q_ref, k_ref, v_ref, nq_ref, nk_ref, nv_ref, h_scr):
    j = pl.program_id(1)
    nbat, tm, d = x_ref.shape
    for s in range(nbat):
        row = jnp.where(j < n_lat_tiles, pl.program_id(0) * nbat + s, ctx_row)
        h = _norm_mod(x_ref[s], g_ref[...], _mod_row(mod_ref, row, 0, d), _mod_row(mod_ref, row, 1, d))
        h_scr[s * tm:(s + 1) * tm, :] = h.astype(BF16)
    h = h_scr[...]
    cm = _dot_nt(wcm_ref[...], h)
    tmj = _dot(h, wtm_ref[...])

    half = HEAD_DIM // 2
    reps = cm.shape[1] // LANES
    gq = jnp.concatenate([gq_ref[...]] * reps, axis=1)
    cosc = jnp.concatenate([cosc_ref[...]] * nbat, axis=1)
    sinc = jnp.concatenate([sinc_ref[...]] * nbat, axis=1)
    for hh in range(GQA_HEADS):
        x = cm[hyc + hh * HEAD_DIM:hyc + (hh + 1) * HEAD_DIM, :]
        ms = jnp.mean(x * x, axis=0, keepdims=True)
        xn = x * lax.rsqrt(ms + NORM_EPS) * gq
        x1, x2 = xn[:half], xn[half:]
        y = jnp.concatenate([x1 * cosc - x2 * sinc, x1 * sinc + x2 * cosc], axis=0)
        y = (y * (ATTN_SCALE * LOG2E)).astype(BF16)
        for s in range(nbat):
            q_ref[s, hh * HEAD_DIM:(hh + 1) * HEAD_DIM, :] = y[:, s * tm:(s + 1) * tm]
    voff = hyc + GQA_HEADS * HEAD_DIM
    nqoff = voff + GQA_KV_HEADS * HEAD_DIM
    nvoff = nqoff + NA_HEADS * HEAD_DIM
    for s in range(nbat):
        cols = slice(s * tm, (s + 1) * tm)
        hy_ref[s] = cm[:hyc, cols]
        v_ref[s] = cm[voff:nqoff, cols].astype(BF16)
        nq_ref[s] = (cm[nqoff:nvoff, cols] * (ATTN_SCALE * LOG2E)).astype(BF16)
        for e in range(NA_HEADS):
            nv_ref[s, e] = cm[nvoff + e * HEAD_DIM:nvoff + (e + 1) * HEAD_DIM, cols].astype(BF16)
        _inproj_token_major(tmj[s * tm:(s + 1) * tm], gk_ref, cos_ref, sin_ref, k_ref.at[s], nk_ref.at[s])


def _inproj_token_major(p, gk_ref, cos_ref, sin_ref, k_ref, nk_ref):
    ri = lax.broadcasted_iota(jnp.int32, (LANES, LANES), 0) // HEAD_DIM
    ci = lax.broadcasted_iota(jnp.int32, (LANES, LANES), 1) // HEAD_DIM
    avg = jnp.where(ri == ci, 1.0 / HEAD_DIM, 0.0).astype(BF16)
    lane = lax.broadcasted_iota(jnp.int32, (1, LANES), 1)
    first_half = (lane % HEAD_DIM) < (HEAD_DIM // 2)
    xc = p[:, :LANES]
    ms = _dot((xc * xc).astype(BF16), avg)
    xn = xc * lax.rsqrt(ms + NORM_EPS) * gk_ref[...]
    partner = jnp.where(first_half, pltpu.roll(xn, LANES - HEAD_DIM // 2, 1),
                        pltpu.roll(xn, HEAD_DIM // 2, 1))
    y = (xn * cos_ref[...] + partner * sin_ref[...]).astype(BF16)
    for e in range(GQA_KV_HEADS):
        k_ref[e] = y[:, e * HEAD_DIM:(e + 1) * HEAD_DIM]
    for e in range(NA_HEADS):
        nk_ref[e] = p[:, LANES + e * HEAD_DIM:LANES + (e + 1) * HEAD_DIM].astype(BF16)


def _inproj_call(xa, mod_l, g, w_cm, w_tm, gq_b, gk2, rope, hyc, s_len):
    bsz, t_all, d = xa.shape
    tm = TOK_TILE
    nbat = _batch_group(bsz)
    cosc, sinc, cos_t, sin_t = rope
    grid = (bsz // nbat, t_all // tm)
    const = lambda b, j: (0, 0)
    resident = lambda a: pl.BlockSpec(a.shape, const, pipeline_mode=pl.Buffered(1))
    heads = lambda n: pl.BlockSpec((nbat, n, tm, HEAD_DIM), lambda b, j: (b, 0, j, 0))
    hshape = lambda n: jax.ShapeDtypeStruct((bsz, n, t_all, HEAD_DIM), BF16)
    chan = lambda n: pl.BlockSpec((nbat, n, tm), lambda b, j: (b, 0, j))
    qc = GQA_HEADS * HEAD_DIM
    vc = GQA_KV_HEADS * HEAD_DIM
    nc = NA_HEADS * HEAD_DIM
    return pl.pallas_call(
        functools.partial(_inproj_kernel, s_len // tm, bsz, hyc),
        grid=grid,
        in_specs=[pl.BlockSpec((nbat, tm, d), lambda b, j: (b, j, 0)),
                  pl.BlockSpec(mod_l.shape, const),
                  pl.BlockSpec((1, d), const),
                  resident(w_cm), resident(w_tm),
                  pl.BlockSpec(gq_b.shape, const),
                  pl.BlockSpec((1, LANES), const),
                  pl.BlockSpec((HEAD_DIM // 2, tm), lambda b, j: (0, j)),
                  pl.BlockSpec((HEAD_DIM // 2, tm), lambda b, j: (0, j)),
                  pl.BlockSpec((tm, LANES), lambda b, j: (j, 0)),
                  pl.BlockSpec((tm, LANES), lambda b, j: (j, 0))],
        out_specs=[chan(hyc), chan(qc), heads(GQA_KV_HEADS), chan(vc),
                   chan(nc), heads(NA_HEADS),
                   pl.BlockSpec((nbat, NA_HEADS, None, HEAD_DIM, tm), lambda b, j: (b, 0, j, 0, 0))],
        out_shape=[jax.ShapeDtypeStruct((bsz, hyc, t_all), F32),
                   jax.ShapeDtypeStruct((bsz, qc, t_all), BF16),
                   hshape(GQA_KV_HEADS),
                   jax.ShapeDtypeStruct((bsz, vc, t_all), BF16),
                   jax.ShapeDtypeStruct((bsz, nc, t_all), BF16),
                   hshape(NA_HEADS),
                   jax.ShapeDtypeStruct((bsz, NA_HEADS, t_all // tm, HEAD_DIM, tm), BF16)],
        scratch_shapes=[pltpu.VMEM((nbat * tm, d), BF16)],
        name="in_proj",
        compiler_params=_cparams(("parallel", "parallel")),
    )(xa, mod_l, g, w_cm, w_tm, gq_b, gk2, cosc, sinc, cos_t, sin_t)


def _gqa_kernel(s_len, lc, tk, q_ref, k_ref, vt_ref, o_ref, s_scr):
    i = pl.program_id(2)
    tm = q_ref.shape[1]
    sub = 8
    per = GQA_GROUP // s_scr.shape[0]
    width = per * tm

    def unit_q(u):
        return jnp.concatenate([q_ref[(u * per + e) * HEAD_DIM:(u * per + e + 1) * HEAD_DIM, :]
                                for e in range(per)], axis=1)

    def scores(u, q, r0, nrow, m):
        s = _dot(k_ref[r0:r0 + nrow, :], q)
        s_scr[u, r0:r0 + nrow, :] = s
        return jnp.maximum(m, jnp.max(s.reshape(nrow // sub, sub, width), axis=0))

    def apply(u, r0, nrow, m, l, acc):
        p = jnp.exp2(s_scr[u, r0:r0 + nrow, :].reshape(nrow // sub, sub, width) - m[None])
        l = l + jnp.sum(p, axis=0)
        acc = acc + _dot(vt_ref[:, r0:r0 + nrow], p.reshape(nrow, width).astype(BF16))
        return l, acc

    def run(chunks):
        units = s_scr.shape[0]
        maxes = [None] * units
        for stage in range(units + 1):
            u1, u2 = stage, stage - 1
            if u1 < units:
                q1 = unit_q(u1)
                m1 = jnp.full((sub, width), NEG, F32)
            if u2 >= 0:
                m2 = jnp.broadcast_to(jnp.max(maxes[u2], axis=0, keepdims=True), (sub, width))
                l = jnp.zeros((sub, width), F32)
                acc = jnp.zeros((HEAD_DIM, width), F32)
            for r0, nrow in chunks:
                if u1 < units:
                    m1 = scores(u1, q1, r0, nrow, m1)
                if u2 >= 0:
                    l, acc = apply(u2, r0, nrow, m2, l, acc)
            if u1 < units:
                maxes[u1] = m1
            if u2 >= 0:
                o = (acc / jnp.sum(l, axis=0, keepdims=True)).astype(o_ref.dtype)
                for e in range(per):
                    h = u2 * per + e
                    o_ref[h * HEAD_DIM:(h + 1) * HEAD_DIM, :] = o[:, e * tm:(e + 1) * tm]

    ctx_chunk = (s_len, lc)

    @pl.when(i * tm < s_len)
    def _():
        run([(c * tk, tk) for c in range(s_len // tk)] + [ctx_chunk])

    @pl.when(i * tm >= s_len)
    def _():
        run([ctx_chunk])


def _gqa_call(qt, k, vt, s_len, n_query):
    bsz, qc, t_all = qt.shape
    hd = HEAD_DIM
    tm = TOK_TILE
    tk = 2 * TOK_TILE
    lc = t_all - s_len
    return pl.pallas_call(
        functools.partial(_gqa_kernel, s_len, lc, tk),
        grid=(bsz, GQA_KV_HEADS, n_query // tm),
        in_specs=[pl.BlockSpec((None, GQA_GROUP * hd, tm), lambda b, g, i: (b, g, i)),
                  pl.BlockSpec((None, None, t_all, hd), lambda b, g, i: (b, g, 0, 0)),
                  pl.BlockSpec((None, hd, t_all), lambda b, g, i: (b, g, 0))],
        out_specs=pl.BlockSpec((None, GQA_GROUP * hd, tm), lambda b, g, i: (b, g, i)),
        out_shape=jax.ShapeDtypeStruct((bsz, qc, t_all), BF16),
        scratch_shapes=[pltpu.VMEM((GQA_GROUP // GQA_PAIR, t_all, GQA_PAIR * tm), F32)],
        name="gqa_attention",
        compiler_params=_cparams(("parallel", "parallel", "parallel")),
    )(qt, k, vt)


def _na_bias_consts(rows):
    assert rows >= NA_WIN + NA_ROWS
    nrow, ncol = 2 * NA_KH - 1, 2 * NA_KW - 1
    rsel = np.zeros((3, NA_ROWS, NA_WIN, nrow), np.float32)
    mrow = np.zeros((3, NA_ROWS, NA_WIN), bool)
    for kind, r0 in enumerate((0, NA_ROWS, rows - NA_ROWS)):
        kr0 = int(np.clip(r0 - NA_KH // 2, 0, rows - NA_WIN))
        for qr in range(NA_ROWS):
            r = r0 + qr
            rs = int(np.clip(r - NA_KH // 2, 0, rows - NA_KH))
            for kr in range(NA_WIN):
                ka = kr0 + kr
                if rs <= ka < rs + NA_KH:
                    mrow[kind, qr, kr] = True
                    rsel[kind, qr, kr, ka - r + NA_KH - 1] = 1.0
    csel = np.zeros((GRID_W, GRID_W, ncol), np.float32)
    mcol = np.zeros((GRID_W, GRID_W), bool)
    for w in range(GRID_W):
        cs = int(np.clip(w - NA_KW // 2, 0, GRID_W - NA_KW))
        for kc in range(cs, cs + NA_KW):
            mcol[w, kc] = True
            csel[w, kc, kc - w + NA_KW - 1] = 1.0
    mask = mrow[:, :, None, :, None] & mcol[None, None, :, None, :]
    return rsel, csel, mask.reshape(3, TOK_TILE, NA_WIN * GRID_W)


def _na_bias_tables(na_rpb, rows):
    rsel, csel, mask = _na_bias_consts(rows)
    depth, heads = na_rpb.shape[:2]
    a = jnp.einsum("kqra,lhab->lkhqrb", jnp.asarray(rsel), na_rpb.astype(F32) * LOG2E, precision=HIGHEST)
    t = jnp.einsum("lkhqrb,wcb->lkhrcqw", a, jnp.asarray(csel), precision=HIGHEST)
    t = t.reshape(depth, 3, heads, NA_WIN * GRID_W, TOK_TILE)
    mask_t = np.swapaxes(mask, 1, 2)
    return jnp.where(jnp.asarray(mask_t)[None, :, None], t, NEG)


def _na_kernel(s_len, lc, rows, q_ref, k_ref, vt_ref, tab_ref, o_ref, s_scr):
    i = pl.program_id(1)
    tm = q_ref.shape[1]
    nblk = s_len // tm
    sub = 8

    def scores(h, part, off, m):
        r0, _, nrow, bias = part
        s = _dot(k_ref[h, pl.ds(r0, nrow), :], q_ref[h * HEAD_DIM:(h + 1) * HEAD_DIM, :])
        if bias is not None:
            s = s + bias[h]
        s_scr[h, off:off + nrow, :] = s
        return jnp.maximum(m, jnp.max(s.reshape(nrow // sub, sub, tm), axis=0))

    def apply(h, part, off, m, l, acc):
        _, c0, nrow, _ = part
        p = jnp.exp2(s_scr[h, off:off + nrow, :].reshape(nrow // sub, sub, tm) - m[None])
        l = l + jnp.sum(p, axis=0)
        vt = jnp.concatenate([vt_ref[h, c0 + c] for c in range(nrow // tm)], axis=1)
        return l, acc + _dot(vt, p.reshape(nrow, tm).astype(BF16))

    def run(parts):
        offs = [sum(p[2] for p in parts[:n]) for n in range(len(parts))]
        maxes = [None] * NA_HEADS
        for stage in range(NA_HEADS + 1):
            h1, h2 = stage, stage - 1
            if h1 < NA_HEADS:
                m1 = jnp.full((sub, tm), NEG, F32)
            if h2 >= 0:
                m2 = jnp.broadcast_to(jnp.max(maxes[h2], axis=0, keepdims=True), (sub, tm))
                l = jnp.zeros((sub, tm), F32)
                acc = jnp.zeros((HEAD_DIM, tm), F32)
            for part, off in zip(parts, offs):
                if h1 < NA_HEADS:
                    m1 = scores(h1, part, off, m1)
                if h2 >= 0:
                    l, acc = apply(h2, part, off, m2, l, acc)
            if h1 < NA_HEADS:
                maxes[h1] = m1
            if h2 >= 0:
                o = acc / jnp.sum(l, axis=0, keepdims=True)
                o_ref[h2 * HEAD_DIM:(h2 + 1) * HEAD_DIM, :] = o.astype(o_ref.dtype)

    ctx_part = (s_len, s_len // tm, lc, None)

    @pl.when(i < nblk)
    def _():
        kind = jnp.where(i == 0, 0, jnp.where(i == nblk - 1, 2, 1))
        c0 = jnp.clip(i - (NA_KH // 2) // NA_ROWS, 0, (rows - NA_WIN) // NA_ROWS)
        ks = pl.multiple_of(c0 * tm, tm)
        run([(ks, c0, NA_WIN * GRID_W, tab_ref.at[kind]), ctx_part])

    @pl.when(i >= nblk)
    def _():
        run([ctx_part])


def _na_call(nqt, nk, nvt, table, s_len, n_query):
    bsz, heads, t_all, hd = nk.shape
    tm = TOK_TILE
    rows = s_len // GRID_W
    assert NA_ROWS * GRID_W == tm and (NA_KH // 2) % NA_ROWS == 0 and (rows - NA_WIN) % NA_ROWS == 0
    return pl.pallas_call(
        functools.partial(_na_kernel, s_len, t_all - s_len, rows),
        grid=(bsz, n_query // tm),
        in_specs=[pl.BlockSpec((None, heads * hd, tm), lambda b, i: (b, 0, i)),
                  pl.BlockSpec((None, heads, t_all, hd), lambda b, i: (b, 0, 0, 0)),
                  pl.BlockSpec((None,) + nvt.shape[1:], lambda b, i: (b, 0, 0, 0, 0)),
                  pl.BlockSpec(table.shape, lambda b, i: (0, 0, 0, 0), pipeline_mode=pl.Buffered(1))],
        out_specs=pl.BlockSpec((None, heads * hd, tm), lambda b, i: (b, 0, i)),
        out_shape=jax.ShapeDtypeStruct((bsz, heads * hd, t_all), BF16),
        scratch_shapes=[pltpu.VMEM((heads, NA_WIN * GRID_W + t_all - s_len, tm), F32)],
        name="neighbourhood_attention",
        compiler_params=_cparams(("parallel", "parallel")),
    )(nqt, nk, nvt, table)


def _hy_filter_kernel(z_ref, t_ref, side_ref, w1_ref, b1_ref, w2_ref, b2_ref, w3_ref, b3_ref,
                      wo_ref, fr_ref, dl_ref, o_ref):
    om = fr_ref[...]
    h = jnp.sin(om * (_dot(z_ref[...], w1_ref[...], precision=HIGHEST) + b1_ref[...]))
    h = jnp.sin(om * (_dot(h, w2_ref[...], precision=HIGHEST) + b2_ref[...]))
    h = jnp.sin(om * (_dot(h, w3_ref[...], precision=HIGHEST) + b3_ref[...]))
    ht = _dot_nt(wo_ref[...], h, precision=HIGHEST)
    c = o_ref.shape[0]
    decay = jnp.exp(-t_ref[...] * jnp.abs(dl_ref[...]))
    side = side_ref[...]
    o_ref[...] = decay * (jnp.where(side > 0.5, ht[:c], 0.0) + jnp.where(side < -0.5, ht[c:], 0.0))


def _hy_filter_consts(length):
    n = np.arange(2 * length)
    pos = np.where(n < length, n, 2 * length - n)
    pos = np.where(n == length, 0, pos)
    side = np.where(n < length, 1.0, -1.0)
    side = np.where(n == length, 0.0, side)
    t = np.linspace(0.0, 1.0, length)[pos]
    bands = (HY_EMB_DIM - 1) // 2
    w = 2.0 * math.pi * pos / length
    fr = np.linspace(1e-4, bands - 1, bands)
    z = np.concatenate([t[:, None], np.cos(fr[None, :] * w[:, None]), -np.sin(fr[None, :] * w[:, None])], axis=-1)
    zp = np.zeros((2 * length, 64), np.float32)
    zp[:, :HY_EMB_DIM] = z
    return zp, t.astype(np.float32)[None, :], side.astype(np.float32)[None, :]


def _hy_filter_call(length, w1, b1, w2, b2, w3, b3, wout, freq):
    f = w2.shape[0]
    c = wout.shape[1] // 2
    zp, t, side = _hy_filter_consts(length)
    w1p = jnp.zeros((64, f), F32).at[:HY_EMB_DIM].set(w1)
    deltas = np.linspace(math.log(HY_DECAY_TARGET) / HY_SLOW_DECAY,
                         math.log(HY_DECAY_TARGET) / HY_FAST_DECAY, c).astype(np.float32)[:, None]
    args = (jnp.asarray(zp), jnp.asarray(t), jnp.asarray(side), w1p, b1[None], w2, b2[None], w3, b3[None],
            wout.T, freq[None], jnp.asarray(deltas))
    return pl.pallas_call(
        _hy_filter_kernel,
        in_specs=[pl.BlockSpec(a.shape, lambda: (0,) * a.ndim) for a in args],
        out_specs=pl.BlockSpec((c, 2 * length), lambda: (0, 0)),
        out_shape=jax.ShapeDtypeStruct((c, 2 * length), F32),
        grid=(),
        name="hyena_filter",
        compiler_params=pltpu.CompilerParams(vmem_limit_bytes=VMEM_LIMIT),
    )(*args)


def _bitrev(n):
    bits = int(math.log2(n))
    return np.array([int(format(i, "0%db" % bits)[::-1], 2) if bits else 0 for i in range(n)])


def _fft_consts(n1, ct):
    n = n1 * FFT_CHUNK
    stages = max(int(math.log2(n1)), 1)
    tw = np.zeros((2, stages, max(n1 // 2, 1)), np.float32)
    for s in range(int(math.log2(n1))):
        half = n1 >> (s + 1)
        ang = -2.0 * math.pi * np.arange(half) / (2 * half)
        tw[0, s, :half] = np.cos(ang)
        tw[1, s, :half] = np.sin(ang)
    k1 = _bitrev(n1)
    ang = -2.0 * math.pi * (np.arange(FFT_CHUNK)[None, :] * k1[:, None]) / n
    twr = np.repeat(np.cos(ang), ct, axis=0).astype(np.float32)
    twi = np.repeat(np.sin(ang), ct, axis=0).astype(np.float32)
    a = -2.0 * math.pi * np.outer(np.arange(FFT_CHUNK), np.arange(FFT_CHUNK)) / FFT_CHUNK
    wr, wi = np.cos(a), np.sin(a)
    wf = np.block([[wr, wi], [-wi, wr]]).astype(np.float32)
    wb = np.block([[wr, -wi], [wi, wr]]).astype(np.float32)
    return tw, twr, twi, _split_bf16(wf), _split_bf16(wb)


def _split_bf16(w):
    w = jnp.asarray(w, F32)
    hi = w.astype(BF16)
    return jnp.stack([hi, (w - hi.astype(F32)).astype(BF16)])


def _slab_fft(zr, zi, tw_ref, n1, ct, inverse, half_input=False, half_output=False):
    nst = int(math.log2(n1))
    order = range(nst - 1, -1, -1) if inverse else range(nst)
    for s in order:
        half = n1 >> (s + 1)
        lh = int(math.log2(half))
        first = s == 0

        def body(idx, carry, s=s, half=half, lh=lh, first=first):
            blk = idx >> lh
            j = idx & (half - 1)
            r0 = pl.multiple_of(((blk << (lh + 1)) + j) * ct, ct)
            r1 = pl.multiple_of(r0 + half * ct, ct)
            wr = tw_ref[0, s, j]
            wi = tw_ref[1, s, j]
            ar = zr[pl.ds(r0, ct), :]
            ai = zi[pl.ds(r0, ct), :]
            if not inverse:
                if first and half_input:
                    zr[pl.ds(r1, ct), :] = ar * wr - ai * wi
                    zi[pl.ds(r1, ct), :] = ar * wi + ai * wr
                else:
                    br = zr[pl.ds(r1, ct), :]
                    bi = zi[pl.ds(r1, ct), :]
                    dr = ar - br
                    di = ai - bi
                    zr[pl.ds(r0, ct), :] = ar + br
                    zi[pl.ds(r0, ct), :] = ai + bi
                    zr[pl.ds(r1, ct), :] = dr * wr - di * wi
                    zi[pl.ds(r1, ct), :] = dr * wi + di * wr
            else:
                br = zr[pl.ds(r1, ct), :]
                bi = zi[pl.ds(r1, ct), :]
                tr = br * wr + bi * wi
                ti = bi * wr - br * wi
                zr[pl.ds(r0, ct), :] = ar + tr
                zi[pl.ds(r0, ct), :] = ai + ti
                if not (first and half_output):
                    zr[pl.ds(r1, ct), :] = ar - tr
                    zi[pl.ds(r1, ct), :] = ai - ti
            return carry

        lax.fori_loop(0, n1 // 2, body, 0, unroll=2)


def _chunk_dft(xr, xi, w_ref):
    x = jnp.concatenate([xr, xi], axis=1)
    x_hi = x.astype(BF16)
    x_lo = (x - x_hi.astype(F32)).astype(BF16)
    y = _dot(x_hi, w_ref[0]) + (_dot(x_lo, w_ref[0]) + _dot(x_hi, w_ref[1]))
    return y[:, :FFT_CHUNK], y[:, FFT_CHUNK:]


def _fft_groups(n1, ct):
    g = min(8, n1)
    return n1 // g, g * ct


def _hy_spec_kernel(n1, ct, f_ref, tw_ref, twr_ref, twi_ref, wf_ref, fr_ref, fi_ref, zr, zi):
    for s in range(n1):
        zr[s * ct:(s + 1) * ct, :] = f_ref[:, s * FFT_CHUNK:(s + 1) * FFT_CHUNK]
    zi[...] = jnp.zeros(zi.shape, F32)
    _slab_fft(zr, zi, tw_ref, n1, ct, inverse=False)
    ngroups, gr = _fft_groups(n1, ct)

    def body(g, carry):
        r0 = pl.multiple_of(g * gr, gr)
        a = zr[pl.ds(r0, gr), :]
        b = zi[pl.ds(r0, gr), :]
        tr = twr_ref[pl.ds(r0, gr), :]
        ti = twi_ref[pl.ds(r0, gr), :]
        yr, yi = _chunk_dft(a * tr - b * ti, a * ti + b * tr, wf_ref)
        fr_ref[pl.ds(r0, gr), :] = yr
        fi_ref[pl.ds(r0, gr), :] = yi
        return carry

    lax.fori_loop(0, ngroups, body, 0, unroll=min(2, ngroups))


def _hy_spec_call(filt_t):
    c, n = filt_t.shape
    n1 = n // FFT_CHUNK
    ct = HY_CT
    tw, twr, twi, wf, _ = _fft_consts(n1, ct)
    const2 = lambda i: (0, 0)
    out = jax.ShapeDtypeStruct((c // ct, n1 * ct, FFT_CHUNK), F32)
    ospec = pl.BlockSpec((None, n1 * ct, FFT_CHUNK), lambda i: (i, 0, 0))
    return pl.pallas_call(
        functools.partial(_hy_spec_kernel, n1, ct),
        grid=(c // ct,),
        in_specs=[pl.BlockSpec((ct, n), lambda i: (i, 0)),
                  pl.BlockSpec(memory_space=pltpu.SMEM),
                  pl.BlockSpec(twr.shape, const2), pl.BlockSpec(twi.shape, const2),
                  pl.BlockSpec(wf.shape, lambda i: (0, 0, 0))],
        out_specs=[ospec, ospec],
        out_shape=[out, out],
        scratch_shapes=[pltpu.VMEM((n1 * ct, FFT_CHUNK), F32), pltpu.VMEM((n1 * ct, FFT_CHUNK), F32)],
        name="hyena_filter_spectrum",
        compiler_params=_cparams(("parallel",)),
    )(filt_t, jnp.asarray(tw), jnp.asarray(twr), jnp.asarray(twi), jnp.asarray(wf))


def _hy_conv_kernel(n1, ct, length, v_ref, x1_ref, x2_ref, cwv_ref, cw1_ref, cw2_ref, skip_ref,
                    fr_ref, fi_ref, tw_ref, twr_ref, twi_ref, wf_ref, wb_ref, o_ref,
                    zr, zi, u_scr, g_scr):
    lane = lax.broadcasted_iota(jnp.int32, (1, length), 1)

    def short_conv(x, cw_ref):
        xm = jnp.where(lane >= 1, pltpu.roll(x, 1, 1), 0.0)
        xp = jnp.where(lane <= length - 2, pltpu.roll(x, length - 1, 1), 0.0)
        return xm * cw_ref[:, 0:1] + x * cw_ref[:, 1:2] + xp * cw_ref[:, 2:3] + cw_ref[:, 3:4]

    for a in range(2):
        u_scr[a] = short_conv(v_ref[a], cwv_ref) * short_conv(x1_ref[a], cw1_ref)
        g_scr[a] = short_conv(x2_ref[a], cw2_ref)
    nz = length // FFT_CHUNK
    for s in range(nz):
        zr[s * ct:(s + 1) * ct, :] = u_scr[0, :, s * FFT_CHUNK:(s + 1) * FFT_CHUNK]
        zi[s * ct:(s + 1) * ct, :] = u_scr[1, :, s * FFT_CHUNK:(s + 1) * FFT_CHUNK]
    _slab_fft(zr, zi, tw_ref, n1, ct, inverse=False, half_input=True)
    ngroups, gr = _fft_groups(n1, ct)

    def body(g, carry):
        r0 = pl.multiple_of(g * gr, gr)
        a = zr[pl.ds(r0, gr), :]
        b = zi[pl.ds(r0, gr), :]
        tr = twr_ref[pl.ds(r0, gr), :]
        ti = twi_ref[pl.ds(r0, gr), :]
        yr, yi = _chunk_dft(a * tr - b * ti, a * ti + b * tr, wf_ref)
        fr = fr_ref[pl.ds(r0, gr), :]
        fi = fi_ref[pl.ds(r0, gr), :]
        qr, qi = _chunk_dft(yr * fr - yi * fi, yr * fi + yi * fr, wb_ref)
        zr[pl.ds(r0, gr), :] = qr * tr + qi * ti
        zi[pl.ds(r0, gr), :] = qi * tr - qr * ti
        return carry

    lax.fori_loop(0, ngroups, body, 0, unroll=min(2, ngroups))
    _slab_fft(zr, zi, tw_ref, n1, ct, inverse=True, half_output=True)
    inv_n = 1.0 / (n1 * FFT_CHUNK)
    skip = skip_ref[...]
    for s in range(nz):
        sl = slice(s * FFT_CHUNK, (s + 1) * FFT_CHUNK)
        for a, z in ((0, zr), (1, zi)):
            y = z[s * ct:(s + 1) * ct, :] * inv_n
            u = u_scr[a, :, sl]
            o_ref[a, :, sl] = (g_scr[a, :, sl] * (y + u * skip)).astype(o_ref.dtype)


def _hy_conv_call(hy_t, spec_r, spec_i, cw, skip, length, lane_block, prev=None):
    _, bh, c3, t_all = hy_t.shape
    c = c3 // 3
    ct = HY_CT
    nct = c // ct
    n1 = 2 * length // FFT_CHUNK
    tw, twr, twi, wf, wb = _fft_consts(n1, ct)
    const2 = lambda p, i: (0, 0)
    xin = lambda k: pl.BlockSpec((2, None, ct, length), lambda p, i, k=k: (0, p, k * nct + i, lane_block))
    cwin = lambda k: pl.BlockSpec((ct, 4), lambda p, i, k=k: (k * nct + i, 0))
    fin = pl.BlockSpec((None, n1 * ct, FFT_CHUNK), lambda p, i: (i, 0, 0))
    in_specs = [xin(0), xin(1), xin(2), cwin(0), cwin(1), cwin(2),
                pl.BlockSpec((ct, 1), lambda p, i: (i, 0)),
                fin, fin,
                pl.BlockSpec(memory_space=pltpu.SMEM),
                pl.BlockSpec(twr.shape, const2), pl.BlockSpec(twi.shape, const2),
                pl.BlockSpec(wf.shape, lambda p, i: (0, 0, 0)),
                pl.BlockSpec(wb.shape, lambda p, i: (0, 0, 0))]
    args = [hy_t, hy_t, hy_t, cw, cw, cw, skip, spec_r, spec_i, jnp.asarray(tw), jnp.asarray(twr),
            jnp.asarray(twi), jnp.asarray(wf), jnp.asarray(wb)]
    kern = functools.partial(_hy_conv_kernel, n1, ct, length)
    aliases = {}
    if prev is not None:
        in_specs.append(pl.BlockSpec(memory_space=pl.ANY))
        args.append(prev)
        aliases = {len(args) - 1: 0}
        kern = lambda *refs, kern=kern: kern(*refs[:14], *refs[15:])
    return pl.pallas_call(
        kern,
        grid=(bh, nct),
        in_specs=in_specs,
        out_specs=pl.BlockSpec((2, None, ct, length), lambda p, i: (0, p, i, lane_block)),
        out_shape=jax.ShapeDtypeStruct((2, bh, c, t_all), BF16),
        scratch_shapes=[pltpu.VMEM((n1 * ct, FFT_CHUNK), F32), pltpu.VMEM((n1 * ct, FFT_CHUNK), F32),
                        pltpu.VMEM((2, ct, length), F32), pltpu.VMEM((2, ct, length), F32)],
        input_output_aliases=aliases,
        name="hyena_conv_%d" % length,
        compiler_params=_cparams(("parallel", "parallel")),
    )(*args)


def _outproj_kernel(n_lat_tiles, ctx_row, x_ref, hy_ref, ga_ref, na_ref, w_ref, mod_ref, o_ref):
    j = pl.program_id(1)
    nbat, tm, d = x_ref.shape
    mix = []
    for s in range(nbat):
        cm = jnp.concatenate([hy_ref[s], ga_ref[s], na_ref[s]], axis=0)
        mix.append(cm.astype(F32).T.astype(BF16))
    y = _dot(jnp.concatenate(mix, axis=0), w_ref[...])
    for s in range(nbat):
        row = jnp.where(j < n_lat_tiles, pl.program_id(0) * nbat + s, ctx_row)
        o_ref[s] = x_ref[s] + _mod_row(mod_ref, row, 2, d) * y[s * tm:(s + 1) * tm]


def _outproj_call(xa, hyo_t, gqa, na, w_out, mod_l, s_len, n_tok):
    bsz, _, d = xa.shape
    tm = TOK_TILE
    nbat = _batch_group(bsz)
    const = lambda b, j: (0, 0)
    tok = lambda n: pl.BlockSpec((nbat, tm, n), lambda b, j: (b, j, 0))
    chan = lambda n: pl.BlockSpec((nbat, n, tm), lambda b, j: (b, 0, j))
    return pl.pallas_call(
        functools.partial(_outproj_kernel, s_len // tm, bsz),
        grid=(bsz // nbat, n_tok // tm),
        in_specs=[tok(d), chan(hyo_t.shape[1]), chan(gqa.shape[1]), chan(na.shape[1]),
                  pl.BlockSpec(w_out.shape, const), pl.BlockSpec(mod_l.shape, const)],
        out_specs=tok(d),
        out_shape=jax.ShapeDtypeStruct((bsz, n_tok, d), F32),
        name="out_proj",
        compiler_params=_cparams(("parallel", "parallel")),
    )(xa, hyo_t, gqa, na, w_out, mod_l)


def _ffn_kernel(tiles_per_batch, n_lat_tiles, ctx_row, th, x_ref, mod_ref, g_ref, wi_ref, wo_ref,
                gf_ref, o_ref, h_scr, act_scr):
    i = pl.program_id(0)
    d = x_ref.shape[-1]
    f = wo_ref.shape[0]
    groups = x_ref.shape[0] // TOK_TILE

    def mod_row(s):
        gi = i * groups + s
        jj = gi % tiles_per_batch
        return jnp.where(jj < n_lat_tiles, gi // tiles_per_batch, ctx_row)

    for s in range(groups):
        row = mod_row(s)
        sl = slice(s * TOK_TILE, (s + 1) * TOK_TILE)
        h = _norm_mod(x_ref[sl, :], g_ref[...], _mod_row(mod_ref, row, 3, d), _mod_row(mod_ref, row, 4, d))
        h_scr[sl, :] = h.astype(BF16)
    h = h_scr[...]
    for j in range(f // th):
        gate = _dot(h, wi_ref[:, j * th:(j + 1) * th])
        up = _dot(h, wi_ref[:, f + j * th:f + (j + 1) * th])
        act_scr[:, j * th:(j + 1) * th] = (gate * jax.nn.sigmoid(gate) * up).astype(BF16)
    y = _dot(act_scr[...], wo_ref[...])
    for s in range(groups):
        row = mod_row(s)
        sl = slice(s * TOK_TILE, (s + 1) * TOK_TILE)
        out = x_ref[sl, :] + _mod_row(mod_ref, row, 5, d) * y[sl, :]
        if gf_ref is not None:
            ms = jnp.mean(out * out, axis=-1, keepdims=True)
            out = out * lax.rsqrt(ms + NORM_EPS) * gf_ref[...]
        o_ref[sl, :] = out


def _ffn_call(xa, mod_l, g, w_in, w_out, s_len, g_final=None):
    bsz, t_all, d = xa.shape
    f = w_out.shape[0]
    r = bsz * t_all
    tm = 1024 if r % 1024 == 0 else TOK_TILE
    th = 256
    const = lambda i: (0, 0)
    resident = lambda a: pl.BlockSpec(a.shape, const, pipeline_mode=pl.Buffered(1))
    in_specs = [pl.BlockSpec((tm, d), lambda i: (i, 0)),
                pl.BlockSpec(mod_l.shape, const),
                pl.BlockSpec((1, d), const),
                resident(w_in), resident(w_out)]
    args = [xa.reshape(r, d), mod_l, g, w_in, w_out]
    kern = functools.partial(_ffn_kernel, t_all // TOK_TILE, s_len // TOK_TILE, bsz, th)
    if g_final is None:
        kern = functools.partial(_ffn_no_final, kern)
    else:
        in_specs.append(pl.BlockSpec((1, d), const))
        args.append(g_final)
    out = pl.pallas_call(
        kern,
        grid=(r // tm,),
        in_specs=in_specs,
        out_specs=pl.BlockSpec((tm, d), lambda i: (i, 0)),
        out_shape=jax.ShapeDtypeStruct((r, d), F32),
        scratch_shapes=[pltpu.VMEM((tm, d), BF16), pltpu.VMEM((tm, f), BF16)],
        name="swiglu_ffn",
        compiler_params=_cparams(("parallel",)),
    )(*args)
    return out.reshape(bsz, t_all, d)


def _ffn_no_final(kern, x_ref, mod_ref, g_ref, wi_ref, wo_ref, o_ref, h_scr, act_scr):
    kern(x_ref, mod_ref, g_ref, wi_ref, wo_ref, None, o_ref, h_scr, act_scr)


def _rope_tables(s_len, lc):
    pos = np.arange(s_len)
    row = (pos // GRID_W).astype(np.float64)
    col = (pos % GRID_W).astype(np.float64)
    n_f = HEAD_DIM // 4
    inv = (ROPE_THETA ** (-np.arange(n_f, dtype=np.float32) / n_f)).astype(np.float64)
    ang = np.concatenate([row[:, None] * inv, col[:, None] * inv], axis=-1).astype(np.float32).astype(np.float64)
    lane = np.arange(LANES)
    sign = np.where((lane % HEAD_DIM) < HEAD_DIM // 2, -1.0, 1.0)
    cos = np.cos(ang)[:, lane % (HEAD_DIM // 2)]
    sin = np.sin(ang)[:, lane % (HEAD_DIM // 2)] * sign[None, :]
    cos = np.concatenate([cos, np.ones((lc, LANES))], axis=0).astype(np.float32)
    sin = np.concatenate([sin, np.zeros((lc, LANES))], axis=0).astype(np.float32)
    cosc = np.concatenate([np.cos(ang).T, np.ones((HEAD_DIM // 2, lc))], axis=1).astype(np.float32)
    sinc = np.concatenate([np.sin(ang).T, np.zeros((HEAD_DIM // 2, lc))], axis=1).astype(np.float32)
    return jnp.asarray(cosc), jnp.asarray(sinc), jnp.asarray(cos), jnp.asarray(sin)


def kernel(x, c, ctx, c_ctx, w_mod, b_mod, g_mix, g_ffn, w_in, w_out, hy_conv_w, hy_conv_b,
           hy_f_w1, hy_f_b1, hy_f_w2, hy_f_b2, hy_f_w3, hy_f_b3, hy_f_wout, hy_f_freq, hy_skip,
           qk_g_q, qk_g_k, na_rpb, w_ffn_in, w_ffn_out, g_final):
    bsz, s_len, d = x.shape
    lc = ctx.shape[1]
    depth = w_mod.shape[0]
    t_all = s_len + lc
    hyw = hy_skip.shape[1]
    hy_cols = 3 * hyw
    assert bsz % 2 == 0 and s_len % TOK_TILE == 0 and lc == TOK_TILE and s_len % lc == 0

    nb = -(-(bsz + 1) // 8) * 8
    cs = jnp.zeros((nb, d), F32).at[:bsz].set(c).at[bsz].set(c_ctx)
    mod = _mod_call(cs, w_mod, b_mod)

    rope = _rope_tables(s_len, lc)
    tables = _na_bias_tables(na_rpb, s_len // GRID_W)
    q0 = hy_cols
    k0 = q0 + GQA_HEADS * HEAD_DIM
    v0 = k0 + GQA_KV_HEADS * HEAD_DIM
    nq0 = v0 + GQA_KV_HEADS * HEAD_DIM
    nk0 = nq0 + NA_HEADS * HEAD_DIM
    nv0 = nk0 + NA_HEADS * HEAD_DIM
    w_cm = jnp.concatenate([w_in[:, :, :k0], w_in[:, :, v0:nk0], w_in[:, :, nv0:]], axis=2)
    w_cm = jnp.swapaxes(w_cm, 1, 2).astype(BF16)
    w_tm = jnp.concatenate([w_in[:, :, k0:v0], w_in[:, :, nk0:nv0]], axis=2).astype(BF16)
    w_out_b = w_out.astype(BF16)
    w_fi = w_ffn_in.astype(BF16)
    w_fo = w_ffn_out.astype(BF16)
    cw = jnp.concatenate([jnp.swapaxes(hy_conv_w, 1, 2), hy_conv_b[:, :, None]], axis=2)
    per = LANES // HEAD_DIM

    xa = jnp.concatenate([x, ctx], axis=1)
    for l in range(depth):
        last = l == depth - 1
        gq_b = jnp.broadcast_to(qk_g_q[l][:, None], (HEAD_DIM, LANES))
        gk2 = jnp.tile(qk_g_k[l], per)[None]
        hy_t, q, k, v, nq, nk, nv = _inproj_call(xa, mod[l], g_mix[l][None], w_cm[l], w_tm[l],
                                                  gq_b, gk2, rope, hy_cols, s_len)
        n_tok = s_len if last else t_all
        gqa = _gqa_call(q, k, v, s_len, n_tok)
        na = _na_call(nq, nk, nv, tables[l], s_len, n_tok)

        fargs = (hy_f_w1[l], hy_f_b1[l], hy_f_w2[l], hy_f_b2[l], hy_f_w3[l], hy_f_b3[l],
                 hy_f_wout[l], hy_f_freq[l])
        hy_p = hy_t.reshape(2, bsz // 2, hy_cols, t_all)
        skip = hy_skip[l][:, None]
        fr, fi = _hy_spec_call(_hy_filter_call(s_len, *fargs))
        hyo = _hy_conv_call(hy_p, fr, fi, cw[l], skip, s_len, 0)
        if not last:
            frc, fic = _hy_spec_call(_hy_filter_call(lc, *fargs))
            hyo = _hy_conv_call(hy_p, frc, fic, cw[l], skip, lc, s_len // lc, prev=hyo)
        hyo = hyo.reshape(bsz, hyw, t_all)

        xa = _outproj_call(xa, hyo, gqa, na, w_out_b[l], mod[l], s_len, n_tok)
        xa = _ffn_call(xa, mod[l], g_ffn[l][None], w_fi[l], w_fo[l], s_len,
                       g_final[None] if last else None)
    return xa
```

```python
import functools
import math

import numpy as np
import jax
import jax.numpy as jnp
from jax import lax
from jax.experimental import pallas as pl
from jax.experimental.pallas import tpu as pltpu

F32 = jnp.float32
BF16 = jnp.bfloat16
HIGHEST = lax.Precision.HIGHEST

GRID_W = 64
HEAD_DIM = 64
GQA_HEADS = 8
GQA_KV_HEADS = 2
GQA_GROUP = GQA_HEADS // GQA_KV_HEADS
NA_HEADS = 4
NA_KH = 8
NA_KW = 16
ROPE_THETA = 10000.0
HY_EMB_DIM = 33
HY_DECAY_TARGET = 1e-2
HY_FAST_DECAY = 0.3
HY_SLOW_DECAY = 1.5
NORM_EPS = 1e-6
ATTN_SCALE = HEAD_DIM ** -0.5
NEG = -1e30
LOG2E = math.log2(math.e)

LANES = 128
TOK_TILE = 256
FFT_CHUNK = LANES
HY_CT = 64
NA_ROWS = TOK_TILE // GRID_W
NA_WIN = NA_ROWS + NA_KH
VMEM_LIMIT = 56 * 1024 * 1024


def _cparams(sem):
    return pltpu.CompilerParams(dimension_semantics=sem, vmem_limit_bytes=VMEM_LIMIT)


def _dot(a, b, **kw):
    return jnp.dot(a, b, preferred_element_type=F32, **kw)


def _dot_nt(a, b, **kw):
    return lax.dot_general(a, b, (((1,), (1,)), ((), ())), preferred_element_type=F32, **kw)


def _mod_kernel(cs_ref, w_ref, b_ref, o_ref):
    cs = cs_ref[...]
    s = cs * jax.nn.sigmoid(cs)
    o_ref[...] = _dot(s, w_ref[...], precision=HIGHEST) + b_ref[...]


def _mod_call(cs, w_mod, b_mod):
    depth, d, n = w_mod.shape
    nb = cs.shape[0]
    tn = 1024
    return pl.pallas_call(
        _mod_kernel,
        grid=(depth, n // tn),
        in_specs=[pl.BlockSpec((nb, d), lambda l, j: (0, 0)),
                  pl.BlockSpec((None, d, tn), lambda l, j: (l, 0, j)),
                  pl.BlockSpec((None, 1, tn), lambda l, j: (l, 0, j))],
        out_specs=pl.BlockSpec((None, nb, tn), lambda l, j: (l, 0, j)),
        out_shape=jax.ShapeDtypeStruct((depth, nb, n), F32),
        name="adaln_mod",
        compiler_params=_cparams(("parallel", "parallel")),
    )(cs, w_mod, b_mod.reshape(depth, 1, n))


def _mod_row(mod_ref, row, k, d):
    return mod_ref[pl.ds(row, 1), k * d:(k + 1) * d]


def _norm_mod(x, g, shift, scale):
    ms = jnp.mean(x * x, axis=-1, keepdims=True)
    y = x * lax.rsqrt(ms + NORM_EPS) * g
    return y * (1.0 + scale) + shift


def _batch_group(bsz):
    return 4 if bsz % 4 == 0 else 2


def _inproj_kernel(n_lat_tiles, ctx_row, hyc, x_ref, mod_ref, g_ref, wcm_ref, wtm_ref, gq_ref, gk_ref,
                   cosc_ref, sinc_ref, cos_ref, sin_ref,
                   hy_ref, q_ref, k_ref, v_ref, nq_ref, nk_ref, nv_ref, h_scr):
    j = pl.program_id(1)
    nbat, tm, d = x_ref.shape
    for s in range(nbat):
        row = jnp.where(j < n_lat_tiles, pl.program_id(0) * nbat + s, ctx_row)
        h = _norm_mod(x_ref[s], g_ref[...], _mod_row(mod_ref, row, 0, d), _mod_row(mod_ref, row, 1, d))
        h_scr[s * tm:(s + 1) * tm, :] = h.astype(BF16)
    h = h_scr[...]
    cm = _dot_nt(wcm_ref[...], h)
    tmj = _dot(h, wtm_ref[...])

    half = HEAD_DIM // 2
    reps = cm.shape[1] // LANES
    gq = jnp.concatenate([gq_ref[...]] * reps, axis=1)
    cosc = jnp.concatenate([cosc_ref[...]] * nbat, axis=1)
    sinc = jnp.concatenate([sinc_ref[...]] * nbat, axis=1)
    for hh in range(GQA_HEADS):
        x = cm[hyc + hh * HEAD_DIM:hyc + (hh + 1) * HEAD_DIM, :]
        ms = jnp.mean(x * x, axis=0, keepdims=True)
        xn = x * lax.rsqrt(ms + NORM_EPS) * gq
        x1, x2 = xn[:half], xn[half:]
        y = jnp.concatenate([x1 * cosc - x2 * sinc, x1 * sinc + x2 * cosc], axis=0)
        y = (y * (ATTN_SCALE * LOG2E)).astype(BF16)
        for s in range(nbat):
            q_ref[s, hh * HEAD_DIM:(hh + 1) * HEAD_DIM, :] = y[:, s * tm:(s + 1) * tm]
    voff = hyc + GQA_HEADS * HEAD_DIM
    nqoff = voff + GQA_KV_HEADS * HEAD_DIM
    nvoff = nqoff + NA_HEADS * HEAD_DIM
    for s in range(nbat):
        cols = slice(s * tm, (s + 1) * tm)
        hy_ref[s] = cm[:hyc, cols]
        v_ref[s] = cm[voff:nqoff, cols].astype(BF16)
        nq_ref[s] = (cm[nqoff:nvoff, cols] * (ATTN_SCALE * LOG2E)).astype(BF16)
        for e in range(NA_HEADS):
            nv_ref[s, e] = cm[nvoff + e * HEAD_DIM:nvoff + (e + 1) * HEAD_DIM, cols].astype(BF16)
        _inproj_token_major(tmj[s * tm:(s + 1) * tm], gk_ref, cos_ref, sin_ref, k_ref.at[s], nk_ref.at[s])


def _inproj_token_major(p, gk_ref, cos_ref, sin_ref, k_ref, nk_ref):
    ri = lax.broadcasted_iota(jnp.int32, (LANES, LANES), 0) // HEAD_DIM
    ci = lax.broadcasted_iota(jnp.int32, (LANES, LANES), 1) // HEAD_DIM
    avg = jnp.where(ri == ci, 1.0 / HEAD_DIM, 0.0).astype(BF16)
    lane = lax.broadcasted_iota(jnp.int32, (1, LANES), 1)
    first_half = (lane % HEAD_DIM) < (HEAD_DIM // 2)
    xc = p[:, :LANES]
    ms = _dot((xc * xc).astype(BF16), avg)
    xn = xc * lax.rsqrt(ms + NORM_EPS) * gk_ref[...]
    partner = jnp.where(first_half, pltpu.roll(xn, LANES - HEAD_DIM // 2, 1),
                        pltpu.roll(xn, HEAD_DIM // 2, 1))
    y = xn * cos_ref[...] + partner * sin_ref[...]
    assert GQA_KV_HEADS * HEAD_DIM == LANES
    tail = jnp.where(lane == HEAD_DIM, 1.0, 0.0)
    for e in range(GQA_KV_HEADS):
        ye = y if e == 0 else pltpu.roll(y, LANES - e * HEAD_DIM, 1)
        k_ref[e] = jnp.where(lane < HEAD_DIM, ye, tail).astype(BF16)
    for e in range(NA_HEADS):
        nk_ref[e] = p[:, LANES + e * HEAD_DIM:LANES + (e + 1) * HEAD_DIM].astype(BF16)


def _inproj_call(xa, mod_l, g, w_cm, w_tm, gq_b, gk2, rope, hyc, s_len):
    bsz, t_all, d = xa.shape
    tm = TOK_TILE
    nbat = _batch_group(bsz)
    cosc, sinc, cos_t, sin_t = rope
    grid = (bsz // nbat, t_all // tm)
    const = lambda b, j: (0, 0)
    resident = lambda a: pl.BlockSpec(a.shape, const, pipeline_mode=pl.Buffered(1))
    heads = lambda n, w=HEAD_DIM: pl.BlockSpec((nbat, n, tm, w), lambda b, j: (b, 0, j, 0))
    hshape = lambda n, w=HEAD_DIM: jax.ShapeDtypeStruct((bsz, n, t_all, w), BF16)
    chan = lambda n: pl.BlockSpec((nbat, n, tm), lambda b, j: (b, 0, j))
    qc = GQA_HEADS * HEAD_DIM
    vc = GQA_KV_HEADS * HEAD_DIM
    nc = NA_HEADS * HEAD_DIM
    return pl.pallas_call(
        functools.partial(_inproj_kernel, s_len // tm, bsz, hyc),
        grid=grid,
        in_specs=[pl.BlockSpec((nbat, tm, d), lambda b, j: (b, j, 0)),
                  pl.BlockSpec(mod_l.shape, const),
                  pl.BlockSpec((1, d), const),
                  resident(w_cm), resident(w_tm),
                  pl.BlockSpec(gq_b.shape, const),
                  pl.BlockSpec((1, LANES), const),
                  pl.BlockSpec((HEAD_DIM // 2, tm), lambda b, j: (0, j)),
                  pl.BlockSpec((HEAD_DIM // 2, tm), lambda b, j: (0, j)),
                  pl.BlockSpec((tm, LANES), lambda b, j: (j, 0)),
                  pl.BlockSpec((tm, LANES), lambda b, j: (j, 0))],
        out_specs=[chan(hyc), chan(qc), heads(GQA_KV_HEADS, LANES), chan(vc),
                   chan(nc), heads(NA_HEADS),
                   pl.BlockSpec((nbat, NA_HEADS, None, HEAD_DIM, tm), lambda b, j: (b, 0, j, 0, 0))],
        out_shape=[jax.ShapeDtypeStruct((bsz, hyc, t_all), F32),
                   jax.ShapeDtypeStruct((bsz, qc, t_all), BF16),
                   hshape(GQA_KV_HEADS, LANES),
                   jax.ShapeDtypeStruct((bsz, vc, t_all), BF16),
                   jax.ShapeDtypeStruct((bsz, nc, t_all), BF16),
                   hshape(NA_HEADS),
                   jax.ShapeDtypeStruct((bsz, NA_HEADS, t_all // tm, HEAD_DIM, tm), BF16)],
        scratch_shapes=[pltpu.VMEM((nbat * tm, d), BF16)],
        name="in_proj",
        compiler_params=_cparams(("parallel", "parallel")),
    )(xa, mod_l, g, w_cm, w_tm, gq_b, gk2, cosc, sinc, cos_t, sin_t)


def _gqa_kernel(s_len, lc, tk, kb_ref, q_ref, k_ref, vt_ref, o_ref, s_scr):
    i = pl.program_id(2)
    tm = q_ref.shape[1]
    sub = 8
    kdim = k_ref.shape[1]
    first_row = lax.broadcasted_iota(jnp.int32, (kdim - HEAD_DIM, tm), 0) == 0

    def head_q(h, shift):
        q = q_ref[h * HEAD_DIM:(h + 1) * HEAD_DIM, :].astype(F32)
        if shift:
            bound = jnp.sqrt(jnp.sum(q * q, axis=0, keepdims=True)) * kb_ref[0]
            tail = jnp.where(first_row, -bound, 0.0)
        else:
            tail = jnp.zeros(first_row.shape, F32)
        return jnp.concatenate([q, tail], axis=0).astype(BF16)

    def finish(h, l, acc):
        o_ref[h * HEAD_DIM:(h + 1) * HEAD_DIM, :] = (acc / jnp.sum(l, axis=0, keepdims=True)).astype(o_ref.dtype)

    def single_pass(chunks):
        for h in range(GQA_GROUP):
            q = head_q(h, True)
            l = jnp.zeros((sub, tm), F32)
            acc = jnp.zeros((HEAD_DIM, tm), F32)
            for r0, nrow in chunks:
                p = jnp.exp2(_dot(k_ref[r0:r0 + nrow, :], q))
                l = l + jnp.sum(p.reshape(nrow // sub, sub, tm), axis=0)
                acc = acc + _dot(vt_ref[:, r0:r0 + nrow], p.astype(BF16))
            finish(h, l, acc)

    def scores(h, q, r0, nrow, m):
        s = _dot(k_ref[r0:r0 + nrow, :], q)
        s_scr[h, r0:r0 + nrow, :] = s
        return jnp.maximum(m, jnp.max(s.reshape(nrow // sub, sub, tm), axis=0))

    def apply(h, r0, nrow, m, l, acc):
        p = jnp.exp2(s_scr[h, r0:r0 + nrow, :].reshape(nrow // sub, sub, tm) - m[None])
        l = l + jnp.sum(p, axis=0)
        acc = acc + _dot(vt_ref[:, r0:r0 + nrow], p.reshape(nrow, tm).astype(BF16))
        return l, acc

    def two_pass(chunks):
        maxes = [None] * GQA_GROUP
        for stage in range(GQA_GROUP + 1):
            h1, h2 = stage, stage - 1
            if h1 < GQA_GROUP:
                q1 = head_q(h1, False)
                m1 = jnp.full((sub, tm), NEG, F32)
            if h2 >= 0:
                m2 = jnp.broadcast_to(jnp.max(maxes[h2], axis=0, keepdims=True), (sub, tm))
                l = jnp.zeros((sub, tm), F32)
                acc = jnp.zeros((HEAD_DIM, tm), F32)
            for r0, nrow in chunks:
                if h1 < GQA_GROUP:
                    m1 = scores(h1, q1, r0, nrow, m1)
                if h2 >= 0:
                    l, acc = apply(h2, r0, nrow, m2, l, acc)
            if h1 < GQA_GROUP:
                maxes[h1] = m1
            if h2 >= 0:
                finish(h2, l, acc)

    ctx_chunk = (s_len, lc)
    all_chunks = [(c * tk, tk) for c in range(s_len // tk)] + [ctx_chunk]
    latent = i * tm < s_len
    bounded = kb_ref[1] > 0.5

    @pl.when(latent & bounded)
    def _():
        single_pass(all_chunks)

    @pl.when(latent & jnp.logical_not(bounded))
    def _():
        two_pass(all_chunks)

    @pl.when(jnp.logical_not(latent))
    def _():
        two_pass([ctx_chunk])


GQA_SHIFT_LIMIT = 60.0


def _gqa_bounds(gq, gk):
    key_norm = math.sqrt(HEAD_DIM) * jnp.max(jnp.abs(gk)) * 1.02
    query_norm = math.sqrt(HEAD_DIM) * jnp.max(jnp.abs(gq)) * 1.02 * ATTN_SCALE * LOG2E
    safe = (key_norm * query_norm <= GQA_SHIFT_LIMIT).astype(F32)
    return jnp.stack([key_norm.astype(F32), safe])


def _gqa_call(qt, k, vt, kb, s_len, n_query):
    bsz, qc, t_all = qt.shape
    hd = HEAD_DIM
    tm = TOK_TILE
    tk = 2 * TOK_TILE
    lc = t_all - s_len
    return pl.pallas_call(
        functools.partial(_gqa_kernel, s_len, lc, tk),
        grid=(bsz, GQA_KV_HEADS, n_query // tm),
        in_specs=[pl.BlockSpec(memory_space=pltpu.SMEM),
                  pl.BlockSpec((None, GQA_GROUP * hd, tm), lambda b, g, i: (b, g, i)),
                  pl.BlockSpec((None, None, t_all, k.shape[-1]), lambda b, g, i: (b, g, 0, 0)),
                  pl.BlockSpec((None, hd, t_all), lambda b, g, i: (b, g, 0))],
        out_specs=pl.BlockSpec((None, GQA_GROUP * hd, tm), lambda b, g, i: (b, g, i)),
        out_shape=jax.ShapeDtypeStruct((bsz, qc, t_all), BF16),
        scratch_shapes=[pltpu.VMEM((GQA_GROUP, t_all, tm), F32)],
        name="gqa_attention",
        compiler_params=_cparams(("parallel", "parallel", "parallel")),
    )(kb, qt, k, vt)


def _na_bias_consts(rows):
    assert rows >= NA_WIN + NA_ROWS
    nrow, ncol = 2 * NA_KH - 1, 2 * NA_KW - 1
    rsel = np.zeros((3, NA_ROWS, NA_WIN, nrow), np.float32)
    mrow = np.zeros((3, NA_ROWS, NA_WIN), bool)
    for kind, r0 in enumerate((0, NA_ROWS, rows - NA_ROWS)):
        kr0 = int(np.clip(r0 - NA_KH // 2, 0, rows - NA_WIN))
        for qr in range(NA_ROWS):
            r = r0 + qr
            rs = int(np.clip(r - NA_KH // 2, 0, rows - NA_KH))
            for kr in range(NA_WIN):
                ka = kr0 + kr
                if rs <= ka < rs + NA_KH:
                    mrow[kind, qr, kr] = True
                    rsel[kind, qr, kr, ka - r + NA_KH - 1] = 1.0
    csel = np.zeros((GRID_W, GRID_W, ncol), np.float32)
    mcol = np.zeros((GRID_W, GRID_W), bool)
    for w in range(GRID_W):
        cs = int(np.clip(w - NA_KW // 2, 0, GRID_W - NA_KW))
        for kc in range(cs, cs + NA_KW):
            mcol[w, kc] = True
            csel[w, kc, kc - w + NA_KW - 1] = 1.0
    mask = mrow[:, :, None, :, None] & mcol[None, None, :, None, :]
    return rsel, csel, mask.reshape(3, TOK_TILE, NA_WIN * GRID_W)


def _na_bias_tables(na_rpb, rows):
    rsel, csel, mask = _na_bias_consts(rows)
    depth, heads = na_rpb.shape[:2]
    a = jnp.einsum("kqra,lhab->lkhqrb", jnp.asarray(rsel), na_rpb.astype(F32) * LOG2E, precision=HIGHEST)
    t = jnp.einsum("lkhqrb,wcb->lkhrcqw", a, jnp.asarray(csel), precision=HIGHEST)
    t = t.reshape(depth, 3, heads, NA_WIN * GRID_W, TOK_TILE)
    mask_t = np.swapaxes(mask, 1, 2)
    return jnp.where(jnp.asarray(mask_t)[None, :, None], t, NEG)


def _na_kernel(s_len, lc, rows, q_ref, k_ref, vt_ref, tab_ref, o_ref, s_scr):
    i = pl.program_id(1)
    tm = q_ref.shape[1]
    nblk = s_len // tm
    sub = 8

    def scores(h, part, off, m):
        r0, _, nrow, bias = part
        s = _dot(k_ref[h, pl.ds(r0, nrow), :], q_ref[h * HEAD_DIM:(h + 1) * HEAD_DIM, :])
        if bias is not None:
            s = s + bias[h]
        s_scr[h, off:off + nrow, :] = s
        return jnp.maximum(m, jnp.max(s.reshape(nrow // sub, sub, tm), axis=0))

    def apply(h, part, off, m, l, acc):
        _, c0, nrow, _ = part
        p = jnp.exp2(s_scr[h, off:off + nrow, :].reshape(nrow // sub, sub, tm) - m[None])
        l = l + jnp.sum(p, axis=0)
        vt = jnp.concatenate([vt_ref[h, c0 + c] for c in range(nrow // tm)], axis=1)
        return l, acc + _dot(vt, p.reshape(nrow, tm).astype(BF16))

    def run(parts):
        offs = [sum(p[2] for p in parts[:n]) for n in range(len(parts))]
        maxes = [None] * NA_HEADS
        for stage in range(NA_HEADS + 1):
            h1, h2 = stage, stage - 1
            if h1 < NA_HEADS:
                m1 = jnp.full((sub, tm), NEG, F32)
            if h2 >= 0:
                m2 = jnp.broadcast_to(jnp.max(maxes[h2], axis=0, keepdims=True), (sub, tm))
                l = jnp.zeros((sub, tm), F32)
                acc = jnp.zeros((HEAD_DIM, tm), F32)
            for part, off in zip(parts, offs):
                if h1 < NA_HEADS:
                    m1 = scores(h1, part, off, m1)
                if h2 >= 0:
                    l, acc = apply(h2, part, off, m2, l, acc)
            if h1 < NA_HEADS:
                maxes[h1] = m1
            if h2 >= 0:
                o = acc / jnp.sum(l, axis=0, keepdims=True)
                o_ref[h2 * HEAD_DIM:(h2 + 1) * HEAD_DIM, :] = o.astype(o_ref.dtype)

    ctx_part = (s_len, s_len // tm, lc, None)

    @pl.when(i < nblk)
    def _():
        kind = jnp.where(i == 0, 0, jnp.where(i == nblk - 1, 2, 1))
        c0 = jnp.clip(i - (NA_KH // 2) // NA_ROWS, 0, (rows - NA_WIN) // NA_ROWS)
        ks = pl.multiple_of(c0 * tm, tm)
        run([(ks, c0, NA_WIN * GRID_W, tab_ref.at[kind]), ctx_part])

    @pl.when(i >= nblk)
    def _():
        run([ctx_part])


def _na_call(nqt, nk, nvt, table, s_len, n_query):
    bsz, heads, t_all, hd = nk.shape
    tm = TOK_TILE
    rows = s_len // GRID_W
    assert NA_ROWS * GRID_W == tm and (NA_KH // 2) % NA_ROWS == 0 and (rows - NA_WIN) % NA_ROWS == 0
    return pl.pallas_call(
        functools.partial(_na_kernel, s_len, t_all - s_len, rows),
        grid=(bsz, n_query // tm),
        in_specs=[pl.BlockSpec((None, heads * hd, tm), lambda b, i: (b, 0, i)),
                  pl.BlockSpec((None, heads, t_all, hd), lambda b, i: (b, 0, 0, 0)),
                  pl.BlockSpec((None,) + nvt.shape[1:], lambda b, i: (b, 0, 0, 0, 0)),
                  pl.BlockSpec(table.shape, lambda b, i: (0, 0, 0, 0), pipeline_mode=pl.Buffered(1))],
        out_specs=pl.BlockSpec((None, heads * hd, tm), lambda b, i: (b, 0, i)),
        out_shape=jax.ShapeDtypeStruct((bsz, heads * hd, t_all), BF16),
        scratch_shapes=[pltpu.VMEM((heads, NA_WIN * GRID_W + t_all - s_len, tm), F32)],
        name="neighbourhood_attention",
        compiler_params=_cparams(("parallel", "parallel")),
    )(nqt, nk, nvt, table)


def _hy_filter_kernel(z_ref, t_ref, side_ref, w1_ref, b1_ref, w2_ref, b2_ref, w3_ref, b3_ref,
                      wo_ref, fr_ref, dl_ref, o_ref):
    om = fr_ref[...]
    h = jnp.sin(om * (_dot(z_ref[...], w1_ref[...], precision=HIGHEST) + b1_ref[...]))
    h = jnp.sin(om * (_dot(h, w2_ref[...], precision=HIGHEST) + b2_ref[...]))
    h = jnp.sin(om * (_dot(h, w3_ref[...], precision=HIGHEST) + b3_ref[...]))
    ht = _dot_nt(wo_ref[...], h, precision=HIGHEST)
    c = o_ref.shape[0]
    decay = jnp.exp(-t_ref[...] * jnp.abs(dl_ref[...]))
    side = side_ref[...]
    o_ref[...] = decay * (jnp.where(side > 0.5, ht[:c], 0.0) + jnp.where(side < -0.5, ht[c:], 0.0))


def _hy_filter_consts(length):
    n = np.arange(2 * length)
    pos = np.where(n < length, n, 2 * length - n)
    pos = np.where(n == length, 0, pos)
    side = np.where(n < length, 1.0, -1.0)
    side = np.where(n == length, 0.0, side)
    t = np.linspace(0.0, 1.0, length)[pos]
    bands = (HY_EMB_DIM - 1) // 2
    w = 2.0 * math.pi * pos / length
    fr = np.linspace(1e-4, bands - 1, bands)
    z = np.concatenate([t[:, None], np.cos(fr[None, :] * w[:, None]), -np.sin(fr[None, :] * w[:, None])], axis=-1)
    zp = np.zeros((2 * length, 64), np.float32)
    zp[:, :HY_EMB_DIM] = z
    return zp, t.astype(np.float32)[None, :], side.astype(np.float32)[None, :]


def _hy_filter_call(length, w1, b1, w2, b2, w3, b3, wout, freq):
    f = w2.shape[0]
    c = wout.shape[1] // 2
    zp, t, side = _hy_filter_consts(length)
    w1p = jnp.zeros((64, f), F32).at[:HY_EMB_DIM].set(w1)
    deltas = np.linspace(math.log(HY_DECAY_TARGET) / HY_SLOW_DECAY,
                         math.log(HY_DECAY_TARGET) / HY_FAST_DECAY, c).astype(np.float32)[:, None]
    args = (jnp.asarray(zp), jnp.asarray(t), jnp.asarray(side), w1p, b1[None], w2, b2[None], w3, b3[None],
            wout.T, freq[None], jnp.asarray(deltas))
    return pl.pallas_call(
        _hy_filter_kernel,
        in_specs=[pl.BlockSpec(a.shape, lambda: (0,) * a.ndim) for a in args],
        out_specs=pl.BlockSpec((c, 2 * length), lambda: (0, 0)),
        out_shape=jax.ShapeDtypeStruct((c, 2 * length), F32),
        grid=(),
        name="hyena_filter",
        compiler_params=pltpu.CompilerParams(vmem_limit_bytes=VMEM_LIMIT),
    )(*args)


def _bitrev(n):
    bits = int(math.log2(n))
    return np.array([int(format(i, "0%db" % bits)[::-1], 2) if bits else 0 for i in range(n)])


def _fft_consts(n1, ct):
    n = n1 * FFT_CHUNK
    stages = max(int(math.log2(n1)), 1)
    tw = np.zeros((2, stages, max(n1 // 2, 1)), np.float32)
    for s in range(int(math.log2(n1))):
        half = n1 >> (s + 1)
        ang = -2.0 * math.pi * np.arange(half) / (2 * half)
        tw[0, s, :half] = np.cos(ang)
        tw[1, s, :half] = np.sin(ang)
    k1 = _bitrev(n1)
    ang = -2.0 * math.pi * (np.arange(FFT_CHUNK)[None, :] * k1[:, None]) / n
    twr = np.repeat(np.cos(ang), ct, axis=0).astype(np.float32)
    twi = np.repeat(np.sin(ang), ct, axis=0).astype(np.float32)
    a = -2.0 * math.pi * np.outer(np.arange(FFT_CHUNK), np.arange(FFT_CHUNK)) / FFT_CHUNK
    wr, wi = np.cos(a), np.sin(a)
    wf = np.block([[wr, wi], [-wi, wr]]).astype(np.float32)
    wb = np.block([[wr, -wi], [wi, wr]]).astype(np.float32)
    return tw, twr, twi, _split_bf16(wf), _split_bf16(wb)


def _split_bf16(w):
    w = jnp.asarray(w, F32)
    hi = w.astype(BF16)
    return jnp.stack([hi, (w - hi.astype(F32)).astype(BF16)])


def _slab_fft(zr, zi, tw_ref, n1, ct, inverse, half_input=False, half_output=False):
    nst = int(math.log2(n1))
    order = range(nst - 1, -1, -1) if inverse else range(nst)
    for s in order:
        half = n1 >> (s + 1)
        lh = int(math.log2(half))
        first = s == 0

        def body(idx, carry, s=s, half=half, lh=lh, first=first):
            blk = idx >> lh
            j = idx & (half - 1)
            r0 = pl.multiple_of(((blk << (lh + 1)) + j) * ct, ct)
            r1 = pl.multiple_of(r0 + half * ct, ct)
            wr = tw_ref[0, s, j]
            wi = tw_ref[1, s, j]
            ar = zr[pl.ds(r0, ct), :]
            ai = zi[pl.ds(r0, ct), :]
            if not inverse:
                if first and half_input:
                    zr[pl.ds(r1, ct), :] = ar * wr - ai * wi
                    zi[pl.ds(r1, ct), :] = ar * wi + ai * wr
                else:
                    br = zr[pl.ds(r1, ct), :]
                    bi = zi[pl.ds(r1, ct), :]
                    dr = ar - br
                    di = ai - bi
                    zr[pl.ds(r0, ct), :] = ar + br
                    zi[pl.ds(r0, ct), :] = ai + bi
                    zr[pl.ds(r1, ct), :] = dr * wr - di * wi
                    zi[pl.ds(r1, ct), :] = dr * wi + di * wr
            else:
                br = zr[pl.ds(r1, ct), :]
                bi = zi[pl.ds(r1, ct), :]
                tr = br * wr + bi * wi
                ti = bi * wr - br * wi
                zr[pl.ds(r0, ct), :] = ar + tr
                zi[pl.ds(r0, ct), :] = ai + ti
                if not (first and half_output):
                    zr[pl.ds(r1, ct), :] = ar - tr
                    zi[pl.ds(r1, ct), :] = ai - ti
            return carry

        lax.fori_loop(0, n1 // 2, body, 0, unroll=2)


def _chunk_dft(xr, xi, w_ref):
    x = jnp.concatenate([xr, xi], axis=1)
    x_hi = x.astype(BF16)
    x_lo = (x - x_hi.astype(F32)).astype(BF16)
    y = _dot(x_hi, w_ref[0]) + (_dot(x_lo, w_ref[0]) + _dot(x_hi, w_ref[1]))
    return y[:, :FFT_CHUNK], y[:, FFT_CHUNK:]


def _fft_groups(n1, ct):
    g = min(8, n1)
    return n1 // g, g * ct


def _hy_spec_kernel(n1, ct, f_ref, tw_ref, twr_ref, twi_ref, wf_ref, fr_ref, fi_ref, zr, zi):
    for s in range(n1):
        zr[s * ct:(s + 1) * ct, :] = f_ref[:, s * FFT_CHUNK:(s + 1) * FFT_CHUNK]
    zi[...] = jnp.zeros(zi.shape, F32)
    _slab_fft(zr, zi, tw_ref, n1, ct, inverse=False)
    ngroups, gr = _fft_groups(n1, ct)

    def body(g, carry):
        r0 = pl.multiple_of(g * gr, gr)
        a = zr[pl.ds(r0, gr), :]
        b = zi[pl.ds(r0, gr), :]
        tr = twr_ref[pl.ds(r0, gr), :]
        ti = twi_ref[pl.ds(r0, gr), :]
        yr, yi = _chunk_dft(a * tr - b * ti, a * ti + b * tr, wf_ref)
        fr_ref[pl.ds(r0, gr), :] = yr
        fi_ref[pl.ds(r0, gr), :] = yi
        return carry

    lax.fori_loop(0, ngroups, body, 0, unroll=min(2, ngroups))


def _hy_spec_call(filt_t):
    c, n = filt_t.shape
    n1 = n // FFT_CHUNK
    ct = HY_CT
    tw, twr, twi, wf, _ = _fft_consts(n1, ct)
    const2 = lambda i: (0, 0)
    out = jax.ShapeDtypeStruct((c // ct, n1 * ct, FFT_CHUNK), F32)
    ospec = pl.BlockSpec((None, n1 * ct, FFT_CHUNK), lambda i: (i, 0, 0))
    return pl.pallas_call(
        functools.partial(_hy_spec_kernel, n1, ct),
        grid=(c // ct,),
        in_specs=[pl.BlockSpec((ct, n), lambda i: (i, 0)),
                  pl.BlockSpec(memory_space=pltpu.SMEM),
                  pl.BlockSpec(twr.shape, const2), pl.BlockSpec(twi.shape, const2),
                  pl.BlockSpec(wf.shape, lambda i: (0, 0, 0))],
        out_specs=[ospec, ospec],
        out_shape=[out, out],
        scratch_shapes=[pltpu.VMEM((n1 * ct, FFT_CHUNK), F32), pltpu.VMEM((n1 * ct, FFT_CHUNK), F32)],
        name="hyena_filter_spectrum",
        compiler_params=_cparams(("parallel",)),
    )(filt_t, jnp.asarray(tw), jnp.asarray(twr), jnp.asarray(twi), jnp.asarray(wf))


def _hy_conv_kernel(n1, ct, length, v_ref, x1_ref, x2_ref, cwv_ref, cw1_ref, cw2_ref, skip_ref,
                    fr_ref, fi_ref, tw_ref, twr_ref, twi_ref, wf_ref, wb_ref, o_ref,
                    zr, zi, u_scr, g_scr):
    lane = lax.broadcasted_iota(jnp.int32, (1, length), 1)

    def short_conv(x, cw_ref):
        xm = jnp.where(lane >= 1, pltpu.roll(x, 1, 1), 0.0)
        xp = jnp.where(lane <= length - 2, pltpu.roll(x, length - 1, 1), 0.0)
        return xm * cw_ref[:, 0:1] + x * cw_ref[:, 1:2] + xp * cw_ref[:, 2:3] + cw_ref[:, 3:4]

    for a in range(2):
        u_scr[a] = short_conv(v_ref[a], cwv_ref) * short_conv(x1_ref[a], cw1_ref)
        g_scr[a] = short_conv(x2_ref[a], cw2_ref)
    nz = length // FFT_CHUNK
    for s in range(nz):
        zr[s * ct:(s + 1) * ct, :] = u_scr[0, :, s * FFT_CHUNK:(s + 1) * FFT_CHUNK]
        zi[s * ct:(s + 1) * ct, :] = u_scr[1, :, s * FFT_CHUNK:(s + 1) * FFT_CHUNK]
    _slab_fft(zr, zi, tw_ref, n1, ct, inverse=False, half_input=True)
    ngroups, gr = _fft_groups(n1, ct)

    def body(g, carry):
        r0 = pl.multiple_of(g * gr, gr)
        a = zr[pl.ds(r0, gr), :]
        b = zi[pl.ds(r0, gr), :]
        tr = twr_ref[pl.ds(r0, gr), :]
        ti = twi_ref[pl.ds(r0, gr), :]
        yr, yi = _chunk_dft(a * tr - b * ti, a * ti + b * tr, wf_ref)
        fr = fr_ref[pl.ds(r0, gr), :]
        fi = fi_ref[pl.ds(r0, gr), :]
        qr, qi = _chunk_dft(yr * fr - yi * fi, yr * fi + yi * fr, wb_ref)
        zr[pl.ds(r0, gr), :] = qr * tr + qi * ti
        zi[pl.ds(r0, gr), :] = qi * tr - qr * ti
        return carry

    lax.fori_loop(0, ngroups, body, 0, unroll=min(2, ngroups))
    _slab_fft(zr, zi, tw_ref, n1, ct, inverse=True, half_output=True)
    inv_n = 1.0 / (n1 * FFT_CHUNK)
    skip = skip_ref[...]
    for s in range(nz):
        sl = slice(s * FFT_CHUNK, (s + 1) * FFT_CHUNK)
        for a, z in ((0, zr), (1, zi)):
            y = z[s * ct:(s + 1) * ct, :] * inv_n
            u = u_scr[a, :, sl]
            o_ref[a, :, sl] = (g_scr[a, :, sl] * (y + u * skip)).astype(o_ref.dtype)


def _hy_conv_call(hy_t, spec_r, spec_i, cw, skip, length, lane_block, prev=None):
    _, bh, c3, t_all = hy_t.shape
    c = c3 // 3
    ct = HY_CT
    nct = c // ct
    n1 = 2 * length // FFT_CHUNK
    tw, twr, twi, wf, wb = _fft_consts(n1, ct)
    const2 = lambda p, i: (0, 0)
    xin = lambda k: pl.BlockSpec((2, None, ct, length), lambda p, i, k=k: (0, p, k * nct + i, lane_block))
    cwin = lambda k: pl.BlockSpec((ct, 4), lambda p, i, k=k: (k * nct + i, 0))
    fin = pl.BlockSpec((None, n1 * ct, FFT_CHUNK), lambda p, i: (i, 0, 0))
    in_specs = [xin(0), xin(1), xin(2), cwin(0), cwin(1), cwin(2),
                pl.BlockSpec((ct, 1), lambda p, i: (i, 0)),
                fin, fin,
                pl.BlockSpec(memory_space=pltpu.SMEM),
                pl.BlockSpec(twr.shape, const2), pl.BlockSpec(twi.shape, const2),
                pl.BlockSpec(wf.shape, lambda p, i: (0, 0, 0)),
                pl.BlockSpec(wb.shape, lambda p, i: (0, 0, 0))]
    args = [hy_t, hy_t, hy_t, cw, cw, cw, skip, spec_r, spec_i, jnp.asarray(tw), jnp.asarray(twr),
            jnp.asarray(twi), jnp.asarray(wf), jnp.asarray(wb)]
    kern = functools.partial(_hy_conv_kernel, n1, ct, length)
    aliases = {}
    if prev is not None:
        in_specs.append(pl.BlockSpec(memory_space=pl.ANY))
        args.append(prev)
        aliases = {len(args) - 1: 0}
        kern = lambda *refs, kern=kern: kern(*refs[:14], *refs[15:])
    return pl.pallas_call(
        kern,
        grid=(bh, nct),
        in_specs=in_specs,
        out_specs=pl.BlockSpec((2, None, ct, length), lambda p, i: (0, p, i, lane_block)),
        out_shape=jax.ShapeDtypeStruct((2, bh, c, t_all), BF16),
        scratch_shapes=[pltpu.VMEM((n1 * ct, FFT_CHUNK), F32), pltpu.VMEM((n1 * ct, FFT_CHUNK), F32),
                        pltpu.VMEM((2, ct, length), F32), pltpu.VMEM((2, ct, length), F32)],
        input_output_aliases=aliases,
        name="hyena_conv_%d" % length,
        compiler_params=_cparams(("parallel", "parallel")),
    )(*args)


def _outproj_kernel(n_lat_tiles, ctx_row, x_ref, hy_ref, ga_ref, na_ref, w_ref, mod_ref, o_ref):
    j = pl.program_id(1)
    nbat, tm, d = x_ref.shape
    mix = []
    for s in range(nbat):
        cm = jnp.concatenate([hy_ref[s], ga_ref[s], na_ref[s]], axis=0)
        mix.append(cm.astype(F32).T.astype(BF16))
    y = _dot(jnp.concatenate(mix, axis=0), w_ref[...])
    for s in range(nbat):
        row = jnp.where(j < n_lat_tiles, pl.program_id(0) * nbat + s, ctx_row)
        o_ref[s] = x_ref[s] + _mod_row(mod_ref, row, 2, d) * y[s * tm:(s + 1) * tm]


def _outproj_call(xa, hyo_t, gqa, na, w_out, mod_l, s_len, n_tok):
    bsz, _, d = xa.shape
    tm = TOK_TILE
    nbat = _batch_group(bsz)
    const = lambda b, j: (0, 0)
    tok = lambda n: pl.BlockSpec((nbat, tm, n), lambda b, j: (b, j, 0))
    chan = lambda n: pl.BlockSpec((nbat, n, tm), lambda b, j: (b, 0, j))
    return pl.pallas_call(
        functools.partial(_outproj_kernel, s_len // tm, bsz),
        grid=(bsz // nbat, n_tok // tm),
        in_specs=[tok(d), chan(hyo_t.shape[1]), chan(gqa.shape[1]), chan(na.shape[1]),
                  pl.BlockSpec(w_out.shape, const), pl.BlockSpec(mod_l.shape, const)],
        out_specs=tok(d),
        out_shape=jax.ShapeDtypeStruct((bsz, n_tok, d), F32),
        name="out_proj",
        compiler_params=_cparams(("parallel", "parallel")),
    )(xa, hyo_t, gqa, na, w_out, mod_l)


def _ffn_kernel(tiles_per_batch, n_lat_tiles, ctx_row, th, x_ref, mod_ref, g_ref, wi_ref, wo_ref,
                gf_ref, o_ref, h_scr, act_scr):
    i = pl.program_id(0)
    d = x_ref.shape[-1]
    f = wo_ref.shape[0]
    groups = x_ref.shape[0] // TOK_TILE

    def mod_row(s):
        gi = i * groups + s
        jj = gi % tiles_per_batch
        return jnp.where(jj < n_lat_tiles, gi // tiles_per_batch, ctx_row)

    for s in range(groups):
        row = mod_row(s)
        sl = slice(s * TOK_TILE, (s + 1) * TOK_TILE)
        h = _norm_mod(x_ref[sl, :], g_ref[...], _mod_row(mod_ref, row, 3, d), _mod_row(mod_ref, row, 4, d))
        h_scr[sl, :] = h.astype(BF16)
    h = h_scr[...]
    for j in range(f // th):
        gate = _dot(h, wi_ref[:, j * th:(j + 1) * th])
        up = _dot(h, wi_ref[:, f + j * th:f + (j + 1) * th])
        act_scr[:, j * th:(j + 1) * th] = (gate * jax.nn.sigmoid(gate) * up).astype(BF16)
    y = _dot(act_scr[...], wo_ref[...])
    for s in range(groups):
        row = mod_row(s)
        sl = slice(s * TOK_TILE, (s + 1) * TOK_TILE)
        out = x_ref[sl, :] + _mod_row(mod_ref, row, 5, d) * y[sl, :]
        if gf_ref is not None:
            ms = jnp.mean(out * out, axis=-1, keepdims=True)
            out = out * lax.rsqrt(ms + NORM_EPS) * gf_ref[...]
        o_ref[sl, :] = out


def _ffn_call(xa, mod_l, g, w_in, w_out, s_len, g_final=None):
    bsz, t_all, d = xa.shape
    f = w_out.shape[0]
    r = bsz * t_all
    tm = 1024 if r % 1024 == 0 else TOK_TILE
    th = 256
    const = lambda i: (0, 0)
    resident = lambda a: pl.BlockSpec(a.shape, const, pipeline_mode=pl.Buffered(1))
    in_specs = [pl.BlockSpec((tm, d), lambda i: (i, 0)),
                pl.BlockSpec(mod_l.shape, const),
                pl.BlockSpec((1, d), const),
                resident(w_in), resident(w_out)]
    args = [xa.reshape(r, d), mod_l, g, w_in, w_out]
    kern = functools.partial(_ffn_kernel, t_all // TOK_TILE, s_len // TOK_TILE, bsz, th)
    if g_final is None:
        kern = functools.partial(_ffn_no_final, kern)
    else:
        in_specs.append(pl.BlockSpec((1, d), const))
        args.append(g_final)
    out = pl.pallas_call(
        kern,
        grid=(r // tm,),
        in_specs=in_specs,
        out_specs=pl.BlockSpec((tm, d), lambda i: (i, 0)),
        out_shape=jax.ShapeDtypeStruct((r, d), F32),
        scratch_shapes=[pltpu.VMEM((tm, d), BF16), pltpu.VMEM((tm, f), BF16)],
        name="swiglu_ffn",
        compiler_params=_cparams(("parallel",)),
    )(*args)
    return out.reshape(bsz, t_all, d)


def _ffn_no_final(kern, x_ref, mod_ref, g_ref, wi_ref, wo_ref, o_ref, h_scr, act_scr):
    kern(x_ref, mod_ref, g_ref, wi_ref, wo_ref, None, o_ref, h_scr, act_scr)


def _rope_tables(s_len, lc):
    pos = np.arange(s_len)
    row = (pos // GRID_W).astype(np.float64)
    col = (pos % GRID_W).astype(np.float64)
    n_f = HEAD_DIM // 4
    inv = (ROPE_THETA ** (-np.arange(n_f, dtype=np.float32) / n_f)).astype(np.float64)
    ang = np.concatenate([row[:, None] * inv, col[:, None] * inv], axis=-1).astype(np.float32).astype(np.float64)
    lane = np.arange(LANES)
    sign = np.where((lane % HEAD_DIM) < HEAD_DIM // 2, -1.0, 1.0)
    cos = np.cos(ang)[:, lane % (HEAD_DIM // 2)]
    sin = np.sin(ang)[:, lane % (HEAD_DIM // 2)] * sign[None, :]
    cos = np.concatenate([cos, np.ones((lc, LANES))], axis=0).astype(np.float32)
    sin = np.concatenate([sin, np.zeros((lc, LANES))], axis=0).astype(np.float32)
    cosc = np.concatenate([np.cos(ang).T, np.ones((HEAD_DIM // 2, lc))], axis=1).astype(np.float32)
    sinc = np.concatenate([np.sin(ang).T, np.zeros((HEAD_DIM // 2, lc))], axis=1).astype(np.float32)
    return jnp.asarray(cosc), jnp.asarray(sinc), jnp.asarray(cos), jnp.asarray(sin)


def kernel(x, c, ctx, c_ctx, w_mod, b_mod, g_mix, g_ffn, w_in, w_out, hy_conv_w, hy_conv_b,
           hy_f_w1, hy_f_b1, hy_f_w2, hy_f_b2, hy_f_w3, hy_f_b3, hy_f_wout, hy_f_freq, hy_skip,
           qk_g_q, qk_g_k, na_rpb, w_ffn_in, w_ffn_out, g_final):
    bsz, s_len, d = x.shape
    lc = ctx.shape[1]
    depth = w_mod.shape[0]
    t_all = s_len + lc
    hyw = hy_skip.shape[1]
    hy_cols = 3 * hyw
    assert bsz % 2 == 0 and s_len % TOK_TILE == 0 and lc == TOK_TILE and s_len % lc == 0

    nb = -(-(bsz + 1) // 8) * 8
    cs = jnp.zeros((nb, d), F32).at[:bsz].set(c).at[bsz].set(c_ctx)
    mod = _mod_call(cs, w_mod, b_mod)

    rope = _rope_tables(s_len, lc)
    tables = _na_bias_tables(na_rpb, s_len // GRID_W)
    q0 = hy_cols
    k0 = q0 + GQA_HEADS * HEAD_DIM
    v0 = k0 + GQA_KV_HEADS * HEAD_DIM
    nq0 = v0 + GQA_KV_HEADS * HEAD_DIM
    nk0 = nq0 + NA_HEADS * HEAD_DIM
    nv0 = nk0 + NA_HEADS * HEAD_DIM
    w_cm = jnp.concatenate([w_in[:, :, :k0], w_in[:, :, v0:nk0], w_in[:, :, nv0:]], axis=2)
    w_cm = jnp.swapaxes(w_cm, 1, 2).astype(BF16)
    w_tm = jnp.concatenate([w_in[:, :, k0:v0], w_in[:, :, nk0:nv0]], axis=2).astype(BF16)
    w_out_b = w_out.astype(BF16)
    w_fi = w_ffn_in.astype(BF16)
    w_fo = w_ffn_out.astype(BF16)
    cw = jnp.concatenate([jnp.swapaxes(hy_conv_w, 1, 2), hy_conv_b[:, :, None]], axis=2)
    per = LANES // HEAD_DIM

    xa = jnp.concatenate([x, ctx], axis=1)
    for l in range(depth):
        last = l == depth - 1
        gq_b = jnp.broadcast_to(qk_g_q[l][:, None], (HEAD_DIM, LANES))
        gk2 = jnp.tile(qk_g_k[l], per)[None]
        hy_t, q, k, v, nq, nk, nv = _inproj_call(xa, mod[l], g_mix[l][None], w_cm[l], w_tm[l],
                                                  gq_b, gk2, rope, hy_cols, s_len)
        n_tok = s_len if last else t_all
        gqa = _gqa_call(q, k, v, _gqa_bounds(qk_g_q[l], qk_g_k[l]), s_len, n_tok)
        na = _na_call(nq, nk, nv, tables[l], s_len, n_tok)

        fargs = (hy_f_w1[l], hy_f_b1[l], hy_f_w2[l], hy_f_b2[l], hy_f_w3[l], hy_f_b3[l],
                 hy_f_wout[l], hy_f_freq[l])
        hy_p = hy_t.reshape(2, bsz // 2, hy_cols, t_all)
        skip = hy_skip[l][:, None]
        fr, fi = _hy_spec_call(_hy_filter_call(s_len, *fargs))
        hyo = _hy_conv_call(hy_p, fr, fi, cw[l], skip, s_len, 0)
        if not last:
            frc, fic = _hy_spec_call(_hy_filter_call(lc, *fargs))
            hyo = _hy_conv_call(hy_p, frc, fic, cw[l], skip, lc, s_len // lc, prev=hyo)
        hyo = hyo.reshape(bsz, hyw, t_all)

        xa = _outproj_call(xa, hyo, gqa, na, w_out_b[l], mod[l], s_len, n_tok)
        xa = _ffn_call(xa, mod[l], g_ffn[l][None], w_fi[l], w_fo[l], s_len,
                       g_final[None] if last else None)
    return xa
```

```python
import functools
import math

import numpy as np
import jax
import jax.numpy as jnp
from jax import lax
from jax.experimental import pallas as pl
from jax.experimental.pallas import tpu as pltpu

F32 = jnp.float32
BF16 = jnp.bfloat16
HIGHEST = lax.Precision.HIGHEST

GRID_W = 64
HEAD_DIM = 64
GQA_HEADS = 8
GQA_KV_HEADS = 2
GQA_GROUP = GQA_HEADS // GQA_KV_HEADS
NA_HEADS = 4
NA_KH = 8
NA_KW = 16
ROPE_THETA = 10000.0
HY_EMB_DIM = 33
HY_DECAY_TARGET = 1e-2
HY_FAST_DECAY = 0.3
HY_SLOW_DECAY = 1.5
NORM_EPS = 1e-6
ATTN_SCALE = HEAD_DIM ** -0.5
NEG = -1e30
LOG2E = math.log2(math.e)

LANES = 128
TOK_TILE = 256
FFT_CHUNK = LANES
HY_CT = 64
NA_ROWS = TOK_TILE // GRID_W
NA_WIN = NA_ROWS + NA_KH
VMEM_LIMIT = 56 * 1024 * 1024


def _cparams(sem):
    return pltpu.CompilerParams(dimension_semantics=sem, vmem_limit_bytes=VMEM_LIMIT)


def _dot(a, b, **kw):
    return jnp.dot(a, b, preferred_element_type=F32, **kw)


def _dot_nt(a, b, **kw):
    return lax.dot_general(a, b, (((1,), (1,)), ((), ())), preferred_element_type=F32, **kw)


def _mod_kernel(cs_ref, w_ref, b_ref, o_ref):
    cs = cs_ref[...]
    s = cs * jax.nn.sigmoid(cs)
    o_ref[...] = _dot(s, w_ref[...], precision=HIGHEST) + b_ref[...]


def _mod_call(cs, w_mod, b_mod):
    depth, d, n = w_mod.shape
    nb = cs.shape[0]
    tn = 1024
    return pl.pallas_call(
        _mod_kernel,
        grid=(depth, n // tn),
        in_specs=[pl.BlockSpec((nb, d), lambda l, j: (0, 0)),
                  pl.BlockSpec((None, d, tn), lambda l, j: (l, 0, j)),
                  pl.BlockSpec((None, 1, tn), lambda l, j: (l, 0, j))],
        out_specs=pl.BlockSpec((None, nb, tn), lambda l, j: (l, 0, j)),
        out_shape=jax.ShapeDtypeStruct((depth, nb, n), F32),
        name="adaln_mod",
        compiler_params=_cparams(("parallel", "parallel")),
    )(cs, w_mod, b_mod.reshape(depth, 1, n))


def _mod_row(mod_ref, row, k, d):
    return mod_ref[pl.ds(row, 1), k * d:(k + 1) * d]


def _norm_mod(x, g, shift, scale):
    ms = jnp.mean(x * x, axis=-1, keepdims=True)
    y = x * lax.rsqrt(ms + NORM_EPS) * g
    return y * (1.0 + scale) + shift


def _batch_group(bsz):
    return 4 if bsz % 4 == 0 else 2


def _inproj_kernel(n_lat_tiles, ctx_row, hyc, x_ref, mod_ref, g_ref, wcm_ref, wtm_ref, gq_ref, gk_ref,
                   cosc_ref, sinc_ref, cos_ref, sin_ref,
                   hy_ref, q_ref, k_ref, v_ref, nq_ref, nk_ref, nv_ref, h_scr):
    j = pl.program_id(1)
    nbat, tm, d = x_ref.shape
    for s in range(nbat):
        row = jnp.where(j < n_lat_tiles, pl.program_id(0) * nbat + s, ctx_row)
        h = _norm_mod(x_ref[s], g_ref[...], _mod_row(mod_ref, row, 0, d), _mod_row(mod_ref, row, 1, d))
        h_scr[s * tm:(s + 1) * tm, :] = h.astype(BF16)
    h = h_scr[...]
    cm = _dot_nt(wcm_ref[...], h)
    tmj = _dot(h, wtm_ref[...])

    half = HEAD_DIM // 2
    reps = cm.shape[1] // LANES
    gq = jnp.concatenate([gq_ref[...]] * reps, axis=1)
    cosc = jnp.concatenate([cosc_ref[...]] * nbat, axis=1)
    sinc = jnp.concatenate([sinc_ref[...]] * nbat, axis=1)
    for hh in range(GQA_HEADS):
        x = cm[hyc + hh * HEAD_DIM:hyc + (hh + 1) * HEAD_DIM, :]
        ms = jnp.mean(x * x, axis=0, keepdims=True)
        xn = x * lax.rsqrt(ms + NORM_EPS) * gq
        x1, x2 = xn[:half], xn[half:]
        y = jnp.concatenate([x1 * cosc - x2 * sinc, x1 * sinc + x2 * cosc], axis=0)
        y = (y * (ATTN_SCALE * LOG2E)).astype(BF16)
        for s in range(nbat):
            q_ref[s, hh * HEAD_DIM:(hh + 1) * HEAD_DIM, :] = y[:, s * tm:(s + 1) * tm]
    voff = hyc + GQA_HEADS * HEAD_DIM
    nqoff = voff + GQA_KV_HEADS * HEAD_DIM
    nvoff = nqoff + NA_HEADS * HEAD_DIM
    for s in range(nbat):
        cols = slice(s * tm, (s + 1) * tm)
        hy_ref[s] = cm[:hyc, cols]
        v_ref[s] = cm[voff:nqoff, cols].astype(BF16)
        nq_ref[s] = (cm[nqoff:nvoff, cols] * (ATTN_SCALE * LOG2E)).astype(BF16)
        for e in range(NA_HEADS):
            nv_ref[s, e] = cm[nvoff + e * HEAD_DIM:nvoff + (e + 1) * HEAD_DIM, cols].astype(BF16)
        _inproj_token_major(tmj[s * tm:(s + 1) * tm], gk_ref, cos_ref, sin_ref, k_ref.at[s], nk_ref.at[s])


def _inproj_token_major(p, gk_ref, cos_ref, sin_ref, k_ref, nk_ref):
    ri = lax.broadcasted_iota(jnp.int32, (LANES, LANES), 0) // HEAD_DIM
    ci = lax.broadcasted_iota(jnp.int32, (LANES, LANES), 1) // HEAD_DIM
    avg = jnp.where(ri == ci, 1.0 / HEAD_DIM, 0.0).astype(BF16)
    lane = lax.broadcasted_iota(jnp.int32, (1, LANES), 1)
    first_half = (lane % HEAD_DIM) < (HEAD_DIM // 2)
    xc = p[:, :LANES]
    ms = _dot((xc * xc).astype(BF16), avg)
    xn = xc * lax.rsqrt(ms + NORM_EPS) * gk_ref[...]
    partner = jnp.where(first_half, pltpu.roll(xn, LANES - HEAD_DIM // 2, 1),
                        pltpu.roll(xn, HEAD_DIM // 2, 1))
    y = (xn * cos_ref[...] + partner * sin_ref[...]).astype(BF16)
    for e in range(GQA_KV_HEADS):
        k_ref[e] = y[:, e * HEAD_DIM:(e + 1) * HEAD_DIM]
    for e in range(NA_HEADS):
        nk_ref[e] = p[:, LANES + e * HEAD_DIM:LANES + (e + 1) * HEAD_DIM].astype(BF16)


def _inproj_call(xa, mod_l, g, w_cm, w_tm, gq_b, gk2, rope, hyc, s_len):
    bsz, t_all, d = xa.shape
    tm = TOK_TILE
    nbat = _batch_group(bsz)
    cosc, sinc, cos_t, sin_t = rope
    grid = (bsz // nbat, t_all // tm)
    const = lambda b, j: (0, 0)
    resident = lambda a: pl.BlockSpec(a.shape, const, pipeline_mode=pl.Buffered(1))
    heads = lambda n: pl.BlockSpec((nbat, n, tm, HEAD_DIM), lambda b, j: (b, 0, j, 0))
    hshape = lambda n: jax.ShapeDtypeStruct((bsz, n, t_all, HEAD_DIM), BF16)
    chan = lambda n: pl.BlockSpec((nbat, n, tm), lambda b, j: (b, 0, j))
    qc = GQA_HEADS * HEAD_DIM
    vc = GQA_KV_HEADS * HEAD_DIM
    nc = NA_HEADS * HEAD_DIM
    return pl.pallas_call(
        functools.partial(_inproj_kernel, s_len // tm, bsz, hyc),
        grid=grid,
        in_specs=[pl.BlockSpec((nbat, tm, d), lambda b, j: (b, j, 0)),
                  pl.BlockSpec(mod_l.shape, const),
                  pl.BlockSpec((1, d), const),
                  resident(w_cm), resident(w_tm),
                  pl.BlockSpec(gq_b.shape, const),
                  pl.BlockSpec((1, LANES), const),
                  pl.BlockSpec((HEAD_DIM // 2, tm), lambda b, j: (0, j)),
                  pl.BlockSpec((HEAD_DIM // 2, tm), lambda b, j: (0, j)),
                  pl.BlockSpec((tm, LANES), lambda b, j: (j, 0)),
                  pl.BlockSpec((tm, LANES), lambda b, j: (j, 0))],
        out_specs=[chan(hyc), chan(qc), heads(GQA_KV_HEADS), chan(vc),
                   chan(nc), heads(NA_HEADS),
                   pl.BlockSpec((nbat, NA_HEADS, None, HEAD_DIM, tm), lambda b, j: (b, 0, j, 0, 0))],
        out_shape=[jax.ShapeDtypeStruct((bsz, hyc, t_all), F32),
                   jax.ShapeDtypeStruct((bsz, qc, t_all), BF16),
                   hshape(GQA_KV_HEADS),
                   jax.ShapeDtypeStruct((bsz, vc, t_all), BF16),
                   jax.ShapeDtypeStruct((bsz, nc, t_all), BF16),
                   hshape(NA_HEADS),
                   jax.ShapeDtypeStruct((bsz, NA_HEADS, t_all // tm, HEAD_DIM, tm), BF16)],
        scratch_shapes=[pltpu.VMEM((nbat * tm, d), BF16)],
        name="in_proj",
        compiler_params=_cparams(("parallel", "parallel")),
    )(xa, mod_l, g, w_cm, w_tm, gq_b, gk2, cosc, sinc, cos_t, sin_t)


def _gqa_kernel(s_len, lc, tk, q_ref, k_ref, vt_ref, o_ref, s_scr):
    i = pl.program_id(2)
    tm = q_ref.shape[1]
    sub = 8

    def finish(h, l, acc):
        o_ref[h * HEAD_DIM:(h + 1) * HEAD_DIM, :] = (acc / jnp.sum(l, axis=0, keepdims=True)).astype(o_ref.dtype)

    def scores(h, q, r0, nrow, m):
        s = _dot(k_ref[r0:r0 + nrow, :], q)
        s_scr[h, r0:r0 + nrow, :] = s
        return jnp.maximum(m, jnp.max(s.reshape(nrow // sub, sub, tm), axis=0))

    def apply(h, r0, nrow, m, l, acc):
        p = jnp.exp2(s_scr[h, r0:r0 + nrow, :].reshape(nrow // sub, sub, tm) - m[None])
        l = l + jnp.sum(p, axis=0)
        acc = acc + _dot(vt_ref[:, r0:r0 + nrow], p.reshape(nrow, tm).astype(BF16))
        return l, acc

    def two_pass(chunks):
        maxes = [None] * GQA_GROUP
        for stage in range(GQA_GROUP + 1):
            h1, h2 = stage, stage - 1
            if h1 < GQA_GROUP:
                q1 = q_ref[h1 * HEAD_DIM:(h1 + 1) * HEAD_DIM, :]
                m1 = jnp.full((sub, tm), NEG, F32)
            if h2 >= 0:
                m2 = jnp.broadcast_to(jnp.max(maxes[h2], axis=0, keepdims=True), (sub, tm))
                l = jnp.zeros((sub, tm), F32)
                acc = jnp.zeros((HEAD_DIM, tm), F32)
            for r0, nrow in chunks:
                if h1 < GQA_GROUP:
                    m1 = scores(h1, q1, r0, nrow, m1)
                if h2 >= 0:
                    l, acc = apply(h2, r0, nrow, m2, l, acc)
            if h1 < GQA_GROUP:
                maxes[h1] = m1
            if h2 >= 0:
                finish(h2, l, acc)

    ctx_chunk = (s_len, lc)

    @pl.when(i * tm < s_len)
    def _():
        two_pass([(c * tk, tk) for c in range(s_len // tk)] + [ctx_chunk])

    @pl.when(i * tm >= s_len)
    def _():
        two_pass([ctx_chunk])


def _gqa_call(qt, k, vt, s_len, n_query):
    bsz, qc, t_all = qt.shape
    hd = HEAD_DIM
    tm = TOK_TILE
    tk = 2 * TOK_TILE
    lc = t_all - s_len
    return pl.pallas_call(
        functools.partial(_gqa_kernel, s_len, lc, tk),
        grid=(bsz, GQA_KV_HEADS, n_query // tm),
        in_specs=[pl.BlockSpec((None, GQA_GROUP * hd, tm), lambda b, g, i: (b, g, i)),
                  pl.BlockSpec((None, None, t_all, hd), lambda b, g, i: (b, g, 0, 0)),
                  pl.BlockSpec((None, hd, t_all), lambda b, g, i: (b, g, 0))],
        out_specs=pl.BlockSpec((None, GQA_GROUP * hd, tm), lambda b, g, i: (b, g, i)),
        out_shape=jax.ShapeDtypeStruct((bsz, qc, t_all), BF16),
        scratch_shapes=[pltpu.VMEM((GQA_GROUP, t_all, tm), F32)],
        name="gqa_attention",
        compiler_params=_cparams(("parallel", "parallel", "parallel")),
    )(qt, k, vt)


def _na_bias_consts(rows):
    assert rows >= NA_WIN + NA_ROWS
    nrow, ncol = 2 * NA_KH - 1, 2 * NA_KW - 1
    rsel = np.zeros((3, NA_ROWS, NA_WIN, nrow), np.float32)
    mrow = np.zeros((3, NA_ROWS, NA_WIN), bool)
    for kind, r0 in enumerate((0, NA_ROWS, rows - NA_ROWS)):
        kr0 = int(np.clip(r0 - NA_KH // 2, 0, rows - NA_WIN))
        for qr in range(NA_ROWS):
            r = r0 + qr
            rs = int(np.clip(r - NA_KH // 2, 0, rows - NA_KH))
            for kr in range(NA_WIN):
                ka = kr0 + kr
                if rs <= ka < rs + NA_KH:
                    mrow[kind, qr, kr] = True
                    rsel[kind, qr, kr, ka - r + NA_KH - 1] = 1.0
    csel = np.zeros((GRID_W, GRID_W, ncol), np.float32)
    mcol = np.zeros((GRID_W, GRID_W), bool)
    for w in range(GRID_W):
        cs = int(np.clip(w - NA_KW // 2, 0, GRID_W - NA_KW))
        for kc in range(cs, cs + NA_KW):
            mcol[w, kc] = True
            csel[w, kc, kc - w + NA_KW - 1] = 1.0
    mask = mrow[:, :, None, :, None] & mcol[None, None, :, None, :]
    return rsel, csel, mask.reshape(3, TOK_TILE, NA_WIN * GRID_W)


def _na_bias_tables(na_rpb, rows):
    rsel, csel, mask = _na_bias_consts(rows)
    depth, heads = na_rpb.shape[:2]
    a = jnp.einsum("kqra,lhab->lkhqrb", jnp.asarray(rsel), na_rpb.astype(F32) * LOG2E, precision=HIGHEST)
    t = jnp.einsum("lkhqrb,wcb->lkhrcqw", a, jnp.asarray(csel), precision=HIGHEST)
    t = t.reshape(depth, 3, heads, NA_WIN * GRID_W, TOK_TILE)
    mask_t = np.swapaxes(mask, 1, 2)
    return jnp.where(jnp.asarray(mask_t)[None, :, None], t, NEG)


def _na_kernel(s_len, lc, rows, q_ref, k_ref, vt_ref, tab_ref, o_ref, s_scr):
    i = pl.program_id(1)
    tm = q_ref.shape[1]
    nblk = s_len // tm
    sub = 8

    def scores(h, part, off, m):
        r0, _, nrow, bias = part
        s = _dot(k_ref[h, pl.ds(r0, nrow), :], q_ref[h * HEAD_DIM:(h + 1) * HEAD_DIM, :])
        if bias is not None:
            s = s + bias[h]
        s_scr[h, off:off + nrow, :] = s
        return jnp.maximum(m, jnp.max(s.reshape(nrow // sub, sub, tm), axis=0))

    def apply(h, part, off, m, l, acc):
        _, c0, nrow, _ = part
        p = jnp.exp2(s_scr[h, off:off + nrow, :].reshape(nrow // sub, sub, tm) - m[None])
        l = l + jnp.sum(p, axis=0)
        vt = jnp.concatenate([vt_ref[h, c0 + c] for c in range(nrow // tm)], axis=1)
        return l, acc + _dot(vt, p.reshape(nrow, tm).astype(BF16))

    def run(parts):
        offs = [sum(p[2] for p in parts[:n]) for n in range(len(parts))]
        maxes = [None] * NA_HEADS
        for stage in range(NA_HEADS + 1):
            h1, h2 = stage, stage - 1
            if h1 < NA_HEADS:
                m1 = jnp.full((sub, tm), NEG, F32)
            if h2 >= 0:
                m2 = jnp.broadcast_to(jnp.max(maxes[h2], axis=0, keepdims=True), (sub, tm))
                l = jnp.zeros((sub, tm), F32)
                acc = jnp.zeros((HEAD_DIM, tm), F32)
            for part, off in zip(parts, offs):
                if h1 < NA_HEADS:
                    m1 = scores(h1, part, off, m1)
                if h2 >= 0:
                    l, acc = apply(h2, part, off, m2, l, acc)
            if h1 < NA_HEADS:
                maxes[h1] = m1
            if h2 >= 0:
                o = acc / jnp.sum(l, axis=0, keepdims=True)
                o_ref[h2 * HEAD_DIM:(h2 + 1) * HEAD_DIM, :] = o.astype(o_ref.dtype)

    ctx_part = (s_len, s_len // tm, lc, None)

    @pl.when(i < nblk)
    def _():
        kind = jnp.where(i == 0, 0, jnp.where(i == nblk - 1, 2, 1))
        c0 = jnp.clip(i - (NA_KH // 2) // NA_ROWS, 0, (rows - NA_WIN) // NA_ROWS)
        ks = pl.multiple_of(c0 * tm, tm)
        run([(ks, c0, NA_WIN * GRID_W, tab_ref.at[kind]), ctx_part])

    @pl.when(i >= nblk)
    def _():
        run([ctx_part])


def _na_call(nqt, nk, nvt, table, s_len, n_query):
    bsz, heads, t_all, hd = nk.shape
    tm = TOK_TILE
    rows = s_len // GRID_W
    assert NA_ROWS * GRID_W == tm and (NA_KH // 2) % NA_ROWS == 0 and (rows - NA_WIN) % NA_ROWS == 0
    return pl.pallas_call(
        functools.partial(_na_kernel, s_len, t_all - s_len, rows),
        grid=(bsz, n_query // tm),
        in_specs=[pl.BlockSpec((None, heads * hd, tm), lambda b, i: (b, 0, i)),
                  pl.BlockSpec((None, heads, t_all, hd), lambda b, i: (b, 0, 0, 0)),
                  pl.BlockSpec((None,) + nvt.shape[1:], lambda b, i: (b, 0, 0, 0, 0)),
                  pl.BlockSpec(table.shape, lambda b, i: (0, 0, 0, 0), pipeline_mode=pl.Buffered(1))],
        out_specs=pl.BlockSpec((None, heads * hd, tm), lambda b, i: (b, 0, i)),
        out_shape=jax.ShapeDtypeStruct((bsz, heads * hd, t_all), BF16),
        scratch_shapes=[pltpu.VMEM((heads, NA_WIN * GRID_W + t_all - s_len, tm), F32)],
        name="neighbourhood_attention",
        compiler_params=_cparams(("parallel", "parallel")),
    )(nqt, nk, nvt, table)


def _hy_filter_kernel(z_ref, t_ref, side_ref, w1_ref, b1_ref, w2_ref, b2_ref, w3_ref, b3_ref,
                      wo_ref, fr_ref, dl_ref, o_ref):
    om = fr_ref[...]
    h = jnp.sin(om * (_dot(z_ref[...], w1_ref[...], precision=HIGHEST) + b1_ref[...]))
    h = jnp.sin(om * (_dot(h, w2_ref[...], precision=HIGHEST) + b2_ref[...]))
    h = jnp.sin(om * (_dot(h, w3_ref[...], precision=HIGHEST) + b3_ref[...]))
    ht = _dot_nt(wo_ref[...], h, precision=HIGHEST)
    c = o_ref.shape[0]
    decay = jnp.exp(-t_ref[...] * jnp.abs(dl_ref[...]))
    side = side_ref[...]
    o_ref[...] = decay * (jnp.where(side > 0.5, ht[:c], 0.0) + jnp.where(side < -0.5, ht[c:], 0.0))


def _hy_filter_consts(length):
    n = np.arange(2 * length)
    pos = np.where(n < length, n, 2 * length - n)
    pos = np.where(n == length, 0, pos)
    side = np.where(n < length, 1.0, -1.0)
    side = np.where(n == length, 0.0, side)
    t = np.linspace(0.0, 1.0, length)[pos]
    bands = (HY_EMB_DIM - 1) // 2
    w = 2.0 * math.pi * pos / length
    fr = np.linspace(1e-4, bands - 1, bands)
    z = np.concatenate([t[:, None], np.cos(fr[None, :] * w[:, None]), -np.sin(fr[None, :] * w[:, None])], axis=-1)
    zp = np.zeros((2 * length, 64), np.float32)
    zp[:, :HY_EMB_DIM] = z
    return zp, t.astype(np.float32)[None, :], side.astype(np.float32)[None, :]


def _hy_filter_call(length, w1, b1, w2, b2, w3, b3, wout, freq):
    f = w2.shape[0]
    c = wout.shape[1] // 2
    zp, t, side = _hy_filter_consts(length)
    w1p = jnp.zeros((64, f), F32).at[:HY_EMB_DIM].set(w1)
    deltas = np.linspace(math.log(HY_DECAY_TARGET) / HY_SLOW_DECAY,
                         math.log(HY_DECAY_TARGET) / HY_FAST_DECAY, c).astype(np.float32)[:, None]
    args = (jnp.asarray(zp), jnp.asarray(t), jnp.asarray(side), w1p, b1[None], w2, b2[None], w3, b3[None],
            wout.T, freq[None], jnp.asarray(deltas))
    return pl.pallas_call(
        _hy_filter_kernel,
        in_specs=[pl.BlockSpec(a.shape, lambda: (0,) * a.ndim) for a in args],
        out_specs=pl.BlockSpec((c, 2 * length), lambda: (0, 0)),
        out_shape=jax.ShapeDtypeStruct((c, 2 * length), F32),
        grid=(),
        name="hyena_filter",
        compiler_params=pltpu.CompilerParams(vmem_limit_bytes=VMEM_LIMIT),
    )(*args)


def _bitrev(n):
    bits = int(math.log2(n))
    return np.array([int(format(i, "0%db" % bits)[::-1], 2) if bits else 0 for i in range(n)])


def _fft_consts(n1, ct):
    n = n1 * FFT_CHUNK
    stages = max(int(math.log2(n1)), 1)
    tw = np.zeros((2, stages, max(n1 // 2, 1)), np.float32)
    for s in range(int(math.log2(n1))):
        half = n1 >> (s + 1)
        ang = -2.0 * math.pi * np.arange(half) / (2 * half)
        tw[0, s, :half] = np.cos(ang)
        tw[1, s, :half] = np.sin(ang)
    k1 = _bitrev(n1)
    ang = -2.0 * math.pi * (np.arange(FFT_CHUNK)[None, :] * k1[:, None]) / n
    twr = np.repeat(np.cos(ang), ct, axis=0).astype(np.float32)
    twi = np.repeat(np.sin(ang), ct, axis=0).astype(np.float32)
    a = -2.0 * math.pi * np.outer(np.arange(FFT_CHUNK), np.arange(FFT_CHUNK)) / FFT_CHUNK
    wr, wi = np.cos(a), np.sin(a)
    wf = np.block([[wr, wi], [-wi, wr]]).astype(np.float32)
    wb = np.block([[wr, -wi], [wi, wr]]).astype(np.float32)
    return tw, twr, twi, _split_bf16(wf), _split_bf16(wb)


def _split_bf16(w):
    w = jnp.asarray(w, F32)
    hi = w.astype(BF16)
    return jnp.stack([hi, (w - hi.astype(F32)).astype(BF16)])


def _slab_fft(zr, zi, tw_ref, n1, ct, inverse, half_input=False, half_output=False):
    nst = int(math.log2(n1))
    order = range(nst - 1, -1, -1) if inverse else range(nst)
    for s in order:
        half = n1 >> (s + 1)
        lh = int(math.log2(half))
        first = s == 0

        def body(idx, carry, s=s, half=half, lh=lh, first=first):
            blk = idx >> lh
            j = idx & (half - 1)
            r0 = pl.multiple_of(((blk << (lh + 1)) + j) * ct, ct)
            r1 = pl.multiple_of(r0 + half * ct, ct)
            wr = tw_ref[0, s, j]
            wi = tw_ref[1, s, j]
            ar = zr[pl.ds(r0, ct), :]
            ai = zi[pl.ds(r0, ct), :]
            if not inverse:
                if first and half_input:
                    zr[pl.ds(r1, ct), :] = ar * wr - ai * wi
                    zi[pl.ds(r1, ct), :] = ar * wi + ai * wr
                else:
                    br = zr[pl.ds(r1, ct), :]
                    bi = zi[pl.ds(r1, ct), :]
                    dr = ar - br
                    di = ai - bi
                    zr[pl.ds(r0, ct), :] = ar + br
                    zi[pl.ds(r0, ct), :] = ai + bi
                    zr[pl.ds(r1, ct), :] = dr * wr - di * wi
                    zi[pl.ds(r1, ct), :] = dr * wi + di * wr
            else:
                br = zr[pl.ds(r1, ct), :]
                bi = zi[pl.ds(r1, ct), :]
                tr = br * wr + bi * wi
                ti = bi * wr - br * wi
                zr[pl.ds(r0, ct), :] = ar + tr
                zi[pl.ds(r0, ct), :] = ai + ti
                if not (first and half_output):
                    zr[pl.ds(r1, ct), :] = ar - tr
                    zi[pl.ds(r1, ct), :] = ai - ti
            return carry

        lax.fori_loop(0, n1 // 2, body, 0, unroll=2)


def _chunk_dft(xr, xi, w_ref):
    x = jnp.concatenate([xr, xi], axis=1)
    x_hi = x.astype(BF16)
    x_lo = (x - x_hi.astype(F32)).astype(BF16)
    y = _dot(x_hi, w_ref[0]) + (_dot(x_lo, w_ref[0]) + _dot(x_hi, w_ref[1]))
    return y[:, :FFT_CHUNK], y[:, FFT_CHUNK:]


def _fft_groups(n1, ct):
    g = min(8, n1)
    return n1 // g, g * ct


def _hy_spec_kernel(n1, ct, f_ref, tw_ref, twr_ref, twi_ref, wf_ref, fr_ref, fi_ref, zr, zi):
    for s in range(n1):
        zr[s * ct:(s + 1) * ct, :] = f_ref[:, s * FFT_CHUNK:(s + 1) * FFT_CHUNK]
    zi[...] = jnp.zeros(zi.shape, F32)
    _slab_fft(zr, zi, tw_ref, n1, ct, inverse=False)
    ngroups, gr = _fft_groups(n1, ct)

    def body(g, carry):
        r0 = pl.multiple_of(g * gr, gr)
        a = zr[pl.ds(r0, gr), :]
        b = zi[pl.ds(r0, gr), :]
        tr = twr_ref[pl.ds(r0, gr), :]
        ti = twi_ref[pl.ds(r0, gr), :]
        yr, yi = _chunk_dft(a * tr - b * ti, a * ti + b * tr, wf_ref)
        fr_ref[pl.ds(r0, gr), :] = yr
        fi_ref[pl.ds(r0, gr), :] = yi
        return carry

    lax.fori_loop(0, ngroups, body, 0, unroll=min(2, ngroups))


def _hy_spec_call(filt_t):
    c, n = filt_t.shape
    n1 = n // FFT_CHUNK
    ct = HY_CT
    tw, twr, twi, wf, _ = _fft_consts(n1, ct)
    const2 = lambda i: (0, 0)
    out = jax.ShapeDtypeStruct((c // ct, n1 * ct, FFT_CHUNK), F32)
    ospec = pl.BlockSpec((None, n1 * ct, FFT_CHUNK), lambda i: (i, 0, 0))
    return pl.pallas_call(
        functools.partial(_hy_spec_kernel, n1, ct),
        grid=(c // ct,),
        in_specs=[pl.BlockSpec((ct, n), lambda i: (i, 0)),
                  pl.BlockSpec(memory_space=pltpu.SMEM),
                  pl.BlockSpec(twr.shape, const2), pl.BlockSpec(twi.shape, const2),
                  pl.BlockSpec(wf.shape, lambda i: (0, 0, 0))],
        out_specs=[ospec, ospec],
        out_shape=[out, out],
        scratch_shapes=[pltpu.VMEM((n1 * ct, FFT_CHUNK), F32), pltpu.VMEM((n1 * ct, FFT_CHUNK), F32)],
        name="hyena_filter_spectrum",
        compiler_params=_cparams(("parallel",)),
    )(filt_t, jnp.asarray(tw), jnp.asarray(twr), jnp.asarray(twi), jnp.asarray(wf))


def _hy_conv_kernel(n1, ct, length, v_ref, x1_ref, x2_ref, cwv_ref, cw1_ref, cw2_ref, skip_ref,
                    fr_ref, fi_ref, tw_ref, twr_ref, twi_ref, wf_ref, wb_ref, o_ref,
                    zr, zi, u_scr, g_scr):
    lane = lax.broadcasted_iota(jnp.int32, (1, length), 1)

    sub = 8

    def short_conv(x, cw):
        xm = jnp.where(lane >= 1, pltpu.roll(x, 1, 1), 0.0)
        xp = jnp.where(lane <= length - 2, pltpu.roll(x, length - 1, 1), 0.0)
        return xm * cw[:, 0:1] + x * cw[:, 1:2] + xp * cw[:, 2:3] + cw[:, 3:4]

    def conv_rows(r, carry):
        rows = pl.ds(pl.multiple_of(r * sub, sub), sub)
        cwv, cw1, cw2 = cwv_ref[rows, :], cw1_ref[rows, :], cw2_ref[rows, :]
        for a in range(2):
            u_scr[a, rows, :] = short_conv(v_ref[a, rows, :], cwv) * short_conv(x1_ref[a, rows, :], cw1)
            g_scr[a, rows, :] = short_conv(x2_ref[a, rows, :], cw2)
        return carry

    lax.fori_loop(0, ct // sub, conv_rows, 0)
    nz = length // FFT_CHUNK
    for s in range(nz):
        zr[s * ct:(s + 1) * ct, :] = u_scr[0, :, s * FFT_CHUNK:(s + 1) * FFT_CHUNK]
        zi[s * ct:(s + 1) * ct, :] = u_scr[1, :, s * FFT_CHUNK:(s + 1) * FFT_CHUNK]
    _slab_fft(zr, zi, tw_ref, n1, ct, inverse=False, half_input=True)
    ngroups, gr = _fft_groups(n1, ct)

    def body(g, carry):
        r0 = pl.multiple_of(g * gr, gr)
        a = zr[pl.ds(r0, gr), :]
        b = zi[pl.ds(r0, gr), :]
        tr = twr_ref[pl.ds(r0, gr), :]
        ti = twi_ref[pl.ds(r0, gr), :]
        yr, yi = _chunk_dft(a * tr - b * ti, a * ti + b * tr, wf_ref)
        fr = fr_ref[pl.ds(r0, gr), :]
        fi = fi_ref[pl.ds(r0, gr), :]
        qr, qi = _chunk_dft(yr * fr - yi * fi, yr * fi + yi * fr, wb_ref)
        zr[pl.ds(r0, gr), :] = qr * tr + qi * ti
        zi[pl.ds(r0, gr), :] = qi * tr - qr * ti
        return carry

    lax.fori_loop(0, ngroups, body, 0, unroll=min(2, ngroups))
    _slab_fft(zr, zi, tw_ref, n1, ct, inverse=True, half_output=True)
    inv_n = 1.0 / (n1 * FFT_CHUNK)
    skip = skip_ref[...]
    for s in range(nz):
        sl = slice(s * FFT_CHUNK, (s + 1) * FFT_CHUNK)
        for a, z in ((0, zr), (1, zi)):
            y = z[s * ct:(s + 1) * ct, :] * inv_n
            u = u_scr[a, :, sl]
            o_ref[a, :, sl] = (g_scr[a, :, sl] * (y + u * skip)).astype(o_ref.dtype)


def _hy_conv_call(hy_t, spec_r, spec_i, cw, skip, length, lane_block, prev=None):
    _, bh, c3, t_all = hy_t.shape
    c = c3 // 3
    ct = HY_CT
    nct = c // ct
    n1 = 2 * length // FFT_CHUNK
    tw, twr, twi, wf, wb = _fft_consts(n1, ct)
    const2 = lambda p, i: (0, 0)
    xin = lambda k: pl.BlockSpec((2, None, ct, length), lambda p, i, k=k: (0, p, k * nct + i, lane_block))
    cwin = lambda k: pl.BlockSpec((ct, 4), lambda p, i, k=k: (k * nct + i, 0))
    fin = pl.BlockSpec((None, n1 * ct, FFT_CHUNK), lambda p, i: (i, 0, 0))
    in_specs = [xin(0), xin(1), xin(2), cwin(0), cwin(1), cwin(2),
                pl.BlockSpec((ct, 1), lambda p, i: (i, 0)),
                fin, fin,
                pl.BlockSpec(memory_space=pltpu.SMEM),
                pl.BlockSpec(twr.shape, const2), pl.BlockSpec(twi.shape, const2),
                pl.BlockSpec(wf.shape, lambda p, i: (0, 0, 0)),
                pl.BlockSpec(wb.shape, lambda p, i: (0, 0, 0))]
    args = [hy_t, hy_t, hy_t, cw, cw, cw, skip, spec_r, spec_i, jnp.asarray(tw), jnp.asarray(twr),
            jnp.asarray(twi), jnp.asarray(wf), jnp.asarray(wb)]
    kern = functools.partial(_hy_conv_kernel, n1, ct, length)
    aliases = {}
    if prev is not None:
        in_specs.append(pl.BlockSpec(memory_space=pl.ANY))
        args.append(prev)
        aliases = {len(args) - 1: 0}
        kern = lambda *refs, kern=kern: kern(*refs[:14], *refs[15:])
    return pl.pallas_call(
        kern,
        grid=(bh, nct),
        in_specs=in_specs,
        out_specs=pl.BlockSpec((2, None, ct, length), lambda p, i: (0, p, i, lane_block)),
        out_shape=jax.ShapeDtypeStruct((2, bh, c, t_all), BF16),
        scratch_shapes=[pltpu.VMEM((n1 * ct, FFT_CHUNK), F32), pltpu.VMEM((n1 * ct, FFT_CHUNK), F32),
                        pltpu.VMEM((2, ct, length), F32), pltpu.VMEM((2, ct, length), F32)],
        input_output_aliases=aliases,
        name="hyena_conv_%d" % length,
        compiler_params=_cparams(("parallel", "parallel")),
    )(*args)


def _outproj_kernel(n_lat_tiles, ctx_row, x_ref, hy_ref, ga_ref, na_ref, w_ref, mod_ref, o_ref):
    j = pl.program_id(1)
    nbat, tm, d = x_ref.shape
    mix = []
    for s in range(nbat):
        cm = jnp.concatenate([hy_ref[s], ga_ref[s], na_ref[s]], axis=0)
        mix.append(cm.astype(F32).T.astype(BF16))
    y = _dot(jnp.concatenate(mix, axis=0), w_ref[...])
    for s in range(nbat):
        row = jnp.where(j < n_lat_tiles, pl.program_id(0) * nbat + s, ctx_row)
        o_ref[s] = x_ref[s] + _mod_row(mod_ref, row, 2, d) * y[s * tm:(s + 1) * tm]


def _outproj_call(xa, hyo_t, gqa, na, w_out, mod_l, s_len, n_tok):
    bsz, _, d = xa.shape
    tm = TOK_TILE
    nbat = _batch_group(bsz)
    const = lambda b, j: (0, 0)
    tok = lambda n: pl.BlockSpec((nbat, tm, n), lambda b, j: (b, j, 0))
    chan = lambda n: pl.BlockSpec((nbat, n, tm), lambda b, j: (b, 0, j))
    return pl.pallas_call(
        functools.partial(_outproj_kernel, s_len // tm, bsz),
        grid=(bsz // nbat, n_tok // tm),
        in_specs=[tok(d), chan(hyo_t.shape[1]), chan(gqa.shape[1]), chan(na.shape[1]),
                  pl.BlockSpec(w_out.shape, const), pl.BlockSpec(mod_l.shape, const)],
        out_specs=tok(d),
        out_shape=jax.ShapeDtypeStruct((bsz, n_tok, d), F32),
        name="out_proj",
        compiler_params=_cparams(("parallel", "parallel")),
    )(xa, hyo_t, gqa, na, w_out, mod_l)


def _ffn_kernel(tiles_per_batch, n_lat_tiles, ctx_row, th, x_ref, mod_ref, g_ref, wi_ref, wo_ref,
                gf_ref, o_ref, h_scr, act_scr):
    i = pl.program_id(0)
    d = x_ref.shape[-1]
    f = wo_ref.shape[0]
    groups = x_ref.shape[0] // TOK_TILE

    def mod_row(s):
        gi = i * groups + s
        jj = gi % tiles_per_batch
        return jnp.where(jj < n_lat_tiles, gi // tiles_per_batch, ctx_row)

    for s in range(groups):
        row = mod_row(s)
        sl = slice(s * TOK_TILE, (s + 1) * TOK_TILE)
        h = _norm_mod(x_ref[sl, :], g_ref[...], _mod_row(mod_ref, row, 3, d), _mod_row(mod_ref, row, 4, d))
        h_scr[sl, :] = h.astype(BF16)
    h = h_scr[...]
    for j in range(f // th):
        gate = _dot(h, wi_ref[:, j * th:(j + 1) * th])
        up = _dot(h, wi_ref[:, f + j * th:f + (j + 1) * th])
        act_scr[:, j * th:(j + 1) * th] = (gate * jax.nn.sigmoid(gate) * up).astype(BF16)
    y = _dot(act_scr[...], wo_ref[...])
    for s in range(groups):
        row = mod_row(s)
        sl = slice(s * TOK_TILE, (s + 1) * TOK_TILE)
        out = x_ref[sl, :] + _mod_row(mod_ref, row, 5, d) * y[sl, :]
        if gf_ref is not None:
            ms = jnp.mean(out * out, axis=-1, keepdims=True)
            out = out * lax.rsqrt(ms + NORM_EPS) * gf_ref[...]
        o_ref[sl, :] = out


def _ffn_call(xa, mod_l, g, w_in, w_out, s_len, g_final=None):
    bsz, t_all, d = xa.shape
    f = w_out.shape[0]
    r = bsz * t_all
    tm = 1024 if r % 1024 == 0 else TOK_TILE
    th = 256
    const = lambda i: (0, 0)
    resident = lambda a: pl.BlockSpec(a.shape, const, pipeline_mode=pl.Buffered(1))
    in_specs = [pl.BlockSpec((tm, d), lambda i: (i, 0)),
                pl.BlockSpec(mod_l.shape, const),
                pl.BlockSpec((1, d), const),
                resident(w_in), resident(w_out)]
    args = [xa.reshape(r, d), mod_l, g, w_in, w_out]
    kern = functools.partial(_ffn_kernel, t_all // TOK_TILE, s_len // TOK_TILE, bsz, th)
    if g_final is None:
        kern = functools.partial(_ffn_no_final, kern)
    else:
        in_specs.append(pl.BlockSpec((1, d), const))
        args.append(g_final)
    out = pl.pallas_call(
        kern,
        grid=(r // tm,),
        in_specs=in_specs,
        out_specs=pl.BlockSpec((tm, d), lambda i: (i, 0)),
        out_shape=jax.ShapeDtypeStruct((r, d), F32),
        scratch_shapes=[pltpu.VMEM((tm, d), BF16), pltpu.VMEM((tm, f), BF16)],
        name="swiglu_ffn",
        compiler_params=_cparams(("parallel",)),
    )(*args)
    return out.reshape(bsz, t_all, d)


def _ffn_no_final(kern, x_ref, mod_ref, g_ref, wi_ref, wo_ref, o_ref, h_scr, act_scr):
    kern(x_ref, mod_ref, g_ref, wi_ref, wo_ref, None, o_ref, h_scr, act_scr)


def _rope_tables(s_len, lc):
    pos = np.arange(s_len)
    row = (pos // GRID_W).astype(np.float64)
    col = (pos % GRID_W).astype(np.float64)
    n_f = HEAD_DIM // 4
    inv = (ROPE_THETA ** (-np.arange(n_f, dtype=np.float32) / n_f)).astype(np.float64)
    ang = np.concatenate([row[:, None] * inv, col[:, None] * inv], axis=-1).astype(np.float32).astype(np.float64)
    lane = np.arange(LANES)
    sign = np.where((lane % HEAD_DIM) < HEAD_DIM // 2, -1.0, 1.0)
    cos = np.cos(ang)[:, lane % (HEAD_DIM // 2)]
    sin = np.sin(ang)[:, lane % (HEAD_DIM // 2)] * sign[None, :]
    cos = np.concatenate([cos, np.ones((lc, LANES))], axis=0).astype(np.float32)
    sin = np.concatenate([sin, np.zeros((lc, LANES))], axis=0).astype(np.float32)
    cosc = np.concatenate([np.cos(ang).T, np.ones((HEAD_DIM // 2, lc))], axis=1).astype(np.float32)
    sinc = np.concatenate([np.sin(ang).T, np.zeros((HEAD_DIM // 2, lc))], axis=1).astype(np.float32)
    return jnp.asarray(cosc), jnp.asarray(sinc), jnp.asarray(cos), jnp.asarray(sin)


def kernel(x, c, ctx, c_ctx, w_mod, b_mod, g_mix, g_ffn, w_in, w_out, hy_conv_w, hy_conv_b,
           hy_f_w1, hy_f_b1, hy_f_w2, hy_f_b2, hy_f_w3, hy_f_b3, hy_f_wout, hy_f_freq, hy_skip,
           qk_g_q, qk_g_k, na_rpb, w_ffn_in, w_ffn_out, g_final):
    bsz, s_len, d = x.shape
    lc = ctx.shape[1]
    depth = w_mod.shape[0]
    t_all = s_len + lc
    hyw = hy_skip.shape[1]
    hy_cols = 3 * hyw
    assert bsz % 2 == 0 and s_len % TOK_TILE == 0 and lc == TOK_TILE and s_len % lc == 0

    nb = -(-(bsz + 1) // 8) * 8
    cs = jnp.zeros((nb, d), F32).at[:bsz].set(c).at[bsz].set(c_ctx)
    mod = _mod_call(cs, w_mod, b_mod)

    rope = _rope_tables(s_len, lc)
    tables = _na_bias_tables(na_rpb, s_len // GRID_W)
    q0 = hy_cols
    k0 = q0 + GQA_HEADS * HEAD_DIM
    v0 = k0 + GQA_KV_HEADS * HEAD_DIM
    nq0 = v0 + GQA_KV_HEADS * HEAD_DIM
    nk0 = nq0 + NA_HEADS * HEAD_DIM
    nv0 = nk0 + NA_HEADS * HEAD_DIM
    w_cm = jnp.concatenate([w_in[:, :, :k0], w_in[:, :, v0:nk0], w_in[:, :, nv0:]], axis=2)
    w_cm = jnp.swapaxes(w_cm, 1, 2).astype(BF16)
    w_tm = jnp.concatenate([w_in[:, :, k0:v0], w_in[:, :, nk0:nv0]], axis=2).astype(BF16)
    w_out_b = w_out.astype(BF16)
    w_fi = w_ffn_in.astype(BF16)
    w_fo = w_ffn_out.astype(BF16)
    cw = jnp.concatenate([jnp.swapaxes(hy_conv_w, 1, 2), hy_conv_b[:, :, None]], axis=2)
    per = LANES // HEAD_DIM

    xa = jnp.concatenate([x, ctx], axis=1)
    for l in range(depth):
        last = l == depth - 1
        gq_b = jnp.broadcast_to(qk_g_q[l][:, None], (HEAD_DIM, LANES))
        gk2 = jnp.tile(qk_g_k[l], per)[None]
        hy_t, q, k, v, nq, nk, nv = _inproj_call(xa, mod[l], g_mix[l][None], w_cm[l], w_tm[l],
                                                  gq_b, gk2, rope, hy_cols, s_len)
        n_tok = s_len if last else t_all
        gqa = _gqa_call(q, k, v, s_len, n_tok)
        na = _na_call(nq, nk, nv, tables[l], s_len, n_tok)

        fargs = (hy_f_w1[l], hy_f_b1[l], hy_f_w2[l], hy_f_b2[l], hy_f_w3[l], hy_f_b3[l],
                 hy_f_wout[l], hy_f_freq[l])
        hy_p = hy_t.reshape(2, bsz // 2, hy_cols, t_all)
        skip = hy_skip[l][:, None]
        fr, fi = _hy_spec_call(_hy_filter_call(s_len, *fargs))
        hyo = _hy_conv_call(hy_p, fr, fi, cw[l], skip, s_len, 0)
        if not last:
            frc, fic = _hy_spec_call(_hy_filter_call(lc, *fargs))
            hyo = _hy_conv_call(hy_p, frc, fic, cw[l], skip, lc, s_len // lc, prev=hyo)
        hyo = hyo.reshape(bsz, hyw, t_all)

        xa = _outproj_call(xa, hyo, gqa, na, w_out_b[l], mod[l], s_len, n_tok)
        xa = _ffn_call(xa, mod[l], g_ffn[l][None], w_fi[l], w_fo[l], s_len,
                       g_final[None] if last else None)
    return xa
```

```python
import functools
import math

import numpy as np
import jax
import jax.numpy as jnp
from jax import lax
from jax.experimental import pallas as pl
from jax.experimental.pallas import tpu as pltpu

F32 = jnp.float32
BF16 = jnp.bfloat16
HIGHEST = lax.Precision.HIGHEST

GRID_W = 64
HEAD_DIM = 64
GQA_HEADS = 8
GQA_KV_HEADS = 2
GQA_GROUP = GQA_HEADS // GQA_KV_HEADS
NA_HEADS = 4
NA_KH = 8
NA_KW = 16
ROPE_THETA = 10000.0
HY_EMB_DIM = 33
HY_DECAY_TARGET = 1e-2
HY_FAST_DECAY = 0.3
HY_SLOW_DECAY = 1.5
NORM_EPS = 1e-6
ATTN_SCALE = HEAD_DIM ** -0.5
NEG = -1e30
LOG2E = math.log2(math.e)

LANES = 128
TOK_TILE = 256
FFT_CHUNK = LANES
HY_CT = 64
NA_ROWS = TOK_TILE // GRID_W
NA_WIN = NA_ROWS + NA_KH
VMEM_LIMIT = 56 * 1024 * 1024


def _cparams(sem):
    return pltpu.CompilerParams(dimension_semantics=sem, vmem_limit_bytes=VMEM_LIMIT)


def _dot(a, b, **kw):
    return jnp.dot(a, b, preferred_element_type=F32, **kw)


def _dot_nt(a, b, **kw):
    return lax.dot_general(a, b, (((1,), (1,)), ((), ())), preferred_element_type=F32, **kw)


def _mod_kernel(cs_ref, w_ref, b_ref, o_ref):
    cs = cs_ref[...]
    s = cs * jax.nn.sigmoid(cs)
    o_ref[...] = _dot(s, w_ref[...], precision=HIGHEST) + b_ref[...]


def _mod_call(cs, w_mod, b_mod):
    depth, d, n = w_mod.shape
    nb = cs.shape[0]
    tn = 1024
    return pl.pallas_call(
        _mod_kernel,
        grid=(depth, n // tn),
        in_specs=[pl.BlockSpec((nb, d), lambda l, j: (0, 0)),
                  pl.BlockSpec((None, d, tn), lambda l, j: (l, 0, j)),
                  pl.BlockSpec((None, 1, tn), lambda l, j: (l, 0, j))],
        out_specs=pl.BlockSpec((None, nb, tn), lambda l, j: (l, 0, j)),
        out_shape=jax.ShapeDtypeStruct((depth, nb, n), F32),
        name="adaln_mod",
        compiler_params=_cparams(("parallel", "parallel")),
    )(cs, w_mod, b_mod.reshape(depth, 1, n))


def _mod_row(mod_ref, row, k, d):
    return mod_ref[pl.ds(row, 1), k * d:(k + 1) * d]


def _norm_mod(x, g, shift, scale):
    ms = jnp.mean(x * x, axis=-1, keepdims=True)
    y = x * lax.rsqrt(ms + NORM_EPS) * g
    return y * (1.0 + scale) + shift


def _batch_group(bsz):
    return 4 if bsz % 4 == 0 else 2


def _inproj_kernel(n_lat_tiles, ctx_row, hyc, x_ref, mod_ref, g_ref, wcm_ref, wtm_ref, gq_ref, gk_ref,
                   cosc_ref, sinc_ref, cos_ref, sin_ref,
                   hy_ref, q_ref, k_ref, v_ref, nq_ref, nk_ref, nv_ref, h_scr):
    j = pl.program_id(1)
    nbat, tm, d = x_ref.shape
    for s in range(nbat):
        row = jnp.where(j < n_lat_tiles, pl.program_id(0) * nbat + s, ctx_row)
        h = _norm_mod(x_ref[s], g_ref[...], _mod_row(mod_ref, row, 0, d), _mod_row(mod_ref, row, 1, d))
        h_scr[s * tm:(s + 1) * tm, :] = h.astype(BF16)
    h = h_scr[...]
    cm = _dot_nt(wcm_ref[...], h)
    tmj = _dot(h, wtm_ref[...])

    half = HEAD_DIM // 2
    reps = cm.shape[1] // LANES
    gq = jnp.concatenate([gq_ref[...]] * reps, axis=1)
    cosc = jnp.concatenate([cosc_ref[...]] * nbat, axis=1)
    sinc = jnp.concatenate([sinc_ref[...]] * nbat, axis=1)
    for hh in range(GQA_HEADS):
        x = cm[hyc + hh * HEAD_DIM:hyc + (hh + 1) * HEAD_DIM, :]
        ms = jnp.mean(x * x, axis=0, keepdims=True)
        xn = x * lax.rsqrt(ms + NORM_EPS) * gq
        x1, x2 = xn[:half], xn[half:]
        y = jnp.concatenate([x1 * cosc - x2 * sinc, x1 * sinc + x2 * cosc], axis=0)
        y = (y * (ATTN_SCALE * LOG2E)).astype(BF16)
        for s in range(nbat):
            q_ref[s, hh * HEAD_DIM:(hh + 1) * HEAD_DIM, :] = y[:, s * tm:(s + 1) * tm]
    voff = hyc + GQA_HEADS * HEAD_DIM
    nqoff = voff + GQA_KV_HEADS * HEAD_DIM
    nvoff = nqoff + NA_HEADS * HEAD_DIM
    for s in range(nbat):
        cols = slice(s * tm, (s + 1) * tm)
        hy_ref[s] = cm[:hyc, cols]
        v_ref[s] = cm[voff:nqoff, cols].astype(BF16)
        nq_ref[s] = (cm[nqoff:nvoff, cols] * (ATTN_SCALE * LOG2E)).astype(BF16)
        for e in range(NA_HEADS):
            nv_ref[s, e] = cm[nvoff + e * HEAD_DIM:nvoff + (e + 1) * HEAD_DIM, cols].astype(BF16)
        _inproj_token_major(tmj[s * tm:(s + 1) * tm], gk_ref, cos_ref, sin_ref, k_ref.at[s], nk_ref.at[s])


def _inproj_token_major(p, gk_ref, cos_ref, sin_ref, k_ref, nk_ref):
    ri = lax.broadcasted_iota(jnp.int32, (LANES, LANES), 0) // HEAD_DIM
    ci = lax.broadcasted_iota(jnp.int32, (LANES, LANES), 1) // HEAD_DIM
    avg = jnp.where(ri == ci, 1.0 / HEAD_DIM, 0.0).astype(BF16)
    lane = lax.broadcasted_iota(jnp.int32, (1, LANES), 1)
    first_half = (lane % HEAD_DIM) < (HEAD_DIM // 2)
    xc = p[:, :LANES]
    ms = _dot((xc * xc).astype(BF16), avg)
    xn = xc * lax.rsqrt(ms + NORM_EPS) * gk_ref[...]
    partner = jnp.where(first_half, pltpu.roll(xn, LANES - HEAD_DIM // 2, 1),
                        pltpu.roll(xn, HEAD_DIM // 2, 1))
    y = (xn * cos_ref[...] + partner * sin_ref[...]).astype(BF16)
    for e in range(GQA_KV_HEADS):
        k_ref[e] = y[:, e * HEAD_DIM:(e + 1) * HEAD_DIM]
    for e in range(NA_HEADS):
        nk_ref[e] = p[:, LANES + e * HEAD_DIM:LANES + (e + 1) * HEAD_DIM].astype(BF16)


def _inproj_call(xa, mod_l, g, w_cm, w_tm, gq_b, gk2, rope, hyc, s_len):
    bsz, t_all, d = xa.shape
    tm = TOK_TILE
    nbat = _batch_group(bsz)
    cosc, sinc, cos_t, sin_t = rope
    grid = (bsz // nbat, t_all // tm)
    const = lambda b, j: (0, 0)
    resident = lambda a: pl.BlockSpec(a.shape, const, pipeline_mode=pl.Buffered(1))
    heads = lambda n: pl.BlockSpec((nbat, n, tm, HEAD_DIM), lambda b, j: (b, 0, j, 0))
    hshape = lambda n: jax.ShapeDtypeStruct((bsz, n, t_all, HEAD_DIM), BF16)
    chan = lambda n: pl.BlockSpec((nbat, n, tm), lambda b, j: (b, 0, j))
    qc = GQA_HEADS * HEAD_DIM
    vc = GQA_KV_HEADS * HEAD_DIM
    nc = NA_HEADS * HEAD_DIM
    return pl.pallas_call(
        functools.partial(_inproj_kernel, s_len // tm, bsz, hyc),
        grid=grid,
        in_specs=[pl.BlockSpec((nbat, tm, d), lambda b, j: (b, j, 0)),
                  pl.BlockSpec(mod_l.shape, const),
                  pl.BlockSpec((1, d), const),
                  resident(w_cm), resident(w_tm),
                  pl.BlockSpec(gq_b.shape, const),
                  pl.BlockSpec((1, LANES), const),
                  pl.BlockSpec((HEAD_DIM // 2, tm), lambda b, j: (0, j)),
                  pl.BlockSpec((HEAD_DIM // 2, tm), lambda b, j: (0, j)),
                  pl.BlockSpec((tm, LANES), lambda b, j: (j, 0)),
                  pl.BlockSpec((tm, LANES), lambda b, j: (j, 0))],
        out_specs=[chan(hyc), chan(qc), heads(GQA_KV_HEADS), chan(vc),
                   chan(nc), heads(NA_HEADS),
                   pl.BlockSpec((nbat, NA_HEADS, None, HEAD_DIM, tm), lambda b, j: (b, 0, j, 0, 0))],
        out_shape=[jax.ShapeDtypeStruct((bsz, hyc, t_all), F32),
                   jax.ShapeDtypeStruct((bsz, qc, t_all), BF16),
                   hshape(GQA_KV_HEADS),
                   jax.ShapeDtypeStruct((bsz, vc, t_all), BF16),
                   jax.ShapeDtypeStruct((bsz, nc, t_all), BF16),
                   hshape(NA_HEADS),
                   jax.ShapeDtypeStruct((bsz, NA_HEADS, t_all // tm, HEAD_DIM, tm), BF16)],
        scratch_shapes=[pltpu.VMEM((nbat * tm, d), BF16)],
        name="in_proj",
        compiler_params=_cparams(("parallel", "parallel")),
    )(xa, mod_l, g, w_cm, w_tm, gq_b, gk2, cosc, sinc, cos_t, sin_t)


def _gqa_kernel(s_len, lc, tk, q_ref, k_ref, vt_ref, o_ref, s_scr):
    i = pl.program_id(2)
    tm = q_ref.shape[1]
    sub = 8

    def finish(h, l, acc):
        o_ref[h * HEAD_DIM:(h + 1) * HEAD_DIM, :] = (acc / jnp.sum(l, axis=0, keepdims=True)).astype(o_ref.dtype)

    def scores(h, q, r0, nrow, m):
        s = _dot(k_ref[r0:r0 + nrow, :], q)
        s_scr[h, r0:r0 + nrow, :] = s
        return jnp.maximum(m, jnp.max(s.reshape(nrow // sub, sub, tm), axis=0))

    def apply(h, r0, nrow, m, l, acc):
        p = jnp.exp2(s_scr[h, r0:r0 + nrow, :].reshape(nrow // sub, sub, tm) - m[None])
        l = l + jnp.sum(p, axis=0)
        acc = acc + _dot(vt_ref[:, r0:r0 + nrow], p.reshape(nrow, tm).astype(BF16))
        return l, acc

    def two_pass(chunks):
        maxes = [None] * GQA_GROUP
        for stage in range(GQA_GROUP + 1):
            h1, h2 = stage, stage - 1
            if h1 < GQA_GROUP:
                q1 = q_ref[h1 * HEAD_DIM:(h1 + 1) * HEAD_DIM, :]
                m1 = jnp.full((sub, tm), NEG, F32)
            if h2 >= 0:
                m2 = jnp.broadcast_to(jnp.max(maxes[h2], axis=0, keepdims=True), (sub, tm))
                l = jnp.zeros((sub, tm), F32)
                acc = jnp.zeros((HEAD_DIM, tm), F32)
            for r0, nrow in chunks:
                if h1 < GQA_GROUP:
                    m1 = scores(h1, q1, r0, nrow, m1)
                if h2 >= 0:
                    l, acc = apply(h2, r0, nrow, m2, l, acc)
            if h1 < GQA_GROUP:
                maxes[h1] = m1
            if h2 >= 0:
                finish(h2, l, acc)

    ctx_chunk = (s_len, lc)

    @pl.when(i * tm < s_len)
    def _():
        two_pass([(c * tk, tk) for c in range(s_len // tk)] + [ctx_chunk])

    @pl.when(i * tm >= s_len)
    def _():
        two_pass([ctx_chunk])


def _gqa_call(qt, k, vt, s_len, n_query):
    bsz, qc, t_all = qt.shape
    hd = HEAD_DIM
    tm = TOK_TILE
    tk = 4 * TOK_TILE
    lc = t_all - s_len
    return pl.pallas_call(
        functools.partial(_gqa_kernel, s_len, lc, tk),
        grid=(bsz, GQA_KV_HEADS, n_query // tm),
        in_specs=[pl.BlockSpec((None, GQA_GROUP * hd, tm), lambda b, g, i: (b, g, i)),
                  pl.BlockSpec((None, None, t_all, hd), lambda b, g, i: (b, g, 0, 0)),
                  pl.BlockSpec((None, hd, t_all), lambda b, g, i: (b, g, 0))],
        out_specs=pl.BlockSpec((None, GQA_GROUP * hd, tm), lambda b, g, i: (b, g, i)),
        out_shape=jax.ShapeDtypeStruct((bsz, qc, t_all), BF16),
        scratch_shapes=[pltpu.VMEM((GQA_GROUP, t_all, tm), F32)],
        name="gqa_attention",
        compiler_params=_cparams(("parallel", "parallel", "parallel")),
    )(qt, k, vt)


def _na_bias_consts(rows):
    assert rows >= NA_WIN + NA_ROWS
    nrow, ncol = 2 * NA_KH - 1, 2 * NA_KW - 1
    rsel = np.zeros((3, NA_ROWS, NA_WIN, nrow), np.float32)
    mrow = np.zeros((3, NA_ROWS, NA_WIN), bool)
    for kind, r0 in enumerate((0, NA_ROWS, rows - NA_ROWS)):
        kr0 = int(np.clip(r0 - NA_KH // 2, 0, rows - NA_WIN))
        for qr in range(NA_ROWS):
            r = r0 + qr
            rs = int(np.clip(r - NA_KH // 2, 0, rows - NA_KH))
            for kr in range(NA_WIN):
                ka = kr0 + kr
                if rs <= ka < rs + NA_KH:
                    mrow[kind, qr, kr] = True
                    rsel[kind, qr, kr, ka - r + NA_KH - 1] = 1.0
    csel = np.zeros((GRID_W, GRID_W, ncol), np.float32)
    mcol = np.zeros((GRID_W, GRID_W), bool)
    for w in range(GRID_W):
        cs = int(np.clip(w - NA_KW // 2, 0, GRID_W - NA_KW))
        for kc in range(cs, cs + NA_KW):
            mcol[w, kc] = True
            csel[w, kc, kc - w + NA_KW - 1] = 1.0
    mask = mrow[:, :, None, :, None] & mcol[None, None, :, None, :]
    return rsel, csel, mask.reshape(3, TOK_TILE, NA_WIN * GRID_W)


def _na_bias_tables(na_rpb, rows):
    rsel, csel, mask = _na_bias_consts(rows)
    depth, heads = na_rpb.shape[:2]
    a = jnp.einsum("kqra,lhab->lkhqrb", jnp.asarray(rsel), na_rpb.astype(F32) * LOG2E, precision=HIGHEST)
    t = jnp.einsum("lkhqrb,wcb->lkhrcqw", a, jnp.asarray(csel), precision=HIGHEST)
    t = t.reshape(depth, 3, heads, NA_WIN * GRID_W, TOK_TILE)
    mask_t = np.swapaxes(mask, 1, 2)
    return jnp.where(jnp.asarray(mask_t)[None, :, None], t, NEG)


def _na_kernel(s_len, lc, rows, q_ref, k_ref, vt_ref, tab_ref, o_ref, s_scr):
    i = pl.program_id(1)
    tm = q_ref.shape[1]
    nblk = s_len // tm
    sub = 8

    def scores(h, part, off, m):
        r0, _, nrow, bias = part
        s = _dot(k_ref[h, pl.ds(r0, nrow), :], q_ref[h * HEAD_DIM:(h + 1) * HEAD_DIM, :])
        if bias is not None:
            s = s + bias[h]
        s_scr[h, off:off + nrow, :] = s
        return jnp.maximum(m, jnp.max(s.reshape(nrow // sub, sub, tm), axis=0))

    def apply(h, part, off, m, l, acc):
        _, c0, nrow, _ = part
        p = jnp.exp2(s_scr[h, off:off + nrow, :].reshape(nrow // sub, sub, tm) - m[None])
        l = l + jnp.sum(p, axis=0)
        vt = jnp.concatenate([vt_ref[h, c0 + c] for c in range(nrow // tm)], axis=1)
        return l, acc + _dot(vt, p.reshape(nrow, tm).astype(BF16))

    def run(parts):
        offs = [sum(p[2] for p in parts[:n]) for n in range(len(parts))]
        maxes = [None] * NA_HEADS
        for stage in range(NA_HEADS + 1):
            h1, h2 = stage, stage - 1
            if h1 < NA_HEADS:
                m1 = jnp.full((sub, tm), NEG, F32)
            if h2 >= 0:
                m2 = jnp.broadcast_to(jnp.max(maxes[h2], axis=0, keepdims=True), (sub, tm))
                l = jnp.zeros((sub, tm), F32)
                acc = jnp.zeros((HEAD_DIM, tm), F32)
            for part, off in zip(parts, offs):
                if h1 < NA_HEADS:
                    m1 = scores(h1, part, off, m1)
                if h2 >= 0:
                    l, acc = apply(h2, part, off, m2, l, acc)
            if h1 < NA_HEADS:
                maxes[h1] = m1
            if h2 >= 0:
                o = acc / jnp.sum(l, axis=0, keepdims=True)
                o_ref[h2 * HEAD_DIM:(h2 + 1) * HEAD_DIM, :] = o.astype(o_ref.dtype)

    ctx_part = (s_len, s_len // tm, lc, None)

    @pl.when(i < nblk)
    def _():
        kind = jnp.where(i == 0, 0, jnp.where(i == nblk - 1, 2, 1))
        c0 = jnp.clip(i - (NA_KH // 2) // NA_ROWS, 0, (rows - NA_WIN) // NA_ROWS)
        ks = pl.multiple_of(c0 * tm, tm)
        run([(ks, c0, NA_WIN * GRID_W, tab_ref.at[kind]), ctx_part])

    @pl.when(i >= nblk)
    def _():
        run([ctx_part])


def _na_call(nqt, nk, nvt, table, s_len, n_query):
    bsz, heads, t_all, hd = nk.shape
    tm = TOK_TILE
    rows = s_len // GRID_W
    assert NA_ROWS * GRID_W == tm and (NA_KH // 2) % NA_ROWS == 0 and (rows - NA_WIN) % NA_ROWS == 0
    return pl.pallas_call(
        functools.partial(_na_kernel, s_len, t_all - s_len, rows),
        grid=(bsz, n_query // tm),
        in_specs=[pl.BlockSpec((None, heads * hd, tm), lambda b, i: (b, 0, i)),
                  pl.BlockSpec((None, heads, t_all, hd), lambda b, i: (b, 0, 0, 0)),
                  pl.BlockSpec((None,) + nvt.shape[1:], lambda b, i: (b, 0, 0, 0, 0)),
                  pl.BlockSpec(table.shape, lambda b, i: (0, 0, 0, 0), pipeline_mode=pl.Buffered(1))],
        out_specs=pl.BlockSpec((None, heads * hd, tm), lambda b, i: (b, 0, i)),
        out_shape=jax.ShapeDtypeStruct((bsz, heads * hd, t_all), BF16),
        scratch_shapes=[pltpu.VMEM((heads, NA_WIN * GRID_W + t_all - s_len, tm), F32)],
        name="neighbourhood_attention",
        compiler_params=_cparams(("parallel", "parallel")),
    )(nqt, nk, nvt, table)


def _hy_filter_kernel(z_ref, t_ref, side_ref, w1_ref, b1_ref, w2_ref, b2_ref, w3_ref, b3_ref,
                      wo_ref, fr_ref, dl_ref, o_ref):
    om = fr_ref[...]
    h = jnp.sin(om * (_dot(z_ref[...], w1_ref[...], precision=HIGHEST) + b1_ref[...]))
    h = jnp.sin(om * (_dot(h, w2_ref[...], precision=HIGHEST) + b2_ref[...]))
    h = jnp.sin(om * (_dot(h, w3_ref[...], precision=HIGHEST) + b3_ref[...]))
    ht = _dot_nt(wo_ref[...], h, precision=HIGHEST)
    c = o_ref.shape[0]
    decay = jnp.exp(-t_ref[...] * jnp.abs(dl_ref[...]))
    side = side_ref[...]
    o_ref[...] = decay * (jnp.where(side > 0.5, ht[:c], 0.0) + jnp.where(side < -0.5, ht[c:], 0.0))


def _hy_filter_consts(length):
    n = np.arange(2 * length)
    pos = np.where(n < length, n, 2 * length - n)
    pos = np.where(n == length, 0, pos)
    side = np.where(n < length, 1.0, -1.0)
    side = np.where(n == length, 0.0, side)
    t = np.linspace(0.0, 1.0, length)[pos]
    bands = (HY_EMB_DIM - 1) // 2
    w = 2.0 * math.pi * pos / length
    fr = np.linspace(1e-4, bands - 1, bands)
    z = np.concatenate([t[:, None], np.cos(fr[None, :] * w[:, None]), -np.sin(fr[None, :] * w[:, None])], axis=-1)
    zp = np.zeros((2 * length, 64), np.float32)
    zp[:, :HY_EMB_DIM] = z
    return zp, t.astype(np.float32)[None, :], side.astype(np.float32)[None, :]


def _hy_filter_call(length, w1, b1, w2, b2, w3, b3, wout, freq):
    f = w2.shape[0]
    c = wout.shape[1] // 2
    zp, t, side = _hy_filter_consts(length)
    w1p = jnp.zeros((64, f), F32).at[:HY_EMB_DIM].set(w1)
    deltas = np.linspace(math.log(HY_DECAY_TARGET) / HY_SLOW_DECAY,
                         math.log(HY_DECAY_TARGET) / HY_FAST_DECAY, c).astype(np.float32)[:, None]
    args = (jnp.asarray(zp), jnp.asarray(t), jnp.asarray(side), w1p, b1[None], w2, b2[None], w3, b3[None],
            wout.T, freq[None], jnp.asarray(deltas))
    return pl.pallas_call(
        _hy_filter_kernel,
        in_specs=[pl.BlockSpec(a.shape, lambda: (0,) * a.ndim) for a in args],
        out_specs=pl.BlockSpec((c, 2 * length), lambda: (0, 0)),
        out_shape=jax.ShapeDtypeStruct((c, 2 * length), F32),
        grid=(),
        name="hyena_filter",
        compiler_params=pltpu.CompilerParams(vmem_limit_bytes=VMEM_LIMIT),
    )(*args)


def _bitrev(n):
    bits = int(math.log2(n))
    return np.array([int(format(i, "0%db" % bits)[::-1], 2) if bits else 0 for i in range(n)])


def _fft_consts(n1, ct):
    n = n1 * FFT_CHUNK
    stages = max(int(math.log2(n1)), 1)
    tw = np.zeros((2, stages, max(n1 // 2, 1)), np.float32)
    for s in range(int(math.log2(n1))):
        half = n1 >> (s + 1)
        ang = -2.0 * math.pi * np.arange(half) / (2 * half)
        tw[0, s, :half] = np.cos(ang)
        tw[1, s, :half] = np.sin(ang)
    k1 = _bitrev(n1)
    ang = -2.0 * math.pi * (np.arange(FFT_CHUNK)[None, :] * k1[:, None]) / n
    twr = np.repeat(np.cos(ang), ct, axis=0).astype(np.float32)
    twi = np.repeat(np.sin(ang), ct, axis=0).astype(np.float32)
    a = -2.0 * math.pi * np.outer(np.arange(FFT_CHUNK), np.arange(FFT_CHUNK)) / FFT_CHUNK
    wr, wi = np.cos(a), np.sin(a)
    wf = np.block([[wr, wi], [-wi, wr]]).astype(np.float32)
    wb = np.block([[wr, -wi], [wi, wr]]).astype(np.float32)
    return tw, twr, twi, _split_bf16(wf), _split_bf16(wb)


def _split_bf16(w):
    w = jnp.asarray(w, F32)
    hi = w.astype(BF16)
    return jnp.stack([hi, (w - hi.astype(F32)).astype(BF16)])


def _slab_fft(zr, zi, tw_ref, n1, ct, inverse, half_input=False, half_output=False):
    nst = int(math.log2(n1))
    order = range(nst - 1, -1, -1) if inverse else range(nst)
    for s in order:
        half = n1 >> (s + 1)
        lh = int(math.log2(half))
        first = s == 0

        def body(idx, carry, s=s, half=half, lh=lh, first=first):
            blk = idx >> lh
            j = idx & (half - 1)
            r0 = pl.multiple_of(((blk << (lh + 1)) + j) * ct, ct)
            r1 = pl.multiple_of(r0 + half * ct, ct)
            wr = tw_ref[0, s, j]
            wi = tw_ref[1, s, j]
            ar = zr[pl.ds(r0, ct), :]
            ai = zi[pl.ds(r0, ct), :]
            if not inverse:
                if first and half_input:
                    zr[pl.ds(r1, ct), :] = ar * wr - ai * wi
                    zi[pl.ds(r1, ct), :] = ar * wi + ai * wr
                else:
                    br = zr[pl.ds(r1, ct), :]
                    bi = zi[pl.ds(r1, ct), :]
                    dr = ar - br
                    di = ai - bi
                    zr[pl.ds(r0, ct), :] = ar + br
                    zi[pl.ds(r0, ct), :] = ai + bi
                    zr[pl.ds(r1, ct), :] = dr * wr - di * wi
                    zi[pl.ds(r1, ct), :] = dr * wi + di * wr
            else:
                br = zr[pl.ds(r1, ct), :]
                bi = zi[pl.ds(r1, ct), :]
                tr = br * wr + bi * wi
                ti = bi * wr - br * wi
                zr[pl.ds(r0, ct), :] = ar + tr
                zi[pl.ds(r0, ct), :] = ai + ti
                if not (first and half_output):
                    zr[pl.ds(r1, ct), :] = ar - tr
                    zi[pl.ds(r1, ct), :] = ai - ti
            return carry

        lax.fori_loop(0, n1 // 2, body, 0, unroll=2)


def _chunk_dft(xr, xi, w_ref, split):
    x = jnp.concatenate([xr, xi], axis=1)
    x_hi = x.astype(BF16)
    y = _dot(x_hi, w_ref[0])
    if split:
        x_lo = (x - x_hi.astype(F32)).astype(BF16)
        y = y + (_dot(x_lo, w_ref[0]) + _dot(x_hi, w_ref[1]))
    return y[:, :FFT_CHUNK], y[:, FFT_CHUNK:]


def _fft_groups(n1, ct):
    g = min(8, n1)
    return n1 // g, g * ct


def _hy_spec_kernel(n1, ct, f_ref, tw_ref, twr_ref, twi_ref, wf_ref, fr_ref, fi_ref, zr, zi):
    for s in range(n1):
        zr[s * ct:(s + 1) * ct, :] = f_ref[:, s * FFT_CHUNK:(s + 1) * FFT_CHUNK]
    zi[...] = jnp.zeros(zi.shape, F32)
    _slab_fft(zr, zi, tw_ref, n1, ct, inverse=False)
    ngroups, gr = _fft_groups(n1, ct)

    def body(g, carry):
        r0 = pl.multiple_of(g * gr, gr)
        a = zr[pl.ds(r0, gr), :]
        b = zi[pl.ds(r0, gr), :]
        tr = twr_ref[pl.ds(r0, gr), :]
        ti = twi_ref[pl.ds(r0, gr), :]
        yr, yi = _chunk_dft(a * tr - b * ti, a * ti + b * tr, wf_ref, split=True)
        fr_ref[pl.ds(r0, gr), :] = yr
        fi_ref[pl.ds(r0, gr), :] = yi
        return carry

    lax.fori_loop(0, ngroups, body, 0, unroll=min(2, ngroups))


def _hy_spec_call(filt_t):
    c, n = filt_t.shape
    n1 = n // FFT_CHUNK
    ct = HY_CT
    tw, twr, twi, wf, _ = _fft_consts(n1, ct)
    const2 = lambda i: (0, 0)
    out = jax.ShapeDtypeStruct((c // ct, n1 * ct, FFT_CHUNK), F32)
    ospec = pl.BlockSpec((None, n1 * ct, FFT_CHUNK), lambda i: (i, 0, 0))
    return pl.pallas_call(
        functools.partial(_hy_spec_kernel, n1, ct),
        grid=(c // ct,),
        in_specs=[pl.BlockSpec((ct, n), lambda i: (i, 0)),
                  pl.BlockSpec(memory_space=pltpu.SMEM),
                  pl.BlockSpec(twr.shape, const2), pl.BlockSpec(twi.shape, const2),
                  pl.BlockSpec(wf.shape, lambda i: (0, 0, 0))],
        out_specs=[ospec, ospec],
        out_shape=[out, out],
        scratch_shapes=[pltpu.VMEM((n1 * ct, FFT_CHUNK), F32), pltpu.VMEM((n1 * ct, FFT_CHUNK), F32)],
        name="hyena_filter_spectrum",
        compiler_params=_cparams(("parallel",)),
    )(filt_t, jnp.asarray(tw), jnp.asarray(twr), jnp.asarray(twi), jnp.asarray(wf))


def _hy_conv_kernel(n1, ct, length, v_ref, x1_ref, x2_ref, cwv_ref, cw1_ref, cw2_ref, skip_ref,
                    fr_ref, fi_ref, tw_ref, twr_ref, twi_ref, wf_ref, wb_ref, o_ref,
                    zr, zi, u_scr, g_scr):
    lane = lax.broadcasted_iota(jnp.int32, (1, length), 1)

    sub = 8

    def short_conv(x, cw):
        xm = jnp.where(lane >= 1, pltpu.roll(x, 1, 1), 0.0)
        xp = jnp.where(lane <= length - 2, pltpu.roll(x, length - 1, 1), 0.0)
        return xm * cw[:, 0:1] + x * cw[:, 1:2] + xp * cw[:, 2:3] + cw[:, 3:4]

    def conv_rows(r, carry):
        rows = pl.ds(pl.multiple_of(r * sub, sub), sub)
        cwv, cw1, cw2 = cwv_ref[rows, :], cw1_ref[rows, :], cw2_ref[rows, :]
        for a in range(2):
            u_scr[a, rows, :] = short_conv(v_ref[a, rows, :], cwv) * short_conv(x1_ref[a, rows, :], cw1)
            g_scr[a, rows, :] = short_conv(x2_ref[a, rows, :], cw2)
        return carry

    lax.fori_loop(0, ct // sub, conv_rows, 0)
    nz = length // FFT_CHUNK
    for s in range(nz):
        zr[s * ct:(s + 1) * ct, :] = u_scr[0, :, s * FFT_CHUNK:(s + 1) * FFT_CHUNK]
        zi[s * ct:(s + 1) * ct, :] = u_scr[1, :, s * FFT_CHUNK:(s + 1) * FFT_CHUNK]
    _slab_fft(zr, zi, tw_ref, n1, ct, inverse=False, half_input=True)
    ngroups, gr = _fft_groups(n1, ct)

    def body(g, carry):
        r0 = pl.multiple_of(g * gr, gr)
        a = zr[pl.ds(r0, gr), :]
        b = zi[pl.ds(r0, gr), :]
        tr = twr_ref[pl.ds(r0, gr), :]
        ti = twi_ref[pl.ds(r0, gr), :]
        yr, yi = _chunk_dft(a * tr - b * ti, a * ti + b * tr, wf_ref, split=False)
        fr = fr_ref[pl.ds(r0, gr), :]
        fi = fi_ref[pl.ds(r0, gr), :]
        qr, qi = _chunk_dft(yr * fr - yi * fi, yr * fi + yi * fr, wb_ref, split=False)
        zr[pl.ds(r0, gr), :] = qr * tr + qi * ti
        zi[pl.ds(r0, gr), :] = qi * tr - qr * ti
        return carry

    lax.fori_loop(0, ngroups, body, 0, unroll=min(2, ngroups))
    _slab_fft(zr, zi, tw_ref, n1, ct, inverse=True, half_output=True)
    inv_n = 1.0 / (n1 * FFT_CHUNK)
    skip = skip_ref[...]
    for s in range(nz):
        sl = slice(s * FFT_CHUNK, (s + 1) * FFT_CHUNK)
        for a, z in ((0, zr), (1, zi)):
            y = z[s * ct:(s + 1) * ct, :] * inv_n
            u = u_scr[a, :, sl]
            o_ref[a, :, sl] = (g_scr[a, :, sl] * (y + u * skip)).astype(o_ref.dtype)


def _hy_conv_call(hy_t, spec_r, spec_i, cw, skip, length, lane_block, prev=None):
    _, bh, c3, t_all = hy_t.shape
    c = c3 // 3
    ct = HY_CT
    nct = c // ct
    n1 = 2 * length // FFT_CHUNK
    tw, twr, twi, wf, wb = _fft_consts(n1, ct)
    const2 = lambda p, i: (0, 0)
    xin = lambda k: pl.BlockSpec((2, None, ct, length), lambda p, i, k=k: (0, p, k * nct + i, lane_block))
    cwin = lambda k: pl.BlockSpec((ct, 4), lambda p, i, k=k: (k * nct + i, 0))
    fin = pl.BlockSpec((None, n1 * ct, FFT_CHUNK), lambda p, i: (i, 0, 0))
    in_specs = [xin(0), xin(1), xin(2), cwin(0), cwin(1), cwin(2),
                pl.BlockSpec((ct, 1), lambda p, i: (i, 0)),
                fin, fin,
                pl.BlockSpec(memory_space=pltpu.SMEM),
                pl.BlockSpec(twr.shape, const2), pl.BlockSpec(twi.shape, const2),
                pl.BlockSpec(wf.shape, lambda p, i: (0, 0, 0)),
                pl.BlockSpec(wb.shape, lambda p, i: (0, 0, 0))]
    args = [hy_t, hy_t, hy_t, cw, cw, cw, skip, spec_r, spec_i, jnp.asarray(tw), jnp.asarray(twr),
            jnp.asarray(twi), jnp.asarray(wf), jnp.asarray(wb)]
    kern = functools.partial(_hy_conv_kernel, n1, ct, length)
    aliases = {}
    if prev is not None:
        in_specs.append(pl.BlockSpec(memory_space=pl.ANY))
        args.append(prev)
        aliases = {len(args) - 1: 0}
        kern = lambda *refs, kern=kern: kern(*refs[:14], *refs[15:])
    return pl.pallas_call(
        kern,
        grid=(bh, nct),
        in_specs=in_specs,
        out_specs=pl.BlockSpec((2, None, ct, length), lambda p, i: (0, p, i, lane_block)),
        out_shape=jax.ShapeDtypeStruct((2, bh, c, t_all), BF16),
        scratch_shapes=[pltpu.VMEM((n1 * ct, FFT_CHUNK), F32), pltpu.VMEM((n1 * ct, FFT_CHUNK), F32),
                        pltpu.VMEM((2, ct, length), F32), pltpu.VMEM((2, ct, length), F32)],
        input_output_aliases=aliases,
        name="hyena_conv_%d" % length,
        compiler_params=_cparams(("parallel", "parallel")),
    )(*args)


def _outproj_kernel(n_lat_tiles, ctx_row, x_ref, hy_ref, ga_ref, na_ref, w_ref, mod_ref, o_ref):
    j = pl.program_id(1)
    nbat, tm, d = x_ref.shape
    mix = []
    for s in range(nbat):
        cm = jnp.concatenate([hy_ref[s], ga_ref[s], na_ref[s]], axis=0)
        mix.append(cm.astype(F32).T.astype(BF16))
    y = _dot(jnp.concatenate(mix, axis=0), w_ref[...])
    for s in range(nbat):
        row = jnp.where(j < n_lat_tiles, pl.program_id(0) * nbat + s, ctx_row)
        o_ref[s] = x_ref[s] + _mod_row(mod_ref, row, 2, d) * y[s * tm:(s + 1) * tm]


def _outproj_call(xa, hyo_t, gqa, na, w_out, mod_l, s_len, n_tok):
    bsz, _, d = xa.shape
    tm = TOK_TILE
    nbat = _batch_group(bsz)
    const = lambda b, j: (0, 0)
    tok = lambda n: pl.BlockSpec((nbat, tm, n), lambda b, j: (b, j, 0))
    chan = lambda n: pl.BlockSpec((nbat, n, tm), lambda b, j: (b, 0, j))
    return pl.pallas_call(
        functools.partial(_outproj_kernel, s_len // tm, bsz),
        grid=(bsz // nbat, n_tok // tm),
        in_specs=[tok(d), chan(hyo_t.shape[1]), chan(gqa.shape[1]), chan(na.shape[1]),
                  pl.BlockSpec(w_out.shape, const), pl.BlockSpec(mod_l.shape, const)],
        out_specs=tok(d),
        out_shape=jax.ShapeDtypeStruct((bsz, n_tok, d), F32),
        name="out_proj",
        compiler_params=_cparams(("parallel", "parallel")),
    )(xa, hyo_t, gqa, na, w_out, mod_l)


def _ffn_kernel(tiles_per_batch, n_lat_tiles, ctx_row, th, x_ref, mod_ref, g_ref, wi_ref, wo_ref,
                gf_ref, o_ref, h_scr, act_scr):
    i = pl.program_id(0)
    d = x_ref.shape[-1]
    f = wo_ref.shape[0]
    groups = x_ref.shape[0] // TOK_TILE

    def mod_row(s):
        gi = i * groups + s
        jj = gi % tiles_per_batch
        return jnp.where(jj < n_lat_tiles, gi // tiles_per_batch, ctx_row)

    for s in range(groups):
        row = mod_row(s)
        sl = slice(s * TOK_TILE, (s + 1) * TOK_TILE)
        h = _norm_mod(x_ref[sl, :], g_ref[...], _mod_row(mod_ref, row, 3, d), _mod_row(mod_ref, row, 4, d))
        h_scr[sl, :] = h.astype(BF16)
    h = h_scr[...]
    for j in range(f // th):
        gate = _dot(h, wi_ref[:, j * th:(j + 1) * th])
        up = _dot(h, wi_ref[:, f + j * th:f + (j + 1) * th])
        act_scr[:, j * th:(j + 1) * th] = (gate * jax.nn.sigmoid(gate) * up).astype(BF16)
    y = _dot(act_scr[...], wo_ref[...])
    for s in range(groups):
        row = mod_row(s)
        sl = slice(s * TOK_TILE, (s + 1) * TOK_TILE)
        out = x_ref[sl, :] + _mod_row(mod_ref, row, 5, d) * y[sl, :]
        if gf_ref is not None:
            ms = jnp.mean(out * out, axis=-1, keepdims=True)
            out = out * lax.rsqrt(ms + NORM_EPS) * gf_ref[...]
        o_ref[sl, :] = out


def _ffn_call(xa, mod_l, g, w_in, w_out, s_len, g_final=None):
    bsz, t_all, d = xa.shape
    f = w_out.shape[0]
    r = bsz * t_all
    tm = 1024 if r % 1024 == 0 else TOK_TILE
    th = 256
    const = lambda i: (0, 0)
    resident = lambda a: pl.BlockSpec(a.shape, const, pipeline_mode=pl.Buffered(1))
    in_specs = [pl.BlockSpec((tm, d), lambda i: (i, 0)),
                pl.BlockSpec(mod_l.shape, const),
                pl.BlockSpec((1, d), const),
                resident(w_in), resident(w_out)]
    args = [xa.reshape(r, d), mod_l, g, w_in, w_out]
    kern = functools.partial(_ffn_kernel, t_all // TOK_TILE, s_len // TOK_TILE, bsz, th)
    if g_final is None:
        kern = functools.partial(_ffn_no_final, kern)
    else:
        in_specs.append(pl.BlockSpec((1, d), const))
        args.append(g_final)
    out = pl.pallas_call(
        kern,
        grid=(r // tm,),
        in_specs=in_specs,
        out_specs=pl.BlockSpec((tm, d), lambda i: (i, 0)),
        out_shape=jax.ShapeDtypeStruct((r, d), F32),
        scratch_shapes=[pltpu.VMEM((tm, d), BF16), pltpu.VMEM((tm, f), BF16)],
        name="swiglu_ffn",
        compiler_params=_cparams(("parallel",)),
    )(*args)
    return out.reshape(bsz, t_all, d)


def _ffn_no_final(kern, x_ref, mod_ref, g_ref, wi_ref, wo_ref, o_ref, h_scr, act_scr):
    kern(x_ref, mod_ref, g_ref, wi_ref, wo_ref, None, o_ref, h_scr, act_scr)


def _rope_tables(s_len, lc):
    pos = np.arange(s_len)
    row = (pos // GRID_W).astype(np.float64)
    col = (pos % GRID_W).astype(np.float64)
    n_f = HEAD_DIM // 4
    inv = (ROPE_THETA ** (-np.arange(n_f, dtype=np.float32) / n_f)).astype(np.float64)
    ang = np.concatenate([row[:, None] * inv, col[:, None] * inv], axis=-1).astype(np.float32).astype(np.float64)
    lane = np.arange(LANES)
    sign = np.where((lane % HEAD_DIM) < HEAD_DIM // 2, -1.0, 1.0)
    cos = np.cos(ang)[:, lane % (HEAD_DIM // 2)]
    sin = np.sin(ang)[:, lane % (HEAD_DIM // 2)] * sign[None, :]
    cos = np.concatenate([cos, np.ones((lc, LANES))], axis=0).astype(np.float32)
    sin = np.concatenate([sin, np.zeros((lc, LANES))], axis=0).astype(np.float32)
    cosc = np.concatenate([np.cos(ang).T, np.ones((HEAD_DIM // 2, lc))], axis=1).astype(np.float32)
    sinc = np.concatenate([np.sin(ang).T, np.zeros((HEAD_DIM // 2, lc))], axis=1).astype(np.float32)
    return jnp.asarray(cosc), jnp.asarray(sinc), jnp.asarray(cos), jnp.asarray(sin)


def kernel(x, c, ctx, c_ctx, w_mod, b_mod, g_mix, g_ffn, w_in, w_out, hy_conv_w, hy_conv_b,
           hy_f_w1, hy_f_b1, hy_f_w2, hy_f_b2, hy_f_w3, hy_f_b3, hy_f_wout, hy_f_freq, hy_skip,
           qk_g_q, qk_g_k, na_rpb, w_ffn_in, w_ffn_out, g_final):
    bsz, s_len, d = x.shape
    lc = ctx.shape[1]
    depth = w_mod.shape[0]
    t_all = s_len + lc
    hyw = hy_skip.shape[1]
    hy_cols = 3 * hyw
    assert bsz % 2 == 0 and s_len % TOK_TILE == 0 and lc == TOK_TILE and s_len % lc == 0

    nb = -(-(bsz + 1) // 8) * 8
    cs = jnp.zeros((nb, d), F32).at[:bsz].set(c).at[bsz].set(c_ctx)
    mod = _mod_call(cs, w_mod, b_mod)

    rope = _rope_tables(s_len, lc)
    tables = _na_bias_tables(na_rpb, s_len // GRID_W)
    q0 = hy_cols
    k0 = q0 + GQA_HEADS * HEAD_DIM
    v0 = k0 + GQA_KV_HEADS * HEAD_DIM
    nq0 = v0 + GQA_KV_HEADS * HEAD_DIM
    nk0 = nq0 + NA_HEADS * HEAD_DIM
    nv0 = nk0 + NA_HEADS * HEAD_DIM
    w_cm = jnp.concatenate([w_in[:, :, :k0], w_in[:, :, v0:nk0], w_in[:, :, nv0:]], axis=2)
    w_cm = jnp.swapaxes(w_cm, 1, 2).astype(BF16)
    w_tm = jnp.concatenate([w_in[:, :, k0:v0], w_in[:, :, nk0:nv0]], axis=2).astype(BF16)
    w_out_b = w_out.astype(BF16)
    w_fi = w_ffn_in.astype(BF16)
    w_fo = w_ffn_out.astype(BF16)
    cw = jnp.concatenate([jnp.swapaxes(hy_conv_w, 1, 2), hy_conv_b[:, :, None]], axis=2)
    per = LANES // HEAD_DIM

    xa = jnp.concatenate([x, ctx], axis=1)
    for l in range(depth):
        last = l == depth - 1
        gq_b = jnp.broadcast_to(qk_g_q[l][:, None], (HEAD_DIM, LANES))
        gk2 = jnp.tile(qk_g_k[l], per)[None]
        hy_t, q, k, v, nq, nk, nv = _inproj_call(xa, mod[l], g_mix[l][None], w_cm[l], w_tm[l],
                                                  gq_b, gk2, rope, hy_cols, s_len)
        n_tok = s_len if last else t_all
        gqa = _gqa_call(q, k, v, s_len, n_tok)
        na = _na_call(nq, nk, nv, tables[l], s_len, n_tok)

        fargs = (hy_f_w1[l], hy_f_b1[l], hy_f_w2[l], hy_f_b2[l], hy_f_w3[l], hy_f_b3[l],
                 hy_f_wout[l], hy_f_freq[l])
        hy_p = hy_t.reshape(2, bsz // 2, hy_cols, t_all)
        skip = hy_skip[l][:, None]
        fr, fi = _hy_spec_call(_hy_filter_call(s_len, *fargs))
        hyo = _hy_conv_call(hy_p, fr, fi, cw[l], skip, s_len, 0)
        if not last:
            frc, fic = _hy_spec_call(_hy_filter_call(lc, *fargs))
            hyo = _hy_conv_call(hy_p, frc, fic, cw[l], skip, lc, s_len // lc, prev=hyo)
        hyo = hyo.reshape(bsz, hyw, t_all)

        xa = _outproj_call(xa, hyo, gqa, na, w_out_b[l], mod[l], s_len, n_tok)
        xa = _ffn_call(xa, mod[l], g_ffn[l][None], w_fi[l], w_fo[l], s_len,
                       g_final[None] if last else None)
    return xa
```

```python
import functools
import math

import numpy as np
import jax
import jax.numpy as jnp
from jax import lax
from jax.experimental import pallas as pl
from jax.experimental.pallas import tpu as pltpu

F32 = jnp.float32
BF16 = jnp.bfloat16
HIGHEST = lax.Precision.HIGHEST

GRID_W = 64
HEAD_DIM = 64
GQA_HEADS = 8
GQA_KV_HEADS = 2
GQA_GROUP = GQA_HEADS // GQA_KV_HEADS
NA_HEADS = 4
NA_KH = 8
NA_KW = 16
ROPE_THETA = 10000.0
HY_EMB_DIM = 33
HY_DECAY_TARGET = 1e-2
HY_FAST_DECAY = 0.3
HY_SLOW_DECAY = 1.5
NORM_EPS = 1e-6
ATTN_SCALE = HEAD_DIM ** -0.5
NEG = -1e30
LOG2E = math.log2(math.e)

LANES = 128
TOK_TILE = 256
FFT_CHUNK = LANES
HY_CT = 64
NA_ROWS = TOK_TILE // GRID_W
NA_WIN = NA_ROWS + NA_KH
VMEM_LIMIT = 56 * 1024 * 1024


def _cparams(sem):
    return pltpu.CompilerParams(dimension_semantics=sem, vmem_limit_bytes=VMEM_LIMIT)


def _dot(a, b, **kw):
    return jnp.dot(a, b, preferred_element_type=F32, **kw)


def _dot_nt(a, b, **kw):
    return lax.dot_general(a, b, (((1,), (1,)), ((), ())), preferred_element_type=F32, **kw)


def _mod_kernel(cs_ref, w_ref, b_ref, o_ref):
    cs = cs_ref[...]
    s = cs * jax.nn.sigmoid(cs)
    o_ref[...] = _dot(s, w_ref[...], precision=HIGHEST) + b_ref[...]


def _mod_call(cs, w_mod, b_mod):
    depth, d, n = w_mod.shape
    nb = cs.shape[0]
    tn = 1024
    return pl.pallas_call(
        _mod_kernel,
        grid=(depth, n // tn),
        in_specs=[pl.BlockSpec((nb, d), lambda l, j: (0, 0)),
                  pl.BlockSpec((None, d, tn), lambda l, j: (l, 0, j)),
                  pl.BlockSpec((None, 1, tn), lambda l, j: (l, 0, j))],
        out_specs=pl.BlockSpec((None, nb, tn), lambda l, j: (l, 0, j)),
        out_shape=jax.ShapeDtypeStruct((depth, nb, n), F32),
        name="adaln_mod",
        compiler_params=_cparams(("parallel", "parallel")),
    )(cs, w_mod, b_mod.reshape(depth, 1, n))


def _mod_row(mod_ref, row, k, d):
    return mod_ref[pl.ds(row, 1), k * d:(k + 1) * d]


def _norm_mod(x, g, shift, scale):
    ms = jnp.mean(x * x, axis=-1, keepdims=True)
    y = x * lax.rsqrt(ms + NORM_EPS) * g
    return y * (1.0 + scale) + shift


def _batch_group(bsz):
    return 4 if bsz % 4 == 0 else 2


def _inproj_kernel(n_lat_tiles, ctx_row, hyc, x_ref, mod_ref, g_ref, wcm_ref, wtm_ref, gq_ref, gk_ref,
                   cosc_ref, sinc_ref, cos_ref, sin_ref,
                   hy_ref, q_ref, k_ref, v_ref, nq_ref, nk_ref, nv_ref, h_scr):
    j = pl.program_id(1)
    nbat, tm, d = x_ref.shape
    for s in range(nbat):
        row = jnp.where(j < n_lat_tiles, pl.program_id(0) * nbat + s, ctx_row)
        h = _norm_mod(x_ref[s], g_ref[...], _mod_row(mod_ref, row, 0, d), _mod_row(mod_ref, row, 1, d))
        h_scr[s * tm:(s + 1) * tm, :] = h.astype(BF16)
    h = h_scr[...]
    cm = _dot_nt(wcm_ref[...], h)
    tmj = _dot(h, wtm_ref[...])

    half = HEAD_DIM // 2
    reps = cm.shape[1] // LANES
    gq = jnp.concatenate([gq_ref[...]] * reps, axis=1)
    cosc = jnp.concatenate([cosc_ref[...]] * nbat, axis=1)
    sinc = jnp.concatenate([sinc_ref[...]] * nbat, axis=1)
    for hh in range(GQA_HEADS):
        x = cm[hyc + hh * HEAD_DIM:hyc + (hh + 1) * HEAD_DIM, :]
        ms = jnp.mean(x * x, axis=0, keepdims=True)
        xn = x * lax.rsqrt(ms + NORM_EPS) * gq
        x1, x2 = xn[:half], xn[half:]
        y = jnp.concatenate([x1 * cosc - x2 * sinc, x1 * sinc + x2 * cosc], axis=0)
        y = (y * (ATTN_SCALE * LOG2E)).astype(BF16)
        for s in range(nbat):
            q_ref[s, hh * HEAD_DIM:(hh + 1) * HEAD_DIM, :] = y[:, s * tm:(s + 1) * tm]
    voff = hyc + GQA_HEADS * HEAD_DIM
    nqoff = voff + GQA_KV_HEADS * HEAD_DIM
    nvoff = nqoff + NA_HEADS * HEAD_DIM
    for s in range(nbat):
        cols = slice(s * tm, (s + 1) * tm)
        hy_ref[s] = cm[:hyc, cols]
        v_ref[s] = cm[voff:nqoff, cols].astype(BF16)
        nq_ref[s] = (cm[nqoff:nvoff, cols] * (ATTN_SCALE * LOG2E)).astype(BF16)
        for e in range(NA_HEADS):
            nv_ref[s, e] = cm[nvoff + e * HEAD_DIM:nvoff + (e + 1) * HEAD_DIM, cols].astype(BF16)
        _inproj_token_major(tmj[s * tm:(s + 1) * tm], gk_ref, cos_ref, sin_ref, k_ref.at[s], nk_ref.at[s])


def _inproj_token_major(p, gk_ref, cos_ref, sin_ref, k_ref, nk_ref):
    ri = lax.broadcasted_iota(jnp.int32, (LANES, LANES), 0) // HEAD_DIM
    ci = lax.broadcasted_iota(jnp.int32, (LANES, LANES), 1) // HEAD_DIM
    avg = jnp.where(ri == ci, 1.0 / HEAD_DIM, 0.0).astype(BF16)
    lane = lax.broadcasted_iota(jnp.int32, (1, LANES), 1)
    first_half = (lane % HEAD_DIM) < (HEAD_DIM // 2)
    xc = p[:, :LANES]
    ms = _dot((xc * xc).astype(BF16), avg)
    xn = xc * lax.rsqrt(ms + NORM_EPS) * gk_ref[...]
    partner = jnp.where(first_half, pltpu.roll(xn, LANES - HEAD_DIM // 2, 1),
                        pltpu.roll(xn, HEAD_DIM // 2, 1))
    y = (xn * cos_ref[...] + partner * sin_ref[...]).astype(BF16)
    for e in range(GQA_KV_HEADS):
        k_ref[e] = y[:, e * HEAD_DIM:(e + 1) * HEAD_DIM]
    for e in range(NA_HEADS):
        nk_ref[e] = p[:, LANES + e * HEAD_DIM:LANES + (e + 1) * HEAD_DIM].astype(BF16)


def _inproj_call(xa, mod_l, g, w_cm, w_tm, gq_b, gk2, rope, hyc, s_len):
    bsz, t_all, d = xa.shape
    tm = TOK_TILE
    nbat = _batch_group(bsz)
    cosc, sinc, cos_t, sin_t = rope
    grid = (bsz // nbat, t_all // tm)
    const = lambda b, j: (0, 0)
    resident = lambda a: pl.BlockSpec(a.shape, const, pipeline_mode=pl.Buffered(1))
    heads = lambda n: pl.BlockSpec((nbat, n, tm, HEAD_DIM), lambda b, j: (b, 0, j, 0))
    hshape = lambda n: jax.ShapeDtypeStruct((bsz, n, t_all, HEAD_DIM), BF16)
    chan = lambda n: pl.BlockSpec((nbat, n, tm), lambda b, j: (b, 0, j))
    qc = GQA_HEADS * HEAD_DIM
    vc = GQA_KV_HEADS * HEAD_DIM
    nc = NA_HEADS * HEAD_DIM
    return pl.pallas_call(
        functools.partial(_inproj_kernel, s_len // tm, bsz, hyc),
        grid=grid,
        in_specs=[pl.BlockSpec((nbat, tm, d), lambda b, j: (b, j, 0)),
                  pl.BlockSpec(mod_l.shape, const),
                  pl.BlockSpec((1, d), const),
                  resident(w_cm), resident(w_tm),
                  pl.BlockSpec(gq_b.shape, const),
                  pl.BlockSpec((1, LANES), const),
                  pl.BlockSpec((HEAD_DIM // 2, tm), lambda b, j: (0, j)),
                  pl.BlockSpec((HEAD_DIM // 2, tm), lambda b, j: (0, j)),
                  pl.BlockSpec((tm, LANES), lambda b, j: (j, 0)),
                  pl.BlockSpec((tm, LANES), lambda b, j: (j, 0))],
        out_specs=[chan(hyc), chan(qc), heads(GQA_KV_HEADS), chan(vc),
                   chan(nc), heads(NA_HEADS),
                   pl.BlockSpec((nbat, NA_HEADS, None, HEAD_DIM, tm), lambda b, j: (b, 0, j, 0, 0))],
        out_shape=[jax.ShapeDtypeStruct((bsz, hyc, t_all), F32),
                   jax.ShapeDtypeStruct((bsz, qc, t_all), BF16),
                   hshape(GQA_KV_HEADS),
                   jax.ShapeDtypeStruct((bsz, vc, t_all), BF16),
                   jax.ShapeDtypeStruct((bsz, nc, t_all), BF16),
                   hshape(NA_HEADS),
                   jax.ShapeDtypeStruct((bsz, NA_HEADS, t_all // tm, HEAD_DIM, tm), BF16)],
        scratch_shapes=[pltpu.VMEM((nbat * tm, d), BF16)],
        name="in_proj",
        compiler_params=_cparams(("parallel", "parallel")),
    )(xa, mod_l, g, w_cm, w_tm, gq_b, gk2, cosc, sinc, cos_t, sin_t)


def _gqa_kernel(s_len, lc, tk, q_ref, k_ref, vt_ref, o_ref, s_scr):
    i = pl.program_id(2)
    tm = q_ref.shape[1]
    sub = 8

    def finish(h, l, acc):
        o_ref[h * HEAD_DIM:(h + 1) * HEAD_DIM, :] = (acc / jnp.sum(l, axis=0, keepdims=True)).astype(o_ref.dtype)

    def scores(h, q, r0, nrow, m):
        s = _dot(k_ref[r0:r0 + nrow, :], q)
        s_scr[h, r0:r0 + nrow, :] = s
        return jnp.maximum(m, jnp.max(s.reshape(nrow // sub, sub, tm), axis=0))

    def apply(h, r0, nrow, m, l, acc):
        p = jnp.exp2(s_scr[h, r0:r0 + nrow, :].reshape(nrow // sub, sub, tm) - m[None])
        l = l + jnp.sum(p, axis=0)
        acc = acc + _dot(vt_ref[:, r0:r0 + nrow], p.reshape(nrow, tm).astype(BF16))
        return l, acc

    def two_pass(chunks):
        maxes = [None] * GQA_GROUP
        for stage in range(GQA_GROUP + 1):
            h1, h2 = stage, stage - 1
            if h1 < GQA_GROUP:
                q1 = q_ref[h1 * HEAD_DIM:(h1 + 1) * HEAD_DIM, :]
                m1 = jnp.full((sub, tm), NEG, F32)
            if h2 >= 0:
                m2 = jnp.broadcast_to(jnp.max(maxes[h2], axis=0, keepdims=True), (sub, tm))
                l = jnp.zeros((sub, tm), F32)
                acc = jnp.zeros((HEAD_DIM, tm), F32)
            for r0, nrow in chunks:
                if h1 < GQA_GROUP:
                    m1 = scores(h1, q1, r0, nrow, m1)
                if h2 >= 0:
                    l, acc = apply(h2, r0, nrow, m2, l, acc)
            if h1 < GQA_GROUP:
                maxes[h1] = m1
            if h2 >= 0:
                finish(h2, l, acc)

    ctx_chunk = (s_len, lc)

    @pl.when(i * tm < s_len)
    def _():
        two_pass([(c * tk, tk) for c in range(s_len // tk)] + [ctx_chunk])

    @pl.when(i * tm >= s_len)
    def _():
        two_pass([ctx_chunk])


def _gqa_call(qt, k, vt, s_len, n_query):
    bsz, qc, t_all = qt.shape
    hd = HEAD_DIM
    tm = TOK_TILE
    tk = min(8 * TOK_TILE, s_len)
    lc = t_all - s_len
    return pl.pallas_call(
        functools.partial(_gqa_kernel, s_len, lc, tk),
        grid=(bsz, GQA_KV_HEADS, n_query // tm),
        in_specs=[pl.BlockSpec((None, GQA_GROUP * hd, tm), lambda b, g, i: (b, g, i)),
                  pl.BlockSpec((None, None, t_all, hd), lambda b, g, i: (b, g, 0, 0)),
                  pl.BlockSpec((None, hd, t_all), lambda b, g, i: (b, g, 0))],
        out_specs=pl.BlockSpec((None, GQA_GROUP * hd, tm), lambda b, g, i: (b, g, i)),
        out_shape=jax.ShapeDtypeStruct((bsz, qc, t_all), BF16),
        scratch_shapes=[pltpu.VMEM((GQA_GROUP, t_all, tm), F32)],
        name="gqa_attention",
        compiler_params=_cparams(("parallel", "parallel", "parallel")),
    )(qt, k, vt)


def _na_bias_consts(rows):
    assert rows >= NA_WIN + NA_ROWS
    nrow, ncol = 2 * NA_KH - 1, 2 * NA_KW - 1
    rsel = np.zeros((3, NA_ROWS, NA_WIN, nrow), np.float32)
    mrow = np.zeros((3, NA_ROWS, NA_WIN), bool)
    for kind, r0 in enumerate((0, NA_ROWS, rows - NA_ROWS)):
        kr0 = int(np.clip(r0 - NA_KH // 2, 0, rows - NA_WIN))
        for qr in range(NA_ROWS):
            r = r0 + qr
            rs = int(np.clip(r - NA_KH // 2, 0, rows - NA_KH))
            for kr in range(NA_WIN):
                ka = kr0 + kr
                if rs <= ka < rs + NA_KH:
                    mrow[kind, qr, kr] = True
                    rsel[kind, qr, kr, ka - r + NA_KH - 1] = 1.0
    csel = np.zeros((GRID_W, GRID_W, ncol), np.float32)
    mcol = np.zeros((GRID_W, GRID_W), bool)
    for w in range(GRID_W):
        cs = int(np.clip(w - NA_KW // 2, 0, GRID_W - NA_KW))
        for kc in range(cs, cs + NA_KW):
            mcol[w, kc] = True
            csel[w, kc, kc - w + NA_KW - 1] = 1.0
    mask = mrow[:, :, None, :, None] & mcol[None, None, :, None, :]
    return rsel, csel, mask.reshape(3, TOK_TILE, NA_WIN * GRID_W)


def _na_bias_tables(na_rpb, rows):
    rsel, csel, mask = _na_bias_consts(rows)
    depth, heads = na_rpb.shape[:2]
    a = jnp.einsum("kqra,lhab->lkhqrb", jnp.asarray(rsel), na_rpb.astype(F32) * LOG2E, precision=HIGHEST)
    t = jnp.einsum("lkhqrb,wcb->lkhrcqw", a, jnp.asarray(csel), precision=HIGHEST)
    t = t.reshape(depth, 3, heads, NA_WIN * GRID_W, TOK_TILE)
    mask_t = np.swapaxes(mask, 1, 2)
    return jnp.where(jnp.asarray(mask_t)[None, :, None], t, NEG)


def _na_kernel(s_len, lc, rows, q_ref, k_ref, vt_ref, tab_ref, o_ref, s_scr):
    i = pl.program_id(1)
    tm = q_ref.shape[1]
    nblk = s_len // tm
    sub = 8

    def scores(h, part, off, m):
        r0, _, nrow, bias = part
        s = _dot(k_ref[h, pl.ds(r0, nrow), :], q_ref[h * HEAD_DIM:(h + 1) * HEAD_DIM, :])
        if bias is not None:
            s = s + bias[h]
        s_scr[h, off:off + nrow, :] = s
        return jnp.maximum(m, jnp.max(s.reshape(nrow // sub, sub, tm), axis=0))

    def apply(h, part, off, m, l, acc):
        _, c0, nrow, _ = part
        p = jnp.exp2(s_scr[h, off:off + nrow, :].reshape(nrow // sub, sub, tm) - m[None])
        l = l + jnp.sum(p, axis=0)
        vt = jnp.concatenate([vt_ref[h, c0 + c] for c in range(nrow // tm)], axis=1)
        return l, acc + _dot(vt, p.reshape(nrow, tm).astype(BF16))

    def run(parts):
        offs = [sum(p[2] for p in parts[:n]) for n in range(len(parts))]
        maxes = [None] * NA_HEADS
        for stage in range(NA_HEADS + 1):
            h1, h2 = stage, stage - 1
            if h1 < NA_HEADS:
                m1 = jnp.full((sub, tm), NEG, F32)
            if h2 >= 0:
                m2 = jnp.broadcast_to(jnp.max(maxes[h2], axis=0, keepdims=True), (sub, tm))
                l = jnp.zeros((sub, tm), F32)
                acc = jnp.zeros((HEAD_DIM, tm), F32)
            for part, off in zip(parts, offs):
                if h1 < NA_HEADS:
                    m1 = scores(h1, part, off, m1)
                if h2 >= 0:
                    l, acc = apply(h2, part, off, m2, l, acc)
            if h1 < NA_HEADS:
                maxes[h1] = m1
            if h2 >= 0:
                o = acc / jnp.sum(l, axis=0, keepdims=True)
                o_ref[h2 * HEAD_DIM:(h2 + 1) * HEAD_DIM, :] = o.astype(o_ref.dtype)

    ctx_part = (s_len, s_len // tm, lc, None)

    @pl.when(i < nblk)
    def _():
        kind = jnp.where(i == 0, 0, jnp.where(i == nblk - 1, 2, 1))
        c0 = jnp.clip(i - (NA_KH // 2) // NA_ROWS, 0, (rows - NA_WIN) // NA_ROWS)
        ks = pl.multiple_of(c0 * tm, tm)
        run([(ks, c0, NA_WIN * GRID_W, tab_ref.at[kind]), ctx_part])

    @pl.when(i >= nblk)
    def _():
        run([ctx_part])


def _na_call(nqt, nk, nvt, table, s_len, n_query):
    bsz, heads, t_all, hd = nk.shape
    tm = TOK_TILE
    rows = s_len // GRID_W
    assert NA_ROWS * GRID_W == tm and (NA_KH // 2) % NA_ROWS == 0 and (rows - NA_WIN) % NA_ROWS == 0
    return pl.pallas_call(
        functools.partial(_na_kernel, s_len, t_all - s_len, rows),
        grid=(bsz, n_query // tm),
        in_specs=[pl.BlockSpec((None, heads * hd, tm), lambda b, i: (b, 0, i)),
                  pl.BlockSpec((None, heads, t_all, hd), lambda b, i: (b, 0, 0, 0)),
                  pl.BlockSpec((None,) + nvt.shape[1:], lambda b, i: (b, 0, 0, 0, 0)),
                  pl.BlockSpec(table.shape, lambda b, i: (0, 0, 0, 0), pipeline_mode=pl.Buffered(1))],
        out_specs=pl.BlockSpec((None, heads * hd, tm), lambda b, i: (b, 0, i)),
        out_shape=jax.ShapeDtypeStruct((bsz, heads * hd, t_all), BF16),
        scratch_shapes=[pltpu.VMEM((heads, NA_WIN * GRID_W + t_all - s_len, tm), F32)],
        name="neighbourhood_attention",
        compiler_params=_cparams(("parallel", "parallel")),
    )(nqt, nk, nvt, table)


def _hy_filter_kernel(z_ref, t_ref, side_ref, w1_ref, b1_ref, w2_ref, b2_ref, w3_ref, b3_ref,
                      wo_ref, fr_ref, dl_ref, o_ref):
    om = fr_ref[...]
    h = jnp.sin(om * (_dot(z_ref[...], w1_ref[...], precision=HIGHEST) + b1_ref[...]))
    h = jnp.sin(om * (_dot(h, w2_ref[...], precision=HIGHEST) + b2_ref[...]))
    h = jnp.sin(om * (_dot(h, w3_ref[...], precision=HIGHEST) + b3_ref[...]))
    ht = _dot_nt(wo_ref[...], h, precision=HIGHEST)
    c = o_ref.shape[0]
    decay = jnp.exp(-t_ref[...] * jnp.abs(dl_ref[...]))
    side = side_ref[...]
    o_ref[...] = decay * (jnp.where(side > 0.5, ht[:c], 0.0) + jnp.where(side < -0.5, ht[c:], 0.0))


def _hy_filter_consts(length):
    n = np.arange(2 * length)
    pos = np.where(n < length, n, 2 * length - n)
    pos = np.where(n == length, 0, pos)
    side = np.where(n < length, 1.0, -1.0)
    side = np.where(n == length, 0.0, side)
    t = np.linspace(0.0, 1.0, length)[pos]
    bands = (HY_EMB_DIM - 1) // 2
    w = 2.0 * math.pi * pos / length
    fr = np.linspace(1e-4, bands - 1, bands)
    z = np.concatenate([t[:, None], np.cos(fr[None, :] * w[:, None]), -np.sin(fr[None, :] * w[:, None])], axis=-1)
    zp = np.zeros((2 * length, 64), np.float32)
    zp[:, :HY_EMB_DIM] = z
    return zp, t.astype(np.float32)[None, :], side.astype(np.float32)[None, :]


def _hy_filter_call(length, w1, b1, w2, b2, w3, b3, wout, freq):
    f = w2.shape[0]
    c = wout.shape[1] // 2
    zp, t, side = _hy_filter_consts(length)
    w1p = jnp.zeros((64, f), F32).at[:HY_EMB_DIM].set(w1)
    deltas = np.linspace(math.log(HY_DECAY_TARGET) / HY_SLOW_DECAY,
                         math.log(HY_DECAY_TARGET) / HY_FAST_DECAY, c).astype(np.float32)[:, None]
    args = (jnp.asarray(zp), jnp.asarray(t), jnp.asarray(side), w1p, b1[None], w2, b2[None], w3, b3[None],
            wout.T, freq[None], jnp.asarray(deltas))
    return pl.pallas_call(
        _hy_filter_kernel,
        in_specs=[pl.BlockSpec(a.shape, lambda: (0,) * a.ndim) for a in args],
        out_specs=pl.BlockSpec((c, 2 * length), lambda: (0, 0)),
        out_shape=jax.ShapeDtypeStruct((c, 2 * length), F32),
        grid=(),
        name="hyena_filter",
        compiler_params=pltpu.CompilerParams(vmem_limit_bytes=VMEM_LIMIT),
    )(*args)


def _bitrev(n):
    bits = int(math.log2(n))
    return np.array([int(format(i, "0%db" % bits)[::-1], 2) if bits else 0 for i in range(n)])


def _fft_consts(n1, ct):
    n = n1 * FFT_CHUNK
    stages = max(int(math.log2(n1)), 1)
    tw = np.zeros((2, stages, max(n1 // 2, 1)), np.float32)
    for s in range(int(math.log2(n1))):
        half = n1 >> (s + 1)
        ang = -2.0 * math.pi * np.arange(half) / (2 * half)
        tw[0, s, :half] = np.cos(ang)
        tw[1, s, :half] = np.sin(ang)
    k1 = _bitrev(n1)
    ang = -2.0 * math.pi * (np.arange(FFT_CHUNK)[None, :] * k1[:, None]) / n
    twr = np.repeat(np.cos(ang), ct, axis=0).astype(np.float32)
    twi = np.repeat(np.sin(ang), ct, axis=0).astype(np.float32)
    a = -2.0 * math.pi * np.outer(np.arange(FFT_CHUNK), np.arange(FFT_CHUNK)) / FFT_CHUNK
    wr, wi = np.cos(a), np.sin(a)
    wf = np.block([[wr, wi], [-wi, wr]]).astype(np.float32)
    wb = np.block([[wr, -wi], [wi, wr]]).astype(np.float32)
    return tw, twr, twi, _split_bf16(wf), _split_bf16(wb)


def _split_bf16(w):
    w = jnp.asarray(w, F32)
    hi = w.astype(BF16)
    return jnp.stack([hi, (w - hi.astype(F32)).astype(BF16)])


def _slab_fft(zr, zi, tw_ref, n1, ct, inverse, half_input=False, half_output=False):
    nst = int(math.log2(n1))
    order = range(nst - 1, -1, -1) if inverse else range(nst)
    for s in order:
        half = n1 >> (s + 1)
        lh = int(math.log2(half))
        first = s == 0

        def body(idx, carry, s=s, half=half, lh=lh, first=first):
            blk = idx >> lh
            j = idx & (half - 1)
            r0 = pl.multiple_of(((blk << (lh + 1)) + j) * ct, ct)
            r1 = pl.multiple_of(r0 + half * ct, ct)
            wr = tw_ref[0, s, j]
            wi = tw_ref[1, s, j]
            ar = zr[pl.ds(r0, ct), :]
            ai = zi[pl.ds(r0, ct), :]
            if not inverse:
                if first and half_input:
                    zr[pl.ds(r1, ct), :] = ar * wr - ai * wi
                    zi[pl.ds(r1, ct), :] = ar * wi + ai * wr
                else:
                    br = zr[pl.ds(r1, ct), :]
                    bi = zi[pl.ds(r1, ct), :]
                    dr = ar - br
                    di = ai - bi
                    zr[pl.ds(r0, ct), :] = ar + br
                    zi[pl.ds(r0, ct), :] = ai + bi
                    zr[pl.ds(r1, ct), :] = dr * wr - di * wi
                    zi[pl.ds(r1, ct), :] = dr * wi + di * wr
            else:
                br = zr[pl.ds(r1, ct), :]
                bi = zi[pl.ds(r1, ct), :]
                tr = br * wr + bi * wi
                ti = bi * wr - br * wi
                zr[pl.ds(r0, ct), :] = ar + tr
                zi[pl.ds(r0, ct), :] = ai + ti
                if not (first and half_output):
                    zr[pl.ds(r1, ct), :] = ar - tr
                    zi[pl.ds(r1, ct), :] = ai - ti
            return carry

        lax.fori_loop(0, n1 // 2, body, 0, unroll=min(4, n1 // 2))


def _chunk_dft(xr, xi, w_ref, split):
    x = jnp.concatenate([xr, xi], axis=1)
    x_hi = x.astype(BF16)
    y = _dot(x_hi, w_ref[0])
    if split:
        x_lo = (x - x_hi.astype(F32)).astype(BF16)
        y = y + (_dot(x_lo, w_ref[0]) + _dot(x_hi, w_ref[1]))
    return y[:, :FFT_CHUNK], y[:, FFT_CHUNK:]


def _fft_groups(n1, ct):
    g = min(8, n1)
    return n1 // g, g * ct


def _hy_spec_kernel(n1, ct, f_ref, tw_ref, twr_ref, twi_ref, wf_ref, fr_ref, fi_ref, zr, zi):
    for s in range(n1):
        zr[s * ct:(s + 1) * ct, :] = f_ref[:, s * FFT_CHUNK:(s + 1) * FFT_CHUNK]
    zi[...] = jnp.zeros(zi.shape, F32)
    _slab_fft(zr, zi, tw_ref, n1, ct, inverse=False)
    ngroups, gr = _fft_groups(n1, ct)

    def body(g, carry):
        r0 = pl.multiple_of(g * gr, gr)
        a = zr[pl.ds(r0, gr), :]
        b = zi[pl.ds(r0, gr), :]
        tr = twr_ref[pl.ds(r0, gr), :]
        ti = twi_ref[pl.ds(r0, gr), :]
        yr, yi = _chunk_dft(a * tr - b * ti, a * ti + b * tr, wf_ref, split=True)
        fr_ref[pl.ds(r0, gr), :] = yr
        fi_ref[pl.ds(r0, gr), :] = yi
        return carry

    lax.fori_loop(0, ngroups, body, 0, unroll=min(2, ngroups))


def _hy_spec_call(filt_t):
    c, n = filt_t.shape
    n1 = n // FFT_CHUNK
    ct = HY_CT
    tw, twr, twi, wf, _ = _fft_consts(n1, ct)
    const2 = lambda i: (0, 0)
    out = jax.ShapeDtypeStruct((c // ct, n1 * ct, FFT_CHUNK), F32)
    ospec = pl.BlockSpec((None, n1 * ct, FFT_CHUNK), lambda i: (i, 0, 0))
    return pl.pallas_call(
        functools.partial(_hy_spec_kernel, n1, ct),
        grid=(c // ct,),
        in_specs=[pl.BlockSpec((ct, n), lambda i: (i, 0)),
                  pl.BlockSpec(memory_space=pltpu.SMEM),
                  pl.BlockSpec(twr.shape, const2), pl.BlockSpec(twi.shape, const2),
                  pl.BlockSpec(wf.shape, lambda i: (0, 0, 0))],
        out_specs=[ospec, ospec],
        out_shape=[out, out],
        scratch_shapes=[pltpu.VMEM((n1 * ct, FFT_CHUNK), F32), pltpu.VMEM((n1 * ct, FFT_CHUNK), F32)],
        name="hyena_filter_spectrum",
        compiler_params=_cparams(("parallel",)),
    )(filt_t, jnp.asarray(tw), jnp.asarray(twr), jnp.asarray(twi), jnp.asarray(wf))


def _hy_conv_kernel(n1, ct, length, v_ref, x1_ref, x2_ref, cwv_ref, cw1_ref, cw2_ref, skip_ref,
                    fr_ref, fi_ref, tw_ref, twr_ref, twi_ref, wf_ref, wb_ref, o_ref,
                    zr, zi, u_scr, g_scr):
    lane = lax.broadcasted_iota(jnp.int32, (1, length), 1)

    sub = 8

    def short_conv(x, cw):
        xm = jnp.where(lane >= 1, pltpu.roll(x, 1, 1), 0.0)
        xp = jnp.where(lane <= length - 2, pltpu.roll(x, length - 1, 1), 0.0)
        return xm * cw[:, 0:1] + x * cw[:, 1:2] + xp * cw[:, 2:3] + cw[:, 3:4]

    def conv_rows(r, carry):
        rows = pl.ds(pl.multiple_of(r * sub, sub), sub)
        cwv, cw1, cw2 = cwv_ref[rows, :], cw1_ref[rows, :], cw2_ref[rows, :]
        for a in range(2):
            u_scr[a, rows, :] = short_conv(v_ref[a, rows, :], cwv) * short_conv(x1_ref[a, rows, :], cw1)
            g_scr[a, rows, :] = short_conv(x2_ref[a, rows, :], cw2)
        return carry

    lax.fori_loop(0, ct // sub, conv_rows, 0)
    nz = length // FFT_CHUNK
    for s in range(nz):
        zr[s * ct:(s + 1) * ct, :] = u_scr[0, :, s * FFT_CHUNK:(s + 1) * FFT_CHUNK]
        zi[s * ct:(s + 1) * ct, :] = u_scr[1, :, s * FFT_CHUNK:(s + 1) * FFT_CHUNK]
    _slab_fft(zr, zi, tw_ref, n1, ct, inverse=False, half_input=True)
    ngroups, gr = _fft_groups(n1, ct)

    def body(g, carry):
        r0 = pl.multiple_of(g * gr, gr)
        a = zr[pl.ds(r0, gr), :]
        b = zi[pl.ds(r0, gr), :]
        tr = twr_ref[pl.ds(r0, gr), :]
        ti = twi_ref[pl.ds(r0, gr), :]
        yr, yi = _chunk_dft(a * tr - b * ti, a * ti + b * tr, wf_ref, split=False)
        fr = fr_ref[pl.ds(r0, gr), :]
        fi = fi_ref[pl.ds(r0, gr), :]
        qr, qi = _chunk_dft(yr * fr - yi * fi, yr * fi + yi * fr, wb_ref, split=False)
        zr[pl.ds(r0, gr), :] = qr * tr + qi * ti
        zi[pl.ds(r0, gr), :] = qi * tr - qr * ti
        return carry

    lax.fori_loop(0, ngroups, body, 0, unroll=min(2, ngroups))
    _slab_fft(zr, zi, tw_ref, n1, ct, inverse=True, half_output=True)
    inv_n = 1.0 / (n1 * FFT_CHUNK)
    skip = skip_ref[...]
    for s in range(nz):
        sl = slice(s * FFT_CHUNK, (s + 1) * FFT_CHUNK)
        for a, z in ((0, zr), (1, zi)):
            y = z[s * ct:(s + 1) * ct, :] * inv_n
            u = u_scr[a, :, sl]
            o_ref[a, :, sl] = (g_scr[a, :, sl] * (y + u * skip)).astype(o_ref.dtype)


def _hy_conv_call(hy_t, spec_r, spec_i, cw, skip, length, lane_block, prev=None):
    _, bh, c3, t_all = hy_t.shape
    c = c3 // 3
    ct = HY_CT
    nct = c // ct
    n1 = 2 * length // FFT_CHUNK
    tw, twr, twi, wf, wb = _fft_consts(n1, ct)
    const2 = lambda p, i: (0, 0)
    xin = lambda k: pl.BlockSpec((2, None, ct, length), lambda p, i, k=k: (0, p, k * nct + i, lane_block))
    cwin = lambda k: pl.BlockSpec((ct, 4), lambda p, i, k=k: (k * nct + i, 0))
    fin = pl.BlockSpec((None, n1 * ct, FFT_CHUNK), lambda p, i: (i, 0, 0))
    in_specs = [xin(0), xin(1), xin(2), cwin(0), cwin(1), cwin(2),
                pl.BlockSpec((ct, 1), lambda p, i: (i, 0)),
                fin, fin,
                pl.BlockSpec(memory_space=pltpu.SMEM),
                pl.BlockSpec(twr.shape, const2), pl.BlockSpec(twi.shape, const2),
                pl.BlockSpec(wf.shape, lambda p, i: (0, 0, 0)),
                pl.BlockSpec(wb.shape, lambda p, i: (0, 0, 0))]
    args = [hy_t, hy_t, hy_t, cw, cw, cw, skip, spec_r, spec_i, jnp.asarray(tw), jnp.asarray(twr),
            jnp.asarray(twi), jnp.asarray(wf), jnp.asarray(wb)]
    kern = functools.partial(_hy_conv_kernel, n1, ct, length)
    aliases = {}
    if prev is not None:
        in_specs.append(pl.BlockSpec(memory_space=pl.ANY))
        args.append(prev)
        aliases = {len(args) - 1: 0}
        kern = lambda *refs, kern=kern: kern(*refs[:14], *refs[15:])
    return pl.pallas_call(
        kern,
        grid=(bh, nct),
        in_specs=in_specs,
        out_specs=pl.BlockSpec((2, None, ct, length), lambda p, i: (0, p, i, lane_block)),
        out_shape=jax.ShapeDtypeStruct((2, bh, c, t_all), BF16),
        scratch_shapes=[pltpu.VMEM((n1 * ct, FFT_CHUNK), F32), pltpu.VMEM((n1 * ct, FFT_CHUNK), F32),
                        pltpu.VMEM((2, ct, length), F32), pltpu.VMEM((2, ct, length), F32)],
        input_output_aliases=aliases,
        name="hyena_conv_%d" % length,
        compiler_params=_cparams(("parallel", "parallel")),
    )(*args)


def _outproj_kernel(n_lat_tiles, ctx_row, x_ref, hy_ref, ga_ref, na_ref, w_ref, mod_ref, o_ref):
    j = pl.program_id(1)
    nbat, tm, d = x_ref.shape
    mix = []
    for s in range(nbat):
        cm = jnp.concatenate([hy_ref[s], ga_ref[s], na_ref[s]], axis=0)
        mix.append(cm.astype(F32).T.astype(BF16))
    y = _dot(jnp.concatenate(mix, axis=0), w_ref[...])
    for s in range(nbat):
        row = jnp.where(j < n_lat_tiles, pl.program_id(0) * nbat + s, ctx_row)
        o_ref[s] = x_ref[s] + _mod_row(mod_ref, row, 2, d) * y[s * tm:(s + 1) * tm]


def _outproj_call(xa, hyo_t, gqa, na, w_out, mod_l, s_len, n_tok):
    bsz, _, d = xa.shape
    tm = TOK_TILE
    nbat = _batch_group(bsz)
    const = lambda b, j: (0, 0)
    tok = lambda n: pl.BlockSpec((nbat, tm, n), lambda b, j: (b, j, 0))
    chan = lambda n: pl.BlockSpec((nbat, n, tm), lambda b, j: (b, 0, j))
    return pl.pallas_call(
        functools.partial(_outproj_kernel, s_len // tm, bsz),
        grid=(bsz // nbat, n_tok // tm),
        in_specs=[tok(d), chan(hyo_t.shape[1]), chan(gqa.shape[1]), chan(na.shape[1]),
                  pl.BlockSpec(w_out.shape, const), pl.BlockSpec(mod_l.shape, const)],
        out_specs=tok(d),
        out_shape=jax.ShapeDtypeStruct((bsz, n_tok, d), F32),
        name="out_proj",
        compiler_params=_cparams(("parallel", "parallel")),
    )(xa, hyo_t, gqa, na, w_out, mod_l)


def _ffn_kernel(tiles_per_batch, n_lat_tiles, ctx_row, th, x_ref, mod_ref, g_ref, wi_ref, wo_ref,
                gf_ref, o_ref, h_scr, act_scr):
    i = pl.program_id(0)
    d = x_ref.shape[-1]
    f = wo_ref.shape[0]
    groups = x_ref.shape[0] // TOK_TILE

    def mod_row(s):
        gi = i * groups + s
        jj = gi % tiles_per_batch
        return jnp.where(jj < n_lat_tiles, gi // tiles_per_batch, ctx_row)

    for s in range(groups):
        row = mod_row(s)
        sl = slice(s * TOK_TILE, (s + 1) * TOK_TILE)
        h = _norm_mod(x_ref[sl, :], g_ref[...], _mod_row(mod_ref, row, 3, d), _mod_row(mod_ref, row, 4, d))
        h_scr[sl, :] = h.astype(BF16)
    h = h_scr[...]
    for j in range(f // th):
        gate = _dot(h, wi_ref[:, j * th:(j + 1) * th])
        up = _dot(h, wi_ref[:, f + j * th:f + (j + 1) * th])
        act_scr[:, j * th:(j + 1) * th] = (gate * jax.nn.sigmoid(gate) * up).astype(BF16)
    y = _dot(act_scr[...], wo_ref[...])
    for s in range(groups):
        row = mod_row(s)
        sl = slice(s * TOK_TILE, (s + 1) * TOK_TILE)
        out = x_ref[sl, :] + _mod_row(mod_ref, row, 5, d) * y[sl, :]
        if gf_ref is not None:
            ms = jnp.mean(out * out, axis=-1, keepdims=True)
            out = out * lax.rsqrt(ms + NORM_EPS) * gf_ref[...]
        o_ref[sl, :] = out


def _ffn_call(xa, mod_l, g, w_in, w_out, s_len, g_final=None):
    bsz, t_all, d = xa.shape
    f = w_out.shape[0]
    r = bsz * t_all
    tm = 1024 if r % 1024 == 0 else TOK_TILE
    th = 256
    const = lambda i: (0, 0)
    resident = lambda a: pl.BlockSpec(a.shape, const, pipeline_mode=pl.Buffered(1))
    in_specs = [pl.BlockSpec((tm, d), lambda i: (i, 0)),
                pl.BlockSpec(mod_l.shape, const),
                pl.BlockSpec((1, d), const),
                resident(w_in), resident(w_out)]
    args = [xa.reshape(r, d), mod_l, g, w_in, w_out]
    kern = functools.partial(_ffn_kernel, t_all // TOK_TILE, s_len // TOK_TILE, bsz, th)
    if g_final is None:
        kern = functools.partial(_ffn_no_final, kern)
    else:
        in_specs.append(pl.BlockSpec((1, d), const))
        args.append(g_final)
    out = pl.pallas_call(
        kern,
        grid=(r // tm,),
        in_specs=in_specs,
        out_specs=pl.BlockSpec((tm, d), lambda i: (i, 0)),
        out_shape=jax.ShapeDtypeStruct((r, d), F32),
        scratch_shapes=[pltpu.VMEM((tm, d), BF16), pltpu.VMEM((tm, f), BF16)],
        name="swiglu_ffn",
        compiler_params=_cparams(("parallel",)),
    )(*args)
    return out.reshape(bsz, t_all, d)


def _ffn_no_final(kern, x_ref, mod_ref, g_ref, wi_ref, wo_ref, o_ref, h_scr, act_scr):
    kern(x_ref, mod_ref, g_ref, wi_ref, wo_ref, None, o_ref, h_scr, act_scr)


def _rope_tables(s_len, lc):
    pos = np.arange(s_len)
    row = (pos // GRID_W).astype(np.float64)
    col = (pos % GRID_W).astype(np.float64)
    n_f = HEAD_DIM // 4
    inv = (ROPE_THETA ** (-np.arange(n_f, dtype=np.float32) / n_f)).astype(np.float64)
    ang = np.concatenate([row[:, None] * inv, col[:, None] * inv], axis=-1).astype(np.float32).astype(np.float64)
    lane = np.arange(LANES)
    sign = np.where((lane % HEAD_DIM) < HEAD_DIM // 2, -1.0, 1.0)
    cos = np.cos(ang)[:, lane % (HEAD_DIM // 2)]
    sin = np.sin(ang)[:, lane % (HEAD_DIM // 2)] * sign[None, :]
    cos = np.concatenate([cos, np.ones((lc, LANES))], axis=0).astype(np.float32)
    sin = np.concatenate([sin, np.zeros((lc, LANES))], axis=0).astype(np.float32)
    cosc = np.concatenate([np.cos(ang).T, np.ones((HEAD_DIM // 2, lc))], axis=1).astype(np.float32)
    sinc = np.concatenate([np.sin(ang).T, np.zeros((HEAD_DIM // 2, lc))], axis=1).astype(np.float32)
    return jnp.asarray(cosc), jnp.asarray(sinc), jnp.asarray(cos), jnp.asarray(sin)


def kernel(x, c, ctx, c_ctx, w_mod, b_mod, g_mix, g_ffn, w_in, w_out, hy_conv_w, hy_conv_b,
           hy_f_w1, hy_f_b1, hy_f_w2, hy_f_b2, hy_f_w3, hy_f_b3, hy_f_wout, hy_f_freq, hy_skip,
           qk_g_q, qk_g_k, na_rpb, w_ffn_in, w_ffn_out, g_final):
    bsz, s_len, d = x.shape
    lc = ctx.shape[1]
    depth = w_mod.shape[0]
    t_all = s_len + lc
    hyw = hy_skip.shape[1]
    hy_cols = 3 * hyw
    assert bsz % 2 == 0 and s_len % TOK_TILE == 0 and lc == TOK_TILE and s_len % lc == 0

    nb = -(-(bsz + 1) // 8) * 8
    cs = jnp.zeros((nb, d), F32).at[:bsz].set(c).at[bsz].set(c_ctx)
    mod = _mod_call(cs, w_mod, b_mod)

    rope = _rope_tables(s_len, lc)
    tables = _na_bias_tables(na_rpb, s_len // GRID_W)
    q0 = hy_cols
    k0 = q0 + GQA_HEADS * HEAD_DIM
    v0 = k0 + GQA_KV_HEADS * HEAD_DIM
    nq0 = v0 + GQA_KV_HEADS * HEAD_DIM
    nk0 = nq0 + NA_HEADS * HEAD_DIM
    nv0 = nk0 + NA_HEADS * HEAD_DIM
    w_cm = jnp.concatenate([w_in[:, :, :k0], w_in[:, :, v0:nk0], w_in[:, :, nv0:]], axis=2)
    w_cm = jnp.swapaxes(w_cm, 1, 2).astype(BF16)
    w_tm = jnp.concatenate([w_in[:, :, k0:v0], w_in[:, :, nk0:nv0]], axis=2).astype(BF16)
    w_out_b = w_out.astype(BF16)
    w_fi = w_ffn_in.astype(BF16)
    w_fo = w_ffn_out.astype(BF16)
    cw = jnp.concatenate([jnp.swapaxes(hy_conv_w, 1, 2), hy_conv_b[:, :, None]], axis=2)
    per = LANES // HEAD_DIM

    xa = jnp.concatenate([x, ctx], axis=1)
    for l in range(depth):
        last = l == depth - 1
        gq_b = jnp.broadcast_to(qk_g_q[l][:, None], (HEAD_DIM, LANES))
        gk2 = jnp.tile(qk_g_k[l], per)[None]
        hy_t, q, k, v, nq, nk, nv = _inproj_call(xa, mod[l], g_mix[l][None], w_cm[l], w_tm[l],
                                                  gq_b, gk2, rope, hy_cols, s_len)
        n_tok = s_len if last else t_all
        gqa = _gqa_call(q, k, v, s_len, n_tok)
        na = _na_call(nq, nk, nv, tables[l], s_len, n_tok)

        fargs = (hy_f_w1[l], hy_f_b1[l], hy_f_w2[l], hy_f_b2[l], hy_f_w3[l], hy_f_b3[l],
                 hy_f_wout[l], hy_f_freq[l])
        hy_p = hy_t.reshape(2, bsz // 2, hy_cols, t_all)
        skip = hy_skip[l][:, None]
        fr, fi = _hy_spec_call(_hy_filter_call(s_len, *fargs))
        hyo = _hy_conv_call(hy_p, fr, fi, cw[l], skip, s_len, 0)
        if not last:
            frc, fic = _hy_spec_call(_hy_filter_call(lc, *fargs))
            hyo = _hy_conv_call(hy_p, frc, fic, cw[l], skip, lc, s_len // lc, prev=hyo)
        hyo = hyo.reshape(bsz, hyw, t_all)

        xa = _outproj_call(xa, hyo, gqa, na, w_out_b[l], mod[l], s_len, n_tok)
        xa = _ffn_call(xa, mod[l], g_ffn[l][None], w_fi[l], w_fo[l], s_len,
                       g_final[None] if last else None)
    return xa
```

```python
import functools
import math

import numpy as np
import jax
import jax.numpy as jnp
from jax import lax
from jax.experimental import pallas as pl
from jax.experimental.pallas import tpu as pltpu

F32 = jnp.float32
BF16 = jnp.bfloat16
HIGHEST = lax.Precision.HIGHEST

GRID_W = 64
HEAD_DIM = 64
GQA_HEADS = 8
GQA_KV_HEADS = 2
GQA_GROUP = GQA_HEADS // GQA_KV_HEADS
NA_HEADS = 4
NA_KH = 8
NA_KW = 16
ROPE_THETA = 10000.0
HY_EMB_DIM = 33
HY_DECAY_TARGET = 1e-2
HY_FAST_DECAY = 0.3
HY_SLOW_DECAY = 1.5
NORM_EPS = 1e-6
ATTN_SCALE = HEAD_DIM ** -0.5
NEG = -1e30
LOG2E = math.log2(math.e)

LANES = 128
SUBLANES = 8
HY_EMB_PAD = 64
TOK_TILE = 256
FFT_CHUNK = LANES
HY_CT = 64
NA_ROWS = TOK_TILE // GRID_W
NA_WIN = NA_ROWS + NA_KH
VMEM_LIMIT = 56 * 1024 * 1024


def _cparams(sem):
    return pltpu.CompilerParams(dimension_semantics=sem, vmem_limit_bytes=VMEM_LIMIT)


def _dot(a, b, **kw):
    return jnp.dot(a, b, preferred_element_type=F32, **kw)


def _dot_nt(a, b, **kw):
    return lax.dot_general(a, b, (((1,), (1,)), ((), ())), preferred_element_type=F32, **kw)


def _mod_kernel(cs_ref, w_ref, b_ref, o_ref):
    cs = cs_ref[...]
    s = cs * jax.nn.sigmoid(cs)
    o_ref[...] = _dot(s, w_ref[...], precision=HIGHEST) + b_ref[...]


def _mod_call(cs, w_mod, b_mod):
    depth, d, n = w_mod.shape
    nb = cs.shape[0]
    tn = 1024
    return pl.pallas_call(
        _mod_kernel,
        grid=(depth, n // tn),
        in_specs=[pl.BlockSpec((nb, d), lambda l, j: (0, 0)),
                  pl.BlockSpec((None, d, tn), lambda l, j: (l, 0, j)),
                  pl.BlockSpec((None, 1, tn), lambda l, j: (l, 0, j))],
        out_specs=pl.BlockSpec((None, nb, tn), lambda l, j: (l, 0, j)),
        out_shape=jax.ShapeDtypeStruct((depth, nb, n), F32),
        name="adaln_mod",
        compiler_params=_cparams(("parallel", "parallel")),
    )(cs, w_mod, b_mod.reshape(depth, 1, n))


def _mod_row(mod_ref, row, k, d):
    return mod_ref[pl.ds(row, 1), k * d:(k + 1) * d]


def _norm_mod(x, g, shift, scale):
    ms = jnp.mean(x * x, axis=-1, keepdims=True)
    y = x * lax.rsqrt(ms + NORM_EPS) * g
    return y * (1.0 + scale) + shift


def _batch_group(bsz):
    return 4 if bsz % 4 == 0 else 2


def _inproj_kernel(n_lat_tiles, ctx_row, hyc, x_ref, mod_ref, g_ref, wcm_ref, wtm_ref, gq_ref, gk_ref,
                   cosc_ref, sinc_ref, cos_ref, sin_ref,
                   hy_ref, q_ref, k_ref, v_ref, nq_ref, nk_ref, nv_ref, h_scr):
    j = pl.program_id(1)
    nbat, tm, d = x_ref.shape
    for s in range(nbat):
        row = jnp.where(j < n_lat_tiles, pl.program_id(0) * nbat + s, ctx_row)
        h = _norm_mod(x_ref[s], g_ref[...], _mod_row(mod_ref, row, 0, d), _mod_row(mod_ref, row, 1, d))
        h_scr[s * tm:(s + 1) * tm, :] = h.astype(BF16)
    h = h_scr[...]
    cm = _dot_nt(wcm_ref[...], h)
    tmj = _dot(h, wtm_ref[...])

    half = HEAD_DIM // 2
    reps = cm.shape[1] // LANES
    gq = jnp.concatenate([gq_ref[...]] * reps, axis=1)
    cosc = jnp.concatenate([cosc_ref[...]] * nbat, axis=1)
    sinc = jnp.concatenate([sinc_ref[...]] * nbat, axis=1)
    for hh in range(GQA_HEADS):
        x = cm[hyc + hh * HEAD_DIM:hyc + (hh + 1) * HEAD_DIM, :]
        ms = jnp.mean(x * x, axis=0, keepdims=True)
        xn = x * lax.rsqrt(ms + NORM_EPS) * gq
        x1, x2 = xn[:half], xn[half:]
        y = jnp.concatenate([x1 * cosc - x2 * sinc, x1 * sinc + x2 * cosc], axis=0)
        y = (y * (ATTN_SCALE * LOG2E)).astype(BF16)
        for s in range(nbat):
            q_ref[s, hh * HEAD_DIM:(hh + 1) * HEAD_DIM, :] = y[:, s * tm:(s + 1) * tm]
    voff = hyc + GQA_HEADS * HEAD_DIM
    nqoff = voff + GQA_KV_HEADS * HEAD_DIM
    nvoff = nqoff + NA_HEADS * HEAD_DIM
    for s in range(nbat):
        cols = slice(s * tm, (s + 1) * tm)
        hy_ref[s] = cm[:hyc, cols]
        v_ref[s] = cm[voff:nqoff, cols].astype(BF16)
        nq_ref[s] = (cm[nqoff:nvoff, cols] * (ATTN_SCALE * LOG2E)).astype(BF16)
        for e in range(NA_HEADS):
            nv_ref[s, e] = cm[nvoff + e * HEAD_DIM:nvoff + (e + 1) * HEAD_DIM, cols].astype(BF16)
        _inproj_token_major(tmj[s * tm:(s + 1) * tm], gk_ref, cos_ref, sin_ref, k_ref.at[s], nk_ref.at[s])


def _inproj_token_major(p, gk_ref, cos_ref, sin_ref, k_ref, nk_ref):
    ri = lax.broadcasted_iota(jnp.int32, (LANES, LANES), 0) // HEAD_DIM
    ci = lax.broadcasted_iota(jnp.int32, (LANES, LANES), 1) // HEAD_DIM
    avg = jnp.where(ri == ci, 1.0 / HEAD_DIM, 0.0).astype(BF16)
    lane = lax.broadcasted_iota(jnp.int32, (1, LANES), 1)
    first_half = (lane % HEAD_DIM) < (HEAD_DIM // 2)
    xc = p[:, :LANES]
    ms = _dot((xc * xc).astype(BF16), avg)
    xn = xc * lax.rsqrt(ms + NORM_EPS) * gk_ref[...]
    partner = jnp.where(first_half, pltpu.roll(xn, LANES - HEAD_DIM // 2, 1),
                        pltpu.roll(xn, HEAD_DIM // 2, 1))
    y = (xn * cos_ref[...] + partner * sin_ref[...]).astype(BF16)
    for e in range(GQA_KV_HEADS):
        k_ref[e] = y[:, e * HEAD_DIM:(e + 1) * HEAD_DIM]
    for e in range(NA_HEADS):
        nk_ref[e] = p[:, LANES + e * HEAD_DIM:LANES + (e + 1) * HEAD_DIM].astype(BF16)


def _inproj_call(xa, mod_l, g, w_cm, w_tm, gq_b, gk2, rope, hyc, s_len):
    bsz, t_all, d = xa.shape
    tm = TOK_TILE
    nbat = _batch_group(bsz)
    cosc, sinc, cos_t, sin_t = rope
    grid = (bsz // nbat, t_all // tm)
    const = lambda b, j: (0, 0)
    resident = lambda a: pl.BlockSpec(a.shape, const, pipeline_mode=pl.Buffered(1))
    heads = lambda n: pl.BlockSpec((nbat, n, tm, HEAD_DIM), lambda b, j: (b, 0, j, 0))
    hshape = lambda n: jax.ShapeDtypeStruct((bsz, n, t_all, HEAD_DIM), BF16)
    chan = lambda n: pl.BlockSpec((nbat, n, tm), lambda b, j: (b, 0, j))
    qc = GQA_HEADS * HEAD_DIM
    vc = GQA_KV_HEADS * HEAD_DIM
    nc = NA_HEADS * HEAD_DIM
    return pl.pallas_call(
        functools.partial(_inproj_kernel, s_len // tm, bsz, hyc),
        grid=grid,
        in_specs=[pl.BlockSpec((nbat, tm, d), lambda b, j: (b, j, 0)),
                  pl.BlockSpec(mod_l.shape, const),
                  pl.BlockSpec((1, d), const),
                  resident(w_cm), resident(w_tm),
                  pl.BlockSpec(gq_b.shape, const),
                  pl.BlockSpec((1, LANES), const),
                  pl.BlockSpec((HEAD_DIM // 2, tm), lambda b, j: (0, j)),
                  pl.BlockSpec((HEAD_DIM // 2, tm), lambda b, j: (0, j)),
                  pl.BlockSpec((tm, LANES), lambda b, j: (j, 0)),
                  pl.BlockSpec((tm, LANES), lambda b, j: (j, 0))],
        out_specs=[chan(hyc), chan(qc), heads(GQA_KV_HEADS), chan(vc),
                   chan(nc), heads(NA_HEADS),
                   pl.BlockSpec((nbat, NA_HEADS, None, HEAD_DIM, tm), lambda b, j: (b, 0, j, 0, 0))],
        out_shape=[jax.ShapeDtypeStruct((bsz, hyc, t_all), F32),
                   jax.ShapeDtypeStruct((bsz, qc, t_all), BF16),
                   hshape(GQA_KV_HEADS),
                   jax.ShapeDtypeStruct((bsz, vc, t_all), BF16),
                   jax.ShapeDtypeStruct((bsz, nc, t_all), BF16),
                   hshape(NA_HEADS),
                   jax.ShapeDtypeStruct((bsz, NA_HEADS, t_all // tm, HEAD_DIM, tm), BF16)],
        scratch_shapes=[pltpu.VMEM((nbat * tm, d), BF16)],
        name="in_proj",
        compiler_params=_cparams(("parallel", "parallel")),
    )(xa, mod_l, g, w_cm, w_tm, gq_b, gk2, cosc, sinc, cos_t, sin_t)


def _gqa_kernel(s_len, lc, tk, q_ref, k_ref, vt_ref, o_ref, s_scr):
    i = pl.program_id(2)
    tm = q_ref.shape[1]
    sub = SUBLANES

    def finish(h, l, acc):
        o_ref[h * HEAD_DIM:(h + 1) * HEAD_DIM, :] = (acc / jnp.sum(l, axis=0, keepdims=True)).astype(o_ref.dtype)

    def scores(h, q, r0, nrow, m):
        s = _dot(k_ref[r0:r0 + nrow, :], q)
        s_scr[h, r0:r0 + nrow, :] = s
        return jnp.maximum(m, jnp.max(s.reshape(nrow // sub, sub, tm), axis=0))

    def apply(h, r0, nrow, m, l, acc):
        p = jnp.exp2(s_scr[h, r0:r0 + nrow, :].reshape(nrow // sub, sub, tm) - m[None])
        l = l + jnp.sum(p, axis=0)
        acc = acc + _dot(vt_ref[:, r0:r0 + nrow], p.reshape(nrow, tm).astype(BF16))
        return l, acc

    def two_pass(chunks):
        maxes = [None] * GQA_GROUP
        for stage in range(GQA_GROUP + 1):
            h1, h2 = stage, stage - 1
            if h1 < GQA_GROUP:
                q1 = q_ref[h1 * HEAD_DIM:(h1 + 1) * HEAD_DIM, :]
                m1 = jnp.full((sub, tm), NEG, F32)
            if h2 >= 0:
                m2 = jnp.broadcast_to(jnp.max(maxes[h2], axis=0, keepdims=True), (sub, tm))
                l = jnp.zeros((sub, tm), F32)
                acc = jnp.zeros((HEAD_DIM, tm), F32)
            for r0, nrow in chunks:
                if h1 < GQA_GROUP:
                    m1 = scores(h1, q1, r0, nrow, m1)
                if h2 >= 0:
                    l, acc = apply(h2, r0, nrow, m2, l, acc)
            if h1 < GQA_GROUP:
                maxes[h1] = m1
            if h2 >= 0:
                finish(h2, l, acc)

    ctx_chunk = (s_len, lc)

    @pl.when(i * tm < s_len)
    def _():
        two_pass([(c * tk, tk) for c in range(s_len // tk)] + [ctx_chunk])

    @pl.when(i * tm >= s_len)
    def _():
        two_pass([ctx_chunk])


def _gqa_call(qt, k, vt, s_len, n_query):
    bsz, qc, t_all = qt.shape
    hd = HEAD_DIM
    tm = TOK_TILE
    tk = s_len
    lc = t_all - s_len
    return pl.pallas_call(
        functools.partial(_gqa_kernel, s_len, lc, tk),
        grid=(bsz, GQA_KV_HEADS, n_query // tm),
        in_specs=[pl.BlockSpec((None, GQA_GROUP * hd, tm), lambda b, g, i: (b, g, i)),
                  pl.BlockSpec((None, None, t_all, hd), lambda b, g, i: (b, g, 0, 0)),
                  pl.BlockSpec((None, hd, t_all), lambda b, g, i: (b, g, 0))],
        out_specs=pl.BlockSpec((None, GQA_GROUP * hd, tm), lambda b, g, i: (b, g, i)),
        out_shape=jax.ShapeDtypeStruct((bsz, qc, t_all), BF16),
        scratch_shapes=[pltpu.VMEM((GQA_GROUP, t_all, tm), F32)],
        name="gqa_attention",
        compiler_params=_cparams(("parallel", "parallel", "parallel")),
    )(qt, k, vt)


def _na_bias_consts(rows):
    assert rows >= NA_WIN + NA_ROWS
    nrow, ncol = 2 * NA_KH - 1, 2 * NA_KW - 1
    rsel = np.zeros((3, NA_ROWS, NA_WIN, nrow), np.float32)
    mrow = np.zeros((3, NA_ROWS, NA_WIN), bool)
    for kind, r0 in enumerate((0, NA_ROWS, rows - NA_ROWS)):
        kr0 = int(np.clip(r0 - NA_KH // 2, 0, rows - NA_WIN))
        for qr in range(NA_ROWS):
            r = r0 + qr
            rs = int(np.clip(r - NA_KH // 2, 0, rows - NA_KH))
            for kr in range(NA_WIN):
                ka = kr0 + kr
                if rs <= ka < rs + NA_KH:
                    mrow[kind, qr, kr] = True
                    rsel[kind, qr, kr, ka - r + NA_KH - 1] = 1.0
    csel = np.zeros((GRID_W, GRID_W, ncol), np.float32)
    mcol = np.zeros((GRID_W, GRID_W), bool)
    for w in range(GRID_W):
        cs = int(np.clip(w - NA_KW // 2, 0, GRID_W - NA_KW))
        for kc in range(cs, cs + NA_KW):
            mcol[w, kc] = True
            csel[w, kc, kc - w + NA_KW - 1] = 1.0
    mask = mrow[:, :, None, :, None] & mcol[None, None, :, None, :]
    return rsel, csel, mask.reshape(3, TOK_TILE, NA_WIN * GRID_W)


def _na_bias_tables(na_rpb, rows):
    rsel, csel, mask = _na_bias_consts(rows)
    depth, heads = na_rpb.shape[:2]
    a = jnp.einsum("kqra,lhab->lkhqrb", jnp.asarray(rsel), na_rpb.astype(F32) * LOG2E, precision=HIGHEST)
    t = jnp.einsum("lkhqrb,wcb->lkhrcqw", a, jnp.asarray(csel), precision=HIGHEST)
    t = t.reshape(depth, 3, heads, NA_WIN * GRID_W, TOK_TILE)
    mask_t = np.swapaxes(mask, 1, 2)
    return jnp.where(jnp.asarray(mask_t)[None, :, None], t, NEG)


def _na_kernel(s_len, lc, rows, q_ref, k_ref, vt_ref, tab_ref, o_ref, s_scr):
    i = pl.program_id(1)
    tm = q_ref.shape[1]
    nblk = s_len // tm
    sub = SUBLANES

    def scores(h, part, off, m):
        r0, _, nrow, bias = part
        s = _dot(k_ref[h, pl.ds(r0, nrow), :], q_ref[h * HEAD_DIM:(h + 1) * HEAD_DIM, :])
        if bias is not None:
            s = s + bias[h]
        s_scr[h, off:off + nrow, :] = s
        return jnp.maximum(m, jnp.max(s.reshape(nrow // sub, sub, tm), axis=0))

    def apply(h, part, off, m, l, acc):
        _, c0, nrow, _ = part
        p = jnp.exp2(s_scr[h, off:off + nrow, :].reshape(nrow // sub, sub, tm) - m[None])
        l = l + jnp.sum(p, axis=0)
        vt = jnp.concatenate([vt_ref[h, c0 + c] for c in range(nrow // tm)], axis=1)
        return l, acc + _dot(vt, p.reshape(nrow, tm).astype(BF16))

    def run(parts):
        offs = [sum(p[2] for p in parts[:n]) for n in range(len(parts))]
        maxes = [None] * NA_HEADS
        for stage in range(NA_HEADS + 1):
            h1, h2 = stage, stage - 1
            if h1 < NA_HEADS:
                m1 = jnp.full((sub, tm), NEG, F32)
            if h2 >= 0:
                m2 = jnp.broadcast_to(jnp.max(maxes[h2], axis=0, keepdims=True), (sub, tm))
                l = jnp.zeros((sub, tm), F32)
                acc = jnp.zeros((HEAD_DIM, tm), F32)
            for part, off in zip(parts, offs):
                if h1 < NA_HEADS:
                    m1 = scores(h1, part, off, m1)
                if h2 >= 0:
                    l, acc = apply(h2, part, off, m2, l, acc)
            if h1 < NA_HEADS:
                maxes[h1] = m1
            if h2 >= 0:
                o = acc / jnp.sum(l, axis=0, keepdims=True)
                o_ref[h2 * HEAD_DIM:(h2 + 1) * HEAD_DIM, :] = o.astype(o_ref.dtype)

    ctx_part = (s_len, s_len // tm, lc, None)

    @pl.when(i < nblk)
    def _():
        kind = jnp.where(i == 0, 0, jnp.where(i == nblk - 1, 2, 1))
        c0 = jnp.clip(i - (NA_KH // 2) // NA_ROWS, 0, (rows - NA_WIN) // NA_ROWS)
        ks = pl.multiple_of(c0 * tm, tm)
        run([(ks, c0, NA_WIN * GRID_W, tab_ref.at[kind]), ctx_part])

    @pl.when(i >= nblk)
    def _():
        run([ctx_part])


def _na_call(nqt, nk, nvt, table, s_len, n_query):
    bsz, heads, t_all, hd = nk.shape
    tm = TOK_TILE
    rows = s_len // GRID_W
    assert NA_ROWS * GRID_W == tm and (NA_KH // 2) % NA_ROWS == 0 and (rows - NA_WIN) % NA_ROWS == 0
    return pl.pallas_call(
        functools.partial(_na_kernel, s_len, t_all - s_len, rows),
        grid=(bsz, n_query // tm),
        in_specs=[pl.BlockSpec((None, heads * hd, tm), lambda b, i: (b, 0, i)),
                  pl.BlockSpec((None, heads, t_all, hd), lambda b, i: (b, 0, 0, 0)),
                  pl.BlockSpec((None,) + nvt.shape[1:], lambda b, i: (b, 0, 0, 0, 0)),
                  pl.BlockSpec(table.shape, lambda b, i: (0, 0, 0, 0), pipeline_mode=pl.Buffered(1))],
        out_specs=pl.BlockSpec((None, heads * hd, tm), lambda b, i: (b, 0, i)),
        out_shape=jax.ShapeDtypeStruct((bsz, heads * hd, t_all), BF16),
        scratch_shapes=[pltpu.VMEM((heads, NA_WIN * GRID_W + t_all - s_len, tm), F32)],
        name="neighbourhood_attention",
        compiler_params=_cparams(("parallel", "parallel")),
    )(nqt, nk, nvt, table)


def _hy_filter_kernel(z_ref, t_ref, side_ref, w1_ref, b1_ref, w2_ref, b2_ref, w3_ref, b3_ref,
                      wo_ref, fr_ref, dl_ref, o_ref):
    om = fr_ref[...]
    h = jnp.sin(om * (_dot(z_ref[...], w1_ref[...], precision=HIGHEST) + b1_ref[...]))
    h = jnp.sin(om * (_dot(h, w2_ref[...], precision=HIGHEST) + b2_ref[...]))
    h = jnp.sin(om * (_dot(h, w3_ref[...], precision=HIGHEST) + b3_ref[...]))
    ht = _dot_nt(wo_ref[...], h, precision=HIGHEST)
    c = o_ref.shape[0]
    decay = jnp.exp(-t_ref[...] * jnp.abs(dl_ref[...]))
    side = side_ref[...]
    o_ref[...] = decay * (jnp.where(side > 0.5, ht[:c], 0.0) + jnp.where(side < -0.5, ht[c:], 0.0))


def _hy_filter_consts(length):
    n = np.arange(2 * length)
    pos = np.where(n < length, n, 2 * length - n)
    pos = np.where(n == length, 0, pos)
    side = np.where(n < length, 1.0, -1.0)
    side = np.where(n == length, 0.0, side)
    t = np.linspace(0.0, 1.0, length)[pos]
    bands = (HY_EMB_DIM - 1) // 2
    w = 2.0 * math.pi * pos / length
    fr = np.linspace(1e-4, bands - 1, bands)
    z = np.concatenate([t[:, None], np.cos(fr[None, :] * w[:, None]), -np.sin(fr[None, :] * w[:, None])], axis=-1)
    zp = np.zeros((2 * length, HY_EMB_PAD), np.float32)
    zp[:, :HY_EMB_DIM] = z
    return zp, t.astype(np.float32)[None, :], side.astype(np.float32)[None, :]


def _hy_filter_call(length, w1, b1, w2, b2, w3, b3, wout, freq):
    f = w2.shape[0]
    c = wout.shape[1] // 2
    zp, t, side = _hy_filter_consts(length)
    w1p = jnp.zeros((HY_EMB_PAD, f), F32).at[:HY_EMB_DIM].set(w1)
    deltas = np.linspace(math.log(HY_DECAY_TARGET) / HY_SLOW_DECAY,
                         math.log(HY_DECAY_TARGET) / HY_FAST_DECAY, c).astype(np.float32)[:, None]
    args = (jnp.asarray(zp), jnp.asarray(t), jnp.asarray(side), w1p, b1[None], w2, b2[None], w3, b3[None],
            wout.T, freq[None], jnp.asarray(deltas))
    return pl.pallas_call(
        _hy_filter_kernel,
        in_specs=[pl.BlockSpec(a.shape, lambda: (0,) * a.ndim) for a in args],
        out_specs=pl.BlockSpec((c, 2 * length), lambda: (0, 0)),
        out_shape=jax.ShapeDtypeStruct((c, 2 * length), F32),
        grid=(),
        name="hyena_filter",
        compiler_params=pltpu.CompilerParams(vmem_limit_bytes=VMEM_LIMIT),
    )(*args)


def _bitrev(n):
    bits = int(math.log2(n))
    return np.array([int(format(i, "0%db" % bits)[::-1], 2) if bits else 0 for i in range(n)])


def _fft_consts(n1, ct):
    n = n1 * FFT_CHUNK
    stages = max(int(math.log2(n1)), 1)
    tw = np.zeros((2, stages, max(n1 // 2, 1)), np.float32)
    for s in range(int(math.log2(n1))):
        half = n1 >> (s + 1)
        ang = -2.0 * math.pi * np.arange(half) / (2 * half)
        tw[0, s, :half] = np.cos(ang)
        tw[1, s, :half] = np.sin(ang)
    k1 = _bitrev(n1)
    ang = -2.0 * math.pi * (np.arange(FFT_CHUNK)[None, :] * k1[:, None]) / n
    twr = np.repeat(np.cos(ang), ct, axis=0).astype(np.float32)
    twi = np.repeat(np.sin(ang), ct, axis=0).astype(np.float32)
    a = -2.0 * math.pi * np.outer(np.arange(FFT_CHUNK), np.arange(FFT_CHUNK)) / FFT_CHUNK
    wr, wi = np.cos(a), np.sin(a)
    wf = np.block([[wr, wi], [-wi, wr]]).astype(np.float32)
    wb = np.block([[wr, -wi], [wi, wr]]).astype(np.float32)
    return tw, twr, twi, _split_bf16(wf), _split_bf16(wb)


def _split_bf16(w):
    w = jnp.asarray(w, F32)
    hi = w.astype(BF16)
    return jnp.stack([hi, (w - hi.astype(F32)).astype(BF16)])


def _slab_fft(zr, zi, tw_ref, n1, ct, inverse, half_input=False, half_output=False):
    nst = int(math.log2(n1))
    order = range(nst - 1, -1, -1) if inverse else range(nst)
    for s in order:
        half = n1 >> (s + 1)
        lh = int(math.log2(half))
        first = s == 0

        def body(idx, carry, s=s, half=half, lh=lh, first=first):
            blk = idx >> lh
            j = idx & (half - 1)
            r0 = pl.multiple_of(((blk << (lh + 1)) + j) * ct, ct)
            r1 = pl.multiple_of(r0 + half * ct, ct)
            wr = tw_ref[0, s, j]
            wi = tw_ref[1, s, j]
            ar = zr[pl.ds(r0, ct), :]
            ai = zi[pl.ds(r0, ct), :]
            if not inverse:
                if first and half_input:
                    zr[pl.ds(r1, ct), :] = ar * wr - ai * wi
                    zi[pl.ds(r1, ct), :] = ar * wi + ai * wr
                else:
                    br = zr[pl.ds(r1, ct), :]
                    bi = zi[pl.ds(r1, ct), :]
                    dr = ar - br
                    di = ai - bi
                    zr[pl.ds(r0, ct), :] = ar + br
                    zi[pl.ds(r0, ct), :] = ai + bi
                    zr[pl.ds(r1, ct), :] = dr * wr - di * wi
                    zi[pl.ds(r1, ct), :] = dr * wi + di * wr
            else:
                br = zr[pl.ds(r1, ct), :]
                bi = zi[pl.ds(r1, ct), :]
                tr = br * wr + bi * wi
                ti = bi * wr - br * wi
                zr[pl.ds(r0, ct), :] = ar + tr
                zi[pl.ds(r0, ct), :] = ai + ti
                if not (first and half_output):
                    zr[pl.ds(r1, ct), :] = ar - tr
                    zi[pl.ds(r1, ct), :] = ai - ti
            return carry

        lax.fori_loop(0, n1 // 2, body, 0, unroll=min(4, n1 // 2))


def _chunk_dft(xr, xi, w_ref, split):
    x = jnp.concatenate([xr, xi], axis=1)
    x_hi = x.astype(BF16)
    y = _dot(x_hi, w_ref[0])
    if split:
        x_lo = (x - x_hi.astype(F32)).astype(BF16)
        y = y + (_dot(x_lo, w_ref[0]) + _dot(x_hi, w_ref[1]))
    return y[:, :FFT_CHUNK], y[:, FFT_CHUNK:]


def _fft_groups(n1, ct):
    g = min(8, n1)
    return n1 // g, g * ct


def _hy_spec_kernel(n1, ct, f_ref, tw_ref, twr_ref, twi_ref, wf_ref, fr_ref, fi_ref, zr, zi):
    for s in range(n1):
        zr[s * ct:(s + 1) * ct, :] = f_ref[:, s * FFT_CHUNK:(s + 1) * FFT_CHUNK]
    zi[...] = jnp.zeros(zi.shape, F32)
    _slab_fft(zr, zi, tw_ref, n1, ct, inverse=False)
    ngroups, gr = _fft_groups(n1, ct)

    def body(g, carry):
        r0 = pl.multiple_of(g * gr, gr)
        a = zr[pl.ds(r0, gr), :]
        b = zi[pl.ds(r0, gr), :]
        tr = twr_ref[pl.ds(r0, gr), :]
        ti = twi_ref[pl.ds(r0, gr), :]
        yr, yi = _chunk_dft(a * tr - b * ti, a * ti + b * tr, wf_ref, split=True)
        fr_ref[pl.ds(r0, gr), :] = yr
        fi_ref[pl.ds(r0, gr), :] = yi
        return carry

    lax.fori_loop(0, ngroups, body, 0, unroll=min(2, ngroups))


def _hy_spec_call(filt_t):
    c, n = filt_t.shape
    n1 = n // FFT_CHUNK
    ct = HY_CT
    tw, twr, twi, wf, _ = _fft_consts(n1, ct)
    const2 = lambda i: (0, 0)
    out = jax.ShapeDtypeStruct((c // ct, n1 * ct, FFT_CHUNK), F32)
    ospec = pl.BlockSpec((None, n1 * ct, FFT_CHUNK), lambda i: (i, 0, 0))
    return pl.pallas_call(
        functools.partial(_hy_spec_kernel, n1, ct),
        grid=(c // ct,),
        in_specs=[pl.BlockSpec((ct, n), lambda i: (i, 0)),
                  pl.BlockSpec(memory_space=pltpu.SMEM),
                  pl.BlockSpec(twr.shape, const2), pl.BlockSpec(twi.shape, const2),
                  pl.BlockSpec(wf.shape, lambda i: (0, 0, 0))],
        out_specs=[ospec, ospec],
        out_shape=[out, out],
        scratch_shapes=[pltpu.VMEM((n1 * ct, FFT_CHUNK), F32), pltpu.VMEM((n1 * ct, FFT_CHUNK), F32)],
        name="hyena_filter_spectrum",
        compiler_params=_cparams(("parallel",)),
    )(filt_t, jnp.asarray(tw), jnp.asarray(twr), jnp.asarray(twi), jnp.asarray(wf))


def _hy_conv_kernel(n1, ct, length, v_ref, x1_ref, x2_ref, cwv_ref, cw1_ref, cw2_ref, skip_ref,
                    fr_ref, fi_ref, tw_ref, twr_ref, twi_ref, wf_ref, wb_ref, o_ref,
                    zr, zi, u_scr, g_scr):
    lane = lax.broadcasted_iota(jnp.int32, (1, length), 1)

    sub = SUBLANES

    def short_conv(x, cw):
        xm = jnp.where(lane >= 1, pltpu.roll(x, 1, 1), 0.0)
        xp = jnp.where(lane <= length - 2, pltpu.roll(x, length - 1, 1), 0.0)
        return xm * cw[:, 0:1] + x * cw[:, 1:2] + xp * cw[:, 2:3] + cw[:, 3:4]

    def conv_rows(r, carry):
        rows = pl.ds(pl.multiple_of(r * sub, sub), sub)
        cwv, cw1, cw2 = cwv_ref[rows, :], cw1_ref[rows, :], cw2_ref[rows, :]
        for a in range(2):
            u_scr[a, rows, :] = short_conv(v_ref[a, rows, :], cwv) * short_conv(x1_ref[a, rows, :], cw1)
            g_scr[a, rows, :] = short_conv(x2_ref[a, rows, :], cw2)
        return carry

    lax.fori_loop(0, ct // sub, conv_rows, 0)
    nz = length // FFT_CHUNK
    for s in range(nz):
        zr[s * ct:(s + 1) * ct, :] = u_scr[0, :, s * FFT_CHUNK:(s + 1) * FFT_CHUNK]
        zi[s * ct:(s + 1) * ct, :] = u_scr[1, :, s * FFT_CHUNK:(s + 1) * FFT_CHUNK]
    _slab_fft(zr, zi, tw_ref, n1, ct, inverse=False, half_input=True)
    ngroups, gr = _fft_groups(n1, ct)

    def body(g, carry):
        r0 = pl.multiple_of(g * gr, gr)
        a = zr[pl.ds(r0, gr), :]
        b = zi[pl.ds(r0, gr), :]
        tr = twr_ref[pl.ds(r0, gr), :]
        ti = twi_ref[pl.ds(r0, gr), :]
        yr, yi = _chunk_dft(a * tr - b * ti, a * ti + b * tr, wf_ref, split=False)
        fr = fr_ref[pl.ds(r0, gr), :]
        fi = fi_ref[pl.ds(r0, gr), :]
        qr, qi = _chunk_dft(yr * fr - yi * fi, yr * fi + yi * fr, wb_ref, split=False)
        zr[pl.ds(r0, gr), :] = qr * tr + qi * ti
        zi[pl.ds(r0, gr), :] = qi * tr - qr * ti
        return carry

    lax.fori_loop(0, ngroups, body, 0, unroll=min(4, ngroups))
    _slab_fft(zr, zi, tw_ref, n1, ct, inverse=True, half_output=True)
    inv_n = 1.0 / (n1 * FFT_CHUNK)
    skip = skip_ref[...]
    for s in range(nz):
        sl = slice(s * FFT_CHUNK, (s + 1) * FFT_CHUNK)
        for a, z in ((0, zr), (1, zi)):
            y = z[s * ct:(s + 1) * ct, :] * inv_n
            u = u_scr[a, :, sl]
            o_ref[a, :, sl] = (g_scr[a, :, sl] * (y + u * skip)).astype(o_ref.dtype)


def _hy_conv_call(hy_t, spec_r, spec_i, cw, skip, length, lane_block, prev=None):
    _, bh, c3, t_all = hy_t.shape
    c = c3 // 3
    ct = HY_CT
    nct = c // ct
    n1 = 2 * length // FFT_CHUNK
    tw, twr, twi, wf, wb = _fft_consts(n1, ct)
    const2 = lambda p, i: (0, 0)
    xin = lambda k: pl.BlockSpec((2, None, ct, length), lambda p, i, k=k: (0, p, k * nct + i, lane_block))
    cwin = lambda k: pl.BlockSpec((ct, 4), lambda p, i, k=k: (k * nct + i, 0))
    fin = pl.BlockSpec((None, n1 * ct, FFT_CHUNK), lambda p, i: (i, 0, 0))
    in_specs = [xin(0), xin(1), xin(2), cwin(0), cwin(1), cwin(2),
                pl.BlockSpec((ct, 1), lambda p, i: (i, 0)),
                fin, fin,
                pl.BlockSpec(memory_space=pltpu.SMEM),
                pl.BlockSpec(twr.shape, const2), pl.BlockSpec(twi.shape, const2),
                pl.BlockSpec(wf.shape, lambda p, i: (0, 0, 0)),
                pl.BlockSpec(wb.shape, lambda p, i: (0, 0, 0))]
    args = [hy_t, hy_t, hy_t, cw, cw, cw, skip, spec_r, spec_i, jnp.asarray(tw), jnp.asarray(twr),
            jnp.asarray(twi), jnp.asarray(wf), jnp.asarray(wb)]
    kern = functools.partial(_hy_conv_kernel, n1, ct, length)
    aliases = {}
    if prev is not None:
        in_specs.append(pl.BlockSpec(memory_space=pl.ANY))
        args.append(prev)
        aliases = {len(args) - 1: 0}
        kern = lambda *refs, kern=kern: kern(*refs[:14], *refs[15:])
    return pl.pallas_call(
        kern,
        grid=(bh, nct),
        in_specs=in_specs,
        out_specs=pl.BlockSpec((2, None, ct, length), lambda p, i: (0, p, i, lane_block)),
        out_shape=jax.ShapeDtypeStruct((2, bh, c, t_all), BF16),
        scratch_shapes=[pltpu.VMEM((n1 * ct, FFT_CHUNK), F32), pltpu.VMEM((n1 * ct, FFT_CHUNK), F32),
                        pltpu.VMEM((2, ct, length), F32), pltpu.VMEM((2, ct, length), F32)],
        input_output_aliases=aliases,
        name="hyena_conv_%d" % length,
        compiler_params=_cparams(("parallel", "parallel")),
    )(*args)


def _outproj_kernel(n_lat_tiles, ctx_row, x_ref, hy_ref, ga_ref, na_ref, w_ref, mod_ref, o_ref):
    j = pl.program_id(1)
    nbat, tm, d = x_ref.shape
    mix = []
    for s in range(nbat):
        cm = jnp.concatenate([hy_ref[s], ga_ref[s], na_ref[s]], axis=0)
        mix.append(cm.astype(F32).T.astype(BF16))
    y = _dot(jnp.concatenate(mix, axis=0), w_ref[...])
    for s in range(nbat):
        row = jnp.where(j < n_lat_tiles, pl.program_id(0) * nbat + s, ctx_row)
        o_ref[s] = x_ref[s] + _mod_row(mod_ref, row, 2, d) * y[s * tm:(s + 1) * tm]


def _outproj_call(xa, hyo_t, gqa, na, w_out, mod_l, s_len, n_tok):
    bsz, _, d = xa.shape
    tm = TOK_TILE
    nbat = _batch_group(bsz)
    const = lambda b, j: (0, 0)
    tok = lambda n: pl.BlockSpec((nbat, tm, n), lambda b, j: (b, j, 0))
    chan = lambda n: pl.BlockSpec((nbat, n, tm), lambda b, j: (b, 0, j))
    return pl.pallas_call(
        functools.partial(_outproj_kernel, s_len // tm, bsz),
        grid=(bsz // nbat, n_tok // tm),
        in_specs=[tok(d), chan(hyo_t.shape[1]), chan(gqa.shape[1]), chan(na.shape[1]),
                  pl.BlockSpec(w_out.shape, const), pl.BlockSpec(mod_l.shape, const)],
        out_specs=tok(d),
        out_shape=jax.ShapeDtypeStruct((bsz, n_tok, d), F32),
        name="out_proj",
        compiler_params=_cparams(("parallel", "parallel")),
    )(xa, hyo_t, gqa, na, w_out, mod_l)


def _ffn_kernel(tiles_per_batch, n_lat_tiles, ctx_row, th, x_ref, mod_ref, g_ref, wi_ref, wo_ref,
                gf_ref, o_ref, h_scr, act_scr):
    i = pl.program_id(0)
    d = x_ref.shape[-1]
    f = wo_ref.shape[0]
    groups = x_ref.shape[0] // TOK_TILE

    def mod_row(s):
        gi = i * groups + s
        jj = gi % tiles_per_batch
        return jnp.where(jj < n_lat_tiles, gi // tiles_per_batch, ctx_row)

    for s in range(groups):
        row = mod_row(s)
        sl = slice(s * TOK_TILE, (s + 1) * TOK_TILE)
        h = _norm_mod(x_ref[sl, :], g_ref[...], _mod_row(mod_ref, row, 3, d), _mod_row(mod_ref, row, 4, d))
        h_scr[sl, :] = h.astype(BF16)
    h = h_scr[...]
    for j in range(f // th):
        gate = _dot(h, wi_ref[:, j * th:(j + 1) * th])
        up = _dot(h, wi_ref[:, f + j * th:f + (j + 1) * th])
        act_scr[:, j * th:(j + 1) * th] = (gate * jax.nn.sigmoid(gate) * up).astype(BF16)
    y = _dot(act_scr[...], wo_ref[...])
    for s in range(groups):
        row = mod_row(s)
        sl = slice(s * TOK_TILE, (s + 1) * TOK_TILE)
        out = x_ref[sl, :] + _mod_row(mod_ref, row, 5, d) * y[sl, :]
        if gf_ref is not None:
            ms = jnp.mean(out * out, axis=-1, keepdims=True)
            out = out * lax.rsqrt(ms + NORM_EPS) * gf_ref[...]
        o_ref[sl, :] = out


def _ffn_call(xa, mod_l, g, w_in, w_out, s_len, g_final=None):
    bsz, t_all, d = xa.shape
    f = w_out.shape[0]
    r = bsz * t_all
    tm = 1024 if r % 1024 == 0 else TOK_TILE
    th = 256
    const = lambda i: (0, 0)
    resident = lambda a: pl.BlockSpec(a.shape, const, pipeline_mode=pl.Buffered(1))
    in_specs = [pl.BlockSpec((tm, d), lambda i: (i, 0)),
                pl.BlockSpec(mod_l.shape, const),
                pl.BlockSpec((1, d), const),
                resident(w_in), resident(w_out)]
    args = [xa.reshape(r, d), mod_l, g, w_in, w_out]
    kern = functools.partial(_ffn_kernel, t_all // TOK_TILE, s_len // TOK_TILE, bsz, th)
    if g_final is None:
        kern = functools.partial(_ffn_no_final, kern)
    else:
        in_specs.append(pl.BlockSpec((1, d), const))
        args.append(g_final)
    out = pl.pallas_call(
        kern,
        grid=(r // tm,),
        in_specs=in_specs,
        out_specs=pl.BlockSpec((tm, d), lambda i: (i, 0)),
        out_shape=jax.ShapeDtypeStruct((r, d), F32),
        scratch_shapes=[pltpu.VMEM((tm, d), BF16), pltpu.VMEM((tm, f), BF16)],
        name="swiglu_ffn",
        compiler_params=_cparams(("parallel",)),
    )(*args)
    return out.reshape(bsz, t_all, d)


def _ffn_no_final(kern, x_ref, mod_ref, g_ref, wi_ref, wo_ref, o_ref, h_scr, act_scr):
    kern(x_ref, mod_ref, g_ref, wi_ref, wo_ref, None, o_ref, h_scr, act_scr)


def _rope_tables(s_len, lc):
    pos = np.arange(s_len)
    row = (pos // GRID_W).astype(np.float64)
    col = (pos % GRID_W).astype(np.float64)
    n_f = HEAD_DIM // 4
    inv = (ROPE_THETA ** (-np.arange(n_f, dtype=np.float32) / n_f)).astype(np.float64)
    ang = np.concatenate([row[:, None] * inv, col[:, None] * inv], axis=-1).astype(np.float32).astype(np.float64)
    lane = np.arange(LANES)
    sign = np.where((lane % HEAD_DIM) < HEAD_DIM // 2, -1.0, 1.0)
    cos = np.cos(ang)[:, lane % (HEAD_DIM // 2)]
    sin = np.sin(ang)[:, lane % (HEAD_DIM // 2)] * sign[None, :]
    cos = np.concatenate([cos, np.ones((lc, LANES))], axis=0).astype(np.float32)
    sin = np.concatenate([sin, np.zeros((lc, LANES))], axis=0).astype(np.float32)
    cosc = np.concatenate([np.cos(ang).T, np.ones((HEAD_DIM // 2, lc))], axis=1).astype(np.float32)
    sinc = np.concatenate([np.sin(ang).T, np.zeros((HEAD_DIM // 2, lc))], axis=1).astype(np.float32)
    return jnp.asarray(cosc), jnp.asarray(sinc), jnp.asarray(cos), jnp.asarray(sin)


def kernel(x, c, ctx, c_ctx, w_mod, b_mod, g_mix, g_ffn, w_in, w_out, hy_conv_w, hy_conv_b,
           hy_f_w1, hy_f_b1, hy_f_w2, hy_f_b2, hy_f_w3, hy_f_b3, hy_f_wout, hy_f_freq, hy_skip,
           qk_g_q, qk_g_k, na_rpb, w_ffn_in, w_ffn_out, g_final):
    bsz, s_len, d = x.shape
    lc = ctx.shape[1]
    depth = w_mod.shape[0]
    t_all = s_len + lc
    hyw = hy_skip.shape[1]
    hy_cols = 3 * hyw
    assert bsz % 2 == 0 and s_len % TOK_TILE == 0 and lc == TOK_TILE and s_len % lc == 0

    nb = -(-(bsz + 1) // 8) * 8
    cs = jnp.zeros((nb, d), F32).at[:bsz].set(c).at[bsz].set(c_ctx)
    mod = _mod_call(cs, w_mod, b_mod)

    rope = _rope_tables(s_len, lc)
    tables = _na_bias_tables(na_rpb, s_len // GRID_W)
    q0 = hy_cols
    k0 = q0 + GQA_HEADS * HEAD_DIM
    v0 = k0 + GQA_KV_HEADS * HEAD_DIM
    nq0 = v0 + GQA_KV_HEADS * HEAD_DIM
    nk0 = nq0 + NA_HEADS * HEAD_DIM
    nv0 = nk0 + NA_HEADS * HEAD_DIM
    w_cm = jnp.concatenate([w_in[:, :, :k0], w_in[:, :, v0:nk0], w_in[:, :, nv0:]], axis=2)
    w_cm = jnp.swapaxes(w_cm, 1, 2).astype(BF16)
    w_tm = jnp.concatenate([w_in[:, :, k0:v0], w_in[:, :, nk0:nv0]], axis=2).astype(BF16)
    w_out_b = w_out.astype(BF16)
    w_fi = w_ffn_in.astype(BF16)
    w_fo = w_ffn_out.astype(BF16)
    cw = jnp.concatenate([jnp.swapaxes(hy_conv_w, 1, 2), hy_conv_b[:, :, None]], axis=2)
    per = LANES // HEAD_DIM

    xa = jnp.concatenate([x, ctx], axis=1)
    for l in range(depth):
        last = l == depth - 1
        gq_b = jnp.broadcast_to(qk_g_q[l][:, None], (HEAD_DIM, LANES))
        gk2 = jnp.tile(qk_g_k[l], per)[None]
        hy_t, q, k, v, nq, nk, nv = _inproj_call(xa, mod[l], g_mix[l][None], w_cm[l], w_tm[l],
                                                  gq_b, gk2, rope, hy_cols, s_len)
        n_tok = s_len if last else t_all
        gqa = _gqa_call(q, k, v, s_len, n_tok)
        na = _na_call(nq, nk, nv, tables[l], s_len, n_tok)

        fargs = (hy_f_w1[l], hy_f_b1[l], hy_f_w2[l], hy_f_b2[l], hy_f_w3[l], hy_f_b3[l],
                 hy_f_wout[l], hy_f_freq[l])
        hy_p = hy_t.reshape(2, bsz // 2, hy_cols, t_all)
        skip = hy_skip[l][:, None]
        fr, fi = _hy_spec_call(_hy_filter_call(s_len, *fargs))
        hyo = _hy_conv_call(hy_p, fr, fi, cw[l], skip, s_len, 0)
        if not last:
            frc, fic = _hy_spec_call(_hy_filter_call(lc, *fargs))
            hyo = _hy_conv_call(hy_p, frc, fic, cw[l], skip, lc, s_len // lc, prev=hyo)
        hyo = hyo.reshape(bsz, hyw, t_all)

        xa = _outproj_call(xa, hyo, gqa, na, w_out_b[l], mod[l], s_len, n_tok)
        xa = _ffn_call(xa, mod[l], g_ffn[l][None], w_fi[l], w_fo[l], s_len,
                       g_final[None] if last else None)
    return xa
```

```python
import functools
import math

import numpy as np
import jax
import jax.numpy as jnp
from jax import lax
from jax.experimental import pallas as pl
from jax.experimental.pallas import tpu as pltpu

F32 = jnp.float32
BF16 = jnp.bfloat16
HIGHEST = lax.Precision.HIGHEST

GRID_W = 64
HEAD_DIM = 64
GQA_HEADS = 8
GQA_KV_HEADS = 2
GQA_GROUP = GQA_HEADS // GQA_KV_HEADS
NA_HEADS = 4
NA_KH = 8
NA_KW = 16
ROPE_THETA = 10000.0
HY_EMB_DIM = 33
HY_DECAY_TARGET = 1e-2
HY_FAST_DECAY = 0.3
HY_SLOW_DECAY = 1.5
NORM_EPS = 1e-6
ATTN_SCALE = HEAD_DIM ** -0.5
NEG = -1e30
LOG2E = math.log2(math.e)

LANES = 128
SUBLANES = 8
HY_EMB_PAD = 64
TOK_TILE = 256
FFT_CHUNK = LANES
HY_CT = 64
NA_ROWS = TOK_TILE // GRID_W
NA_WIN = NA_ROWS + NA_KH
VMEM_LIMIT = 56 * 1024 * 1024


def _cparams(sem):
    return pltpu.CompilerParams(dimension_semantics=sem, vmem_limit_bytes=VMEM_LIMIT)


def _dot(a, b, **kw):
    return jnp.dot(a, b, preferred_element_type=F32, **kw)


def _dot_nt(a, b, **kw):
    return lax.dot_general(a, b, (((1,), (1,)), ((), ())), preferred_element_type=F32, **kw)


def _mod_kernel(cs_ref, w_ref, b_ref, o_ref):
    cs = cs_ref[...]
    s = cs * jax.nn.sigmoid(cs)
    o_ref[...] = _dot(s, w_ref[...], precision=HIGHEST) + b_ref[...]


def _mod_call(cs, w_mod, b_mod):
    depth, d, n = w_mod.shape
    nb = cs.shape[0]
    tn = 1024
    return pl.pallas_call(
        _mod_kernel,
        grid=(depth, n // tn),
        in_specs=[pl.BlockSpec((nb, d), lambda l, j: (0, 0)),
                  pl.BlockSpec((None, d, tn), lambda l, j: (l, 0, j)),
                  pl.BlockSpec((None, 1, tn), lambda l, j: (l, 0, j))],
        out_specs=pl.BlockSpec((None, nb, tn), lambda l, j: (l, 0, j)),
        out_shape=jax.ShapeDtypeStruct((depth, nb, n), F32),
        name="adaln_mod",
        compiler_params=_cparams(("parallel", "parallel")),
    )(cs, w_mod, b_mod.reshape(depth, 1, n))


def _mod_row(mod_ref, row, k, d):
    return mod_ref[pl.ds(row, 1), k * d:(k + 1) * d]


def _norm_mod(x, g, shift, scale):
    ms = jnp.mean(x * x, axis=-1, keepdims=True)
    y = x * lax.rsqrt(ms + NORM_EPS) * g
    return y * (1.0 + scale) + shift


def _batch_group(bsz):
    return 4 if bsz % 4 == 0 else 2


def _inproj_kernel(n_lat_tiles, ctx_row, hyc, x_ref, mod_ref, g_ref, wcm_ref, wtm_ref, gq_ref, gk_ref,
                   cosc_ref, sinc_ref, cos_ref, sin_ref,
                   hy_ref, q_ref, k_ref, v_ref, nq_ref, nk_ref, nv_ref, h_scr):
    j = pl.program_id(1)
    nbat, tm, d = x_ref.shape
    for s in range(nbat):
        row = jnp.where(j < n_lat_tiles, pl.program_id(0) * nbat + s, ctx_row)
        h = _norm_mod(x_ref[s], g_ref[...], _mod_row(mod_ref, row, 0, d), _mod_row(mod_ref, row, 1, d))
        h_scr[s * tm:(s + 1) * tm, :] = h.astype(BF16)
    h = h_scr[...]
    cm = _dot_nt(wcm_ref[...], h)
    tmj = _dot(h, wtm_ref[...])

    half = HEAD_DIM // 2
    reps = cm.shape[1] // LANES
    gq = jnp.concatenate([gq_ref[...]] * reps, axis=1)
    cosc = jnp.concatenate([cosc_ref[...]] * nbat, axis=1)
    sinc = jnp.concatenate([sinc_ref[...]] * nbat, axis=1)
    for hh in range(GQA_HEADS):
        x = cm[hyc + hh * HEAD_DIM:hyc + (hh + 1) * HEAD_DIM, :]
        ms = jnp.mean(x * x, axis=0, keepdims=True)
        xn = x * lax.rsqrt(ms + NORM_EPS) * gq
        x1, x2 = xn[:half], xn[half:]
        y = jnp.concatenate([x1 * cosc - x2 * sinc, x1 * sinc + x2 * cosc], axis=0)
        y = (y * (ATTN_SCALE * LOG2E)).astype(BF16)
        for s in range(nbat):
            q_ref[s, hh * HEAD_DIM:(hh + 1) * HEAD_DIM, :] = y[:, s * tm:(s + 1) * tm]
    voff = hyc + GQA_HEADS * HEAD_DIM
    nqoff = voff + GQA_KV_HEADS * HEAD_DIM
    nvoff = nqoff + NA_HEADS * HEAD_DIM
    for s in range(nbat):
        cols = slice(s * tm, (s + 1) * tm)
        hy_ref[s] = cm[:hyc, cols]
        v_ref[s] = cm[voff:nqoff, cols].astype(BF16)
        nq_ref[s] = (cm[nqoff:nvoff, cols] * (ATTN_SCALE * LOG2E)).astype(BF16)
        for e in range(NA_HEADS):
            nv_ref[s, e] = cm[nvoff + e * HEAD_DIM:nvoff + (e + 1) * HEAD_DIM, cols].astype(BF16)
        _inproj_token_major(tmj[s * tm:(s + 1) * tm], gk_ref, cos_ref, sin_ref, k_ref.at[s], nk_ref.at[s])


def _inproj_token_major(p, gk_ref, cos_ref, sin_ref, k_ref, nk_ref):
    ri = lax.broadcasted_iota(jnp.int32, (LANES, LANES), 0) // HEAD_DIM
    ci = lax.broadcasted_iota(jnp.int32, (LANES, LANES), 1) // HEAD_DIM
    avg = jnp.where(ri == ci, 1.0 / HEAD_DIM, 0.0).astype(BF16)
    lane = lax.broadcasted_iota(jnp.int32, (1, LANES), 1)
    first_half = (lane % HEAD_DIM) < (HEAD_DIM // 2)
    xc = p[:, :LANES]
    ms = _dot((xc * xc).astype(BF16), avg)
    xn = xc * lax.rsqrt(ms + NORM_EPS) * gk_ref[...]
    partner = jnp.where(first_half, pltpu.roll(xn, LANES - HEAD_DIM // 2, 1),
                        pltpu.roll(xn, HEAD_DIM // 2, 1))
    y = (xn * cos_ref[...] + partner * sin_ref[...]).astype(BF16)
    for e in range(GQA_KV_HEADS):
        k_ref[e] = y[:, e * HEAD_DIM:(e + 1) * HEAD_DIM]
    for e in range(NA_HEADS):
        nk_ref[e] = p[:, LANES + e * HEAD_DIM:LANES + (e + 1) * HEAD_DIM].astype(BF16)


def _inproj_call(xa, mod_l, g, w_cm, w_tm, gq_b, gk2, rope, hyc, s_len):
    bsz, t_all, d = xa.shape
    tm = TOK_TILE
    nbat = _batch_group(bsz)
    cosc, sinc, cos_t, sin_t = rope
    grid = (bsz // nbat, t_all // tm)
    const = lambda b, j: (0, 0)
    resident = lambda a: pl.BlockSpec(a.shape, const, pipeline_mode=pl.Buffered(1))
    heads = lambda n: pl.BlockSpec((nbat, n, tm, HEAD_DIM), lambda b, j: (b, 0, j, 0))
    hshape = lambda n: jax.ShapeDtypeStruct((bsz, n, t_all, HEAD_DIM), BF16)
    chan = lambda n: pl.BlockSpec((nbat, n, tm), lambda b, j: (b, 0, j))
    qc = GQA_HEADS * HEAD_DIM
    vc = GQA_KV_HEADS * HEAD_DIM
    nc = NA_HEADS * HEAD_DIM
    return pl.pallas_call(
        functools.partial(_inproj_kernel, s_len // tm, bsz, hyc),
        grid=grid,
        in_specs=[pl.BlockSpec((nbat, tm, d), lambda b, j: (b, j, 0)),
                  pl.BlockSpec(mod_l.shape, const),
                  pl.BlockSpec((1, d), const),
                  resident(w_cm), resident(w_tm),
                  pl.BlockSpec(gq_b.shape, const),
                  pl.BlockSpec((1, LANES), const),
                  pl.BlockSpec((HEAD_DIM // 2, tm), lambda b, j: (0, j)),
                  pl.BlockSpec((HEAD_DIM // 2, tm), lambda b, j: (0, j)),
                  pl.BlockSpec((tm, LANES), lambda b, j: (j, 0)),
                  pl.BlockSpec((tm, LANES), lambda b, j: (j, 0))],
        out_specs=[chan(hyc), chan(qc), heads(GQA_KV_HEADS), chan(vc),
                   chan(nc), heads(NA_HEADS),
                   pl.BlockSpec((nbat, NA_HEADS, None, HEAD_DIM, tm), lambda b, j: (b, 0, j, 0, 0))],
        out_shape=[jax.ShapeDtypeStruct((bsz, hyc, t_all), F32),
                   jax.ShapeDtypeStruct((bsz, qc, t_all), BF16),
                   hshape(GQA_KV_HEADS),
                   jax.ShapeDtypeStruct((bsz, vc, t_all), BF16),
                   jax.ShapeDtypeStruct((bsz, nc, t_all), BF16),
                   hshape(NA_HEADS),
                   jax.ShapeDtypeStruct((bsz, NA_HEADS, t_all // tm, HEAD_DIM, tm), BF16)],
        scratch_shapes=[pltpu.VMEM((nbat * tm, d), BF16)],
        name="in_proj",
        compiler_params=_cparams(("parallel", "parallel")),
    )(xa, mod_l, g, w_cm, w_tm, gq_b, gk2, cosc, sinc, cos_t, sin_t)


def _gqa_kernel(s_len, lc, tk, q_ref, k_ref, vt_ref, o_ref, s_scr):
    i = pl.program_id(2)
    tm = q_ref.shape[1]
    sub = SUBLANES

    def finish(h, l, acc):
        o_ref[h * HEAD_DIM:(h + 1) * HEAD_DIM, :] = (acc / jnp.sum(l, axis=0, keepdims=True)).astype(o_ref.dtype)

    def scores(h, q, r0, nrow, m):
        s = _dot(k_ref[r0:r0 + nrow, :], q)
        s_scr[h, r0:r0 + nrow, :] = s
        return jnp.maximum(m, jnp.max(s.reshape(nrow // sub, sub, tm), axis=0))

    def apply(h, r0, nrow, m, l, acc):
        p = jnp.exp2(s_scr[h, r0:r0 + nrow, :].reshape(nrow // sub, sub, tm) - m[None])
        l = l + jnp.sum(p, axis=0)
        acc = acc + _dot(vt_ref[:, r0:r0 + nrow], p.reshape(nrow, tm).astype(BF16))
        return l, acc

    def two_pass(chunks):
        maxes = [None] * GQA_GROUP
        for stage in range(GQA_GROUP + 1):
            h1, h2 = stage, stage - 1
            if h1 < GQA_GROUP:
                q1 = q_ref[h1 * HEAD_DIM:(h1 + 1) * HEAD_DIM, :]
                m1 = jnp.full((sub, tm), NEG, F32)
            if h2 >= 0:
                m2 = jnp.broadcast_to(jnp.max(maxes[h2], axis=0, keepdims=True), (sub, tm))
                l = jnp.zeros((sub, tm), F32)
                acc = jnp.zeros((HEAD_DIM, tm), F32)
            for r0, nrow in chunks:
                if h1 < GQA_GROUP:
                    m1 = scores(h1, q1, r0, nrow, m1)
                if h2 >= 0:
                    l, acc = apply(h2, r0, nrow, m2, l, acc)
            if h1 < GQA_GROUP:
                maxes[h1] = m1
            if h2 >= 0:
                finish(h2, l, acc)

    ctx_chunk = (s_len, lc)

    @pl.when(i * tm < s_len)
    def _():
        two_pass([(c * tk, tk) for c in range(s_len // tk)] + [ctx_chunk])

    @pl.when(i * tm >= s_len)
    def _():
        two_pass([ctx_chunk])


def _gqa_call(qt, k, vt, s_len, n_query):
    bsz, qc, t_all = qt.shape
    hd = HEAD_DIM
    tm = TOK_TILE
    tk = min(8 * TOK_TILE, s_len)
    lc = t_all - s_len
    return pl.pallas_call(
        functools.partial(_gqa_kernel, s_len, lc, tk),
        grid=(bsz, GQA_KV_HEADS, n_query // tm),
        in_specs=[pl.BlockSpec((None, GQA_GROUP * hd, tm), lambda b, g, i: (b, g, i)),
                  pl.BlockSpec((None, None, t_all, hd), lambda b, g, i: (b, g, 0, 0)),
                  pl.BlockSpec((None, hd, t_all), lambda b, g, i: (b, g, 0))],
        out_specs=pl.BlockSpec((None, GQA_GROUP * hd, tm), lambda b, g, i: (b, g, i)),
        out_shape=jax.ShapeDtypeStruct((bsz, qc, t_all), BF16),
        scratch_shapes=[pltpu.VMEM((GQA_GROUP, t_all, tm), F32)],
        name="gqa_attention",
        compiler_params=_cparams(("parallel", "parallel", "parallel")),
    )(qt, k, vt)


def _na_bias_consts(rows):
    assert rows >= NA_WIN + NA_ROWS
    nrow, ncol = 2 * NA_KH - 1, 2 * NA_KW - 1
    rsel = np.zeros((3, NA_ROWS, NA_WIN, nrow), np.float32)
    mrow = np.zeros((3, NA_ROWS, NA_WIN), bool)
    for kind, r0 in enumerate((0, NA_ROWS, rows - NA_ROWS)):
        kr0 = int(np.clip(r0 - NA_KH // 2, 0, rows - NA_WIN))
        for qr in range(NA_ROWS):
            r = r0 + qr
            rs = int(np.clip(r - NA_KH // 2, 0, rows - NA_KH))
            for kr in range(NA_WIN):
                ka = kr0 + kr
                if rs <= ka < rs + NA_KH:
                    mrow[kind, qr, kr] = True
                    rsel[kind, qr, kr, ka - r + NA_KH - 1] = 1.0
    csel = np.zeros((GRID_W, GRID_W, ncol), np.float32)
    mcol = np.zeros((GRID_W, GRID_W), bool)
    for w in range(GRID_W):
        cs = int(np.clip(w - NA_KW // 2, 0, GRID_W - NA_KW))
        for kc in range(cs, cs + NA_KW):
            mcol[w, kc] = True
            csel[w, kc, kc - w + NA_KW - 1] = 1.0
    mask = mrow[:, :, None, :, None] & mcol[None, None, :, None, :]
    return rsel, csel, mask.reshape(3, TOK_TILE, NA_WIN * GRID_W)


def _na_bias_tables(na_rpb, rows):
    rsel, csel, mask = _na_bias_consts(rows)
    depth, heads = na_rpb.shape[:2]
    a = jnp.einsum("kqra,lhab->lkhqrb", jnp.asarray(rsel), na_rpb.astype(F32) * LOG2E, precision=HIGHEST)
    t = jnp.einsum("lkhqrb,wcb->lkhrcqw", a, jnp.asarray(csel), precision=HIGHEST)
    t = t.reshape(depth, 3, heads, NA_WIN * GRID_W, TOK_TILE)
    mask_t = np.swapaxes(mask, 1, 2)
    return jnp.where(jnp.asarray(mask_t)[None, :, None], t, NEG)


def _na_kernel(s_len, lc, rows, q_ref, k_ref, vt_ref, tab_ref, o_ref, s_scr):
    i = pl.program_id(1)
    tm = q_ref.shape[1]
    nblk = s_len // tm
    sub = SUBLANES

    def scores(h, part, off, m):
        r0, _, nrow, bias = part
        s = _dot(k_ref[h, pl.ds(r0, nrow), :], q_ref[h * HEAD_DIM:(h + 1) * HEAD_DIM, :])
        if bias is not None:
            s = s + bias[h]
        s_scr[h, off:off + nrow, :] = s
        return jnp.maximum(m, jnp.max(s.reshape(nrow // sub, sub, tm), axis=0))

    def apply(h, part, off, m, l, acc):
        _, c0, nrow, _ = part
        p = jnp.exp2(s_scr[h, off:off + nrow, :].reshape(nrow // sub, sub, tm) - m[None])
        l = l + jnp.sum(p, axis=0)
        vt = jnp.concatenate([vt_ref[h, c0 + c] for c in range(nrow // tm)], axis=1)
        return l, acc + _dot(vt, p.reshape(nrow, tm).astype(BF16))

    def run(parts):
        offs = [sum(p[2] for p in parts[:n]) for n in range(len(parts))]
        maxes = [None] * NA_HEADS
        for stage in range(NA_HEADS + 1):
            h1, h2 = stage, stage - 1
            if h1 < NA_HEADS:
                m1 = jnp.full((sub, tm), NEG, F32)
            if h2 >= 0:
                m2 = jnp.broadcast_to(jnp.max(maxes[h2], axis=0, keepdims=True), (sub, tm))
                l = jnp.zeros((sub, tm), F32)
                acc = jnp.zeros((HEAD_DIM, tm), F32)
            for part, off in zip(parts, offs):
                if h1 < NA_HEADS:
                    m1 = scores(h1, part, off, m1)
                if h2 >= 0:
                    l, acc = apply(h2, part, off, m2, l, acc)
            if h1 < NA_HEADS:
                maxes[h1] = m1
            if h2 >= 0:
                o = acc / jnp.sum(l, axis=0, keepdims=True)
                o_ref[h2 * HEAD_DIM:(h2 + 1) * HEAD_DIM, :] = o.astype(o_ref.dtype)

    ctx_part = (s_len, s_len // tm, lc, None)

    @pl.when(i < nblk)
    def _():
        kind = jnp.where(i == 0, 0, jnp.where(i == nblk - 1, 2, 1))
        c0 = jnp.clip(i - (NA_KH // 2) // NA_ROWS, 0, (rows - NA_WIN) // NA_ROWS)
        ks = pl.multiple_of(c0 * tm, tm)
        run([(ks, c0, NA_WIN * GRID_W, tab_ref.at[kind]), ctx_part])

    @pl.when(i >= nblk)
    def _():
        run([ctx_part])


def _na_call(nqt, nk, nvt, table, s_len, n_query):
    bsz, heads, t_all, hd = nk.shape
    tm = TOK_TILE
    rows = s_len // GRID_W
    assert NA_ROWS * GRID_W == tm and (NA_KH // 2) % NA_ROWS == 0 and (rows - NA_WIN) % NA_ROWS == 0
    return pl.pallas_call(
        functools.partial(_na_kernel, s_len, t_all - s_len, rows),
        grid=(bsz, n_query // tm),
        in_specs=[pl.BlockSpec((None, heads * hd, tm), lambda b, i: (b, 0, i)),
                  pl.BlockSpec((None, heads, t_all, hd), lambda b, i: (b, 0, 0, 0)),
                  pl.BlockSpec((None,) + nvt.shape[1:], lambda b, i: (b, 0, 0, 0, 0)),
                  pl.BlockSpec(table.shape, lambda b, i: (0, 0, 0, 0), pipeline_mode=pl.Buffered(1))],
        out_specs=pl.BlockSpec((None, heads * hd, tm), lambda b, i: (b, 0, i)),
        out_shape=jax.ShapeDtypeStruct((bsz, heads * hd, t_all), BF16),
        scratch_shapes=[pltpu.VMEM((heads, NA_WIN * GRID_W + t_all - s_len, tm), F32)],
        name="neighbourhood_attention",
        compiler_params=_cparams(("parallel", "parallel")),
    )(nqt, nk, nvt, table)


def _hy_filter_kernel(z_ref, t_ref, side_ref, w1_ref, b1_ref, w2_ref, b2_ref, w3_ref, b3_ref,
                      wo_ref, fr_ref, dl_ref, o_ref):
    om = fr_ref[...]
    h = jnp.sin(om * (_dot(z_ref[...], w1_ref[...], precision=HIGHEST) + b1_ref[...]))
    h = jnp.sin(om * (_dot(h, w2_ref[...], precision=HIGHEST) + b2_ref[...]))
    h = jnp.sin(om * (_dot(h, w3_ref[...], precision=HIGHEST) + b3_ref[...]))
    ht = _dot_nt(wo_ref[...], h, precision=HIGHEST)
    c = o_ref.shape[0]
    decay = jnp.exp(-t_ref[...] * jnp.abs(dl_ref[...]))
    side = side_ref[...]
    o_ref[...] = decay * (jnp.where(side > 0.5, ht[:c], 0.0) + jnp.where(side < -0.5, ht[c:], 0.0))


def _hy_filter_consts(length):
    n = np.arange(2 * length)
    pos = np.where(n < length, n, 2 * length - n)
    pos = np.where(n == length, 0, pos)
    side = np.where(n < length, 1.0, -1.0)
    side = np.where(n == length, 0.0, side)
    t = np.linspace(0.0, 1.0, length)[pos]
    bands = (HY_EMB_DIM - 1) // 2
    w = 2.0 * math.pi * pos / length
    fr = np.linspace(1e-4, bands - 1, bands)
    z = np.concatenate([t[:, None], np.cos(fr[None, :] * w[:, None]), -np.sin(fr[None, :] * w[:, None])], axis=-1)
    zp = np.zeros((2 * length, HY_EMB_PAD), np.float32)
    zp[:, :HY_EMB_DIM] = z
    return zp, t.astype(np.float32)[None, :], side.astype(np.float32)[None, :]


def _hy_filter_call(length, w1, b1, w2, b2, w3, b3, wout, freq):
    f = w2.shape[0]
    c = wout.shape[1] // 2
    zp, t, side = _hy_filter_consts(length)
    w1p = jnp.zeros((HY_EMB_PAD, f), F32).at[:HY_EMB_DIM].set(w1)
    deltas = np.linspace(math.log(HY_DECAY_TARGET) / HY_SLOW_DECAY,
                         math.log(HY_DECAY_TARGET) / HY_FAST_DECAY, c).astype(np.float32)[:, None]
    args = (jnp.asarray(zp), jnp.asarray(t), jnp.asarray(side), w1p, b1[None], w2, b2[None], w3, b3[None],
            wout.T, freq[None], jnp.asarray(deltas))
    return pl.pallas_call(
        _hy_filter_kernel,
        in_specs=[pl.BlockSpec(a.shape, lambda: (0,) * a.ndim) for a in args],
        out_specs=pl.BlockSpec((c, 2 * length), lambda: (0, 0)),
        out_shape=jax.ShapeDtypeStruct((c, 2 * length), F32),
        grid=(),
        name="hyena_filter",
        compiler_params=pltpu.CompilerParams(vmem_limit_bytes=VMEM_LIMIT),
    )(*args)


def _bitrev(n):
    bits = int(math.log2(n))
    return np.array([int(format(i, "0%db" % bits)[::-1], 2) if bits else 0 for i in range(n)])


def _fft_consts(n1, ct):
    n = n1 * FFT_CHUNK
    stages = max(int(math.log2(n1)), 1)
    tw = np.zeros((2, stages, max(n1 // 2, 1)), np.float32)
    for s in range(int(math.log2(n1))):
        half = n1 >> (s + 1)
        ang = -2.0 * math.pi * np.arange(half) / (2 * half)
        tw[0, s, :half] = np.cos(ang)
        tw[1, s, :half] = np.sin(ang)
    k1 = _bitrev(n1)
    ang = -2.0 * math.pi * (np.arange(FFT_CHUNK)[None, :] * k1[:, None]) / n
    twr = np.repeat(np.cos(ang), ct, axis=0).astype(np.float32)
    twi = np.repeat(np.sin(ang), ct, axis=0).astype(np.float32)
    a = -2.0 * math.pi * np.outer(np.arange(FFT_CHUNK), np.arange(FFT_CHUNK)) / FFT_CHUNK
    wr, wi = np.cos(a), np.sin(a)
    wf = np.block([[wr, wi], [-wi, wr]]).astype(np.float32)
    wb = np.block([[wr, -wi], [wi, wr]]).astype(np.float32)
    return tw, twr, twi, _split_bf16(wf), _split_bf16(wb)


def _split_bf16(w):
    w = jnp.asarray(w, F32)
    hi = w.astype(BF16)
    return jnp.stack([hi, (w - hi.astype(F32)).astype(BF16)])


def _slab_fft(zr, zi, tw_ref, n1, ct, inverse, half_input=False, half_output=False):
    nst = int(math.log2(n1))
    order = range(nst - 1, -1, -1) if inverse else range(nst)
    for s in order:
        half = n1 >> (s + 1)
        lh = int(math.log2(half))
        first = s == 0

        def body(idx, carry, s=s, half=half, lh=lh, first=first):
            blk = idx >> lh
            j = idx & (half - 1)
            r0 = pl.multiple_of(((blk << (lh + 1)) + j) * ct, ct)
            r1 = pl.multiple_of(r0 + half * ct, ct)
            wr = tw_ref[0, s, j]
            wi = tw_ref[1, s, j]
            ar = zr[pl.ds(r0, ct), :]
            ai = zi[pl.ds(r0, ct), :]
            if not inverse:
                if first and half_input:
                    zr[pl.ds(r1, ct), :] = ar * wr - ai * wi
                    zi[pl.ds(r1, ct), :] = ar * wi + ai * wr
                else:
                    br = zr[pl.ds(r1, ct), :]
                    bi = zi[pl.ds(r1, ct), :]
                    dr = ar - br
                    di = ai - bi
                    zr[pl.ds(r0, ct), :] = ar + br
                    zi[pl.ds(r0, ct), :] = ai + bi
                    zr[pl.ds(r1, ct), :] = dr * wr - di * wi
                    zi[pl.ds(r1, ct), :] = dr * wi + di * wr
            else:
                br = zr[pl.ds(r1, ct), :]
                bi = zi[pl.ds(r1, ct), :]
                tr = br * wr + bi * wi
                ti = bi * wr - br * wi
                zr[pl.ds(r0, ct), :] = ar + tr
                zi[pl.ds(r0, ct), :] = ai + ti
                if not (first and half_output):
                    zr[pl.ds(r1, ct), :] = ar - tr
                    zi[pl.ds(r1, ct), :] = ai - ti
            return carry

        lax.fori_loop(0, n1 // 2, body, 0, unroll=min(4, n1 // 2))


def _chunk_dft(xr, xi, w_ref, split):
    x = jnp.concatenate([xr, xi], axis=1)
    x_hi = x.astype(BF16)
    y = _dot(x_hi, w_ref[0])
    if split:
        x_lo = (x - x_hi.astype(F32)).astype(BF16)
        y = y + (_dot(x_lo, w_ref[0]) + _dot(x_hi, w_ref[1]))
    return y[:, :FFT_CHUNK], y[:, FFT_CHUNK:]


def _fft_groups(n1, ct):
    g = min(8, n1)
    return n1 // g, g * ct


def _hy_spec_kernel(n1, ct, f_ref, tw_ref, twr_ref, twi_ref, wf_ref, fr_ref, fi_ref, zr, zi):
    for s in range(n1):
        zr[s * ct:(s + 1) * ct, :] = f_ref[:, s * FFT_CHUNK:(s + 1) * FFT_CHUNK]
    zi[...] = jnp.zeros(zi.shape, F32)
    _slab_fft(zr, zi, tw_ref, n1, ct, inverse=False)
    ngroups, gr = _fft_groups(n1, ct)

    def body(g, carry):
        r0 = pl.multiple_of(g * gr, gr)
        a = zr[pl.ds(r0, gr), :]
        b = zi[pl.ds(r0, gr), :]
        tr = twr_ref[pl.ds(r0, gr), :]
        ti = twi_ref[pl.ds(r0, gr), :]
        yr, yi = _chunk_dft(a * tr - b * ti, a * ti + b * tr, wf_ref, split=True)
        fr_ref[pl.ds(r0, gr), :] = yr
        fi_ref[pl.ds(r0, gr), :] = yi
        return carry

    lax.fori_loop(0, ngroups, body, 0, unroll=min(2, ngroups))


def _hy_spec_call(filt_t):
    c, n = filt_t.shape
    n1 = n // FFT_CHUNK
    ct = HY_CT
    tw, twr, twi, wf, _ = _fft_consts(n1, ct)
    const2 = lambda i: (0, 0)
    out = jax.ShapeDtypeStruct((c // ct, n1 * ct, FFT_CHUNK), F32)
    ospec = pl.BlockSpec((None, n1 * ct, FFT_CHUNK), lambda i: (i, 0, 0))
    return pl.pallas_call(
        functools.partial(_hy_spec_kernel, n1, ct),
        grid=(c // ct,),
        in_specs=[pl.BlockSpec((ct, n), lambda i: (i, 0)),
                  pl.BlockSpec(memory_space=pltpu.SMEM),
                  pl.BlockSpec(twr.shape, const2), pl.BlockSpec(twi.shape, const2),
                  pl.BlockSpec(wf.shape, lambda i: (0, 0, 0))],
        out_specs=[ospec, ospec],
        out_shape=[out, out],
        scratch_shapes=[pltpu.VMEM((n1 * ct, FFT_CHUNK), F32), pltpu.VMEM((n1 * ct, FFT_CHUNK), F32)],
        name="hyena_filter_spectrum",
        compiler_params=_cparams(("parallel",)),
    )(filt_t, jnp.asarray(tw), jnp.asarray(twr), jnp.asarray(twi), jnp.asarray(wf))


def _hy_conv_kernel(n1, ct, length, v_ref, x1_ref, x2_ref, cwv_ref, cw1_ref, cw2_ref, skip_ref,
                    fr_ref, fi_ref, tw_ref, twr_ref, twi_ref, wf_ref, wb_ref, o_ref,
                    zr, zi, u_scr, g_scr):
    lane = lax.broadcasted_iota(jnp.int32, (1, length), 1)

    sub = SUBLANES

    def short_conv(x, cw):
        xm = jnp.where(lane >= 1, pltpu.roll(x, 1, 1), 0.0)
        xp = jnp.where(lane <= length - 2, pltpu.roll(x, length - 1, 1), 0.0)
        return xm * cw[:, 0:1] + x * cw[:, 1:2] + xp * cw[:, 2:3] + cw[:, 3:4]

    def conv_rows(r, carry):
        rows = pl.ds(pl.multiple_of(r * sub, sub), sub)
        cwv, cw1, cw2 = cwv_ref[rows, :], cw1_ref[rows, :], cw2_ref[rows, :]
        for a in range(2):
            u_scr[a, rows, :] = short_conv(v_ref[a, rows, :], cwv) * short_conv(x1_ref[a, rows, :], cw1)
            g_scr[a, rows, :] = short_conv(x2_ref[a, rows, :], cw2)
        return carry

    lax.fori_loop(0, ct // sub, conv_rows, 0)
    nz = length // FFT_CHUNK
    for s in range(nz):
        zr[s * ct:(s + 1) * ct, :] = u_scr[0, :, s * FFT_CHUNK:(s + 1) * FFT_CHUNK]
        zi[s * ct:(s + 1) * ct, :] = u_scr[1, :, s * FFT_CHUNK:(s + 1) * FFT_CHUNK]
    _slab_fft(zr, zi, tw_ref, n1, ct, inverse=False, half_input=True)
    ngroups, gr = _fft_groups(n1, ct)

    def body(g, carry):
        r0 = pl.multiple_of(g * gr, gr)
        a = zr[pl.ds(r0, gr), :]
        b = zi[pl.ds(r0, gr), :]
        tr = twr_ref[pl.ds(r0, gr), :]
        ti = twi_ref[pl.ds(r0, gr), :]
        yr, yi = _chunk_dft(a * tr - b * ti, a * ti + b * tr, wf_ref, split=False)
        fr = fr_ref[pl.ds(r0, gr), :]
        fi = fi_ref[pl.ds(r0, gr), :]
        qr, qi = _chunk_dft(yr * fr - yi * fi, yr * fi + yi * fr, wb_ref, split=False)
        zr[pl.ds(r0, gr), :] = qr * tr + qi * ti
        zi[pl.ds(r0, gr), :] = qi * tr - qr * ti
        return carry

    lax.fori_loop(0, ngroups, body, 0, unroll=min(4, ngroups))
    _slab_fft(zr, zi, tw_ref, n1, ct, inverse=True, half_output=True)
    inv_n = 1.0 / (n1 * FFT_CHUNK)
    skip = skip_ref[...]
    for s in range(nz):
        sl = slice(s * FFT_CHUNK, (s + 1) * FFT_CHUNK)
        for a, z in ((0, zr), (1, zi)):
            y = z[s * ct:(s + 1) * ct, :] * inv_n
            u = u_scr[a, :, sl]
            o_ref[a, :, sl] = (g_scr[a, :, sl] * (y + u * skip)).astype(o_ref.dtype)


def _hy_conv_call(hy_t, spec_r, spec_i, cw, skip, length, lane_block, prev=None):
    _, bh, c3, t_all = hy_t.shape
    c = c3 // 3
    ct = HY_CT
    nct = c // ct
    n1 = 2 * length // FFT_CHUNK
    tw, twr, twi, wf, wb = _fft_consts(n1, ct)
    const2 = lambda p, i: (0, 0)
    xin = lambda k: pl.BlockSpec((2, None, ct, length), lambda p, i, k=k: (0, p, k * nct + i, lane_block))
    cwin = lambda k: pl.BlockSpec((ct, 4), lambda p, i, k=k: (k * nct + i, 0))
    fin = pl.BlockSpec((None, n1 * ct, FFT_CHUNK), lambda p, i: (i, 0, 0))
    in_specs = [xin(0), xin(1), xin(2), cwin(0), cwin(1), cwin(2),
                pl.BlockSpec((ct, 1), lambda p, i: (i, 0)),
                fin, fin,
                pl.BlockSpec(memory_space=pltpu.SMEM),
                pl.BlockSpec(twr.shape, const2), pl.BlockSpec(twi.shape, const2),
                pl.BlockSpec(wf.shape, lambda p, i: (0, 0, 0)),
                pl.BlockSpec(wb.shape, lambda p, i: (0, 0, 0))]
    args = [hy_t, hy_t, hy_t, cw, cw, cw, skip, spec_r, spec_i, jnp.asarray(tw), jnp.asarray(twr),
            jnp.asarray(twi), jnp.asarray(wf), jnp.asarray(wb)]
    kern = functools.partial(_hy_conv_kernel, n1, ct, length)
    aliases = {}
    if prev is not None:
        in_specs.append(pl.BlockSpec(memory_space=pl.ANY))
        args.append(prev)
        aliases = {len(args) - 1: 0}
        kern = lambda *refs, kern=kern: kern(*refs[:14], *refs[15:])
    return pl.pallas_call(
        kern,
        grid=(bh, nct),
        in_specs=in_specs,
        out_specs=pl.BlockSpec((2, None, ct, length), lambda p, i: (0, p, i, lane_block)),
        out_shape=jax.ShapeDtypeStruct((2, bh, c, t_all), BF16),
        scratch_shapes=[pltpu.VMEM((n1 * ct, FFT_CHUNK), F32), pltpu.VMEM((n1 * ct, FFT_CHUNK), F32),
                        pltpu.VMEM((2, ct, length), F32), pltpu.VMEM((2, ct, length), F32)],
        input_output_aliases=aliases,
        name="hyena_conv_%d" % length,
        compiler_params=_cparams(("parallel", "parallel")),
    )(*args)


def _outproj_kernel(n_lat_tiles, ctx_row, x_ref, hy_ref, ga_ref, na_ref, w_ref, mod_ref, o_ref):
    j = pl.program_id(1)
    nbat, tm, d = x_ref.shape
    mix = []
    for s in range(nbat):
        cm = jnp.concatenate([hy_ref[s], ga_ref[s], na_ref[s]], axis=0)
        mix.append(cm.astype(F32).T.astype(BF16))
    y = _dot(jnp.concatenate(mix, axis=0), w_ref[...])
    for s in range(nbat):
        row = jnp.where(j < n_lat_tiles, pl.program_id(0) * nbat + s, ctx_row)
        o_ref[s] = x_ref[s] + _mod_row(mod_ref, row, 2, d) * y[s * tm:(s + 1) * tm]


def _outproj_call(xa, hyo_t, gqa, na, w_out, mod_l, s_len, n_tok):
    bsz, _, d = xa.shape
    tm = TOK_TILE
    nbat = _batch_group(bsz)
    const = lambda b, j: (0, 0)
    tok = lambda n: pl.BlockSpec((nbat, tm, n), lambda b, j: (b, j, 0))
    chan = lambda n: pl.BlockSpec((nbat, n, tm), lambda b, j: (b, 0, j))
    return pl.pallas_call(
        functools.partial(_outproj_kernel, s_len // tm, bsz),
        grid=(bsz // nbat, n_tok // tm),
        in_specs=[tok(d), chan(hyo_t.shape[1]), chan(gqa.shape[1]), chan(na.shape[1]),
                  pl.BlockSpec(w_out.shape, const), pl.BlockSpec(mod_l.shape, const)],
        out_specs=tok(d),
        out_shape=jax.ShapeDtypeStruct((bsz, n_tok, d), F32),
        name="out_proj",
        compiler_params=_cparams(("parallel", "parallel")),
    )(xa, hyo_t, gqa, na, w_out, mod_l)


def _ffn_kernel(tiles_per_batch, n_lat_tiles, ctx_row, th, x_ref, mod_ref, g_ref, wi_ref, wo_ref,
                gf_ref, o_ref, h_scr, act_scr):
    i = pl.program_id(0)
    d = x_ref.shape[-1]
    f = wo_ref.shape[0]
    groups = x_ref.shape[0] // TOK_TILE

    def mod_row(s):
        gi = i * groups + s
        jj = gi % tiles_per_batch
        return jnp.where(jj < n_lat_tiles, gi // tiles_per_batch, ctx_row)

    for s in range(groups):
        row = mod_row(s)
        sl = slice(s * TOK_TILE, (s + 1) * TOK_TILE)
        h = _norm_mod(x_ref[sl, :], g_ref[...], _mod_row(mod_ref, row, 3, d), _mod_row(mod_ref, row, 4, d))
        h_scr[sl, :] = h.astype(BF16)
    h = h_scr[...]
    for j in range(f // th):
        gate = _dot(h, wi_ref[:, j * th:(j + 1) * th])
        up = _dot(h, wi_ref[:, f + j * th:f + (j + 1) * th])
        act_scr[:, j * th:(j + 1) * th] = (gate * jax.nn.sigmoid(gate) * up).astype(BF16)
    y = _dot(act_scr[...], wo_ref[...])
    for s in range(groups):
        row = mod_row(s)
        sl = slice(s * TOK_TILE, (s + 1) * TOK_TILE)
        out = x_ref[sl, :] + _mod_row(mod_ref, row, 5, d) * y[sl, :]
        if gf_ref is not None:
            ms = jnp.mean(out * out, axis=-1, keepdims=True)
            out = out * lax.rsqrt(ms + NORM_EPS) * gf_ref[...]
        o_ref[sl, :] = out


def _ffn_call(xa, mod_l, g, w_in, w_out, s_len, g_final=None):
    bsz, t_all, d = xa.shape
    f = w_out.shape[0]
    r = bsz * t_all
    tm = 1024 if r % 1024 == 0 else TOK_TILE
    th = 256
    const = lambda i: (0, 0)
    resident = lambda a: pl.BlockSpec(a.shape, const, pipeline_mode=pl.Buffered(1))
    in_specs = [pl.BlockSpec((tm, d), lambda i: (i, 0)),
                pl.BlockSpec(mod_l.shape, const),
                pl.BlockSpec((1, d), const),
                resident(w_in), resident(w_out)]
    args = [xa.reshape(r, d), mod_l, g, w_in, w_out]
    kern = functools.partial(_ffn_kernel, t_all // TOK_TILE, s_len // TOK_TILE, bsz, th)
    if g_final is None:
        kern = functools.partial(_ffn_no_final, kern)
    else:
        in_specs.append(pl.BlockSpec((1, d), const))
        args.append(g_final)
    out = pl.pallas_call(
        kern,
        grid=(r // tm,),
        in_specs=in_specs,
        out_specs=pl.BlockSpec((tm, d), lambda i: (i, 0)),
        out_shape=jax.ShapeDtypeStruct((r, d), F32),
        scratch_shapes=[pltpu.VMEM((tm, d), BF16), pltpu.VMEM((tm, f), BF16)],
        name="swiglu_ffn",
        compiler_params=_cparams(("parallel",)),
    )(*args)
    return out.reshape(bsz, t_all, d)


def _ffn_no_final(kern, x_ref, mod_ref, g_ref, wi_ref, wo_ref, o_ref, h_scr, act_scr):
    kern(x_ref, mod_ref, g_ref, wi_ref, wo_ref, None, o_ref, h_scr, act_scr)


def _rope_tables(s_len, lc):
    pos = np.arange(s_len)
    row = (pos // GRID_W).astype(np.float64)
    col = (pos % GRID_W).astype(np.float64)
    n_f = HEAD_DIM // 4
    inv = (ROPE_THETA ** (-np.arange(n_f, dtype=np.float32) / n_f)).astype(np.float64)
    ang = np.concatenate([row[:, None] * inv, col[:, None] * inv], axis=-1).astype(np.float32).astype(np.float64)
    lane = np.arange(LANES)
    sign = np.where((lane % HEAD_DIM) < HEAD_DIM // 2, -1.0, 1.0)
    cos = np.cos(ang)[:, lane % (HEAD_DIM // 2)]
    sin = np.sin(ang)[:, lane % (HEAD_DIM // 2)] * sign[None, :]
    cos = np.concatenate([cos, np.ones((lc, LANES))], axis=0).astype(np.float32)
    sin = np.concatenate([sin, np.zeros((lc, LANES))], axis=0).astype(np.float32)
    cosc = np.concatenate([np.cos(ang).T, np.ones((HEAD_DIM // 2, lc))], axis=1).astype(np.float32)
    sinc = np.concatenate([np.sin(ang).T, np.zeros((HEAD_DIM // 2, lc))], axis=1).astype(np.float32)
    return jnp.asarray(cosc), jnp.asarray(sinc), jnp.asarray(cos), jnp.asarray(sin)


def kernel(x, c, ctx, c_ctx, w_mod, b_mod, g_mix, g_ffn, w_in, w_out, hy_conv_w, hy_conv_b,
           hy_f_w1, hy_f_b1, hy_f_w2, hy_f_b2, hy_f_w3, hy_f_b3, hy_f_wout, hy_f_freq, hy_skip,
           qk_g_q, qk_g_k, na_rpb, w_ffn_in, w_ffn_out, g_final):
    bsz, s_len, d = x.shape
    lc = ctx.shape[1]
    depth = w_mod.shape[0]
    t_all = s_len + lc
    hyw = hy_skip.shape[1]
    hy_cols = 3 * hyw
    assert bsz % 2 == 0 and s_len % TOK_TILE == 0 and lc == TOK_TILE and s_len % lc == 0

    nb = -(-(bsz + 1) // 8) * 8
    cs = jnp.zeros((nb, d), F32).at[:bsz].set(c).at[bsz].set(c_ctx)
    mod = _mod_call(cs, w_mod, b_mod)

    rope = _rope_tables(s_len, lc)
    tables = _na_bias_tables(na_rpb, s_len // GRID_W)
    q0 = hy_cols
    k0 = q0 + GQA_HEADS * HEAD_DIM
    v0 = k0 + GQA_KV_HEADS * HEAD_DIM
    nq0 = v0 + GQA_KV_HEADS * HEAD_DIM
    nk0 = nq0 + NA_HEADS * HEAD_DIM
    nv0 = nk0 + NA_HEADS * HEAD_DIM
    w_cm = jnp.concatenate([w_in[:, :, :k0], w_in[:, :, v0:nk0], w_in[:, :, nv0:]], axis=2)
    w_cm = jnp.swapaxes(w_cm, 1, 2).astype(BF16)
    w_tm = jnp.concatenate([w_in[:, :, k0:v0], w_in[:, :, nk0:nv0]], axis=2).astype(BF16)
    w_out_b = w_out.astype(BF16)
    w_fi = w_ffn_in.astype(BF16)
    w_fo = w_ffn_out.astype(BF16)
    cw = jnp.concatenate([jnp.swapaxes(hy_conv_w, 1, 2), hy_conv_b[:, :, None]], axis=2)
    per = LANES // HEAD_DIM

    xa = jnp.concatenate([x, ctx], axis=1)
    for l in range(depth):
        last = l == depth - 1
        gq_b = jnp.broadcast_to(qk_g_q[l][:, None], (HEAD_DIM, LANES))
        gk2 = jnp.tile(qk_g_k[l], per)[None]
        hy_t, q, k, v, nq, nk, nv = _inproj_call(xa, mod[l], g_mix[l][None], w_cm[l], w_tm[l],
                                                  gq_b, gk2, rope, hy_cols, s_len)
        n_tok = s_len if last else t_all
        gqa = _gqa_call(q, k, v, s_len, n_tok)
        na = _na_call(nq, nk, nv, tables[l], s_len, n_tok)

        fargs = (hy_f_w1[l], hy_f_b1[l], hy_f_w2[l], hy_f_b2[l], hy_f_w3[l], hy_f_b3[l],
                 hy_f_wout[l], hy_f_freq[l])
        hy_p = hy_t.reshape(2, bsz // 2, hy_cols, t_all)
        skip = hy_skip[l][:, None]
        fr, fi = _hy_spec_call(_hy_filter_call(s_len, *fargs))
        hyo = _hy_conv_call(hy_p, fr, fi, cw[l], skip, s_len, 0)
        if not last:
            frc, fic = _hy_spec_call(_hy_filter_call(lc, *fargs))
            hyo = _hy_conv_call(hy_p, frc, fic, cw[l], skip, lc, s_len // lc, prev=hyo)
        hyo = hyo.reshape(bsz, hyw, t_all)

        xa = _outproj_call(xa, hyo, gqa, na, w_out_b[l], mod[l], s_len, n_tok)
        xa = _ffn_call(xa, mod[l], g_ffn[l][None], w_fi[l], w_fo[l], s_len,
                       g_final[None] if last else None)
    return xa
```

```python
import functools
import math

import numpy as np
import jax
import jax.numpy as jnp
from jax import lax
from jax.experimental import pallas as pl
from jax.experimental.pallas import tpu as pltpu

F32 = jnp.float32
BF16 = jnp.bfloat16
HIGHEST = lax.Precision.HIGHEST

GRID_W = 64
HEAD_DIM = 64
GQA_HEADS = 8
GQA_KV_HEADS = 2
GQA_GROUP = GQA_HEADS // GQA_KV_HEADS
NA_HEADS = 4
NA_KH = 8
NA_KW = 16
ROPE_THETA = 10000.0
HY_EMB_DIM = 33
HY_DECAY_TARGET = 1e-2
HY_FAST_DECAY = 0.3
HY_SLOW_DECAY = 1.5
NORM_EPS = 1e-6
ATTN_SCALE = HEAD_DIM ** -0.5
NEG = -1e30
LOG2E = math.log2(math.e)

LANES = 128
SUBLANES = 8
HY_EMB_PAD = 64
TOK_TILE = 256
FFT_CHUNK = LANES
HY_CT = 64
NA_ROWS = TOK_TILE // GRID_W
NA_WIN = NA_ROWS + NA_KH
VMEM_LIMIT = 56 * 1024 * 1024


def _cparams(sem):
    return pltpu.CompilerParams(dimension_semantics=sem, vmem_limit_bytes=VMEM_LIMIT)


def _dot(a, b, **kw):
    return jnp.dot(a, b, preferred_element_type=F32, **kw)


def _dot_nt(a, b, **kw):
    return lax.dot_general(a, b, (((1,), (1,)), ((), ())), preferred_element_type=F32, **kw)


def _mod_kernel(cs_ref, w_ref, b_ref, o_ref):
    cs = cs_ref[...]
    s = cs * jax.nn.sigmoid(cs)
    o_ref[...] = _dot(s, w_ref[...], precision=HIGHEST) + b_ref[...]


def _mod_call(cs, w_mod, b_mod):
    depth, d, n = w_mod.shape
    nb = cs.shape[0]
    tn = 1024
    return pl.pallas_call(
        _mod_kernel,
        grid=(depth, n // tn),
        in_specs=[pl.BlockSpec((nb, d), lambda l, j: (0, 0)),
                  pl.BlockSpec((None, d, tn), lambda l, j: (l, 0, j)),
                  pl.BlockSpec((None, 1, tn), lambda l, j: (l, 0, j))],
        out_specs=pl.BlockSpec((None, nb, tn), lambda l, j: (l, 0, j)),
        out_shape=jax.ShapeDtypeStruct((depth, nb, n), F32),
        name="adaln_mod",
        compiler_params=_cparams(("parallel", "parallel")),
    )(cs, w_mod, b_mod.reshape(depth, 1, n))


def _mod_row(mod_ref, row, k, d):
    return mod_ref[pl.ds(row, 1), k * d:(k + 1) * d]


def _norm_mod(x, g, shift, scale):
    ms = jnp.mean(x * x, axis=-1, keepdims=True)
    y = x * lax.rsqrt(ms + NORM_EPS) * g
    return y * (1.0 + scale) + shift


def _batch_group(bsz):
    return 4 if bsz % 4 == 0 else 2


def _inproj_kernel(n_lat_tiles, ctx_row, hyc, x_ref, mod_ref, g_ref, wcm_ref, wtm_ref, gq_ref, gk_ref,
                   cosc_ref, sinc_ref, cos_ref, sin_ref,
                   hy_ref, q_ref, k_ref, v_ref, nq_ref, nk_ref, nv_ref, h_scr):
    j = pl.program_id(1)
    nbat, tm, d = x_ref.shape
    for s in range(nbat):
        row = jnp.where(j < n_lat_tiles, pl.program_id(0) * nbat + s, ctx_row)
        h = _norm_mod(x_ref[s], g_ref[...], _mod_row(mod_ref, row, 0, d), _mod_row(mod_ref, row, 1, d))
        h_scr[s * tm:(s + 1) * tm, :] = h.astype(BF16)
    h = h_scr[...]
    cm = _dot_nt(wcm_ref[...], h)
    tmj = _dot(h, wtm_ref[...])

    half = HEAD_DIM // 2
    reps = cm.shape[1] // LANES
    gq = jnp.concatenate([gq_ref[...]] * reps, axis=1)
    cosc = jnp.concatenate([cosc_ref[...]] * nbat, axis=1)
    sinc = jnp.concatenate([sinc_ref[...]] * nbat, axis=1)
    for hh in range(GQA_HEADS):
        x = cm[hyc + hh * HEAD_DIM:hyc + (hh + 1) * HEAD_DIM, :]
        ms = jnp.mean(x * x, axis=0, keepdims=True)
        xn = x * lax.rsqrt(ms + NORM_EPS) * gq
        x1, x2 = xn[:half], xn[half:]
        y = jnp.concatenate([x1 * cosc - x2 * sinc, x1 * sinc + x2 * cosc], axis=0)
        y = (y * (ATTN_SCALE * LOG2E)).astype(BF16)
        for s in range(nbat):
            q_ref[s, hh * HEAD_DIM:(hh + 1) * HEAD_DIM, :] = y[:, s * tm:(s + 1) * tm]
    voff = hyc + GQA_HEADS * HEAD_DIM
    nqoff = voff + GQA_KV_HEADS * HEAD_DIM
    nvoff = nqoff + NA_HEADS * HEAD_DIM
    for s in range(nbat):
        cols = slice(s * tm, (s + 1) * tm)
        hy_ref[s] = cm[:hyc, cols]
        v_ref[s] = cm[voff:nqoff, cols].astype(BF16)
        nq_ref[s] = (cm[nqoff:nvoff, cols] * (ATTN_SCALE * LOG2E)).astype(BF16)
        for e in range(NA_HEADS):
            nv_ref[s, e] = cm[nvoff + e * HEAD_DIM:nvoff + (e + 1) * HEAD_DIM, cols].astype(BF16)
        _inproj_token_major(tmj[s * tm:(s + 1) * tm], gk_ref, cos_ref, sin_ref, k_ref.at[s], nk_ref.at[s])


def _inproj_token_major(p, gk_ref, cos_ref, sin_ref, k_ref, nk_ref):
    ri = lax.broadcasted_iota(jnp.int32, (LANES, LANES), 0) // HEAD_DIM
    ci = lax.broadcasted_iota(jnp.int32, (LANES, LANES), 1) // HEAD_DIM
    avg = jnp.where(ri == ci, 1.0 / HEAD_DIM, 0.0).astype(BF16)
    lane = lax.broadcasted_iota(jnp.int32, (1, LANES), 1)
    first_half = (lane % HEAD_DIM) < (HEAD_DIM // 2)
    xc = p[:, :LANES]
    ms = _dot((xc * xc).astype(BF16), avg)
    xn = xc * lax.rsqrt(ms + NORM_EPS) * gk_ref[...]
    partner = jnp.where(first_half, pltpu.roll(xn, LANES - HEAD_DIM // 2, 1),
                        pltpu.roll(xn, HEAD_DIM // 2, 1))
    y = (xn * cos_ref[...] + partner * sin_ref[...]).astype(BF16)
    for e in range(GQA_KV_HEADS):
        k_ref[e] = y[:, e * HEAD_DIM:(e + 1) * HEAD_DIM]
    for e in range(NA_HEADS):
        nk_ref[e] = p[:, LANES + e * HEAD_DIM:LANES + (e + 1) * HEAD_DIM].astype(BF16)


def _inproj_call(xa, mod_l, g, w_cm, w_tm, gq_b, gk2, rope, hyc, s_len):
    bsz, t_all, d = xa.shape
    tm = TOK_TILE
    nbat = _batch_group(bsz)
    cosc, sinc, cos_t, sin_t = rope
    grid = (bsz // nbat, t_all // tm)
    const = lambda b, j: (0, 0)
    resident = lambda a: pl.BlockSpec(a.shape, const, pipeline_mode=pl.Buffered(1))
    heads = lambda n: pl.BlockSpec((nbat, n, tm, HEAD_DIM), lambda b, j: (b, 0, j, 0))
    hshape = lambda n: jax.ShapeDtypeStruct((bsz, n, t_all, HEAD_DIM), BF16)
    chan = lambda n: pl.BlockSpec((nbat, n, tm), lambda b, j: (b, 0, j))
    qc = GQA_HEADS * HEAD_DIM
    vc = GQA_KV_HEADS * HEAD_DIM
    nc = NA_HEADS * HEAD_DIM
    return pl.pallas_call(
        functools.partial(_inproj_kernel, s_len // tm, bsz, hyc),
        grid=grid,
        in_specs=[pl.BlockSpec((nbat, tm, d), lambda b, j: (b, j, 0)),
                  pl.BlockSpec(mod_l.shape, const),
                  pl.BlockSpec((1, d), const),
                  resident(w_cm), resident(w_tm),
                  pl.BlockSpec(gq_b.shape, const),
                  pl.BlockSpec((1, LANES), const),
                  pl.BlockSpec((HEAD_DIM // 2, tm), lambda b, j: (0, j)),
                  pl.BlockSpec((HEAD_DIM // 2, tm), lambda b, j: (0, j)),
                  pl.BlockSpec((tm, LANES), lambda b, j: (j, 0)),
                  pl.BlockSpec((tm, LANES), lambda b, j: (j, 0))],
        out_specs=[chan(hyc), chan(qc), heads(GQA_KV_HEADS), chan(vc),
                   chan(nc), heads(NA_HEADS),
                   pl.BlockSpec((nbat, NA_HEADS, None, HEAD_DIM, tm), lambda b, j: (b, 0, j, 0, 0))],
        out_shape=[jax.ShapeDtypeStruct((bsz, hyc, t_all), F32),
                   jax.ShapeDtypeStruct((bsz, qc, t_all), BF16),
                   hshape(GQA_KV_HEADS),
                   jax.ShapeDtypeStruct((bsz, vc, t_all), BF16),
                   jax.ShapeDtypeStruct((bsz, nc, t_all), BF16),
                   hshape(NA_HEADS),
                   jax.ShapeDtypeStruct((bsz, NA_HEADS, t_all // tm, HEAD_DIM, tm), BF16)],
        scratch_shapes=[pltpu.VMEM((nbat * tm, d), BF16)],
        name="in_proj",
        compiler_params=_cparams(("parallel", "parallel")),
    )(xa, mod_l, g, w_cm, w_tm, gq_b, gk2, cosc, sinc, cos_t, sin_t)


def _gqa_kernel(s_len, lc, tk, q_ref, k_ref, vt_ref, o_ref, s_scr):
    i = pl.program_id(2)
    tm = q_ref.shape[1]
    sub = SUBLANES

    def finish(h, l, acc):
        o_ref[h * HEAD_DIM:(h + 1) * HEAD_DIM, :] = (acc / jnp.sum(l, axis=0, keepdims=True)).astype(o_ref.dtype)

    def scores(h, q, r0, nrow, m):
        s = _dot(k_ref[r0:r0 + nrow, :], q)
        s_scr[h, r0:r0 + nrow, :] = s
        return jnp.maximum(m, jnp.max(s.reshape(nrow // sub, sub, tm), axis=0))

    def apply(h, r0, nrow, m, l, acc):
        p = jnp.exp2(s_scr[h, r0:r0 + nrow, :].reshape(nrow // sub, sub, tm) - m[None])
        l = l + jnp.sum(p, axis=0)
        acc = acc + _dot(vt_ref[:, r0:r0 + nrow], p.reshape(nrow, tm).astype(BF16))
        return l, acc

    def two_pass(chunks):
        maxes = [None] * GQA_GROUP
        for stage in range(GQA_GROUP + 1):
            h1, h2 = stage, stage - 1
            if h1 < GQA_GROUP:
                q1 = q_ref[h1 * HEAD_DIM:(h1 + 1) * HEAD_DIM, :]
                m1 = jnp.full((sub, tm), NEG, F32)
            if h2 >= 0:
                m2 = jnp.broadcast_to(jnp.max(maxes[h2], axis=0, keepdims=True), (sub, tm))
                l = jnp.zeros((sub, tm), F32)
                acc = jnp.zeros((HEAD_DIM, tm), F32)
            for r0, nrow in chunks:
                if h2 >= 0:
                    l, acc = apply(h2, r0, nrow, m2, l, acc)
                if h1 < GQA_GROUP:
                    m1 = scores(h1, q1, r0, nrow, m1)
            if h1 < GQA_GROUP:
                maxes[h1] = m1
            if h2 >= 0:
                finish(h2, l, acc)

    ctx_chunk = (s_len, lc)

    @pl.when(i * tm < s_len)
    def _():
        two_pass([(c * tk, tk) for c in range(s_len // tk)] + [ctx_chunk])

    @pl.when(i * tm >= s_len)
    def _():
        two_pass([ctx_chunk])


def _gqa_call(qt, k, vt, s_len, n_query):
    bsz, qc, t_all = qt.shape
    hd = HEAD_DIM
    tm = TOK_TILE
    tk = min(8 * TOK_TILE, s_len)
    lc = t_all - s_len
    return pl.pallas_call(
        functools.partial(_gqa_kernel, s_len, lc, tk),
        grid=(bsz, GQA_KV_HEADS, n_query // tm),
        in_specs=[pl.BlockSpec((None, GQA_GROUP * hd, tm), lambda b, g, i: (b, g, i)),
                  pl.BlockSpec((None, None, t_all, hd), lambda b, g, i: (b, g, 0, 0)),
                  pl.BlockSpec((None, hd, t_all), lambda b, g, i: (b, g, 0))],
        out_specs=pl.BlockSpec((None, GQA_GROUP * hd, tm), lambda b, g, i: (b, g, i)),
        out_shape=jax.ShapeDtypeStruct((bsz, qc, t_all), BF16),
        scratch_shapes=[pltpu.VMEM((GQA_GROUP, t_all, tm), F32)],
        name="gqa_attention",
        compiler_params=_cparams(("parallel", "parallel", "parallel")),
    )(qt, k, vt)


def _na_bias_consts(rows):
    assert rows >= NA_WIN + NA_ROWS
    nrow, ncol = 2 * NA_KH - 1, 2 * NA_KW - 1
    rsel = np.zeros((3, NA_ROWS, NA_WIN, nrow), np.float32)
    mrow = np.zeros((3, NA_ROWS, NA_WIN), bool)
    for kind, r0 in enumerate((0, NA_ROWS, rows - NA_ROWS)):
        kr0 = int(np.clip(r0 - NA_KH // 2, 0, rows - NA_WIN))
        for qr in range(NA_ROWS):
            r = r0 + qr
            rs = int(np.clip(r - NA_KH // 2, 0, rows - NA_KH))
            for kr in range(NA_WIN):
                ka = kr0 + kr
                if rs <= ka < rs + NA_KH:
                    mrow[kind, qr, kr] = True
                    rsel[kind, qr, kr, ka - r + NA_KH - 1] = 1.0
    csel = np.zeros((GRID_W, GRID_W, ncol), np.float32)
    mcol = np.zeros((GRID_W, GRID_W), bool)
    for w in range(GRID_W):
        cs = int(np.clip(w - NA_KW // 2, 0, GRID_W - NA_KW))
        for kc in range(cs, cs + NA_KW):
            mcol[w, kc] = True
            csel[w, kc, kc - w + NA_KW - 1] = 1.0
    mask = mrow[:, :, None, :, None] & mcol[None, None, :, None, :]
    return rsel, csel, mask.reshape(3, TOK_TILE, NA_WIN * GRID_W)


def _na_bias_tables(na_rpb, rows):
    rsel, csel, mask = _na_bias_consts(rows)
    depth, heads = na_rpb.shape[:2]
    a = jnp.einsum("kqra,lhab->lkhqrb", jnp.asarray(rsel), na_rpb.astype(F32) * LOG2E, precision=HIGHEST)
    t = jnp.einsum("lkhqrb,wcb->lkhrcqw", a, jnp.asarray(csel), precision=HIGHEST)
    t = t.reshape(depth, 3, heads, NA_WIN * GRID_W, TOK_TILE)
    mask_t = np.swapaxes(mask, 1, 2)
    return jnp.where(jnp.asarray(mask_t)[None, :, None], t, NEG)


def _na_kernel(s_len, lc, rows, q_ref, k_ref, vt_ref, tab_ref, o_ref, s_scr):
    i = pl.program_id(1)
    tm = q_ref.shape[1]
    nblk = s_len // tm
    sub = SUBLANES

    def scores(h, part, off, m):
        r0, _, nrow, bias = part
        s = _dot(k_ref[h, pl.ds(r0, nrow), :], q_ref[h * HEAD_DIM:(h + 1) * HEAD_DIM, :])
        if bias is not None:
            s = s + bias[h]
        s_scr[h, off:off + nrow, :] = s
        return jnp.maximum(m, jnp.max(s.reshape(nrow // sub, sub, tm), axis=0))

    def apply(h, part, off, m, l, acc):
        _, c0, nrow, _ = part
        p = jnp.exp2(s_scr[h, off:off + nrow, :].reshape(nrow // sub, sub, tm) - m[None])
        l = l + jnp.sum(p, axis=0)
        vt = jnp.concatenate([vt_ref[h, c0 + c] for c in range(nrow // tm)], axis=1)
        return l, acc + _dot(vt, p.reshape(nrow, tm).astype(BF16))

    def run(parts):
        offs = [sum(p[2] for p in parts[:n]) for n in range(len(parts))]
        maxes = [None] * NA_HEADS
        for stage in range(NA_HEADS + 1):
            h1, h2 = stage, stage - 1
            if h1 < NA_HEADS:
                m1 = jnp.full((sub, tm), NEG, F32)
            if h2 >= 0:
                m2 = jnp.broadcast_to(jnp.max(maxes[h2], axis=0, keepdims=True), (sub, tm))
                l = jnp.zeros((sub, tm), F32)
                acc = jnp.zeros((HEAD_DIM, tm), F32)
            for part, off in zip(parts, offs):
                if h1 < NA_HEADS:
                    m1 = scores(h1, part, off, m1)
                if h2 >= 0:
                    l, acc = apply(h2, part, off, m2, l, acc)
            if h1 < NA_HEADS:
                maxes[h1] = m1
            if h2 >= 0:
                o = acc / jnp.sum(l, axis=0, keepdims=True)
                o_ref[h2 * HEAD_DIM:(h2 + 1) * HEAD_DIM, :] = o.astype(o_ref.dtype)

    ctx_part = (s_len, s_len // tm, lc, None)

    @pl.when(i < nblk)
    def _():
        kind = jnp.where(i == 0, 0, jnp.where(i == nblk - 1, 2, 1))
        c0 = jnp.clip(i - (NA_KH // 2) // NA_ROWS, 0, (rows - NA_WIN) // NA_ROWS)
        ks = pl.multiple_of(c0 * tm, tm)
        run([(ks, c0, NA_WIN * GRID_W, tab_ref.at[kind]), ctx_part])

    @pl.when(i >= nblk)
    def _():
        run([ctx_part])


def _na_call(nqt, nk, nvt, table, s_len, n_query):
    bsz, heads, t_all, hd = nk.shape
    tm = TOK_TILE
    rows = s_len // GRID_W
    assert NA_ROWS * GRID_W == tm and (NA_KH // 2) % NA_ROWS == 0 and (rows - NA_WIN) % NA_ROWS == 0
    return pl.pallas_call(
        functools.partial(_na_kernel, s_len, t_all - s_len, rows),
        grid=(bsz, n_query // tm),
        in_specs=[pl.BlockSpec((None, heads * hd, tm), lambda b, i: (b, 0, i)),
                  pl.BlockSpec((None, heads, t_all, hd), lambda b, i: (b, 0, 0, 0)),
                  pl.BlockSpec((None,) + nvt.shape[1:], lambda b, i: (b, 0, 0, 0, 0)),
                  pl.BlockSpec(table.shape, lambda b, i: (0, 0, 0, 0), pipeline_mode=pl.Buffered(1))],
        out_specs=pl.BlockSpec((None, heads * hd, tm), lambda b, i: (b, 0, i)),
        out_shape=jax.ShapeDtypeStruct((bsz, heads * hd, t_all), BF16),
        scratch_shapes=[pltpu.VMEM((heads, NA_WIN * GRID_W + t_all - s_len, tm), F32)],
        name="neighbourhood_attention",
        compiler_params=_cparams(("parallel", "parallel")),
    )(nqt, nk, nvt, table)


def _hy_filter_kernel(z_ref, t_ref, side_ref, w1_ref, b1_ref, w2_ref, b2_ref, w3_ref, b3_ref,
                      wo_ref, fr_ref, dl_ref, o_ref):
    om = fr_ref[...]
    h = jnp.sin(om * (_dot(z_ref[...], w1_ref[...], precision=HIGHEST) + b1_ref[...]))
    h = jnp.sin(om * (_dot(h, w2_ref[...], precision=HIGHEST) + b2_ref[...]))
    h = jnp.sin(om * (_dot(h, w3_ref[...], precision=HIGHEST) + b3_ref[...]))
    ht = _dot_nt(wo_ref[...], h, precision=HIGHEST)
    c = o_ref.shape[0]
    decay = jnp.exp(-t_ref[...] * jnp.abs(dl_ref[...]))
    side = side_ref[...]
    o_ref[...] = decay * (jnp.where(side > 0.5, ht[:c], 0.0) + jnp.where(side < -0.5, ht[c:], 0.0))


def _hy_filter_consts(length):
    n = np.arange(2 * length)
    pos = np.where(n < length, n, 2 * length - n)
    pos = np.where(n == length, 0, pos)
    side = np.where(n < length, 1.0, -1.0)
    side = np.where(n == length, 0.0, side)
    t = np.linspace(0.0, 1.0, length)[pos]
    bands = (HY_EMB_DIM - 1) // 2
    w = 2.0 * math.pi * pos / length
    fr = np.linspace(1e-4, bands - 1, bands)
    z = np.concatenate([t[:, None], np.cos(fr[None, :] * w[:, None]), -np.sin(fr[None, :] * w[:, None])], axis=-1)
    zp = np.zeros((2 * length, HY_EMB_PAD), np.float32)
    zp[:, :HY_EMB_DIM] = z
    return zp, t.astype(np.float32)[None, :], side.astype(np.float32)[None, :]


def _hy_filter_call(length, w1, b1, w2, b2, w3, b3, wout, freq):
    f = w2.shape[0]
    c = wout.shape[1] // 2
    zp, t, side = _hy_filter_consts(length)
    w1p = jnp.zeros((HY_EMB_PAD, f), F32).at[:HY_EMB_DIM].set(w1)
    deltas = np.linspace(math.log(HY_DECAY_TARGET) / HY_SLOW_DECAY,
                         math.log(HY_DECAY_TARGET) / HY_FAST_DECAY, c).astype(np.float32)[:, None]
    args = (jnp.asarray(zp), jnp.asarray(t), jnp.asarray(side), w1p, b1[None], w2, b2[None], w3, b3[None],
            wout.T, freq[None], jnp.asarray(deltas))
    return pl.pallas_call(
        _hy_filter_kernel,
        in_specs=[pl.BlockSpec(a.shape, lambda: (0,) * a.ndim) for a in args],
        out_specs=pl.BlockSpec((c, 2 * length), lambda: (0, 0)),
        out_shape=jax.ShapeDtypeStruct((c, 2 * length), F32),
        grid=(),
        name="hyena_filter",
        compiler_params=pltpu.CompilerParams(vmem_limit_bytes=VMEM_LIMIT),
    )(*args)


def _bitrev(n):
    bits = int(math.log2(n))
    return np.array([int(format(i, "0%db" % bits)[::-1], 2) if bits else 0 for i in range(n)])


def _fft_consts(n1, ct):
    n = n1 * FFT_CHUNK
    stages = max(int(math.log2(n1)), 1)
    tw = np.zeros((2, stages, max(n1 // 2, 1)), np.float32)
    for s in range(int(math.log2(n1))):
        half = n1 >> (s + 1)
        ang = -2.0 * math.pi * np.arange(half) / (2 * half)
        tw[0, s, :half] = np.cos(ang)
        tw[1, s, :half] = np.sin(ang)
    k1 = _bitrev(n1)
    ang = -2.0 * math.pi * (np.arange(FFT_CHUNK)[None, :] * k1[:, None]) / n
    twr = np.repeat(np.cos(ang), ct, axis=0).astype(np.float32)
    twi = np.repeat(np.sin(ang), ct, axis=0).astype(np.float32)
    a = -2.0 * math.pi * np.outer(np.arange(FFT_CHUNK), np.arange(FFT_CHUNK)) / FFT_CHUNK
    wr, wi = np.cos(a), np.sin(a)
    wf = np.block([[wr, wi], [-wi, wr]]).astype(np.float32)
    wb = np.block([[wr, -wi], [wi, wr]]).astype(np.float32)
    return tw, twr, twi, _split_bf16(wf), _split_bf16(wb)


def _split_bf16(w):
    w = jnp.asarray(w, F32)
    hi = w.astype(BF16)
    return jnp.stack([hi, (w - hi.astype(F32)).astype(BF16)])


def _slab_fft(zr, zi, tw_ref, n1, ct, inverse, half_input=False, half_output=False):
    nst = int(math.log2(n1))
    order = range(nst - 1, -1, -1) if inverse else range(nst)
    for s in order:
        half = n1 >> (s + 1)
        lh = int(math.log2(half))
        first = s == 0

        def body(idx, carry, s=s, half=half, lh=lh, first=first):
            blk = idx >> lh
            j = idx & (half - 1)
            r0 = pl.multiple_of(((blk << (lh + 1)) + j) * ct, ct)
            r1 = pl.multiple_of(r0 + half * ct, ct)
            wr = tw_ref[0, s, j]
            wi = tw_ref[1, s, j]
            ar = zr[pl.ds(r0, ct), :]
            ai = zi[pl.ds(r0, ct), :]
            if not inverse:
                if first and half_input:
                    zr[pl.ds(r1, ct), :] = ar * wr - ai * wi
                    zi[pl.ds(r1, ct), :] = ar * wi + ai * wr
                else:
                    br = zr[pl.ds(r1, ct), :]
                    bi = zi[pl.ds(r1, ct), :]
                    dr = ar - br
                    di = ai - bi
                    zr[pl.ds(r0, ct), :] = ar + br
                    zi[pl.ds(r0, ct), :] = ai + bi
                    zr[pl.ds(r1, ct), :] = dr * wr - di * wi
                    zi[pl.ds(r1, ct), :] = dr * wi + di * wr
            else:
                br = zr[pl.ds(r1, ct), :]
                bi = zi[pl.ds(r1, ct), :]
                tr = br * wr + bi * wi
                ti = bi * wr - br * wi
                zr[pl.ds(r0, ct), :] = ar + tr
                zi[pl.ds(r0, ct), :] = ai + ti
                if not (first and half_output):
                    zr[pl.ds(r1, ct), :] = ar - tr
                    zi[pl.ds(r1, ct), :] = ai - ti
            return carry

        lax.fori_loop(0, n1 // 2, body, 0, unroll=min(4, n1 // 2))


def _chunk_dft(xr, xi, w_ref, split):
    x = jnp.concatenate([xr, xi], axis=1)
    x_hi = x.astype(BF16)
    y = _dot(x_hi, w_ref[0])
    if split:
        x_lo = (x - x_hi.astype(F32)).astype(BF16)
        y = y + (_dot(x_lo, w_ref[0]) + _dot(x_hi, w_ref[1]))
    return y[:, :FFT_CHUNK], y[:, FFT_CHUNK:]


def _fft_groups(n1, ct):
    g = min(8, n1)
    return n1 // g, g * ct


def _hy_spec_kernel(n1, ct, f_ref, tw_ref, twr_ref, twi_ref, wf_ref, fr_ref, fi_ref, zr, zi):
    for s in range(n1):
        zr[s * ct:(s + 1) * ct, :] = f_ref[:, s * FFT_CHUNK:(s + 1) * FFT_CHUNK]
    zi[...] = jnp.zeros(zi.shape, F32)
    _slab_fft(zr, zi, tw_ref, n1, ct, inverse=False)
    ngroups, gr = _fft_groups(n1, ct)

    def body(g, carry):
        r0 = pl.multiple_of(g * gr, gr)
        a = zr[pl.ds(r0, gr), :]
        b = zi[pl.ds(r0, gr), :]
        tr = twr_ref[pl.ds(r0, gr), :]
        ti = twi_ref[pl.ds(r0, gr), :]
        yr, yi = _chunk_dft(a * tr - b * ti, a * ti + b * tr, wf_ref, split=True)
        fr_ref[pl.ds(r0, gr), :] = yr
        fi_ref[pl.ds(r0, gr), :] = yi
        return carry

    lax.fori_loop(0, ngroups, body, 0, unroll=min(2, ngroups))


def _hy_spec_call(filt_t):
    c, n = filt_t.shape
    n1 = n // FFT_CHUNK
    ct = HY_CT
    tw, twr, twi, wf, _ = _fft_consts(n1, ct)
    const2 = lambda i: (0, 0)
    out = jax.ShapeDtypeStruct((c // ct, n1 * ct, FFT_CHUNK), F32)
    ospec = pl.BlockSpec((None, n1 * ct, FFT_CHUNK), lambda i: (i, 0, 0))
    return pl.pallas_call(
        functools.partial(_hy_spec_kernel, n1, ct),
        grid=(c // ct,),
        in_specs=[pl.BlockSpec((ct, n), lambda i: (i, 0)),
                  pl.BlockSpec(memory_space=pltpu.SMEM),
                  pl.BlockSpec(twr.shape, const2), pl.BlockSpec(twi.shape, const2),
                  pl.BlockSpec(wf.shape, lambda i: (0, 0, 0))],
        out_specs=[ospec, ospec],
        out_shape=[out, out],
        scratch_shapes=[pltpu.VMEM((n1 * ct, FFT_CHUNK), F32), pltpu.VMEM((n1 * ct, FFT_CHUNK), F32)],
        name="hyena_filter_spectrum",
        compiler_params=_cparams(("parallel",)),
    )(filt_t, jnp.asarray(tw), jnp.asarray(twr), jnp.asarray(twi), jnp.asarray(wf))


def _hy_conv_kernel(n1, ct, length, v_ref, x1_ref, x2_ref, cwv_ref, cw1_ref, cw2_ref, skip_ref,
                    fr_ref, fi_ref, tw_ref, twr_ref, twi_ref, wf_ref, wb_ref, o_ref,
                    zr, zi, u_scr, g_scr):
    lane = lax.broadcasted_iota(jnp.int32, (1, length), 1)

    sub = SUBLANES

    def short_conv(x, cw):
        xm = jnp.where(lane >= 1, pltpu.roll(x, 1, 1), 0.0)
        xp = jnp.where(lane <= length - 2, pltpu.roll(x, length - 1, 1), 0.0)
        return xm * cw[:, 0:1] + x * cw[:, 1:2] + xp * cw[:, 2:3] + cw[:, 3:4]

    def conv_rows(r, carry):
        rows = pl.ds(pl.multiple_of(r * sub, sub), sub)
        cwv, cw1, cw2 = cwv_ref[rows, :], cw1_ref[rows, :], cw2_ref[rows, :]
        for a in range(2):
            u_scr[a, rows, :] = short_conv(v_ref[a, rows, :], cwv) * short_conv(x1_ref[a, rows, :], cw1)
            g_scr[a, rows, :] = short_conv(x2_ref[a, rows, :], cw2)
        return carry

    lax.fori_loop(0, ct // sub, conv_rows, 0)
    nz = length // FFT_CHUNK
    for s in range(nz):
        zr[s * ct:(s + 1) * ct, :] = u_scr[0, :, s * FFT_CHUNK:(s + 1) * FFT_CHUNK]
        zi[s * ct:(s + 1) * ct, :] = u_scr[1, :, s * FFT_CHUNK:(s + 1) * FFT_CHUNK]
    _slab_fft(zr, zi, tw_ref, n1, ct, inverse=False, half_input=True)
    ngroups, gr = _fft_groups(n1, ct)

    def body(g, carry):
        r0 = pl.multiple_of(g * gr, gr)
        a = zr[pl.ds(r0, gr), :]
        b = zi[pl.ds(r0, gr), :]
        tr = twr_ref[pl.ds(r0, gr), :]
        ti = twi_ref[pl.ds(r0, gr), :]
        yr, yi = _chunk_dft(a * tr - b * ti, a * ti + b * tr, wf_ref, split=False)
        fr = fr_ref[pl.ds(r0, gr), :]
        fi = fi_ref[pl.ds(r0, gr), :]
        qr, qi = _chunk_dft(yr * fr - yi * fi, yr * fi + yi * fr, wb_ref, split=False)
        zr[pl.ds(r0, gr), :] = qr * tr + qi * ti
        zi[pl.ds(r0, gr), :] = qi * tr - qr * ti
        return carry

    lax.fori_loop(0, ngroups, body, 0, unroll=min(4, ngroups))
    _slab_fft(zr, zi, tw_ref, n1, ct, inverse=True, half_output=True)
    inv_n = 1.0 / (n1 * FFT_CHUNK)
    skip = skip_ref[...]
    for s in range(nz):
        sl = slice(s * FFT_CHUNK, (s + 1) * FFT_CHUNK)
        for a, z in ((0, zr), (1, zi)):
            y = z[s * ct:(s + 1) * ct, :] * inv_n
            u = u_scr[a, :, sl]
            o_ref[a, :, sl] = (g_scr[a, :, sl] * (y + u * skip)).astype(o_ref.dtype)


def _hy_conv_call(hy_t, spec_r, spec_i, cw, skip, length, lane_block, prev=None):
    _, bh, c3, t_all = hy_t.shape
    c = c3 // 3
    ct = HY_CT
    nct = c // ct
    n1 = 2 * length // FFT_CHUNK
    tw, twr, twi, wf, wb = _fft_consts(n1, ct)
    const2 = lambda p, i: (0, 0)
    xin = lambda k: pl.BlockSpec((2, None, ct, length), lambda p, i, k=k: (0, p, k * nct + i, lane_block))
    cwin = lambda k: pl.BlockSpec((ct, 4), lambda p, i, k=k: (k * nct + i, 0))
    fin = pl.BlockSpec((None, n1 * ct, FFT_CHUNK), lambda p, i: (i, 0, 0))
    in_specs = [xin(0), xin(1), xin(2), cwin(0), cwin(1), cwin(2),
                pl.BlockSpec((ct, 1), lambda p, i: (i, 0)),
                fin, fin,
                pl.BlockSpec(memory_space=pltpu.SMEM),
                pl.BlockSpec(twr.shape, const2), pl.BlockSpec(twi.shape, const2),
                pl.BlockSpec(wf.shape, lambda p, i: (0, 0, 0)),
                pl.BlockSpec(wb.shape, lambda p, i: (0, 0, 0))]
    args = [hy_t, hy_t, hy_t, cw, cw, cw, skip, spec_r, spec_i, jnp.asarray(tw), jnp.asarray(twr),
            jnp.asarray(twi), jnp.asarray(wf), jnp.asarray(wb)]
    kern = functools.partial(_hy_conv_kernel, n1, ct, length)
    aliases = {}
    if prev is not None:
        in_specs.append(pl.BlockSpec(memory_space=pl.ANY))
        args.append(prev)
        aliases = {len(args) - 1: 0}
        kern = lambda *refs, kern=kern: kern(*refs[:14], *refs[15:])
    return pl.pallas_call(
        kern,
        grid=(bh, nct),
        in_specs=in_specs,
        out_specs=pl.BlockSpec((2, None, ct, length), lambda p, i: (0, p, i, lane_block)),
        out_shape=jax.ShapeDtypeStruct((2, bh, c, t_all), BF16),
        scratch_shapes=[pltpu.VMEM((n1 * ct, FFT_CHUNK), F32), pltpu.VMEM((n1 * ct, FFT_CHUNK), F32),
                        pltpu.VMEM((2, ct, length), F32), pltpu.VMEM((2, ct, length), F32)],
        input_output_aliases=aliases,
        name="hyena_conv_%d" % length,
        compiler_params=_cparams(("parallel", "parallel")),
    )(*args)


def _outproj_kernel(n_lat_tiles, ctx_row, x_ref, hy_ref, ga_ref, na_ref, w_ref, mod_ref, o_ref):
    j = pl.program_id(1)
    nbat, tm, d = x_ref.shape
    mix = []
    for s in range(nbat):
        cm = jnp.concatenate([hy_ref[s], ga_ref[s], na_ref[s]], axis=0)
        mix.append(cm.astype(F32).T.astype(BF16))
    y = _dot(jnp.concatenate(mix, axis=0), w_ref[...])
    for s in range(nbat):
        row = jnp.where(j < n_lat_tiles, pl.program_id(0) * nbat + s, ctx_row)
        o_ref[s] = x_ref[s] + _mod_row(mod_ref, row, 2, d) * y[s * tm:(s + 1) * tm]


def _outproj_call(xa, hyo_t, gqa, na, w_out, mod_l, s_len, n_tok):
    bsz, _, d = xa.shape
    tm = TOK_TILE
    nbat = _batch_group(bsz)
    const = lambda b, j: (0, 0)
    tok = lambda n: pl.BlockSpec((nbat, tm, n), lambda b, j: (b, j, 0))
    chan = lambda n: pl.BlockSpec((nbat, n, tm), lambda b, j: (b, 0, j))
    return pl.pallas_call(
        functools.partial(_outproj_kernel, s_len // tm, bsz),
        grid=(bsz // nbat, n_tok // tm),
        in_specs=[tok(d), chan(hyo_t.shape[1]), chan(gqa.shape[1]), chan(na.shape[1]),
                  pl.BlockSpec(w_out.shape, const), pl.BlockSpec(mod_l.shape, const)],
        out_specs=tok(d),
        out_shape=jax.ShapeDtypeStruct((bsz, n_tok, d), F32),
        name="out_proj",
        compiler_params=_cparams(("parallel", "parallel")),
    )(xa, hyo_t, gqa, na, w_out, mod_l)


def _ffn_kernel(tiles_per_batch, n_lat_tiles, ctx_row, th, x_ref, mod_ref, g_ref, wi_ref, wo_ref,
                gf_ref, o_ref, h_scr, act_scr):
    i = pl.program_id(0)
    d = x_ref.shape[-1]
    f = wo_ref.shape[0]
    groups = x_ref.shape[0] // TOK_TILE

    def mod_row(s):
        gi = i * groups + s
        jj = gi % tiles_per_batch
        return jnp.where(jj < n_lat_tiles, gi // tiles_per_batch, ctx_row)

    for s in range(groups):
        row = mod_row(s)
        sl = slice(s * TOK_TILE, (s + 1) * TOK_TILE)
        h = _norm_mod(x_ref[sl, :], g_ref[...], _mod_row(mod_ref, row, 3, d), _mod_row(mod_ref, row, 4, d))
        h_scr[sl, :] = h.astype(BF16)
    h = h_scr[...]
    for j in range(f // th):
        gate = _dot(h, wi_ref[:, j * th:(j + 1) * th])
        up = _dot(h, wi_ref[:, f + j * th:f + (j + 1) * th])
        act_scr[:, j * th:(j + 1) * th] = (gate * jax.nn.sigmoid(gate) * up).astype(BF16)
    y = _dot(act_scr[...], wo_ref[...])
    for s in range(groups):
        row = mod_row(s)
        sl = slice(s * TOK_TILE, (s + 1) * TOK_TILE)
        out = x_ref[sl, :] + _mod_row(mod_ref, row, 5, d) * y[sl, :]
        if gf_ref is not None:
            ms = jnp.mean(out * out, axis=-1, keepdims=True)
            out = out * lax.rsqrt(ms + NORM_EPS) * gf_ref[...]
        o_ref[sl, :] = out


def _ffn_call(xa, mod_l, g, w_in, w_out, s_len, g_final=None):
    bsz, t_all, d = xa.shape
    f = w_out.shape[0]
    r = bsz * t_all
    tm = 1024 if r % 1024 == 0 else TOK_TILE
    th = 256
    const = lambda i: (0, 0)
    resident = lambda a: pl.BlockSpec(a.shape, const, pipeline_mode=pl.Buffered(1))
    in_specs = [pl.BlockSpec((tm, d), lambda i: (i, 0)),
                pl.BlockSpec(mod_l.shape, const),
                pl.BlockSpec((1, d), const),
                resident(w_in), resident(w_out)]
    args = [xa.reshape(r, d), mod_l, g, w_in, w_out]
    kern = functools.partial(_ffn_kernel, t_all // TOK_TILE, s_len // TOK_TILE, bsz, th)
    if g_final is None:
        kern = functools.partial(_ffn_no_final, kern)
    else:
        in_specs.append(pl.BlockSpec((1, d), const))
        args.append(g_final)
    out = pl.pallas_call(
        kern,
        grid=(r // tm,),
        in_specs=in_specs,
        out_specs=pl.BlockSpec((tm, d), lambda i: (i, 0)),
        out_shape=jax.ShapeDtypeStruct((r, d), F32),
        scratch_shapes=[pltpu.VMEM((tm, d), BF16), pltpu.VMEM((tm, f), BF16)],
        name="swiglu_ffn",
        compiler_params=_cparams(("parallel",)),
    )(*args)
    return out.reshape(bsz, t_all, d)


def _ffn_no_final(kern, x_ref, mod_ref, g_ref, wi_ref, wo_ref, o_ref, h_scr, act_scr):
    kern(x_ref, mod_ref, g_ref, wi_ref, wo_ref, None, o_ref, h_scr, act_scr)


def _rope_tables(s_len, lc):
    pos = np.arange(s_len)
    row = (pos // GRID_W).astype(np.float64)
    col = (pos % GRID_W).astype(np.float64)
    n_f = HEAD_DIM // 4
    inv = (ROPE_THETA ** (-np.arange(n_f, dtype=np.float32) / n_f)).astype(np.float64)
    ang = np.concatenate([row[:, None] * inv, col[:, None] * inv], axis=-1).astype(np.float32).astype(np.float64)
    lane = np.arange(LANES)
    sign = np.where((lane % HEAD_DIM) < HEAD_DIM // 2, -1.0, 1.0)
    cos = np.cos(ang)[:, lane % (HEAD_DIM // 2)]
    sin = np.sin(ang)[:, lane % (HEAD_DIM // 2)] * sign[None, :]
    cos = np.concatenate([cos, np.ones((lc, LANES))], axis=0).astype(np.float32)
    sin = np.concatenate([sin, np.zeros((lc, LANES))], axis=0).astype(np.float32)
    cosc = np.concatenate([np.cos(ang).T, np.ones((HEAD_DIM // 2, lc))], axis=1).astype(np.float32)
    sinc = np.concatenate([np.sin(ang).T, np.zeros((HEAD_DIM // 2, lc))], axis=1).astype(np.float32)
    return jnp.asarray(cosc), jnp.asarray(sinc), jnp.asarray(cos), jnp.asarray(sin)


def kernel(x, c, ctx, c_ctx, w_mod, b_mod, g_mix, g_ffn, w_in, w_out, hy_conv_w, hy_conv_b,
           hy_f_w1, hy_f_b1, hy_f_w2, hy_f_b2, hy_f_w3, hy_f_b3, hy_f_wout, hy_f_freq, hy_skip,
           qk_g_q, qk_g_k, na_rpb, w_ffn_in, w_ffn_out, g_final):
    bsz, s_len, d = x.shape
    lc = ctx.shape[1]
    depth = w_mod.shape[0]
    t_all = s_len + lc
    hyw = hy_skip.shape[1]
    hy_cols = 3 * hyw
    assert bsz % 2 == 0 and s_len % TOK_TILE == 0 and lc == TOK_TILE and s_len % lc == 0

    nb = -(-(bsz + 1) // 8) * 8
    cs = jnp.zeros((nb, d), F32).at[:bsz].set(c).at[bsz].set(c_ctx)
    mod = _mod_call(cs, w_mod, b_mod)

    rope = _rope_tables(s_len, lc)
    tables = _na_bias_tables(na_rpb, s_len // GRID_W)
    q0 = hy_cols
    k0 = q0 + GQA_HEADS * HEAD_DIM
    v0 = k0 + GQA_KV_HEADS * HEAD_DIM
    nq0 = v0 + GQA_KV_HEADS * HEAD_DIM
    nk0 = nq0 + NA_HEADS * HEAD_DIM
    nv0 = nk0 + NA_HEADS * HEAD_DIM
    w_cm = jnp.concatenate([w_in[:, :, :k0], w_in[:, :, v0:nk0], w_in[:, :, nv0:]], axis=2)
    w_cm = jnp.swapaxes(w_cm, 1, 2).astype(BF16)
    w_tm = jnp.concatenate([w_in[:, :, k0:v0], w_in[:, :, nk0:nv0]], axis=2).astype(BF16)
    w_out_b = w_out.astype(BF16)
    w_fi = w_ffn_in.astype(BF16)
    w_fo = w_ffn_out.astype(BF16)
    cw = jnp.concatenate([jnp.swapaxes(hy_conv_w, 1, 2), hy_conv_b[:, :, None]], axis=2)
    per = LANES // HEAD_DIM

    xa = jnp.concatenate([x, ctx], axis=1)
    for l in range(depth):
        last = l == depth - 1
        gq_b = jnp.broadcast_to(qk_g_q[l][:, None], (HEAD_DIM, LANES))
        gk2 = jnp.tile(qk_g_k[l], per)[None]
        hy_t, q, k, v, nq, nk, nv = _inproj_call(xa, mod[l], g_mix[l][None], w_cm[l], w_tm[l],
                                                  gq_b, gk2, rope, hy_cols, s_len)
        n_tok = s_len if last else t_all
        gqa = _gqa_call(q, k, v, s_len, n_tok)
        na = _na_call(nq, nk, nv, tables[l], s_len, n_tok)

        fargs = (hy_f_w1[l], hy_f_b1[l], hy_f_w2[l], hy_f_b2[l], hy_f_w3[l], hy_f_b3[l],
                 hy_f_wout[l], hy_f_freq[l])
        hy_p = hy_t.reshape(2, bsz // 2, hy_cols, t_all)
        skip = hy_skip[l][:, None]
        fr, fi = _hy_spec_call(_hy_filter_call(s_len, *fargs))
        hyo = _hy_conv_call(hy_p, fr, fi, cw[l], skip, s_len, 0)
        if not last:
            frc, fic = _hy_spec_call(_hy_filter_call(lc, *fargs))
            hyo = _hy_conv_call(hy_p, frc, fic, cw[l], skip, lc, s_len // lc, prev=hyo)
        hyo = hyo.reshape(bsz, hyw, t_all)

        xa = _outproj_call(xa, hyo, gqa, na, w_out_b[l], mod[l], s_len, n_tok)
        xa = _ffn_call(xa, mod[l], g_ffn[l][None], w_fi[l], w_fo[l], s_len,
                       g_final[None] if last else None)
    return xa
```
